```python
import math
import jax, jax.numpy as jnp
from jax import lax
import numpy as np


D_MODEL = 1024
BATCH = 2
SEQ = 16384
DEPTH = 2

N_MIXERS = 2
ROPE_THETA = 500000.0
ROT_DIM = 16
Q_BLOCK = 128
NORM_EPS = 1e-6
NEG_INF = -1e30

DA_HEAD_DIM = 64
DA_V_DIM = 2 * DA_HEAD_DIM
DA_HEADS = D_MODEL // DA_V_DIM

NSA_HEAD_DIM = 64
NSA_HEADS = D_MODEL // NSA_HEAD_DIM
NSA_KV_GROUPS = 2
NSA_HEADS_PER_GROUP = NSA_HEADS // NSA_KV_GROUPS
NSA_CMP_BLOCK = 32
NSA_CMP_STRIDE = 16
NSA_CMP_HIDDEN = 4 * NSA_HEAD_DIM
NSA_SEL_BLOCK = 64
NSA_TOP_N = 16
NSA_WINDOW = 512
NSA_FORCE = 1e4
NSA_IN_DIM = NSA_HEADS * NSA_HEAD_DIM + 6 * NSA_KV_GROUPS * NSA_HEAD_DIM + 3 * NSA_HEADS

MOE_GROUPS = 4
MOE_EXPERTS_PER_GROUP = 8
MOE_EXPERTS = MOE_GROUPS * MOE_EXPERTS_PER_GROUP
MOE_TOP_K = 2
MOE_HIDDEN = D_MODEL // 4

kernel_name = 'hybrid_diffattn_nsa_hiermoe_adaln'


def _rmsnorm(x, g):
    xf = x.astype(jnp.float32)
    y = xf * lax.rsqrt(jnp.mean(xf * xf, axis=-1, keepdims=True) + NORM_EPS)
    return (y * g.astype(jnp.float32)).astype(x.dtype)


def _rope_tables(positions):
    inv_freq = ROPE_THETA ** (-jnp.arange(0, ROT_DIM, 2, dtype=jnp.float32) / ROT_DIM)
    ang = positions.astype(jnp.float32)[..., None] * inv_freq
    return jnp.cos(ang)[:, :, None, :], jnp.sin(ang)[:, :, None, :]


def _rotary(x, cos, sin):
    half = ROT_DIM // 2
    xr = x[..., :ROT_DIM].astype(jnp.float32)
    x1, x2 = xr[..., :half], xr[..., half:]
    rot = jnp.concatenate([x1 * cos - x2 * sin, x2 * cos + x1 * sin], axis=-1).astype(x.dtype)
    return jnp.concatenate([rot, x[..., ROT_DIM:]], axis=-1)


def _masked_softmax(s, mask):
    s = jnp.where(mask, s.astype(jnp.float32), NEG_INF)
    p = jnp.where(mask, jnp.exp(s - jnp.max(s, axis=-1, keepdims=True)), 0.0)
    return p / jnp.maximum(jnp.sum(p, axis=-1, keepdims=True), 1e-30)


def diff_attention(h, cos, sin, w_in, w_out, lam, subln_g, lambda_init):
    B, S, D = h.shape
    H, d = DA_HEADS, DA_HEAD_DIM
    q, k, v = jnp.split(h @ w_in, 3, axis=-1)
    q = _rotary(q.reshape(B, S, 2 * H, d), cos, sin).reshape(B, S, H, 2, d).transpose(3, 0, 2, 1, 4)
    k = _rotary(k.reshape(B, S, 2 * H, d), cos, sin).reshape(B, S, H, 2, d).transpose(3, 0, 2, 1, 4)
    v = v.reshape(B, S, H, DA_V_DIM).transpose(0, 2, 1, 3)
    lam = lam.astype(jnp.float32)
    lam_full = jnp.exp(jnp.sum(lam[0] * lam[1])) - jnp.exp(jnp.sum(lam[2] * lam[3])) + lambda_init
    scale = d ** -0.5
    k_pos = jnp.arange(S)

    def block(i):
        q0 = i * Q_BLOCK
        qb = lax.dynamic_slice_in_dim(q, q0, Q_BLOCK, axis=3)
        s = jnp.einsum('cbhqd,cbhkd->cbhqk', qb, k) * scale
        mask = k_pos[None, :] <= (q0 + jnp.arange(Q_BLOCK))[:, None]
        p = _masked_softmax(s, mask)
        a = p[0] - lam_full * p[1]
        return jnp.einsum('bhqk,bhkv->bhqv', a.astype(v.dtype), v)

    o = lax.map(block, jnp.arange(S // Q_BLOCK))
    o = o.transpose(1, 0, 3, 2, 4).reshape(B, S, H, DA_V_DIM)
    o = _rmsnorm(o, subln_g) * (1.0 - lambda_init)
    return o.reshape(B, S, D) @ w_out


def _nsa_compress(t, pos_emb, w1, b1, w2):
    B, G, S, d = t.shape
    ch = t.reshape(B, G, S // NSA_CMP_STRIDE, NSA_CMP_STRIDE, d)
    blk = jnp.concatenate([ch[:, :, :-1], ch[:, :, 1:]], axis=3) + pos_emb
    blk = blk.reshape(B, G, blk.shape[2], NSA_CMP_BLOCK * d)
    return jax.nn.gelu(blk @ w1 + b1) @ w2


def nsa_attention(h, cos, sin, w_in, w_out, cmp_pos, cmp_w1, cmp_b1, cmp_w2):
    B, S, D = h.shape
    H, G, Hg, d = NSA_HEADS, NSA_KV_GROUPS, NSA_HEADS_PER_GROUP, NSA_HEAD_DIM
    kvw = G * d
    proj = h @ w_in
    bounds = [H * d + j * kvw for j in range(7)]
    q, k_c, v_c, k_s, v_s, k_w, v_w, g_raw = jnp.split(proj, bounds, axis=-1)
    q = _rotary(q.reshape(B, S, H, d), cos, sin).reshape(B, S, G, Hg, d).transpose(0, 2, 3, 1, 4)

    def kv(t, rope):
        t = t.reshape(B, S, G, d)
        if rope:
            t = _rotary(t, cos, sin)
        return t.transpose(0, 2, 1, 3)

    k_c, v_c = kv(k_c, False), kv(v_c, False)
    k_s, v_s = kv(k_s, True), kv(v_s, True)
    k_w, v_w = kv(k_w, True), kv(v_w, True)
    gates = jax.nn.sigmoid(g_raw.astype(jnp.float32)).reshape(B, S, G, Hg, 3).transpose(0, 2, 3, 1, 4)

    k_cmp = _nsa_compress(k_c, cmp_pos[0], cmp_w1[0], cmp_b1[0], cmp_w2[0])
    v_cmp = _nsa_compress(v_c, cmp_pos[1], cmp_w1[1], cmp_b1[1], cmp_w2[1])
    n_cmp = k_cmp.shape[2]
    n_sel = S // NSA_SEL_BLOCK
    n_top = min(NSA_TOP_N, n_sel)
    cmp_start = jnp.arange(n_cmp) * NSA_CMP_STRIDE
    cmp_end = cmp_start + NSA_CMP_BLOCK - 1
    sel_start = jnp.arange(n_sel) * NSA_SEL_BLOCK
    cmp_to_sel = ((cmp_start[:, None] < sel_start[None, :] + NSA_SEL_BLOCK)
                  & (cmp_start[:, None] + NSA_CMP_BLOCK > sel_start[None, :])).astype(jnp.float32)
    k_blocks = k_s.reshape(B, G, n_sel, NSA_SEL_BLOCK, d)
    v_blocks = v_s.reshape(B, G, n_sel, NSA_SEL_BLOCK, d)
    pad = ((0, 0), (0, 0), (NSA_WINDOW, 0), (0, 0))
    k_w_pad, v_w_pad = jnp.pad(k_w, pad), jnp.pad(v_w, pad)
    b_ix = jnp.arange(B)[:, None, None, None]
    g_ix = jnp.arange(G)[None, :, None, None]
    blk_off = jnp.arange(NSA_SEL_BLOCK)
    sel_ids = jnp.arange(n_sel)
    scale = d ** -0.5

    def block(i):
        q0 = i * Q_BLOCK
        tq = q0 + jnp.arange(Q_BLOCK)
        qb = lax.dynamic_slice_in_dim(q, q0, Q_BLOCK, axis=3)
        s_c = jnp.einsum('bghqd,bgnd->bghqn', qb, k_cmp) * scale
        p_c = _masked_softmax(s_c, cmp_end[None, :] <= tq[:, None])
        o_c = jnp.einsum('bghqn,bgnd->bghqd', p_c.astype(v_cmp.dtype), v_cmp)
        imp = jnp.einsum('bgqn,ns->bgqs', p_c.sum(axis=2), cmp_to_sel)
        q_blk = tq // NSA_SEL_BLOCK
        forced = ((sel_ids[None, :] == 0) | (sel_ids[None, :] == q_blk[:, None])
                  | (sel_ids[None, :] == q_blk[:, None] - 1))
        causal = sel_ids[None, :] <= q_blk[:, None]
        imp = jnp.where(forced, NSA_FORCE, jnp.where(causal, imp, -NSA_FORCE))
        _, sel = lax.top_k(imp, n_top)
        k_sel = k_blocks[b_ix, g_ix, sel].reshape(B, G, Q_BLOCK, n_top * NSA_SEL_BLOCK, d)
        v_sel = v_blocks[b_ix, g_ix, sel].reshape(B, G, Q_BLOCK, n_top * NSA_SEL_BLOCK, d)
        tok = (sel[..., None] * NSA_SEL_BLOCK + blk_off).reshape(B, G, Q_BLOCK, n_top * NSA_SEL_BLOCK)
        s_s = jnp.einsum('bghqd,bgqkd->bghqk', qb, k_sel) * scale
        p_s = _masked_softmax(s_s, tok[:, :, None] <= tq[:, None])
        o_s = jnp.einsum('bghqk,bgqkd->bghqd', p_s.astype(v_sel.dtype), v_sel)
        k_win = lax.dynamic_slice_in_dim(k_w_pad, q0, Q_BLOCK + NSA_WINDOW, axis=2)
        v_win = lax.dynamic_slice_in_dim(v_w_pad, q0, Q_BLOCK + NSA_WINDOW, axis=2)
        tk = q0 - NSA_WINDOW + jnp.arange(Q_BLOCK + NSA_WINDOW)
        dist = tq[:, None] - tk[None, :]
        mask_w = (tk[None, :] >= 0) & (dist >= 0) & (dist < NSA_WINDOW)
        s_w = jnp.einsum('bghqd,bgkd->bghqk', qb, k_win) * scale
        p_w = _masked_softmax(s_w, mask_w)
        o_w = jnp.einsum('bghqk,bgkd->bghqd', p_w.astype(v_win.dtype), v_win)
        gb = lax.dynamic_slice_in_dim(gates, q0, Q_BLOCK, axis=3)
        o = gb[..., 0:1] * o_c + gb[..., 1:2] * o_s + gb[..., 2:3] * o_w
        return o.astype(h.dtype)

    o = lax.map(block, jnp.arange(S // Q_BLOCK))
    o = o.transpose(1, 0, 4, 2, 3, 5).reshape(B, S, D)
    return o @ w_out


def hier_moe(h, w_grp, b_grp, w_exp, b_exp, w_gate, w_up, w_down):
    B, S, D = h.shape
    t = h.reshape(B * S, D)
    T = t.shape[0]
    pg = jax.nn.softmax((t @ w_grp).astype(jnp.float32) + b_grp, axis=-1)
    p_top, grp = lax.top_k(pg, 1)
    le = jnp.einsum('td,gde->tge', t, w_exp).astype(jnp.float32) + b_exp
    le_sel = le[jnp.arange(T), grp[:, 0]]
    val, idx = lax.top_k(le_sel, MOE_TOP_K)
    wts = jax.nn.softmax(val, axis=-1) * p_top
    eid = grp * MOE_EXPERTS_PER_GROUP + idx
    combine = jnp.sum(jax.nn.one_hot(eid, MOE_EXPERTS, dtype=jnp.float32) * wts[..., None], axis=1)
    y = jnp.zeros_like(t)
    for gi in range(MOE_GROUPS):
        sl = slice(gi * MOE_EXPERTS_PER_GROUP, (gi + 1) * MOE_EXPERTS_PER_GROUP)
        a = jax.nn.silu(jnp.einsum('td,edh->teh', t, w_gate[sl])) * jnp.einsum('td,edh->teh', t, w_up[sl])
        a = a * combine[:, sl, None].astype(a.dtype)
        y = y + jnp.einsum('teh,ehd->td', a, w_down[sl]).astype(t.dtype)
    return y.reshape(B, S, D)


def setup_inputs(seed: int = 0) -> dict:
    key = jax.random.key(seed)
    ks = iter(jax.random.split(key, 32))
    D = D_MODEL
    n_a = (DEPTH + N_MIXERS - 1) // N_MIXERS
    n_b = DEPTH // N_MIXERS

    def nrm(shape, scale):
        return jax.random.normal(next(ks), shape, jnp.float32) * scale

    return {
        'x': nrm((BATCH, SEQ, D), 1.0),
        'c': nrm((BATCH, D), 1.0),
        'positions': (jax.random.randint(next(ks), (BATCH, 1), 0, 4096) + jnp.arange(SEQ)[None, :]).astype(jnp.int32),
        'ada_w': nrm((DEPTH, D, 6 * D), 0.5 * D ** -0.5),
        'ada_b': nrm((DEPTH, 6 * D), 0.02),
        'norm_g': 1.0 + nrm((DEPTH, 2, D), 0.02),
        'final_g': 1.0 + nrm((D,), 0.02),
        'diff_w_in': nrm((n_a, D, 3 * D), D ** -0.5),
        'diff_w_out': nrm((n_a, D, D), D ** -0.5),
        'diff_lambda': nrm((n_a, 4, DA_HEAD_DIM), 0.1),
        'diff_subln_g': 1.0 + nrm((n_a, DA_V_DIM), 0.02),
        'nsa_w_in': nrm((n_b, D, NSA_IN_DIM), D ** -0.5),
        'nsa_w_out': nrm((n_b, D, D), D ** -0.5),
        'nsa_cmp_pos': nrm((n_b, 2, NSA_CMP_BLOCK, NSA_HEAD_DIM), 0.02),
        'nsa_cmp_w1': nrm((n_b, 2, NSA_CMP_BLOCK * NSA_HEAD_DIM, NSA_CMP_HIDDEN), (NSA_CMP_BLOCK * NSA_HEAD_DIM) ** -0.5),
        'nsa_cmp_b1': nrm((n_b, 2, NSA_CMP_HIDDEN), 0.02),
        'nsa_cmp_w2': nrm((n_b, 2, NSA_CMP_HIDDEN, NSA_HEAD_DIM), NSA_CMP_HIDDEN ** -0.5),
        'moe_w_group': nrm((DEPTH, D, MOE_GROUPS), D ** -0.5),
        'moe_b_group': nrm((DEPTH, MOE_GROUPS), 0.01),
        'moe_w_expert': nrm((DEPTH, MOE_GROUPS, D, MOE_EXPERTS_PER_GROUP), D ** -0.5),
        'moe_b_expert': nrm((DEPTH, MOE_GROUPS, MOE_EXPERTS_PER_GROUP), 0.01),
        'moe_w_gate': nrm((DEPTH, MOE_EXPERTS, D, MOE_HIDDEN), D ** -0.5),
        'moe_w_up': nrm((DEPTH, MOE_EXPERTS, D, MOE_HIDDEN), D ** -0.5),
        'moe_w_down': nrm((DEPTH, MOE_EXPERTS, MOE_HIDDEN, D), MOE_HIDDEN ** -0.5),
    }


def reference(x, c, positions, ada_w, ada_b, norm_g, final_g,
              diff_w_in, diff_w_out, diff_lambda, diff_subln_g,
              nsa_w_in, nsa_w_out, nsa_cmp_pos, nsa_cmp_w1, nsa_cmp_b1, nsa_cmp_w2,
              moe_w_group, moe_b_group, moe_w_expert, moe_b_expert,
              moe_w_gate, moe_w_up, moe_w_down):
    cos, sin = _rope_tables(positions)
    c_act = jax.nn.silu(c)
    for i in range(DEPTH):
        mod = c_act @ ada_w[i] + ada_b[i]
        sh1, sc1, g1, sh2, sc2, g2 = jnp.split(mod[:, None, :], 6, axis=-1)
        h = _rmsnorm(x, norm_g[i, 0]) * (1.0 + sc1) + sh1
        j = i // N_MIXERS
        if i % N_MIXERS == 0:
            lambda_init = 0.8 - 0.6 * math.exp(-0.3 * i)
            mix = diff_attention(h, cos, sin, diff_w_in[j], diff_w_out[j], diff_lambda[j],
                                 diff_subln_g[j], lambda_init)
        else:
            mix = nsa_attention(h, cos, sin, nsa_w_in[j], nsa_w_out[j], nsa_cmp_pos[j],
                                nsa_cmp_w1[j], nsa_cmp_b1[j], nsa_cmp_w2[j])
        x = x + g1 * mix
        h = _rmsnorm(x, norm_g[i, 1]) * (1.0 + sc2) + sh2
        x = x + g2 * hier_moe(h, moe_w_group[i], moe_b_group[i], moe_w_expert[i], moe_b_expert[i],
                              moe_w_gate[i], moe_w_up[i], moe_w_down[i])
    return _rmsnorm(x, final_g)
```

```python
import functools
import math

import jax
import jax.numpy as jnp
from jax import lax
from jax.experimental import pallas as pl
from jax.experimental.pallas import tpu as pltpu

F32 = jnp.float32
BF16 = jnp.bfloat16
HIGHEST = lax.Precision.HIGHEST

ROPE_THETA = 500000.0
ROT_DIM = 16
NORM_EPS = 1e-6
NEG_INF = -1e30
HEAD_DIM = 64
DA_V_DIM = 2 * HEAD_DIM
NSA_KV_GROUPS = 2
NSA_CMP_BLOCK = 32
NSA_CMP_STRIDE = 16
NSA_SEL_BLOCK = 64
NSA_TOP_N = 16
NSA_WINDOW = 512
NSA_FORCE = 1e4
MOE_GROUPS = 4
MOE_EXPERTS_PER_GROUP = 8
MOE_TOP_K = 2
N_MIXERS = 2
LOG2E = 1.4426950408889634

LANES = 128
VMEM_LIMIT_BYTES = 56 * 1024 * 1024

NT_DIMS = (((1,), (1,)), ((), ()))
TN_DIMS = (((0,), (0,)), ((), ()))


def _cparams(*sem):
    return pltpu.CompilerParams(dimension_semantics=sem, vmem_limit_bytes=VMEM_LIMIT_BYTES)


def _mod_kernel(c_ref, w_ref, b_ref, o_ref):
    c = c_ref[...]
    ca = c * jax.nn.sigmoid(c)
    o_ref[0] = jnp.dot(ca, w_ref[0], precision=HIGHEST, preferred_element_type=F32) + b_ref[0]


def _adaln_mod(c, ada_w, ada_b):
    depth, d, n = ada_w.shape
    b = c.shape[0]
    rows = 8
    tn = 1536
    c_pad = jnp.zeros((rows, d), F32).at[:b].set(c)
    out = pl.pallas_call(
        _mod_kernel,
        grid=(depth, n // tn),
        in_specs=[
            pl.BlockSpec((rows, d), lambda i, j: (0, 0)),
            pl.BlockSpec((1, d, tn), lambda i, j: (i, 0, j)),
            pl.BlockSpec((1, 1, tn), lambda i, j: (i, 0, j)),
        ],
        out_specs=pl.BlockSpec((1, rows, tn), lambda i, j: (i, 0, j)),
        out_shape=jax.ShapeDtypeStruct((depth, rows, n), F32),
        compiler_params=_cparams("arbitrary", "arbitrary"),
        name="adaln_mod",
    )(c_pad, ada_w, ada_b.reshape(depth, 1, n))
    return out[:, :b, :]


def _rope_tables(positions):
    half = ROT_DIM // 2
    inv_freq = ROPE_THETA ** (-jnp.arange(0, ROT_DIM, 2, dtype=F32) / ROT_DIM)
    ang = positions.astype(F32).reshape(-1)[:, None] * inv_freq
    cos, sin = jnp.cos(ang), jnp.sin(ang)
    t = ang.shape[0]
    rest = HEAD_DIM - ROT_DIM
    z8 = jnp.zeros((t, half), F32)
    cos_h = jnp.concatenate([cos, cos, jnp.ones((t, rest), F32)], axis=-1)
    sina_h = jnp.concatenate([-sin, z8, jnp.zeros((t, rest), F32)], axis=-1)
    sinb_h = jnp.concatenate([z8, sin, jnp.zeros((t, rest), F32)], axis=-1)
    rep = LANES // HEAD_DIM
    return tuple(jnp.tile(a, (1, rep)) for a in (cos_h, sina_h, sinb_h))


def _rms_modulate(x, g, sc, sh):
    ms = jnp.mean(x * x, axis=-1, keepdims=True)
    y = x * lax.rsqrt(ms + NORM_EPS)
    return (y * g) * (1.0 + sc) + sh


def _proj_kernel(*refs, plain_tiles, with_gates):
    if with_gates:
        (x_ref, g_ref, sc_ref, sh_ref, w_ref, cos_ref, sa_ref, sb_ref, wg_ref,
         o_ref, gt_ref, h_scr) = refs
    else:
        x_ref, g_ref, sc_ref, sh_ref, w_ref, cos_ref, sa_ref, sb_ref, o_ref, h_scr = refs
    j = pl.program_id(1)
    tn = o_ref.shape[1]

    @pl.when(j == 0)
    def _():
        h = _rms_modulate(x_ref[...], g_ref[...], sc_ref[0], sh_ref[0])
        h_scr[...] = h.astype(BF16)
        if with_gates:
            logits = lax.dot_general(wg_ref[...], h_scr[...], NT_DIMS, preferred_element_type=F32)
            gt_ref[...] = jax.nn.sigmoid(logits)

    acc = jnp.dot(h_scr[...], w_ref[...], preferred_element_type=F32)

    is_plain = functools.reduce(jnp.logical_or, [j == t for t in plain_tiles], j < 0)

    @pl.when(is_plain)
    def _():
        o_ref[...] = acc.astype(o_ref.dtype)

    @pl.when(jnp.logical_not(is_plain))
    def _():
        shift_up = pltpu.roll(acc, tn - ROT_DIM // 2, 1)
        shift_dn = pltpu.roll(acc, ROT_DIM // 2, 1)
        cos, sa, sb = cos_ref[...], sa_ref[...], sb_ref[...]
        for c in range(tn // LANES):
            sl = slice(c * LANES, (c + 1) * LANES)
            y = acc[:, sl] * cos + shift_up[:, sl] * sa + shift_dn[:, sl] * sb
            o_ref[:, sl] = y.astype(o_ref.dtype)


def _norm_proj(x2d, g, sc, sh, w, tables, seq, *, tn, plain_tiles, w_gates=None, tm=512):
    t, d = x2d.shape
    n = w.shape[1]
    with_gates = w_gates is not None
    per_b = seq // tm
    in_specs = [
        pl.BlockSpec((tm, d), lambda i, j: (i, 0)),
        pl.BlockSpec((1, d), lambda i, j: (0, 0)),
        pl.BlockSpec((1, 1, d), lambda i, j: (i // per_b, 0, 0)),
        pl.BlockSpec((1, 1, d), lambda i, j: (i // per_b, 0, 0)),
        pl.BlockSpec((d, tn), lambda i, j: (0, j)),
        pl.BlockSpec((tm, LANES), lambda i, j: (i, 0)),
        pl.BlockSpec((tm, LANES), lambda i, j: (i, 0)),
        pl.BlockSpec((tm, LANES), lambda i, j: (i, 0)),
    ]
    args = [x2d, g.reshape(1, d), sc[:, None, :], sh[:, None, :], w, *tables]
    out_specs = [pl.BlockSpec((tm, tn), lambda i, j: (i, j))]
    out_shape = [jax.ShapeDtypeStruct((t, n), BF16)]
    if with_gates:
        ng = w_gates.shape[0]
        in_specs.append(pl.BlockSpec((ng, d), lambda i, j: (0, 0)))
        args.append(w_gates)
        out_specs.append(pl.BlockSpec((ng, tm), lambda i, j: (0, i)))
        out_shape.append(jax.ShapeDtypeStruct((ng, t), F32))
    res = pl.pallas_call(
        functools.partial(_proj_kernel, plain_tiles=tuple(plain_tiles), with_gates=with_gates),
        grid=(t // tm, n // tn),
        in_specs=in_specs,
        out_specs=out_specs,
        out_shape=out_shape,
        scratch_shapes=[pltpu.VMEM((tm, d), BF16)],
        compiler_params=_cparams("arbitrary", "arbitrary"),
        name="norm_proj",
    )(*args)
    return res if with_gates else res[0]


def _softmax_step(s, v, m_ref, l_ref, acc_ref):
    m_prev = m_ref[...]
    m_new = jnp.maximum(m_prev, jnp.max(s, axis=0, keepdims=True))
    alpha = jnp.exp2(m_prev - m_new)
    p = jnp.exp2(s - m_new)
    l_ref[...] = alpha * l_ref[...] + jnp.sum(p, axis=0, keepdims=True)
    m_ref[...] = m_new
    pv = lax.dot_general(v, p.astype(BF16), TN_DIMS, preferred_element_type=F32)
    acc_ref[...] = acc_ref[...] * alpha + pv


def _split_halves(q, rows):
    lane = lax.broadcasted_iota(jnp.int32, q.shape, 1)
    zero = jnp.zeros_like(q)
    return jnp.where(lane < HEAD_DIM, q, zero), jnp.where(lane >= HEAD_DIM, q, zero)


def _diff_attn_kernel(q_ref, k_ref, v_ref, lam_ref, g_ref, o_ref, qm, m_ref, l_ref, acc_ref,
                      *, tq, tk, lambda_init):
    qi = pl.program_id(2)
    lo, hi = _split_halves(q_ref[...], tq)
    qm[0:tq, :] = lo
    qm[tq:2 * tq, :] = hi
    m_ref[...] = jnp.full(m_ref.shape, NEG_INF, F32)
    l_ref[...] = jnp.zeros(l_ref.shape, F32)
    acc_ref[...] = jnp.zeros(acc_ref.shape, F32)
    q0 = qi * tq

    def scores(ki):
        off = pl.multiple_of(ki * tk, tk)
        k = k_ref[pl.ds(off, tk), :]
        v = v_ref[pl.ds(off, tk), :]
        return lax.dot_general(k, qm[...], NT_DIMS, preferred_element_type=F32), v

    def full_tile(ki, carry):
        s, v = scores(ki)
        _softmax_step(s, v, m_ref, l_ref, acc_ref)
        return carry

    n_full = q0 // tk
    lax.fori_loop(0, n_full, full_tile, 0)

    s, v = scores(n_full)
    k_idx = n_full * tk + lax.broadcasted_iota(jnp.int32, s.shape, 0)
    q_idx = q0 + lax.broadcasted_iota(jnp.int32, s.shape, 1) % tq
    _softmax_step(jnp.where(k_idx <= q_idx, s, NEG_INF), v, m_ref, l_ref, acc_ref)

    lam = lam_ref[...]
    lam_full = (jnp.exp(jnp.sum(lam[0:1] * lam[1:2], axis=-1, keepdims=True))
                - jnp.exp(jnp.sum(lam[2:3] * lam[3:4], axis=-1, keepdims=True)) + lambda_init)
    acc = acc_ref[...]
    l = jnp.maximum(l_ref[...], 1e-30)
    ot = acc[:, 0:tq] / l[:, 0:tq] - lam_full * (acc[:, tq:2 * tq] / l[:, tq:2 * tq])
    ms = jnp.mean(ot * ot, axis=0, keepdims=True)
    ot = (ot * lax.rsqrt(ms + NORM_EPS)) * g_ref[...] * (1.0 - lambda_init)
    o_ref[...] = ot.T.astype(o_ref.dtype)


def _diff_attention(qkv, lam, subln_g, batch, seq, lambda_init, *, tq=256, tk=512):
    t, n3 = qkv.shape
    d = n3 // 3
    heads = d // DA_V_DIM
    nq = seq // tq
    cb = d // LANES
    g_b = jnp.broadcast_to(subln_g.astype(F32)[:, None], (DA_V_DIM, tq))
    return pl.pallas_call(
        functools.partial(_diff_attn_kernel, tq=tq, tk=tk, lambda_init=lambda_init),
        grid=(batch, heads, nq),
        in_specs=[
            pl.BlockSpec((tq, LANES), lambda b, h, i: (b * nq + i, h)),
            pl.BlockSpec((seq, LANES), lambda b, h, i: (b, cb + h)),
            pl.BlockSpec((seq, LANES), lambda b, h, i: (b, 2 * cb + h)),
            pl.BlockSpec(lam.shape, lambda b, h, i: (0, 0)),
            pl.BlockSpec((DA_V_DIM, tq), lambda b, h, i: (0, 0)),
        ],
        out_specs=pl.BlockSpec((tq, LANES), lambda b, h, i: (b * nq + i, h)),
        out_shape=jax.ShapeDtypeStruct((t, d), BF16),
        scratch_shapes=[
            pltpu.VMEM((2 * tq, LANES), BF16),
            pltpu.VMEM((1, 2 * tq), F32),
            pltpu.VMEM((1, 2 * tq), F32),
            pltpu.VMEM((DA_V_DIM, 2 * tq), F32),
        ],
        compiler_params=_cparams("arbitrary", "arbitrary", "arbitrary"),
        name="diff_attn",
    )(qkv, qkv, qkv, lam.astype(F32), g_b)


def _route(logits, n_exp):
    lane = lax.broadcasted_iota(jnp.int32, logits.shape, 1)
    lane_f = lane.astype(F32)
    big = float(LANES)
    is_grp = (lane >= n_exp) & (lane < n_exp + MOE_GROUPS)
    gl = jnp.where(is_grp, logits, -jnp.inf)
    ge = jnp.where(is_grp, jnp.exp(gl - jnp.max(gl, axis=-1, keepdims=True)), 0.0)
    pg = ge / jnp.sum(ge, axis=-1, keepdims=True)
    p_top = jnp.max(pg, axis=-1, keepdims=True)
    grp = jnp.min(jnp.where(is_grp & (pg == p_top), lane_f, big), axis=-1, keepdims=True) - n_exp
    in_grp = (lane < n_exp) & ((lane // MOE_EXPERTS_PER_GROUP).astype(F32) == grp)
    sel = jnp.where(in_grp, logits, -jnp.inf)
    v1 = jnp.max(sel, axis=-1, keepdims=True)
    i1 = jnp.min(jnp.where(sel == v1, lane_f, big), axis=-1, keepdims=True)
    sel2 = jnp.where(lane_f == i1, -jnp.inf, sel)
    v2 = jnp.max(sel2, axis=-1, keepdims=True)
    i2 = jnp.min(jnp.where(sel2 == v2, lane_f, big), axis=-1, keepdims=True)
    e2 = jnp.exp(v2 - v1)
    den = 1.0 + e2
    w1 = (1.0 / den) * p_top
    w2 = (e2 / den) * p_top
    return jnp.where(lane_f == i1, w1, 0.0) + jnp.where(lane_f == i2, w2, 0.0)


def _outproj_router_kernel(o_ref, w_ref, x_ref, g1_ref, ng_ref, sc_ref, sh_ref, wr_ref, br_ref,
                           xo_ref, h_ref, comb_ref, *, n_exp):
    mix = jnp.dot(o_ref[...], w_ref[...], preferred_element_type=F32)
    x = x_ref[...] + g1_ref[0] * mix
    xo_ref[...] = x
    h = _rms_modulate(x, ng_ref[...], sc_ref[0], sh_ref[0])
    h_ref[...] = h.astype(BF16)
    logits = jnp.dot(h, wr_ref[...], precision=HIGHEST, preferred_element_type=F32) + br_ref[...]
    comb_ref[...] = _route(logits, n_exp)[:, 0:n_exp]


def _outproj_router(o, w_out, x2d, g1, ng, sc, sh, wr, br, seq, n_exp, *, tm=512):
    t, d = x2d.shape
    per_b = seq // tm
    bvec = lambda i: (i // per_b, 0, 0)
    return pl.pallas_call(
        functools.partial(_outproj_router_kernel, n_exp=n_exp),
        grid=(t // tm,),
        in_specs=[
            pl.BlockSpec((tm, d), lambda i: (i, 0)),
            pl.BlockSpec((d, d), lambda i: (0, 0)),
            pl.BlockSpec((tm, d), lambda i: (i, 0)),
            pl.BlockSpec((1, 1, d), bvec),
            pl.BlockSpec((1, d), lambda i: (0, 0)),
            pl.BlockSpec((1, 1, d), bvec),
            pl.BlockSpec((1, 1, d), bvec),
            pl.BlockSpec((d, LANES), lambda i: (0, 0)),
            pl.BlockSpec((1, LANES), lambda i: (0, 0)),
        ],
        out_specs=[
            pl.BlockSpec((tm, d), lambda i: (i, 0)),
            pl.BlockSpec((tm, d), lambda i: (i, 0)),
            pl.BlockSpec((tm, n_exp), lambda i: (i, 0)),
        ],
        out_shape=[
            jax.ShapeDtypeStruct((t, d), F32),
            jax.ShapeDtypeStruct((t, d), BF16),
            jax.ShapeDtypeStruct((t, n_exp), F32),
        ],
        compiler_params=_cparams("arbitrary"),
        name="outproj_router",
    )(o, w_out, x2d, g1[:, None, :], ng.reshape(1, d), sc[:, None, :], sh[:, None, :], wr, br)


def _moe_kernel(h_ref, wgu_ref, wd_ref, comb_ref, x_ref, g2_ref, fg_ref, o_ref, acc_ref,
                *, hidden, final_norm):
    e = pl.program_id(1)

    @pl.when(e == 0)
    def _():
        acc_ref[...] = jnp.zeros(acc_ref.shape, F32)

    gu = jnp.dot(h_ref[...], wgu_ref[0], preferred_element_type=F32)
    gate, up = gu[:, 0:hidden], gu[:, hidden:2 * hidden]
    a = (gate * jax.nn.sigmoid(gate)) * up
    comb = comb_ref[...]
    lane = lax.broadcasted_iota(jnp.int32, comb.shape, 1)
    ce = jnp.sum(jnp.where(lane == e, comb, 0.0), axis=-1, keepdims=True)
    acc_ref[...] += jnp.dot((a * ce).astype(BF16), wd_ref[0], preferred_element_type=F32)

    @pl.when(e == pl.num_programs(1) - 1)
    def _():
        x = x_ref[...] + g2_ref[0] * acc_ref[...]
        if final_norm:
            ms = jnp.mean(x * x, axis=-1, keepdims=True)
            x = (x * lax.rsqrt(ms + NORM_EPS)) * fg_ref[...]
        o_ref[...] = x


def _moe(h, wgu, wd, comb, x2d, g2, final_g, seq, *, final_norm, tm=1024):
    t, d = x2d.shape
    n_exp, _, two_h = wgu.shape
    hidden = two_h // 2
    per_b = seq // tm
    return pl.pallas_call(
        functools.partial(_moe_kernel, hidden=hidden, final_norm=final_norm),
        grid=(t // tm, n_exp),
        in_specs=[
            pl.BlockSpec((tm, d), lambda i, e: (i, 0)),
            pl.BlockSpec((1, d, two_h), lambda i, e: (e, 0, 0)),
            pl.BlockSpec((1, hidden, d), lambda i, e: (e, 0, 0)),
            pl.BlockSpec((tm, n_exp), lambda i, e: (i, 0)),
            pl.BlockSpec((tm, d), lambda i, e: (i, 0)),
            pl.BlockSpec((1, 1, d), lambda i, e: (i // per_b, 0, 0)),
            pl.BlockSpec((1, d), lambda i, e: (0, 0)),
        ],
        out_specs=pl.BlockSpec((tm, d), lambda i, e: (i, 0)),
        out_shape=jax.ShapeDtypeStruct((t, d), F32),
        scratch_shapes=[pltpu.VMEM((tm, d), F32)],
        compiler_params=_cparams("arbitrary", "arbitrary"),
        name="moe",
    )(h, wgu, wd, comb, x2d, g2[:, None, :], final_g.reshape(1, d))


def _compress_kernel(ch_ref, w1a_ref, w1b_ref, pos_ref, w1f_ref, b1_ref, w2_ref, o_ref):
    ch = ch_ref[0, 0, 0]
    n = ch.shape[0]
    a = jnp.dot(ch, w1a_ref[0], preferred_element_type=F32)
    b = jnp.dot(ch, w1b_ref[0], preferred_element_type=F32)
    const = jnp.dot(pos_ref[0], w1f_ref[0], precision=HIGHEST, preferred_element_type=F32)[0:1]
    hid = a + pltpu.roll(b, n - 1, 0) + (const + b1_ref[0])
    act = jax.nn.gelu(hid, approximate=True)
    o_ref[0, 0, 0] = jnp.dot(act.astype(BF16), w2_ref[0], preferred_element_type=F32).astype(o_ref.dtype)


def _nsa_compress(ch, cmp_pos, cmp_w1, cmp_b1, cmp_w2):
    b, two, g, nc, f = ch.shape
    hid = cmp_w1.shape[-1]
    w1a = cmp_w1[:, :f, :].astype(BF16)
    w1b = cmp_w1[:, f:, :].astype(BF16)
    pos = jnp.zeros((two, 8, 2 * f), F32).at[:, 0, :].set(cmp_pos.reshape(two, 2 * f))
    w2 = jnp.concatenate([cmp_w2, cmp_w2], axis=-1).astype(BF16)
    return pl.pallas_call(
        _compress_kernel,
        grid=(b, two, g),
        in_specs=[
            pl.BlockSpec((1, 1, 1, nc, f), lambda i, j, k: (i, j, k, 0, 0)),
            pl.BlockSpec((1, f, hid), lambda i, j, k: (j, 0, 0)),
            pl.BlockSpec((1, f, hid), lambda i, j, k: (j, 0, 0)),
            pl.BlockSpec((1, 8, 2 * f), lambda i, j, k: (j, 0, 0)),
            pl.BlockSpec((1, 2 * f, hid), lambda i, j, k: (j, 0, 0)),
            pl.BlockSpec((1, 1, hid), lambda i, j, k: (j, 0, 0)),
            pl.BlockSpec((1, hid, LANES), lambda i, j, k: (j, 0, 0)),
        ],
        out_specs=pl.BlockSpec((1, 1, 1, nc, LANES), lambda i, j, k: (i, j, k, 0, 0)),
        out_shape=jax.ShapeDtypeStruct((b, two, g, nc, LANES), BF16),
        compiler_params=_cparams("arbitrary", "arbitrary", "arbitrary"),
        name="nsa_compress",
    )(ch, w1a, w1b, pos, cmp_w1.astype(F32), cmp_b1[:, None, :].astype(F32), w2)


def _stack_group_queries(q_ref, qm, tq):
    for j in range(q_ref.shape[1] // LANES):
        lo, hi = _split_halves(q_ref[:, j * LANES:(j + 1) * LANES], tq)
        qm[(2 * j) * tq:(2 * j + 1) * tq, :] = lo
        qm[(2 * j + 1) * tq:(2 * j + 2) * tq, :] = hi


def _gated_pairs_to_token_major(ot, gt_ref, branch, tq, hpg):
    g = pl.program_id(1)
    tiles = []
    for j in range(hpg // 2):
        parts = []
        for h in (2 * j, 2 * j + 1):
            row = gt_ref[pl.ds((g * hpg + h) * 3 + branch, 1), :]
            parts.append(ot[:, h * tq:(h + 1) * tq] * row)
        tiles.append(jnp.concatenate(parts, axis=0).T)
    return tiles


def _nsa_cmp_kernel(q_ref, kc_ref, vc_ref, mt_ref, gt_ref, o_ref, mask_ref, qm, *, tq, hpg, n_top):
    qi = pl.program_id(2)
    q0 = qi * tq
    _stack_group_queries(q_ref, qm, tq)
    kc = kc_ref[0, 0, 0]
    s = lax.dot_general(kc, qm[...], NT_DIMS, preferred_element_type=F32)
    n_idx = lax.broadcasted_iota(jnp.int32, s.shape, 0)
    q_idx = q0 + lax.broadcasted_iota(jnp.int32, s.shape, 1) % tq
    valid = n_idx * NSA_CMP_STRIDE + (NSA_CMP_BLOCK - 1) <= q_idx
    s = jnp.where(valid, s, NEG_INF)
    m = jnp.max(s, axis=0, keepdims=True)
    e = jnp.where(valid, jnp.exp2(s - m), 0.0)
    r = 1.0 / jnp.maximum(jnp.sum(e, axis=0, keepdims=True), 1e-30)
    p = e * r
    ot = lax.dot_general(vc_ref[0, 0, 0], p.astype(BF16), TN_DIMS, preferred_element_type=F32)
    tiles = _gated_pairs_to_token_major(ot[0:HEAD_DIM], gt_ref, 0, tq, hpg)
    for j, tile in enumerate(tiles):
        o_ref[:, j * LANES:(j + 1) * LANES] = tile.astype(o_ref.dtype)

    psum = p[:, 0:tq]
    for h in range(1, hpg):
        psum = psum + p[:, h * tq:(h + 1) * tq]
    p_hi = psum.astype(BF16)
    p_lo = (psum - p_hi.astype(F32)).astype(BF16)
    mt = mt_ref[...]
    imp = (jnp.dot(mt, p_hi, preferred_element_type=F32)
           + jnp.dot(mt, p_lo, preferred_element_type=F32))
    n_sel = imp.shape[0]
    j_idx = lax.broadcasted_iota(jnp.int32, imp.shape, 0)
    j_f = j_idx.astype(F32)
    q_blk = (q0 + lax.broadcasted_iota(jnp.int32, imp.shape, 1)) // NSA_SEL_BLOCK
    forced = (j_idx == 0) | (j_idx == q_blk) | (j_idx == q_blk - 1)
    val = jnp.where(forced, NSA_FORCE, jnp.where(j_idx <= q_blk, imp, -NSA_FORCE))
    sel = jnp.zeros(imp.shape, F32)
    for _ in range(n_top):
        mx = jnp.max(val, axis=0, keepdims=True)
        first = jnp.min(jnp.where(val == mx, j_f, float(n_sel)), axis=0, keepdims=True)
        hit = j_f == first
        sel = jnp.where(hit, 1.0, sel)
        val = jnp.where(hit, -jnp.inf, val)
    mask_ref[0, 0] = sel


def _nsa_win_kernel(q_ref, k_ref, v_ref, gt_ref, o_ref, qm, acc_ref, *, tq, hpg, span):
    g = pl.program_id(1)
    qi = pl.program_id(2)
    q0 = qi * tq
    _stack_group_queries(q_ref, qm, tq)
    start = pl.multiple_of(jnp.maximum(q0 - NSA_WINDOW, 0), tq)
    k = k_ref[pl.ds(start, span), :]
    v = v_ref[pl.ds(start, span), :]
    s = lax.dot_general(k, qm[...], NT_DIMS, preferred_element_type=F32)
    k_idx = start + lax.broadcasted_iota(jnp.int32, s.shape, 0)
    q_idx = q0 + lax.broadcasted_iota(jnp.int32, s.shape, 1) % tq
    dist = q_idx - k_idx
    valid = (dist >= 0) & (dist < NSA_WINDOW)
    s = jnp.where(valid, s, NEG_INF)
    m = jnp.max(s, axis=0, keepdims=True)
    e = jnp.exp2(s - m)
    r = 1.0 / jnp.maximum(jnp.sum(e, axis=0, keepdims=True), 1e-30)
    acc_ref[...] = lax.dot_general(v, e.astype(BF16), TN_DIMS, preferred_element_type=F32) * r
    ot = acc_ref[pl.ds(pl.multiple_of(g * HEAD_DIM, HEAD_DIM), HEAD_DIM), :]
    tiles = _gated_pairs_to_token_major(ot, gt_ref, 2, tq, hpg)
    for j, tile in enumerate(tiles):
        o_ref[:, j * LANES:(j + 1) * LANES] = tile.astype(o_ref.dtype)


def _nsa_sel_kernel(q_ref, k_ref, v_ref, mask_ref, gt_ref, oc_ref, ow_ref, o_ref,
                    qm, m_ref, l_ref, acc_ref, *, tq, tk, hpg):
    g = pl.program_id(1)
    qi = pl.program_id(2)
    q0 = qi * tq
    _stack_group_queries(q_ref, qm, tq)
    m_ref[...] = jnp.full(m_ref.shape, NEG_INF, F32)
    l_ref[...] = jnp.zeros(l_ref.shape, F32)
    acc_ref[...] = jnp.zeros(acc_ref.shape, F32)
    blocks = tk // NSA_SEL_BLOCK

    def masked_scores(ki):
        off = pl.multiple_of(ki * tk, tk)
        k = k_ref[pl.ds(off, tk), :]
        v = v_ref[pl.ds(off, tk), :]
        s = lax.dot_general(k, qm[...], NT_DIMS, preferred_element_type=F32)
        rows = mask_ref[0, 0, pl.ds(pl.multiple_of(ki * blocks, blocks), blocks), :]
        exp_rows = [jnp.broadcast_to(rows[r:r + 1, :], (NSA_SEL_BLOCK, tq)) for r in range(blocks)]
        sel = jnp.concatenate(exp_rows, axis=0)
        sel = jnp.concatenate([sel] * hpg, axis=1)
        return s, v, sel > 0.0

    def full_tile(ki, carry):
        s, v, ok = masked_scores(ki)
        _softmax_step(jnp.where(ok, s, NEG_INF), v, m_ref, l_ref, acc_ref)
        return carry

    n_full = q0 // tk
    lax.fori_loop(0, n_full, full_tile, 0)

    s, v, ok = masked_scores(n_full)
    k_idx = n_full * tk + lax.broadcasted_iota(jnp.int32, s.shape, 0)
    q_idx = q0 + lax.broadcasted_iota(jnp.int32, s.shape, 1) % tq
    _softmax_step(jnp.where(ok & (k_idx <= q_idx), s, NEG_INF), v, m_ref, l_ref, acc_ref)

    inv_l = 1.0 / jnp.maximum(l_ref[...], 1e-30)
    ot = acc_ref[pl.ds(pl.multiple_of(g * HEAD_DIM, HEAD_DIM), HEAD_DIM), :] * inv_l
    tiles = _gated_pairs_to_token_major(ot, gt_ref, 1, tq, hpg)
    for j, tile in enumerate(tiles):
        sl = slice(j * LANES, (j + 1) * LANES)
        total = tile + oc_ref[:, sl].astype(F32) + ow_ref[:, sl].astype(F32)
        o_ref[:, sl] = total.astype(o_ref.dtype)


def _nsa_attention(proj, gates_t, cmp_kv, batch, seq, d_model, *, tq=128, tk=512):
    t = proj.shape[0]
    groups = NSA_KV_GROUPS
    hpg = d_model // HEAD_DIM // groups
    gw = hpg * HEAD_DIM
    nq = seq // tq
    nc = cmp_kv.shape[3]
    n_sel = seq // NSA_SEL_BLOCK
    n_top = min(NSA_TOP_N, n_sel)
    ng = gates_t.shape[0]
    span = NSA_WINDOW + tq
    qcb = d_model // LANES

    ci = jnp.arange(nc)[None, :] * NSA_CMP_STRIDE
    sj = jnp.arange(n_sel)[:, None] * NSA_SEL_BLOCK
    mt = ((ci < sj + NSA_SEL_BLOCK) & (ci + NSA_CMP_BLOCK > sj)
          & (jnp.arange(nc)[None, :] < nc - 1)).astype(BF16)

    q_spec = pl.BlockSpec((tq, gw), lambda b, g, i: (b * nq + i, g))
    gt_spec = pl.BlockSpec((ng, tq), lambda b, g, i: (0, b * nq + i))
    o_spec = pl.BlockSpec((tq, gw), lambda b, g, i: (b * nq + i, g))
    o_shape = jax.ShapeDtypeStruct((t, d_model), BF16)
    sem = ("arbitrary", "arbitrary", "arbitrary")
    qm_scr = pltpu.VMEM((hpg * tq, LANES), BF16)

    o_cmp, sel_mask = pl.pallas_call(
        functools.partial(_nsa_cmp_kernel, tq=tq, hpg=hpg, n_top=n_top),
        grid=(batch, groups, nq),
        in_specs=[
            q_spec,
            pl.BlockSpec((1, 1, 1, nc, LANES), lambda b, g, i: (b, 0, g, 0, 0)),
            pl.BlockSpec((1, 1, 1, nc, LANES), lambda b, g, i: (b, 1, g, 0, 0)),
            pl.BlockSpec((n_sel, nc), lambda b, g, i: (0, 0)),
            gt_spec,
        ],
        out_specs=[o_spec, pl.BlockSpec((1, 1, n_sel, tq), lambda b, g, i: (b, g, 0, i))],
        out_shape=[o_shape, jax.ShapeDtypeStruct((batch, groups, n_sel, seq), F32)],
        scratch_shapes=[qm_scr],
        compiler_params=_cparams(*sem),
        name="nsa_cmp",
    )(proj, cmp_kv, cmp_kv, mt, gates_t)

    o_win = pl.pallas_call(
        functools.partial(_nsa_win_kernel, tq=tq, hpg=hpg, span=span),
        grid=(batch, groups, nq),
        in_specs=[
            q_spec,
            pl.BlockSpec((seq, LANES), lambda b, g, i: (b, qcb + 5 + g)),
            pl.BlockSpec((seq, LANES), lambda b, g, i: (b, qcb + 7)),
            gt_spec,
        ],
        out_specs=o_spec,
        out_shape=o_shape,
        scratch_shapes=[qm_scr, pltpu.VMEM((LANES, hpg * tq), F32)],
        compiler_params=_cparams(*sem),
        name="nsa_win",
    )(proj, proj, proj, gates_t)

    return pl.pallas_call(
        functools.partial(_nsa_sel_kernel, tq=tq, tk=tk, hpg=hpg),
        grid=(batch, groups, nq),
        in_specs=[
            q_spec,
            pl.BlockSpec((seq, LANES), lambda b, g, i: (b, qcb + 2 + g)),
            pl.BlockSpec((seq, LANES), lambda b, g, i: (b, qcb + 4)),
            pl.BlockSpec((1, 1, n_sel, tq), lambda b, g, i: (b, g, 0, i)),
            gt_spec,
            o_spec,
            o_spec,
        ],
        out_specs=o_spec,
        out_shape=o_shape,
        scratch_shapes=[
            qm_scr,
            pltpu.VMEM((1, hpg * tq), F32),
            pltpu.VMEM((1, hpg * tq), F32),
            pltpu.VMEM((LANES, hpg * tq), F32),
        ],
        compiler_params=_cparams(*sem),
        name="nsa_sel",
    )(proj, proj, proj, sel_mask, gates_t, o_cmp, o_win)


def _diff_in_weights(w_in, d_model):
    scale = HEAD_DIM ** -0.5 * LOG2E
    wq = w_in[:, :d_model] * scale
    return jnp.concatenate([wq, w_in[:, d_model:]], axis=1).astype(BF16)


def _nsa_in_weights(w_in, d_model):
    g, hd = NSA_KV_GROUPS, HEAD_DIM
    kvw = g * hd
    scale = hd ** -0.5 * LOG2E
    q = w_in[:, :d_model] * scale
    off = d_model
    sec = [w_in[:, off + i * kvw: off + (i + 1) * kvw] for i in range(6)]
    dup = lambda w: jnp.concatenate([w[:, i * hd:(i + 1) * hd] for i in range(g) for _ in range(2)], axis=1)
    w_main = jnp.concatenate([q, sec[0], sec[1], dup(sec[2]), sec[3], dup(sec[4]), sec[5]], axis=1)
    w_gates = w_in[:, off + 6 * kvw:].T
    return w_main.astype(BF16), w_gates.astype(BF16)


def _router_weights(w_grp, b_grp, w_exp, b_exp):
    d = w_grp.shape[0]
    n_exp = w_exp.shape[0] * w_exp.shape[2]
    we = jnp.transpose(w_exp, (1, 0, 2)).reshape(d, n_exp)
    pad = LANES - n_exp - w_grp.shape[1]
    wr = jnp.concatenate([we, w_grp, jnp.zeros((d, pad), F32)], axis=1).astype(F32)
    br = jnp.concatenate([b_exp.reshape(-1), b_grp, jnp.zeros((pad,), F32)])[None, :].astype(F32)
    return wr, br, n_exp


def kernel(x, c, positions, ada_w, ada_b, norm_g, final_g, diff_w_in, diff_w_out, diff_lambda,
           diff_subln_g, nsa_w_in, nsa_w_out, nsa_cmp_pos, nsa_cmp_w1, nsa_cmp_b1, nsa_cmp_w2,
           moe_w_group, moe_b_group, moe_w_expert, moe_b_expert, moe_w_gate, moe_w_up, moe_w_down):
    batch, seq, d = x.shape
    depth = ada_w.shape[0]
    x2d = x.reshape(batch * seq, d)
    tables = _rope_tables(positions)
    mod = _adaln_mod(c, ada_w, ada_b)

    for i in range(depth):
        sh1, sc1, g1, sh2, sc2, g2 = jnp.split(mod[i], 6, axis=-1)
        j = i // N_MIXERS
        if i % N_MIXERS == 0:
            lambda_init = 0.8 - 0.6 * math.exp(-0.3 * i)
            qkv = _norm_proj(x2d, norm_g[i, 0], sc1, sh1, _diff_in_weights(diff_w_in[j], d), tables,
                             seq, tn=512, plain_tiles=range(2 * d // 512, 3 * d // 512))
            o = _diff_attention(qkv, diff_lambda[j], diff_subln_g[j], batch, seq, lambda_init)
            w_out = diff_w_out[j]
        else:
            w_main, w_gates = _nsa_in_weights(nsa_w_in[j], d)
            proj, gates_t = _norm_proj(x2d, norm_g[i, 0], sc1, sh1, w_main, tables, seq,
                                       tn=256, plain_tiles=(d // 256,), w_gates=w_gates)
            groups = NSA_KV_GROUPS
            kcvc = proj[:, d:d + 2 * groups * HEAD_DIM]
            ch = kcvc.reshape(batch, seq, 2, groups, HEAD_DIM).transpose(0, 2, 3, 1, 4)
            ch = ch.reshape(batch, 2, groups, seq // NSA_CMP_STRIDE, NSA_CMP_STRIDE * HEAD_DIM)
            cmp_kv = _nsa_compress(ch, nsa_cmp_pos[j], nsa_cmp_w1[j], nsa_cmp_b1[j], nsa_cmp_w2[j])
            o = _nsa_attention(proj, gates_t, cmp_kv, batch, seq, d)
            w_out = nsa_w_out[j]
        wr, br, n_exp = _router_weights(moe_w_group[i], moe_b_group[i], moe_w_expert[i], moe_b_expert[i])
        x2d, h2, comb = _outproj_router(o, w_out.astype(BF16), x2d, g1, norm_g[i, 1], sc2, sh2,
                                        wr, br, seq, n_exp)
        wgu = jnp.concatenate([moe_w_gate[i], moe_w_up[i]], axis=-1).astype(BF16)
        x2d = _moe(h2, wgu, moe_w_down[i].astype(BF16), comb, x2d, g2, final_g, seq,
                   final_norm=(i == depth - 1))
    return x2d.reshape(batch, seq, d)
```

```python
import functools
import math

import jax
import jax.numpy as jnp
from jax import lax
from jax.experimental import pallas as pl
from jax.experimental.pallas import tpu as pltpu

F32 = jnp.float32
BF16 = jnp.bfloat16
HIGHEST = lax.Precision.HIGHEST

ROPE_THETA = 500000.0
ROT_DIM = 16
NORM_EPS = 1e-6
NEG_INF = -1e30
HEAD_DIM = 64
DA_V_DIM = 2 * HEAD_DIM
NSA_KV_GROUPS = 2
NSA_CMP_BLOCK = 32
NSA_CMP_STRIDE = 16
NSA_SEL_BLOCK = 64
NSA_TOP_N = 16
NSA_WINDOW = 512
NSA_FORCE = 1e4
MOE_GROUPS = 4
MOE_EXPERTS_PER_GROUP = 8
MOE_TOP_K = 2
N_MIXERS = 2
LOG2E = 1.4426950408889634

LANES = 128
VMEM_LIMIT_BYTES = 56 * 1024 * 1024

NT_DIMS = (((1,), (1,)), ((), ()))
TN_DIMS = (((0,), (0,)), ((), ()))


def _cparams(*sem):
    return pltpu.CompilerParams(dimension_semantics=sem, vmem_limit_bytes=VMEM_LIMIT_BYTES)


def _mod_kernel(c_ref, w_ref, b_ref, o_ref):
    c = c_ref[...]
    ca = c * jax.nn.sigmoid(c)
    o_ref[0] = jnp.dot(ca, w_ref[0], precision=HIGHEST, preferred_element_type=F32) + b_ref[0]


def _adaln_mod(c, ada_w, ada_b):
    depth, d, n = ada_w.shape
    b = c.shape[0]
    rows = 8
    tn = 1536
    c_pad = jnp.zeros((rows, d), F32).at[:b].set(c)
    out = pl.pallas_call(
        _mod_kernel,
        grid=(depth, n // tn),
        in_specs=[
            pl.BlockSpec((rows, d), lambda i, j: (0, 0)),
            pl.BlockSpec((1, d, tn), lambda i, j: (i, 0, j)),
            pl.BlockSpec((1, 1, tn), lambda i, j: (i, 0, j)),
        ],
        out_specs=pl.BlockSpec((1, rows, tn), lambda i, j: (i, 0, j)),
        out_shape=jax.ShapeDtypeStruct((depth, rows, n), F32),
        compiler_params=_cparams("arbitrary", "arbitrary"),
        name="adaln_mod",
    )(c_pad, ada_w, ada_b.reshape(depth, 1, n))
    return out[:, :b, :]


def _rope_tables(positions):
    half = ROT_DIM // 2
    inv_freq = ROPE_THETA ** (-jnp.arange(0, ROT_DIM, 2, dtype=F32) / ROT_DIM)
    ang = positions.astype(F32).reshape(-1)[:, None] * inv_freq
    cos, sin = jnp.cos(ang), jnp.sin(ang)
    t = ang.shape[0]
    rest = HEAD_DIM - ROT_DIM
    z8 = jnp.zeros((t, half), F32)
    cos_h = jnp.concatenate([cos, cos, jnp.ones((t, rest), F32)], axis=-1)
    sina_h = jnp.concatenate([-sin, z8, jnp.zeros((t, rest), F32)], axis=-1)
    sinb_h = jnp.concatenate([z8, sin, jnp.zeros((t, rest), F32)], axis=-1)
    rep = LANES // HEAD_DIM
    return tuple(jnp.tile(a, (1, rep)) for a in (cos_h, sina_h, sinb_h))


def _rms_modulate(x, g, sc, sh):
    ms = jnp.mean(x * x, axis=-1, keepdims=True)
    y = x * lax.rsqrt(ms + NORM_EPS)
    return (y * g) * (1.0 + sc) + sh


def _proj_kernel(*refs, plain_tiles, with_gates):
    if with_gates:
        (x_ref, g_ref, sc_ref, sh_ref, w_ref, cos_ref, sa_ref, sb_ref, wg_ref,
         o_ref, gt_ref, h_scr) = refs
    else:
        x_ref, g_ref, sc_ref, sh_ref, w_ref, cos_ref, sa_ref, sb_ref, o_ref, h_scr = refs
    j = pl.program_id(1)
    tn = o_ref.shape[1]

    @pl.when(j == 0)
    def _():
        h = _rms_modulate(x_ref[...], g_ref[...], sc_ref[0], sh_ref[0])
        h_scr[...] = h.astype(BF16)
        if with_gates:
            logits = lax.dot_general(wg_ref[...], h_scr[...], NT_DIMS, preferred_element_type=F32)
            gt_ref[...] = jax.nn.sigmoid(logits)

    acc = jnp.dot(h_scr[...], w_ref[...], preferred_element_type=F32)

    is_plain = functools.reduce(jnp.logical_or, [j == t for t in plain_tiles], j < 0)

    @pl.when(is_plain)
    def _():
        o_ref[...] = acc.astype(o_ref.dtype)

    @pl.when(jnp.logical_not(is_plain))
    def _():
        shift_up = pltpu.roll(acc, tn - ROT_DIM // 2, 1)
        shift_dn = pltpu.roll(acc, ROT_DIM // 2, 1)
        cos, sa, sb = cos_ref[...], sa_ref[...], sb_ref[...]
        for c in range(tn // LANES):
            sl = slice(c * LANES, (c + 1) * LANES)
            y = acc[:, sl] * cos + shift_up[:, sl] * sa + shift_dn[:, sl] * sb
            o_ref[:, sl] = y.astype(o_ref.dtype)


def _norm_proj(x2d, g, sc, sh, w, tables, seq, *, tn, plain_tiles, w_gates=None, tm=1024):
    t, d = x2d.shape
    n = w.shape[1]
    with_gates = w_gates is not None
    per_b = seq // tm
    in_specs = [
        pl.BlockSpec((tm, d), lambda i, j: (i, 0)),
        pl.BlockSpec((1, d), lambda i, j: (0, 0)),
        pl.BlockSpec((1, 1, d), lambda i, j: (i // per_b, 0, 0)),
        pl.BlockSpec((1, 1, d), lambda i, j: (i // per_b, 0, 0)),
        pl.BlockSpec((d, tn), lambda i, j: (0, j)),
        pl.BlockSpec((tm, LANES), lambda i, j: (i, 0)),
        pl.BlockSpec((tm, LANES), lambda i, j: (i, 0)),
        pl.BlockSpec((tm, LANES), lambda i, j: (i, 0)),
    ]
    args = [x2d, g.reshape(1, d), sc[:, None, :], sh[:, None, :], w, *tables]
    out_specs = [pl.BlockSpec((tm, tn), lambda i, j: (i, j))]
    out_shape = [jax.ShapeDtypeStruct((t, n), BF16)]
    if with_gates:
        ng = w_gates.shape[0]
        in_specs.append(pl.BlockSpec((ng, d), lambda i, j: (0, 0)))
        args.append(w_gates)
        out_specs.append(pl.BlockSpec((ng, tm), lambda i, j: (0, i)))
        out_shape.append(jax.ShapeDtypeStruct((ng, t), F32))
    res = pl.pallas_call(
        functools.partial(_proj_kernel, plain_tiles=tuple(plain_tiles), with_gates=with_gates),
        grid=(t // tm, n // tn),
        in_specs=in_specs,
        out_specs=out_specs,
        out_shape=out_shape,
        scratch_shapes=[pltpu.VMEM((tm, d), BF16)],
        compiler_params=_cparams("arbitrary", "arbitrary"),
        name="norm_proj",
    )(*args)
    return res if with_gates else res[0]


def _attn_scratch(tk, nq, dv):
    return [
        pltpu.VMEM((2, tk, nq), F32),
        pltpu.VMEM((2, 1, nq), F32),
        pltpu.VMEM((2, tk, nq), BF16),
        pltpu.VMEM((2, 1, nq), F32),
        pltpu.VMEM((1, nq), F32),
        pltpu.VMEM((1, nq), F32),
        pltpu.VMEM((dv, nq), F32),
    ]


def _pipelined_attention(n_full, scores_full, scores_diag, v_tile, scratch):
    s_scr, mx_scr, p_scr, al_scr, m_ref, l_ref, acc_ref = scratch

    def stage_scores(fn, tile, slot):
        s = fn(tile)
        s_scr[slot] = s
        mx_scr[slot] = jnp.max(s, axis=0, keepdims=True)

    def stage_softmax(slot):
        m_prev = m_ref[...]
        m_new = jnp.maximum(m_prev, mx_scr[slot])
        alpha = jnp.exp2(m_prev - m_new)
        p = jnp.exp2(s_scr[slot] - m_new)
        l_ref[...] = alpha * l_ref[...] + jnp.sum(p, axis=0, keepdims=True)
        m_ref[...] = m_new
        al_scr[slot] = alpha
        p_scr[slot] = p.astype(BF16)

    def stage_pv(pos, slot):
        tile = jnp.where(pos == 0, n_full, jnp.maximum(pos - 1, 0))
        pv = lax.dot_general(v_tile(tile), p_scr[slot], TN_DIMS, preferred_element_type=F32)
        acc_ref[...] = acc_ref[...] * al_scr[slot] + pv

    def step(j, slot):
        stage_scores(scores_full, j, 1 - slot)
        stage_softmax(slot)
        stage_pv(j - 1, 1 - slot)

    m_ref[...] = jnp.full(m_ref.shape, NEG_INF, F32)
    l_ref[...] = jnp.zeros(l_ref.shape, F32)
    acc_ref[...] = jnp.zeros(acc_ref.shape, F32)
    p_scr[1] = jnp.zeros(p_scr.shape[1:], BF16)
    al_scr[1] = jnp.ones(al_scr.shape[1:], F32)
    stage_scores(scores_diag, n_full, 0)

    def pair(jj, carry):
        step(2 * jj, 0)
        step(2 * jj + 1, 1)
        return carry

    lax.fori_loop(0, n_full // 2, pair, 0)
    odd = n_full % 2 == 1

    @pl.when(odd)
    def _():
        step(n_full - 1, 0)
        stage_softmax(1)
        stage_pv(n_full - 1, 0)
        stage_pv(n_full, 1)

    @pl.when(jnp.logical_not(odd))
    def _():
        stage_softmax(0)
        stage_pv(n_full - 1, 1)
        stage_pv(n_full, 0)


def _split_halves(q, rows):
    lane = lax.broadcasted_iota(jnp.int32, q.shape, 1)
    zero = jnp.zeros_like(q)
    return jnp.where(lane < HEAD_DIM, q, zero), jnp.where(lane >= HEAD_DIM, q, zero)


def _diff_attn_kernel(q_ref, k_ref, v_ref, lam_ref, g_ref, o_ref, qm, *scratch,
                      tq, tk, lambda_init):
    qi = pl.program_id(2)
    lo, hi = _split_halves(q_ref[...], tq)
    qm[0:tq, :] = lo
    qm[tq:2 * tq, :] = hi
    q0 = qi * tq
    acc_ref, l_ref = scratch[-1], scratch[-2]

    def scores_full(t):
        k = k_ref[pl.ds(pl.multiple_of(t * tk, tk), tk), :]
        return lax.dot_general(k, qm[...], NT_DIMS, preferred_element_type=F32)

    def scores_diag(t):
        s = scores_full(t)
        k_idx = t * tk + lax.broadcasted_iota(jnp.int32, s.shape, 0)
        q_idx = q0 + lax.broadcasted_iota(jnp.int32, s.shape, 1) % tq
        return jnp.where(k_idx <= q_idx, s, NEG_INF)

    def v_tile(t):
        return v_ref[pl.ds(pl.multiple_of(t * tk, tk), tk), :]

    _pipelined_attention(q0 // tk, scores_full, scores_diag, v_tile, scratch)

    lam = lam_ref[...]
    lam_full = (jnp.exp(jnp.sum(lam[0:1] * lam[1:2], axis=-1, keepdims=True))
                - jnp.exp(jnp.sum(lam[2:3] * lam[3:4], axis=-1, keepdims=True)) + lambda_init)
    acc = acc_ref[...]
    l = jnp.maximum(l_ref[...], 1e-30)
    ot = acc[:, 0:tq] / l[:, 0:tq] - lam_full * (acc[:, tq:2 * tq] / l[:, tq:2 * tq])
    ms = jnp.mean(ot * ot, axis=0, keepdims=True)
    ot = (ot * lax.rsqrt(ms + NORM_EPS)) * g_ref[...] * (1.0 - lambda_init)
    o_ref[...] = ot.T.astype(o_ref.dtype)


def _diff_attention(qkv, lam, subln_g, batch, seq, lambda_init, *, tq=1024, tk=1024):
    assert tk % tq == 0 and seq % tk == 0
    t, n3 = qkv.shape
    d = n3 // 3
    heads = d // DA_V_DIM
    nq = seq // tq
    cb = d // LANES
    g_b = jnp.broadcast_to(subln_g.astype(F32)[:, None], (DA_V_DIM, tq))
    return pl.pallas_call(
        functools.partial(_diff_attn_kernel, tq=tq, tk=tk, lambda_init=lambda_init),
        grid=(batch, heads, nq),
        in_specs=[
            pl.BlockSpec((tq, LANES), lambda b, h, i: (b * nq + i, h)),
            pl.BlockSpec((seq, LANES), lambda b, h, i: (b, cb + h)),
            pl.BlockSpec((seq, LANES), lambda b, h, i: (b, 2 * cb + h)),
            pl.BlockSpec(lam.shape, lambda b, h, i: (0, 0)),
            pl.BlockSpec((DA_V_DIM, tq), lambda b, h, i: (0, 0)),
        ],
        out_specs=pl.BlockSpec((tq, LANES), lambda b, h, i: (b * nq + i, h)),
        out_shape=jax.ShapeDtypeStruct((t, d), BF16),
        scratch_shapes=[pltpu.VMEM((2 * tq, LANES), BF16)] + _attn_scratch(tk, 2 * tq, DA_V_DIM),
        compiler_params=_cparams("arbitrary", "arbitrary", "arbitrary"),
        name="diff_attn",
    )(qkv, qkv, qkv, lam.astype(F32), g_b)


def _route(logits, n_exp):
    lane = lax.broadcasted_iota(jnp.int32, logits.shape, 1)
    lane_f = lane.astype(F32)
    big = float(LANES)
    is_grp = (lane >= n_exp) & (lane < n_exp + MOE_GROUPS)
    gl = jnp.where(is_grp, logits, -jnp.inf)
    ge = jnp.where(is_grp, jnp.exp(gl - jnp.max(gl, axis=-1, keepdims=True)), 0.0)
    pg = ge / jnp.sum(ge, axis=-1, keepdims=True)
    p_top = jnp.max(pg, axis=-1, keepdims=True)
    grp = jnp.min(jnp.where(is_grp & (pg == p_top), lane_f, big), axis=-1, keepdims=True) - n_exp
    in_grp = (lane < n_exp) & ((lane // MOE_EXPERTS_PER_GROUP).astype(F32) == grp)
    sel = jnp.where(in_grp, logits, -jnp.inf)
    v1 = jnp.max(sel, axis=-1, keepdims=True)
    i1 = jnp.min(jnp.where(sel == v1, lane_f, big), axis=-1, keepdims=True)
    sel2 = jnp.where(lane_f == i1, -jnp.inf, sel)
    v2 = jnp.max(sel2, axis=-1, keepdims=True)
    i2 = jnp.min(jnp.where(sel2 == v2, lane_f, big), axis=-1, keepdims=True)
    e2 = jnp.exp(v2 - v1)
    den = 1.0 + e2
    w1 = (1.0 / den) * p_top
    w2 = (e2 / den) * p_top
    return jnp.where(lane_f == i1, w1, 0.0) + jnp.where(lane_f == i2, w2, 0.0)


def _outproj_router_kernel(o_ref, w_ref, x_ref, g1_ref, ng_ref, sc_ref, sh_ref, wr_ref, br_ref,
                           xo_ref, h_ref, comb_ref, *, n_exp):
    mix = jnp.dot(o_ref[...], w_ref[...], preferred_element_type=F32)
    x = x_ref[...] + g1_ref[0] * mix
    xo_ref[...] = x
    h = _rms_modulate(x, ng_ref[...], sc_ref[0], sh_ref[0])
    h_ref[...] = h.astype(BF16)
    logits = jnp.dot(h, wr_ref[...], precision=HIGHEST, preferred_element_type=F32) + br_ref[...]
    comb_ref[...] = _route(logits, n_exp)[:, 0:n_exp]


def _outproj_router(o, w_out, x2d, g1, ng, sc, sh, wr, br, seq, n_exp, *, tm=512):
    t, d = x2d.shape
    per_b = seq // tm
    bvec = lambda i: (i // per_b, 0, 0)
    return pl.pallas_call(
        functools.partial(_outproj_router_kernel, n_exp=n_exp),
        grid=(t // tm,),
        in_specs=[
            pl.BlockSpec((tm, d), lambda i: (i, 0)),
            pl.BlockSpec((d, d), lambda i: (0, 0)),
            pl.BlockSpec((tm, d), lambda i: (i, 0)),
            pl.BlockSpec((1, 1, d), bvec),
            pl.BlockSpec((1, d), lambda i: (0, 0)),
            pl.BlockSpec((1, 1, d), bvec),
            pl.BlockSpec((1, 1, d), bvec),
            pl.BlockSpec((d, LANES), lambda i: (0, 0)),
            pl.BlockSpec((1, LANES), lambda i: (0, 0)),
        ],
        out_specs=[
            pl.BlockSpec((tm, d), lambda i: (i, 0)),
            pl.BlockSpec((tm, d), lambda i: (i, 0)),
            pl.BlockSpec((tm, n_exp), lambda i: (i, 0)),
        ],
        out_shape=[
            jax.ShapeDtypeStruct((t, d), F32),
            jax.ShapeDtypeStruct((t, d), BF16),
            jax.ShapeDtypeStruct((t, n_exp), F32),
        ],
        compiler_params=_cparams("arbitrary"),
        name="outproj_router",
    )(o, w_out, x2d, g1[:, None, :], ng.reshape(1, d), sc[:, None, :], sh[:, None, :], wr, br)


def _moe_kernel(h_ref, wgu_ref, wd_ref, comb_ref, x_ref, g2_ref, fg_ref, o_ref, acc_ref,
                *, hidden, final_norm):
    e = pl.program_id(1)

    @pl.when(e == 0)
    def _():
        acc_ref[...] = jnp.zeros(acc_ref.shape, F32)

    gu = jnp.dot(h_ref[...], wgu_ref[0], preferred_element_type=F32)
    gate, up = gu[:, 0:hidden], gu[:, hidden:2 * hidden]
    a = (gate * jax.nn.sigmoid(gate)) * up
    comb = comb_ref[...]
    lane = lax.broadcasted_iota(jnp.int32, comb.shape, 1)
    ce = jnp.sum(jnp.where(lane == e, comb, 0.0), axis=-1, keepdims=True)
    acc_ref[...] += jnp.dot((a * ce).astype(BF16), wd_ref[0], preferred_element_type=F32)

    @pl.when(e == pl.num_programs(1) - 1)
    def _():
        x = x_ref[...] + g2_ref[0] * acc_ref[...]
        if final_norm:
            ms = jnp.mean(x * x, axis=-1, keepdims=True)
            x = (x * lax.rsqrt(ms + NORM_EPS)) * fg_ref[...]
        o_ref[...] = x


def _moe(h, wgu, wd, comb, x2d, g2, final_g, seq, *, final_norm, tm=1024):
    t, d = x2d.shape
    n_exp, _, two_h = wgu.shape
    hidden = two_h // 2
    per_b = seq // tm
    return pl.pallas_call(
        functools.partial(_moe_kernel, hidden=hidden, final_norm=final_norm),
        grid=(t // tm, n_exp),
        in_specs=[
            pl.BlockSpec((tm, d), lambda i, e: (i, 0)),
            pl.BlockSpec((1, d, two_h), lambda i, e: (e, 0, 0)),
            pl.BlockSpec((1, hidden, d), lambda i, e: (e, 0, 0)),
            pl.BlockSpec((tm, n_exp), lambda i, e: (i, 0)),
            pl.BlockSpec((tm, d), lambda i, e: (i, 0)),
            pl.BlockSpec((1, 1, d), lambda i, e: (i // per_b, 0, 0)),
            pl.BlockSpec((1, d), lambda i, e: (0, 0)),
        ],
        out_specs=pl.BlockSpec((tm, d), lambda i, e: (i, 0)),
        out_shape=jax.ShapeDtypeStruct((t, d), F32),
        scratch_shapes=[pltpu.VMEM((tm, d), F32)],
        compiler_params=_cparams("arbitrary", "arbitrary"),
        name="moe",
    )(h, wgu, wd, comb, x2d, g2[:, None, :], final_g.reshape(1, d))


def _compress_kernel(ch_ref, w1a_ref, w1b_ref, pos_ref, w1f_ref, b1_ref, w2_ref, o_ref):
    ch = ch_ref[0, 0, 0]
    n = ch.shape[0]
    a = jnp.dot(ch, w1a_ref[0], preferred_element_type=F32)
    b = jnp.dot(ch, w1b_ref[0], preferred_element_type=F32)
    const = jnp.dot(pos_ref[0], w1f_ref[0], precision=HIGHEST, preferred_element_type=F32)[0:1]
    hid = a + pltpu.roll(b, n - 1, 0) + (const + b1_ref[0])
    act = jax.nn.gelu(hid, approximate=True)
    o_ref[0, 0, 0] = jnp.dot(act.astype(BF16), w2_ref[0], preferred_element_type=F32).astype(o_ref.dtype)


def _nsa_compress(ch, cmp_pos, cmp_w1, cmp_b1, cmp_w2):
    b, two, g, nc, f = ch.shape
    hid = cmp_w1.shape[-1]
    w1a = cmp_w1[:, :f, :].astype(BF16)
    w1b = cmp_w1[:, f:, :].astype(BF16)
    pos = jnp.zeros((two, 8, 2 * f), F32).at[:, 0, :].set(cmp_pos.reshape(two, 2 * f))
    w2 = jnp.concatenate([cmp_w2, cmp_w2], axis=-1).astype(BF16)
    return pl.pallas_call(
        _compress_kernel,
        grid=(b, two, g),
        in_specs=[
            pl.BlockSpec((1, 1, 1, nc, f), lambda i, j, k: (i, j, k, 0, 0)),
            pl.BlockSpec((1, f, hid), lambda i, j, k: (j, 0, 0)),
            pl.BlockSpec((1, f, hid), lambda i, j, k: (j, 0, 0)),
            pl.BlockSpec((1, 8, 2 * f), lambda i, j, k: (j, 0, 0)),
            pl.BlockSpec((1, 2 * f, hid), lambda i, j, k: (j, 0, 0)),
            pl.BlockSpec((1, 1, hid), lambda i, j, k: (j, 0, 0)),
            pl.BlockSpec((1, hid, LANES), lambda i, j, k: (j, 0, 0)),
        ],
        out_specs=pl.BlockSpec((1, 1, 1, nc, LANES), lambda i, j, k: (i, j, k, 0, 0)),
        out_shape=jax.ShapeDtypeStruct((b, two, g, nc, LANES), BF16),
        compiler_params=_cparams("arbitrary", "arbitrary", "arbitrary"),
        name="nsa_compress",
    )(ch, w1a, w1b, pos, cmp_w1.astype(F32), cmp_b1[:, None, :].astype(F32), w2)


def _stack_group_queries(q_ref, qm, tq):
    for j in range(q_ref.shape[1] // LANES):
        lo, hi = _split_halves(q_ref[:, j * LANES:(j + 1) * LANES], tq)
        qm[(2 * j) * tq:(2 * j + 1) * tq, :] = lo
        qm[(2 * j + 1) * tq:(2 * j + 2) * tq, :] = hi


def _gated_pairs_to_token_major(ot, gt_ref, branch, tq, hpg):
    g = pl.program_id(1)
    tiles = []
    for j in range(hpg // 2):
        parts = []
        for h in (2 * j, 2 * j + 1):
            row = gt_ref[pl.ds((g * hpg + h) * 3 + branch, 1), :]
            parts.append(ot[:, h * tq:(h + 1) * tq] * row)
        tiles.append(jnp.concatenate(parts, axis=0).T)
    return tiles


def _nsa_cmp_kernel(q_ref, kc_ref, vc_ref, mt_ref, gt_ref, o_ref, mask_ref, qm, *, tq, hpg, n_top):
    qi = pl.program_id(2)
    q0 = qi * tq
    _stack_group_queries(q_ref, qm, tq)
    kc = kc_ref[0, 0, 0]
    s = lax.dot_general(kc, qm[...], NT_DIMS, preferred_element_type=F32)
    n_idx = lax.broadcasted_iota(jnp.int32, s.shape, 0)
    q_idx = q0 + lax.broadcasted_iota(jnp.int32, s.shape, 1) % tq
    valid = n_idx * NSA_CMP_STRIDE + (NSA_CMP_BLOCK - 1) <= q_idx
    s = jnp.where(valid, s, NEG_INF)
    m = jnp.max(s, axis=0, keepdims=True)
    e = jnp.where(valid, jnp.exp2(s - m), 0.0)
    r = 1.0 / jnp.maximum(jnp.sum(e, axis=0, keepdims=True), 1e-30)
    p = e * r
    ot = lax.dot_general(vc_ref[0, 0, 0], p.astype(BF16), TN_DIMS, preferred_element_type=F32)
    tiles = _gated_pairs_to_token_major(ot[0:HEAD_DIM], gt_ref, 0, tq, hpg)
    for j, tile in enumerate(tiles):
        o_ref[:, j * LANES:(j + 1) * LANES] = tile.astype(o_ref.dtype)

    psum = p[:, 0:tq]
    for h in range(1, hpg):
        psum = psum + p[:, h * tq:(h + 1) * tq]
    p_hi = psum.astype(BF16)
    p_lo = (psum - p_hi.astype(F32)).astype(BF16)
    mt = mt_ref[...]
    imp = (jnp.dot(mt, p_hi, preferred_element_type=F32)
           + jnp.dot(mt, p_lo, preferred_element_type=F32))
    n_sel = imp.shape[0]
    j_idx = lax.broadcasted_iota(jnp.int32, imp.shape, 0)
    j_f = j_idx.astype(F32)
    q_blk = (q0 + lax.broadcasted_iota(jnp.int32, imp.shape, 1)) // NSA_SEL_BLOCK
    forced = (j_idx == 0) | (j_idx == q_blk) | (j_idx == q_blk - 1)
    val = jnp.where(forced, NSA_FORCE, jnp.where(j_idx <= q_blk, imp, -NSA_FORCE))
    sel = jnp.zeros(imp.shape, F32)
    for _ in range(n_top):
        mx = jnp.max(val, axis=0, keepdims=True)
        first = jnp.min(jnp.where(val == mx, j_f, float(n_sel)), axis=0, keepdims=True)
        hit = j_f == first
        sel = jnp.where(hit, 1.0, sel)
        val = jnp.where(hit, -jnp.inf, val)
    mask_ref[0, 0] = sel


def _nsa_win_kernel(q_ref, k_ref, v_ref, gt_ref, o_ref, qm, acc_ref, *, tq, hpg, span):
    g = pl.program_id(1)
    qi = pl.program_id(2)
    q0 = qi * tq
    _stack_group_queries(q_ref, qm, tq)
    start = pl.multiple_of(jnp.maximum(q0 - NSA_WINDOW, 0), tq)
    k = k_ref[pl.ds(start, span), :]
    v = v_ref[pl.ds(start, span), :]
    s = lax.dot_general(k, qm[...], NT_DIMS, preferred_element_type=F32)
    k_idx = start + lax.broadcasted_iota(jnp.int32, s.shape, 0)
    q_idx = q0 + lax.broadcasted_iota(jnp.int32, s.shape, 1) % tq
    dist = q_idx - k_idx
    valid = (dist >= 0) & (dist < NSA_WINDOW)
    s = jnp.where(valid, s, NEG_INF)
    m = jnp.max(s, axis=0, keepdims=True)
    e = jnp.exp2(s - m)
    r = 1.0 / jnp.maximum(jnp.sum(e, axis=0, keepdims=True), 1e-30)
    acc_ref[...] = lax.dot_general(v, e.astype(BF16), TN_DIMS, preferred_element_type=F32) * r
    ot = acc_ref[pl.ds(pl.multiple_of(g * HEAD_DIM, HEAD_DIM), HEAD_DIM), :]
    tiles = _gated_pairs_to_token_major(ot, gt_ref, 2, tq, hpg)
    for j, tile in enumerate(tiles):
        o_ref[:, j * LANES:(j + 1) * LANES] = tile.astype(o_ref.dtype)


def _nsa_sel_kernel(q_ref, k_ref, v_ref, mask_ref, gt_ref, oc_ref, ow_ref, o_ref,
                    qm, *scratch, tq, tk, hpg):
    g = pl.program_id(1)
    qi = pl.program_id(2)
    q0 = qi * tq
    _stack_group_queries(q_ref, qm, tq)
    acc_ref, l_ref = scratch[-1], scratch[-2]
    blocks = tk // NSA_SEL_BLOCK

    def scores_full(t):
        k = k_ref[pl.ds(pl.multiple_of(t * tk, tk), tk), :]
        s = lax.dot_general(k, qm[...], NT_DIMS, preferred_element_type=F32)
        rows = mask_ref[0, 0, pl.ds(pl.multiple_of(t * blocks, blocks), blocks), :]
        exp_rows = [jnp.broadcast_to(rows[r:r + 1, :], (NSA_SEL_BLOCK, tq)) for r in range(blocks)]
        sel = jnp.concatenate(exp_rows, axis=0)
        sel = jnp.concatenate([sel] * hpg, axis=1)
        return jnp.where(sel > 0.0, s, NEG_INF)

    def scores_diag(t):
        s = scores_full(t)
        k_idx = t * tk + lax.broadcasted_iota(jnp.int32, s.shape, 0)
        q_idx = q0 + lax.broadcasted_iota(jnp.int32, s.shape, 1) % tq
        return jnp.where(k_idx <= q_idx, s, NEG_INF)

    def v_tile(t):
        return v_ref[pl.ds(pl.multiple_of(t * tk, tk), tk), :]

    _pipelined_attention(q0 // tk, scores_full, scores_diag, v_tile, scratch)

    inv_l = 1.0 / jnp.maximum(l_ref[...], 1e-30)
    ot = acc_ref[pl.ds(pl.multiple_of(g * HEAD_DIM, HEAD_DIM), HEAD_DIM), :] * inv_l
    tiles = _gated_pairs_to_token_major(ot, gt_ref, 1, tq, hpg)
    for j, tile in enumerate(tiles):
        sl = slice(j * LANES, (j + 1) * LANES)
        total = tile + oc_ref[:, sl].astype(F32) + ow_ref[:, sl].astype(F32)
        o_ref[:, sl] = total.astype(o_ref.dtype)


def _nsa_attention(proj, gates_t, cmp_kv, batch, seq, d_model, *, tq=128, tq_sel=256, tk=512):
    assert tk % tq_sel == 0 and seq % tk == 0
    t = proj.shape[0]
    groups = NSA_KV_GROUPS
    hpg = d_model // HEAD_DIM // groups
    gw = hpg * HEAD_DIM
    nq = seq // tq
    nq_sel = seq // tq_sel
    nc = cmp_kv.shape[3]
    n_sel = seq // NSA_SEL_BLOCK
    n_top = min(NSA_TOP_N, n_sel)
    ng = gates_t.shape[0]
    span = NSA_WINDOW + tq
    qcb = d_model // LANES

    ci = jnp.arange(nc)[None, :] * NSA_CMP_STRIDE
    sj = jnp.arange(n_sel)[:, None] * NSA_SEL_BLOCK
    mt = ((ci < sj + NSA_SEL_BLOCK) & (ci + NSA_CMP_BLOCK > sj)
          & (jnp.arange(nc)[None, :] < nc - 1)).astype(BF16)

    q_spec = pl.BlockSpec((tq, gw), lambda b, g, i: (b * nq + i, g))
    gt_spec = pl.BlockSpec((ng, tq), lambda b, g, i: (0, b * nq + i))
    o_spec = pl.BlockSpec((tq, gw), lambda b, g, i: (b * nq + i, g))
    o_shape = jax.ShapeDtypeStruct((t, d_model), BF16)
    sem = ("arbitrary", "arbitrary", "arbitrary")
    qm_scr = pltpu.VMEM((hpg * tq, LANES), BF16)

    o_cmp, sel_mask = pl.pallas_call(
        functools.partial(_nsa_cmp_kernel, tq=tq, hpg=hpg, n_top=n_top),
        grid=(batch, groups, nq),
        in_specs=[
            q_spec,
            pl.BlockSpec((1, 1, 1, nc, LANES), lambda b, g, i: (b, 0, g, 0, 0)),
            pl.BlockSpec((1, 1, 1, nc, LANES), lambda b, g, i: (b, 1, g, 0, 0)),
            pl.BlockSpec((n_sel, nc), lambda b, g, i: (0, 0)),
            gt_spec,
        ],
        out_specs=[o_spec, pl.BlockSpec((1, 1, n_sel, tq), lambda b, g, i: (b, g, 0, i))],
        out_shape=[o_shape, jax.ShapeDtypeStruct((batch, groups, n_sel, seq), F32)],
        scratch_shapes=[qm_scr],
        compiler_params=_cparams(*sem),
        name="nsa_cmp",
    )(proj, cmp_kv, cmp_kv, mt, gates_t)

    o_win = pl.pallas_call(
        functools.partial(_nsa_win_kernel, tq=tq, hpg=hpg, span=span),
        grid=(batch, groups, nq),
        in_specs=[
            q_spec,
            pl.BlockSpec((seq, LANES), lambda b, g, i: (b, qcb + 5 + g)),
            pl.BlockSpec((seq, LANES), lambda b, g, i: (b, qcb + 7)),
            gt_spec,
        ],
        out_specs=o_spec,
        out_shape=o_shape,
        scratch_shapes=[qm_scr, pltpu.VMEM((LANES, hpg * tq), F32)],
        compiler_params=_cparams(*sem),
        name="nsa_win",
    )(proj, proj, proj, gates_t)

    rows_sel = pl.BlockSpec((tq_sel, gw), lambda b, g, i: (b * nq_sel + i, g))
    return pl.pallas_call(
        functools.partial(_nsa_sel_kernel, tq=tq_sel, tk=tk, hpg=hpg),
        grid=(batch, groups, nq_sel),
        in_specs=[
            rows_sel,
            pl.BlockSpec((seq, LANES), lambda b, g, i: (b, qcb + 2 + g)),
            pl.BlockSpec((seq, LANES), lambda b, g, i: (b, qcb + 4)),
            pl.BlockSpec((1, 1, n_sel, tq_sel), lambda b, g, i: (b, g, 0, i)),
            pl.BlockSpec((ng, tq_sel), lambda b, g, i: (0, b * nq_sel + i)),
            rows_sel,
            rows_sel,
        ],
        out_specs=rows_sel,
        out_shape=o_shape,
        scratch_shapes=[pltpu.VMEM((hpg * tq_sel, LANES), BF16)] + _attn_scratch(tk, hpg * tq_sel, LANES),
        compiler_params=_cparams(*sem),
        name="nsa_sel",
    )(proj, proj, proj, sel_mask, gates_t, o_cmp, o_win)


def _diff_in_weights(w_in, d_model):
    scale = HEAD_DIM ** -0.5 * LOG2E
    wq = w_in[:, :d_model] * scale
    return jnp.concatenate([wq, w_in[:, d_model:]], axis=1).astype(BF16)


def _nsa_in_weights(w_in, d_model):
    g, hd = NSA_KV_GROUPS, HEAD_DIM
    kvw = g * hd
    scale = hd ** -0.5 * LOG2E
    q = w_in[:, :d_model] * scale
    off = d_model
    sec = [w_in[:, off + i * kvw: off + (i + 1) * kvw] for i in range(6)]
    dup = lambda w: jnp.concatenate([w[:, i * hd:(i + 1) * hd] for i in range(g) for _ in range(2)], axis=1)
    w_main = jnp.concatenate([q, sec[0], sec[1], dup(sec[2]), sec[3], dup(sec[4]), sec[5]], axis=1)
    w_gates = w_in[:, off + 6 * kvw:].T
    return w_main.astype(BF16), w_gates.astype(BF16)


def _router_weights(w_grp, b_grp, w_exp, b_exp):
    d = w_grp.shape[0]
    n_exp = w_exp.shape[0] * w_exp.shape[2]
    we = jnp.transpose(w_exp, (1, 0, 2)).reshape(d, n_exp)
    pad = LANES - n_exp - w_grp.shape[1]
    wr = jnp.concatenate([we, w_grp, jnp.zeros((d, pad), F32)], axis=1).astype(F32)
    br = jnp.concatenate([b_exp.reshape(-1), b_grp, jnp.zeros((pad,), F32)])[None, :].astype(F32)
    return wr, br, n_exp


def kernel(x, c, positions, ada_w, ada_b, norm_g, final_g, diff_w_in, diff_w_out, diff_lambda,
           diff_subln_g, nsa_w_in, nsa_w_out, nsa_cmp_pos, nsa_cmp_w1, nsa_cmp_b1, nsa_cmp_w2,
           moe_w_group, moe_b_group, moe_w_expert, moe_b_expert, moe_w_gate, moe_w_up, moe_w_down):
    batch, seq, d = x.shape
    depth = ada_w.shape[0]
    x2d = x.reshape(batch * seq, d)
    tables = _rope_tables(positions)
    mod = _adaln_mod(c, ada_w, ada_b)

    for i in range(depth):
        sh1, sc1, g1, sh2, sc2, g2 = jnp.split(mod[i], 6, axis=-1)
        j = i // N_MIXERS
        if i % N_MIXERS == 0:
            lambda_init = 0.8 - 0.6 * math.exp(-0.3 * i)
            qkv = _norm_proj(x2d, norm_g[i, 0], sc1, sh1, _diff_in_weights(diff_w_in[j], d), tables,
                             seq, tn=512, plain_tiles=range(2 * d // 512, 3 * d // 512))
            o = _diff_attention(qkv, diff_lambda[j], diff_subln_g[j], batch, seq, lambda_init)
            w_out = diff_w_out[j]
        else:
            w_main, w_gates = _nsa_in_weights(nsa_w_in[j], d)
            proj, gates_t = _norm_proj(x2d, norm_g[i, 0], sc1, sh1, w_main, tables, seq,
                                       tn=256, plain_tiles=(d // 256,), w_gates=w_gates)
            groups = NSA_KV_GROUPS
            kcvc = proj[:, d:d + 2 * groups * HEAD_DIM]
            ch = kcvc.reshape(batch, seq, 2, groups, HEAD_DIM).transpose(0, 2, 3, 1, 4)
            ch = ch.reshape(batch, 2, groups, seq // NSA_CMP_STRIDE, NSA_CMP_STRIDE * HEAD_DIM)
            cmp_kv = _nsa_compress(ch, nsa_cmp_pos[j], nsa_cmp_w1[j], nsa_cmp_b1[j], nsa_cmp_w2[j])
            o = _nsa_attention(proj, gates_t, cmp_kv, batch, seq, d)
            w_out = nsa_w_out[j]
        wr, br, n_exp = _router_weights(moe_w_group[i], moe_b_group[i], moe_w_expert[i], moe_b_expert[i])
        x2d, h2, comb = _outproj_router(o, w_out.astype(BF16), x2d, g1, norm_g[i, 1], sc2, sh2,
                                        wr, br, seq, n_exp)
        wgu = jnp.concatenate([moe_w_gate[i], moe_w_up[i]], axis=-1).astype(BF16)
        x2d = _moe(h2, wgu, moe_w_down[i].astype(BF16), comb, x2d, g2, final_g, seq,
                   final_norm=(i == depth - 1))
    return x2d.reshape(batch, seq, d)
```

```python
import functools
import math

import jax
import jax.numpy as jnp
from jax import lax
from jax.experimental import pallas as pl
from jax.experimental.pallas import tpu as pltpu

F32 = jnp.float32
BF16 = jnp.bfloat16
HIGHEST = lax.Precision.HIGHEST

ROPE_THETA = 500000.0
ROT_DIM = 16
NORM_EPS = 1e-6
NEG_INF = -1e30
HEAD_DIM = 64
DA_V_DIM = 2 * HEAD_DIM
NSA_KV_GROUPS = 2
NSA_CMP_BLOCK = 32
NSA_CMP_STRIDE = 16
NSA_SEL_BLOCK = 64
NSA_TOP_N = 16
NSA_WINDOW = 512
NSA_FORCE = 1e4
MOE_GROUPS = 4
MOE_EXPERTS_PER_GROUP = 8
MOE_TOP_K = 2
N_MIXERS = 2
LOG2E = 1.4426950408889634

LANES = 128
VMEM_LIMIT_BYTES = 56 * 1024 * 1024

NT_DIMS = (((1,), (1,)), ((), ()))
TN_DIMS = (((0,), (0,)), ((), ()))


def _cparams(*sem):
    return pltpu.CompilerParams(dimension_semantics=sem, vmem_limit_bytes=VMEM_LIMIT_BYTES)


def _mod_kernel(c_ref, w_ref, b_ref, o_ref):
    c = c_ref[...]
    ca = c * jax.nn.sigmoid(c)
    o_ref[0] = jnp.dot(ca, w_ref[0], precision=HIGHEST, preferred_element_type=F32) + b_ref[0]


def _adaln_mod(c, ada_w, ada_b):
    depth, d, n = ada_w.shape
    b = c.shape[0]
    rows = 8
    tn = 1536
    c_pad = jnp.zeros((rows, d), F32).at[:b].set(c)
    out = pl.pallas_call(
        _mod_kernel,
        grid=(depth, n // tn),
        in_specs=[
            pl.BlockSpec((rows, d), lambda i, j: (0, 0)),
            pl.BlockSpec((1, d, tn), lambda i, j: (i, 0, j)),
            pl.BlockSpec((1, 1, tn), lambda i, j: (i, 0, j)),
        ],
        out_specs=pl.BlockSpec((1, rows, tn), lambda i, j: (i, 0, j)),
        out_shape=jax.ShapeDtypeStruct((depth, rows, n), F32),
        compiler_params=_cparams("arbitrary", "arbitrary"),
        name="adaln_mod",
    )(c_pad, ada_w, ada_b.reshape(depth, 1, n))
    return out[:, :b, :]


def _rope_tables(positions):
    half = ROT_DIM // 2
    inv_freq = ROPE_THETA ** (-jnp.arange(0, ROT_DIM, 2, dtype=F32) / ROT_DIM)
    ang = positions.astype(F32).reshape(-1)[:, None] * inv_freq
    cos, sin = jnp.cos(ang), jnp.sin(ang)
    t = ang.shape[0]
    rest = HEAD_DIM - ROT_DIM
    z8 = jnp.zeros((t, half), F32)
    cos_h = jnp.concatenate([cos, cos, jnp.ones((t, rest), F32)], axis=-1)
    sina_h = jnp.concatenate([-sin, z8, jnp.zeros((t, rest), F32)], axis=-1)
    sinb_h = jnp.concatenate([z8, sin, jnp.zeros((t, rest), F32)], axis=-1)
    rep = LANES // HEAD_DIM
    return tuple(jnp.tile(a, (1, rep)) for a in (cos_h, sina_h, sinb_h))


def _rms_modulate(x, g, sc, sh):
    ms = jnp.mean(x * x, axis=-1, keepdims=True)
    y = x * lax.rsqrt(ms + NORM_EPS)
    return (y * g) * (1.0 + sc) + sh


def _proj_kernel(*refs, plain_tiles, with_gates):
    if with_gates:
        (x_ref, g_ref, sc_ref, sh_ref, w_ref, cos_ref, sa_ref, sb_ref, wg_ref,
         o_ref, gt_ref, h_scr) = refs
    else:
        x_ref, g_ref, sc_ref, sh_ref, w_ref, cos_ref, sa_ref, sb_ref, o_ref, h_scr = refs
    j = pl.program_id(1)
    tn = o_ref.shape[1]

    @pl.when(j == 0)
    def _():
        h = _rms_modulate(x_ref[...], g_ref[...], sc_ref[0], sh_ref[0])
        h_scr[...] = h.astype(BF16)
        if with_gates:
            logits = lax.dot_general(wg_ref[...], h_scr[...], NT_DIMS, preferred_element_type=F32)
            gt_ref[...] = jax.nn.sigmoid(logits)

    acc = jnp.dot(h_scr[...], w_ref[...], preferred_element_type=F32)

    is_plain = functools.reduce(jnp.logical_or, [j == t for t in plain_tiles], j < 0)

    @pl.when(is_plain)
    def _():
        o_ref[...] = acc.astype(o_ref.dtype)

    @pl.when(jnp.logical_not(is_plain))
    def _():
        shift_up = pltpu.roll(acc, tn - ROT_DIM // 2, 1)
        shift_dn = pltpu.roll(acc, ROT_DIM // 2, 1)
        cos, sa, sb = cos_ref[...], sa_ref[...], sb_ref[...]
        for c in range(tn // LANES):
            sl = slice(c * LANES, (c + 1) * LANES)
            y = acc[:, sl] * cos + shift_up[:, sl] * sa + shift_dn[:, sl] * sb
            o_ref[:, sl] = y.astype(o_ref.dtype)


def _norm_proj(x2d, g, sc, sh, w, tables, seq, *, tn, plain_tiles, w_gates=None, tm=1024):
    t, d = x2d.shape
    n = w.shape[1]
    with_gates = w_gates is not None
    per_b = seq // tm
    in_specs = [
        pl.BlockSpec((tm, d), lambda i, j: (i, 0)),
        pl.BlockSpec((1, d), lambda i, j: (0, 0)),
        pl.BlockSpec((1, 1, d), lambda i, j: (i // per_b, 0, 0)),
        pl.BlockSpec((1, 1, d), lambda i, j: (i // per_b, 0, 0)),
        pl.BlockSpec((d, tn), lambda i, j: (0, j)),
        pl.BlockSpec((tm, LANES), lambda i, j: (i, 0)),
        pl.BlockSpec((tm, LANES), lambda i, j: (i, 0)),
        pl.BlockSpec((tm, LANES), lambda i, j: (i, 0)),
    ]
    args = [x2d, g.reshape(1, d), sc[:, None, :], sh[:, None, :], w, *tables]
    out_specs = [pl.BlockSpec((tm, tn), lambda i, j: (i, j))]
    out_shape = [jax.ShapeDtypeStruct((t, n), BF16)]
    if with_gates:
        ng = w_gates.shape[0]
        in_specs.append(pl.BlockSpec((ng, d), lambda i, j: (0, 0)))
        args.append(w_gates)
        out_specs.append(pl.BlockSpec((ng, tm), lambda i, j: (0, i)))
        out_shape.append(jax.ShapeDtypeStruct((ng, t), F32))
    res = pl.pallas_call(
        functools.partial(_proj_kernel, plain_tiles=tuple(plain_tiles), with_gates=with_gates),
        grid=(t // tm, n // tn),
        in_specs=in_specs,
        out_specs=out_specs,
        out_shape=out_shape,
        scratch_shapes=[pltpu.VMEM((tm, d), BF16)],
        compiler_params=_cparams("arbitrary", "arbitrary"),
        name="norm_proj",
    )(*args)
    return res if with_gates else res[0]


def _attn_scratch(tk, nq, dv):
    return [
        pltpu.VMEM((2, tk, nq), F32),
        pltpu.VMEM((2, 1, nq), F32),
        pltpu.VMEM((2, tk, nq), BF16),
        pltpu.VMEM((2, 1, nq), F32),
        pltpu.VMEM((1, nq), F32),
        pltpu.VMEM((1, nq), F32),
        pltpu.VMEM((dv, nq), F32),
    ]


def _pipelined_attention(n_full, scores_full, scores_diag, v_tile, scratch):
    s_scr, mx_scr, p_scr, al_scr, m_ref, l_ref, acc_ref = scratch

    def stage_scores(fn, tile, slot):
        s = fn(tile)
        s_scr[slot] = s
        mx_scr[slot] = jnp.max(s, axis=0, keepdims=True)

    def stage_softmax(slot):
        m_prev = m_ref[...]
        m_new = jnp.maximum(m_prev, mx_scr[slot])
        alpha = jnp.exp2(m_prev - m_new)
        p = jnp.exp2(s_scr[slot] - m_new)
        l_ref[...] = alpha * l_ref[...] + jnp.sum(p, axis=0, keepdims=True)
        m_ref[...] = m_new
        al_scr[slot] = alpha
        p_scr[slot] = p.astype(BF16)

    def stage_pv(pos, slot):
        tile = jnp.where(pos == 0, n_full, jnp.maximum(pos - 1, 0))
        pv = lax.dot_general(v_tile(tile), p_scr[slot], TN_DIMS, preferred_element_type=F32)
        acc_ref[...] = acc_ref[...] * al_scr[slot] + pv

    def step(j, slot):
        stage_scores(scores_full, j, 1 - slot)
        stage_softmax(slot)
        stage_pv(j - 1, 1 - slot)

    m_ref[...] = jnp.full(m_ref.shape, NEG_INF, F32)
    l_ref[...] = jnp.zeros(l_ref.shape, F32)
    acc_ref[...] = jnp.zeros(acc_ref.shape, F32)
    p_scr[1] = jnp.zeros(p_scr.shape[1:], BF16)
    al_scr[1] = jnp.ones(al_scr.shape[1:], F32)
    stage_scores(scores_diag, n_full, 0)

    def pair(jj, carry):
        step(2 * jj, 0)
        step(2 * jj + 1, 1)
        return carry

    lax.fori_loop(0, n_full // 2, pair, 0)
    odd = n_full % 2 == 1

    @pl.when(odd)
    def _():
        step(n_full - 1, 0)
        stage_softmax(1)
        stage_pv(n_full - 1, 0)
        stage_pv(n_full, 1)

    @pl.when(jnp.logical_not(odd))
    def _():
        stage_softmax(0)
        stage_pv(n_full - 1, 1)
        stage_pv(n_full, 0)


def _split_halves(q, rows):
    lane = lax.broadcasted_iota(jnp.int32, q.shape, 1)
    zero = jnp.zeros_like(q)
    return jnp.where(lane < HEAD_DIM, q, zero), jnp.where(lane >= HEAD_DIM, q, zero)


def _diff_attn_kernel(q_ref, k_ref, v_ref, lam_ref, g_ref, o_ref, qm, *scratch,
                      tq, tk, lambda_init):
    qi = pl.program_id(2)
    lo, hi = _split_halves(q_ref[...], tq)
    qm[0:tq, :] = lo
    qm[tq:2 * tq, :] = hi
    q0 = qi * tq
    acc_ref, l_ref = scratch[-1], scratch[-2]

    def scores_full(t):
        k = k_ref[pl.ds(pl.multiple_of(t * tk, tk), tk), :]
        return lax.dot_general(k, qm[...], NT_DIMS, preferred_element_type=F32)

    def scores_diag(t):
        s = scores_full(t)
        k_idx = t * tk + lax.broadcasted_iota(jnp.int32, s.shape, 0)
        q_idx = q0 + lax.broadcasted_iota(jnp.int32, s.shape, 1) % tq
        return jnp.where(k_idx <= q_idx, s, NEG_INF)

    def v_tile(t):
        return v_ref[pl.ds(pl.multiple_of(t * tk, tk), tk), :]

    _pipelined_attention(q0 // tk, scores_full, scores_diag, v_tile, scratch)

    lam = lam_ref[...]
    lam_full = (jnp.exp(jnp.sum(lam[0:1] * lam[1:2], axis=-1, keepdims=True))
                - jnp.exp(jnp.sum(lam[2:3] * lam[3:4], axis=-1, keepdims=True)) + lambda_init)
    acc = acc_ref[...]
    l = jnp.maximum(l_ref[...], 1e-30)
    ot = acc[:, 0:tq] / l[:, 0:tq] - lam_full * (acc[:, tq:2 * tq] / l[:, tq:2 * tq])
    ms = jnp.mean(ot * ot, axis=0, keepdims=True)
    ot = (ot * lax.rsqrt(ms + NORM_EPS)) * g_ref[...] * (1.0 - lambda_init)
    o_ref[...] = ot.T.astype(o_ref.dtype)


def _diff_attention(qkv, lam, subln_g, batch, seq, lambda_init, *, tq=1024, tk=1024):
    assert tk % tq == 0 and seq % tk == 0
    t, n3 = qkv.shape
    d = n3 // 3
    heads = d // DA_V_DIM
    nq = seq // tq
    cb = d // LANES
    g_b = jnp.broadcast_to(subln_g.astype(F32)[:, None], (DA_V_DIM, tq))
    return pl.pallas_call(
        functools.partial(_diff_attn_kernel, tq=tq, tk=tk, lambda_init=lambda_init),
        grid=(batch, heads, nq),
        in_specs=[
            pl.BlockSpec((tq, LANES), lambda b, h, i: (b * nq + i, h)),
            pl.BlockSpec((seq, LANES), lambda b, h, i: (b, cb + h)),
            pl.BlockSpec((seq, LANES), lambda b, h, i: (b, 2 * cb + h)),
            pl.BlockSpec(lam.shape, lambda b, h, i: (0, 0)),
            pl.BlockSpec((DA_V_DIM, tq), lambda b, h, i: (0, 0)),
        ],
        out_specs=pl.BlockSpec((tq, LANES), lambda b, h, i: (b * nq + i, h)),
        out_shape=jax.ShapeDtypeStruct((t, d), BF16),
        scratch_shapes=[pltpu.VMEM((2 * tq, LANES), BF16)] + _attn_scratch(tk, 2 * tq, DA_V_DIM),
        compiler_params=_cparams("arbitrary", "arbitrary", "arbitrary"),
        name="diff_attn",
    )(qkv, qkv, qkv, lam.astype(F32), g_b)


def _route(logits, n_exp):
    lane = lax.broadcasted_iota(jnp.int32, logits.shape, 1)
    lane_f = lane.astype(F32)
    big = float(LANES)
    is_grp = (lane >= n_exp) & (lane < n_exp + MOE_GROUPS)
    gl = jnp.where(is_grp, logits, -jnp.inf)
    ge = jnp.where(is_grp, jnp.exp(gl - jnp.max(gl, axis=-1, keepdims=True)), 0.0)
    pg = ge / jnp.sum(ge, axis=-1, keepdims=True)
    p_top = jnp.max(pg, axis=-1, keepdims=True)
    grp = jnp.min(jnp.where(is_grp & (pg == p_top), lane_f, big), axis=-1, keepdims=True) - n_exp
    in_grp = (lane < n_exp) & ((lane // MOE_EXPERTS_PER_GROUP).astype(F32) == grp)
    sel = jnp.where(in_grp, logits, -jnp.inf)
    v1 = jnp.max(sel, axis=-1, keepdims=True)
    i1 = jnp.min(jnp.where(sel == v1, lane_f, big), axis=-1, keepdims=True)
    sel2 = jnp.where(lane_f == i1, -jnp.inf, sel)
    v2 = jnp.max(sel2, axis=-1, keepdims=True)
    i2 = jnp.min(jnp.where(sel2 == v2, lane_f, big), axis=-1, keepdims=True)
    e2 = jnp.exp(v2 - v1)
    den = 1.0 + e2
    w1 = (1.0 / den) * p_top
    w2 = (e2 / den) * p_top
    return jnp.where(lane_f == i1, w1, 0.0) + jnp.where(lane_f == i2, w2, 0.0), grp


def _outproj_router_kernel(o_ref, w_ref, x_ref, g1_ref, ng_ref, sc_ref, sh_ref, wr_ref, br_ref,
                           xo_ref, h_ref, comb_ref, gid_ref, *, n_exp):
    mix = jnp.dot(o_ref[...], w_ref[...], preferred_element_type=F32)
    x = x_ref[...] + g1_ref[0] * mix
    xo_ref[...] = x
    h = _rms_modulate(x, ng_ref[...], sc_ref[0], sh_ref[0])
    h_ref[...] = h.astype(BF16)
    logits = jnp.dot(h, wr_ref[...], precision=HIGHEST, preferred_element_type=F32) + br_ref[...]
    comb, grp = _route(logits, n_exp)
    comb_ref[...] = comb[:, 0:n_exp]
    gid_ref[...] = jnp.broadcast_to(grp, logits.shape).T[0:gid_ref.shape[0], :]


def _outproj_router(o, w_out, x2d, g1, ng, sc, sh, wr, br, seq, n_exp, *, tm=512):
    t, d = x2d.shape
    per_b = seq // tm
    bvec = lambda i: (i // per_b, 0, 0)
    return pl.pallas_call(
        functools.partial(_outproj_router_kernel, n_exp=n_exp),
        grid=(t // tm,),
        in_specs=[
            pl.BlockSpec((tm, d), lambda i: (i, 0)),
            pl.BlockSpec((d, d), lambda i: (0, 0)),
            pl.BlockSpec((tm, d), lambda i: (i, 0)),
            pl.BlockSpec((1, 1, d), bvec),
            pl.BlockSpec((1, d), lambda i: (0, 0)),
            pl.BlockSpec((1, 1, d), bvec),
            pl.BlockSpec((1, 1, d), bvec),
            pl.BlockSpec((d, LANES), lambda i: (0, 0)),
            pl.BlockSpec((1, LANES), lambda i: (0, 0)),
        ],
        out_specs=[
            pl.BlockSpec((tm, d), lambda i: (i, 0)),
            pl.BlockSpec((tm, d), lambda i: (i, 0)),
            pl.BlockSpec((tm, n_exp), lambda i: (i, 0)),
            pl.BlockSpec((8, tm), lambda i: (0, i)),
        ],
        out_shape=[
            jax.ShapeDtypeStruct((t, d), F32),
            jax.ShapeDtypeStruct((t, d), BF16),
            jax.ShapeDtypeStruct((t, n_exp), F32),
            jax.ShapeDtypeStruct((8, t), F32),
        ],
        compiler_params=_cparams("arbitrary"),
        name="outproj_router",
    )(o, w_out, x2d, g1[:, None, :], ng.reshape(1, d), sc[:, None, :], sh[:, None, :], wr, br)


MOE_CHUNK = 128
MOE_EXPERTS_PER_STEP = 4


def _moe_kernel(h_ref, wgu_ref, wd_ref, comb_ref, gid_ref, tri_ref, x_ref, g2_ref, fg_ref, o_ref,
                perm_scr, hs_scr, cs_scr, ys_scr, off_smem, nch_smem, *, hidden, final_norm):
    step = pl.program_id(1)
    tm = h_ref.shape[0]
    rows = perm_scr.shape[0]
    n_exp = comb_ref.shape[1]
    steps_per_group = MOE_EXPERTS_PER_GROUP // MOE_EXPERTS_PER_STEP

    @pl.when(step == 0)
    def _():
        gid = gid_ref[0:1, :]
        grow = lax.broadcasted_iota(jnp.int32, (8, tm), 0).astype(F32)
        onehot = jnp.where(grow == gid, 1.0, 0.0)
        rank = jnp.dot(onehot.astype(BF16), tri_ref[...], preferred_element_type=F32)
        off = jnp.int32(0)
        off_col = jnp.zeros((8, 1), F32)
        grow_col = lax.broadcasted_iota(jnp.int32, (8, 1), 0)
        for g in range(MOE_GROUPS):
            cnt = jnp.sum(onehot[g:g + 1, :]).astype(jnp.int32)
            nch = (cnt + (MOE_CHUNK - 1)) // MOE_CHUNK
            off_smem[g] = off
            nch_smem[g] = nch
            off_col = jnp.where(grow_col == g, off.astype(F32), off_col)
            off = off + nch * MOE_CHUNK
        dest = jnp.sum(onehot * (rank + off_col), axis=0, keepdims=True)
        row_id = lax.broadcasted_iota(jnp.int32, (rows, tm), 0)
        perm = jnp.where(row_id == dest.astype(jnp.int32), 1.0, 0.0).astype(BF16)
        perm_scr[...] = perm
        hs_scr[...] = jnp.dot(perm, h_ref[...], preferred_element_type=F32).astype(BF16)
        comb = comb_ref[...]
        comb_hi = comb.astype(BF16)
        comb_lo = (comb - comb_hi.astype(F32)).astype(BF16)
        cs_scr[...] = (jnp.dot(perm, comb_hi, preferred_element_type=F32)
                       + jnp.dot(perm, comb_lo, preferred_element_type=F32))
        ys_scr[...] = jnp.zeros(ys_scr.shape, F32)

    g = step // steps_per_group
    off = off_smem[g]
    lane = lax.broadcasted_iota(jnp.int32, (MOE_CHUNK, n_exp), 1)

    def chunk(c, carry):
        r0 = pl.multiple_of(off + c * MOE_CHUNK, MOE_CHUNK)
        xs = hs_scr[pl.ds(r0, MOE_CHUNK), :]
        cs = cs_scr[pl.ds(r0, MOE_CHUNK), :]
        y = jnp.zeros((MOE_CHUNK, xs.shape[1]), F32)
        for k in range(MOE_EXPERTS_PER_STEP):
            gu = jnp.dot(xs, wgu_ref[k], preferred_element_type=F32)
            gate, up = gu[:, 0:hidden], gu[:, hidden:2 * hidden]
            a = (gate * jax.nn.sigmoid(gate)) * up
            e = step * MOE_EXPERTS_PER_STEP + k
            ce = jnp.sum(jnp.where(lane == e, cs, 0.0), axis=-1, keepdims=True)
            y = y + jnp.dot((a * ce).astype(BF16), wd_ref[k], preferred_element_type=F32)
        ys_scr[pl.ds(r0, MOE_CHUNK), :] += y
        return carry

    lax.fori_loop(0, nch_smem[g], chunk, 0)

    @pl.when(step == pl.num_programs(1) - 1)
    def _():
        y = lax.dot_general(perm_scr[...], ys_scr[...].astype(BF16), TN_DIMS,
                            preferred_element_type=F32)
        x = x_ref[...] + g2_ref[0] * y
        if final_norm:
            ms = jnp.mean(x * x, axis=-1, keepdims=True)
            x = (x * lax.rsqrt(ms + NORM_EPS)) * fg_ref[...]
        o_ref[...] = x


def _moe(h, wgu, wd, comb, gid_t, x2d, g2, final_g, seq, *, final_norm, tm=1024):
    t, d = x2d.shape
    n_exp, _, two_h = wgu.shape
    hidden = two_h // 2
    per_b = seq // tm
    eps = MOE_EXPERTS_PER_STEP
    rows = tm + MOE_GROUPS * MOE_CHUNK
    idx = jnp.arange(tm)
    tri = (idx[:, None] < idx[None, :]).astype(BF16)
    return pl.pallas_call(
        functools.partial(_moe_kernel, hidden=hidden, final_norm=final_norm),
        grid=(t // tm, n_exp // eps),
        in_specs=[
            pl.BlockSpec((tm, d), lambda i, s: (i, 0)),
            pl.BlockSpec((eps, d, two_h), lambda i, s: (s, 0, 0)),
            pl.BlockSpec((eps, hidden, d), lambda i, s: (s, 0, 0)),
            pl.BlockSpec((tm, n_exp), lambda i, s: (i, 0)),
            pl.BlockSpec((8, tm), lambda i, s: (0, i)),
            pl.BlockSpec((tm, tm), lambda i, s: (0, 0)),
            pl.BlockSpec((tm, d), lambda i, s: (i, 0)),
            pl.BlockSpec((1, 1, d), lambda i, s: (i // per_b, 0, 0)),
            pl.BlockSpec((1, d), lambda i, s: (0, 0)),
        ],
        out_specs=pl.BlockSpec((tm, d), lambda i, s: (i, 0)),
        out_shape=jax.ShapeDtypeStruct((t, d), F32),
        scratch_shapes=[
            pltpu.VMEM((rows, tm), BF16),
            pltpu.VMEM((rows, d), BF16),
            pltpu.VMEM((rows, n_exp), F32),
            pltpu.VMEM((rows, d), F32),
            pltpu.SMEM((MOE_GROUPS,), jnp.int32),
            pltpu.SMEM((MOE_GROUPS,), jnp.int32),
        ],
        compiler_params=_cparams("arbitrary", "arbitrary"),
        name="moe",
    )(h, wgu, wd, comb, gid_t, tri, x2d, g2[:, None, :], final_g.reshape(1, d))


def _compress_kernel(ch_ref, w1a_ref, w1b_ref, pos_ref, w1f_ref, b1_ref, w2_ref, o_ref):
    ch = ch_ref[0, 0, 0]
    n = ch.shape[0]
    a = jnp.dot(ch, w1a_ref[0], preferred_element_type=F32)
    b = jnp.dot(ch, w1b_ref[0], preferred_element_type=F32)
    const = jnp.dot(pos_ref[0], w1f_ref[0], precision=HIGHEST, preferred_element_type=F32)[0:1]
    hid = a + pltpu.roll(b, n - 1, 0) + (const + b1_ref[0])
    act = jax.nn.gelu(hid, approximate=True)
    o_ref[0, 0, 0] = jnp.dot(act.astype(BF16), w2_ref[0], preferred_element_type=F32).astype(o_ref.dtype)


def _nsa_compress(ch, cmp_pos, cmp_w1, cmp_b1, cmp_w2):
    b, two, g, nc, f = ch.shape
    hid = cmp_w1.shape[-1]
    w1a = cmp_w1[:, :f, :].astype(BF16)
    w1b = cmp_w1[:, f:, :].astype(BF16)
    pos = jnp.zeros((two, 8, 2 * f), F32).at[:, 0, :].set(cmp_pos.reshape(two, 2 * f))
    w2 = jnp.concatenate([cmp_w2, cmp_w2], axis=-1).astype(BF16)
    return pl.pallas_call(
        _compress_kernel,
        grid=(b, two, g),
        in_specs=[
            pl.BlockSpec((1, 1, 1, nc, f), lambda i, j, k: (i, j, k, 0, 0)),
            pl.BlockSpec((1, f, hid), lambda i, j, k: (j, 0, 0)),
            pl.BlockSpec((1, f, hid), lambda i, j, k: (j, 0, 0)),
            pl.BlockSpec((1, 8, 2 * f), lambda i, j, k: (j, 0, 0)),
            pl.BlockSpec((1, 2 * f, hid), lambda i, j, k: (j, 0, 0)),
            pl.BlockSpec((1, 1, hid), lambda i, j, k: (j, 0, 0)),
            pl.BlockSpec((1, hid, LANES), lambda i, j, k: (j, 0, 0)),
        ],
        out_specs=pl.BlockSpec((1, 1, 1, nc, LANES), lambda i, j, k: (i, j, k, 0, 0)),
        out_shape=jax.ShapeDtypeStruct((b, two, g, nc, LANES), BF16),
        compiler_params=_cparams("arbitrary", "arbitrary", "arbitrary"),
        name="nsa_compress",
    )(ch, w1a, w1b, pos, cmp_w1.astype(F32), cmp_b1[:, None, :].astype(F32), w2)


def _stack_group_queries(q_ref, qm, tq):
    for j in range(q_ref.shape[1] // LANES):
        lo, hi = _split_halves(q_ref[:, j * LANES:(j + 1) * LANES], tq)
        qm[(2 * j) * tq:(2 * j + 1) * tq, :] = lo
        qm[(2 * j + 1) * tq:(2 * j + 2) * tq, :] = hi


def _gated_pairs_to_token_major(ot, gt_ref, branch, tq, hpg):
    g = pl.program_id(1)
    tiles = []
    for j in range(hpg // 2):
        parts = []
        for h in (2 * j, 2 * j + 1):
            row = gt_ref[pl.ds((g * hpg + h) * 3 + branch, 1), :]
            parts.append(ot[:, h * tq:(h + 1) * tq] * row)
        tiles.append(jnp.concatenate(parts, axis=0).T)
    return tiles


NSA_CMP_CLASSES = 4


def _nsa_cmp_body(q0, qm, kc_ref, vc_ref, mt_ref, gt_ref, o_ref, mask_ref, *, tq, hpg, n_top,
                  n_cmp, n_blk):
    kc = kc_ref[0, 0, 0, 0:n_cmp, :]
    s = lax.dot_general(kc, qm[...], NT_DIMS, preferred_element_type=F32)
    n_idx = lax.broadcasted_iota(jnp.int32, (n_cmp, tq), 0)
    q_idx = q0 + lax.broadcasted_iota(jnp.int32, (n_cmp, tq), 1)
    valid = n_idx * NSA_CMP_STRIDE + (NSA_CMP_BLOCK - 1) <= q_idx
    bias = jnp.where(valid, 0.0, NEG_INF)
    s = s + jnp.concatenate([bias] * hpg, axis=1)
    m = jnp.maximum(jnp.max(s, axis=0, keepdims=True), 0.1 * NEG_INF)
    e = jnp.exp2(s - m)
    r = 1.0 / jnp.maximum(jnp.sum(e, axis=0, keepdims=True), 1e-30)
    p = e * r
    ot = lax.dot_general(vc_ref[0, 0, 0, 0:n_cmp, :], p.astype(BF16), TN_DIMS,
                         preferred_element_type=F32)
    tiles = _gated_pairs_to_token_major(ot[0:HEAD_DIM], gt_ref, 0, tq, hpg)
    for j, tile in enumerate(tiles):
        o_ref[:, j * LANES:(j + 1) * LANES] = tile.astype(o_ref.dtype)

    psum = p[:, 0:tq]
    for h in range(1, hpg):
        psum = psum + p[:, h * tq:(h + 1) * tq]
    p_hi = psum.astype(BF16)
    p_lo = (psum - p_hi.astype(F32)).astype(BF16)
    mt = mt_ref[0:n_blk, 0:n_cmp]
    imp = (jnp.dot(mt, p_hi, preferred_element_type=F32)
           + jnp.dot(mt, p_lo, preferred_element_type=F32))
    j_idx = lax.broadcasted_iota(jnp.int32, imp.shape, 0)
    j_f = j_idx.astype(F32)
    q_blk = (q0 + lax.broadcasted_iota(jnp.int32, imp.shape, 1)) // NSA_SEL_BLOCK
    forced = (j_idx == 0) | (j_idx == q_blk) | (j_idx == q_blk - 1)
    val = jnp.where(forced, NSA_FORCE, jnp.where(j_idx <= q_blk, imp, -NSA_FORCE))
    sel_bias = jnp.full(imp.shape, NEG_INF, F32)
    for _ in range(n_top):
        mx = jnp.max(val, axis=0, keepdims=True)
        first = jnp.min(jnp.where(val == mx, j_f, float(n_blk)), axis=0, keepdims=True)
        hit = j_f == first
        sel_bias = jnp.where(hit, 0.0, sel_bias)
        val = jnp.where(hit, -jnp.inf, val)
    n_sel = mask_ref.shape[2]
    mask_ref[0, 0, 0:n_blk, :] = sel_bias
    if n_blk < n_sel:
        mask_ref[0, 0, n_blk:n_sel, :] = jnp.full((n_sel - n_blk, tq), NEG_INF, F32)


def _nsa_cmp_kernel(q_ref, kc_ref, vc_ref, mt_ref, gt_ref, o_ref, mask_ref, qm, *, tq, hpg, n_top):
    qi = pl.program_id(2)
    q0 = qi * tq
    _stack_group_queries(q_ref, qm, tq)
    nc = kc_ref.shape[3]
    n_sel = mask_ref.shape[2]
    seq = n_sel * NSA_SEL_BLOCK
    cls = (q0 + tq - 1) // (seq // NSA_CMP_CLASSES)
    for c in range(NSA_CMP_CLASSES):
        @pl.when(cls == c)
        def _(c=c):
            _nsa_cmp_body(q0, qm, kc_ref, vc_ref, mt_ref, gt_ref, o_ref, mask_ref, tq=tq, hpg=hpg,
                          n_top=min(n_top, (c + 1) * n_sel // NSA_CMP_CLASSES),
                          n_cmp=(c + 1) * nc // NSA_CMP_CLASSES,
                          n_blk=(c + 1) * n_sel // NSA_CMP_CLASSES)


def _nsa_win_kernel(q_ref, k_ref, v_ref, gt_ref, o_ref, qm, acc_ref, *, tq, hpg, span):
    g = pl.program_id(1)
    qi = pl.program_id(2)
    q0 = qi * tq
    _stack_group_queries(q_ref, qm, tq)
    start = pl.multiple_of(jnp.maximum(q0 - NSA_WINDOW, 0), tq)
    k = k_ref[pl.ds(start, span), :]
    v = v_ref[pl.ds(start, span), :]
    s = lax.dot_general(k, qm[...], NT_DIMS, preferred_element_type=F32)
    k_idx = start + lax.broadcasted_iota(jnp.int32, (span, tq), 0)
    q_idx = q0 + lax.broadcasted_iota(jnp.int32, (span, tq), 1)
    dist = q_idx - k_idx
    bias = jnp.where(dist >= 0, jnp.where(dist < NSA_WINDOW, 0.0, NEG_INF), NEG_INF)
    s = s + jnp.concatenate([bias] * hpg, axis=1)
    m = jnp.max(s, axis=0, keepdims=True)
    e = jnp.exp2(s - m)
    r = 1.0 / jnp.maximum(jnp.sum(e, axis=0, keepdims=True), 1e-30)
    acc_ref[...] = lax.dot_general(v, e.astype(BF16), TN_DIMS, preferred_element_type=F32) * r
    ot = acc_ref[pl.ds(pl.multiple_of(g * HEAD_DIM, HEAD_DIM), HEAD_DIM), :]
    tiles = _gated_pairs_to_token_major(ot, gt_ref, 2, tq, hpg)
    for j, tile in enumerate(tiles):
        o_ref[:, j * LANES:(j + 1) * LANES] = tile.astype(o_ref.dtype)


def _nsa_sel_kernel(q_ref, k_ref, v_ref, mask_ref, gt_ref, oc_ref, ow_ref, o_ref,
                    qm, *scratch, tq, tk, hpg):
    g = pl.program_id(1)
    qi = pl.program_id(2)
    q0 = qi * tq
    _stack_group_queries(q_ref, qm, tq)
    acc_ref, l_ref = scratch[-1], scratch[-2]
    blocks = tk // NSA_SEL_BLOCK

    def scores_full(t):
        k = k_ref[pl.ds(pl.multiple_of(t * tk, tk), tk), :]
        s = lax.dot_general(k, qm[...], NT_DIMS, preferred_element_type=F32)
        rows = mask_ref[0, 0, pl.ds(pl.multiple_of(t * blocks, blocks), blocks), :]
        exp_rows = [jnp.broadcast_to(rows[r:r + 1, :], (NSA_SEL_BLOCK, tq)) for r in range(blocks)]
        bias = jnp.concatenate(exp_rows, axis=0)
        return s + jnp.concatenate([bias] * hpg, axis=1)

    def scores_diag(t):
        s = scores_full(t)
        k_idx = t * tk + lax.broadcasted_iota(jnp.int32, s.shape, 0)
        q_idx = q0 + lax.broadcasted_iota(jnp.int32, s.shape, 1) % tq
        return jnp.where(k_idx <= q_idx, s, NEG_INF)

    def v_tile(t):
        return v_ref[pl.ds(pl.multiple_of(t * tk, tk), tk), :]

    _pipelined_attention(q0 // tk, scores_full, scores_diag, v_tile, scratch)

    inv_l = 1.0 / jnp.maximum(l_ref[...], 1e-30)
    ot = acc_ref[pl.ds(pl.multiple_of(g * HEAD_DIM, HEAD_DIM), HEAD_DIM), :] * inv_l
    tiles = _gated_pairs_to_token_major(ot, gt_ref, 1, tq, hpg)
    for j, tile in enumerate(tiles):
        sl = slice(j * LANES, (j + 1) * LANES)
        total = tile + oc_ref[:, sl].astype(F32) + ow_ref[:, sl].astype(F32)
        o_ref[:, sl] = total.astype(o_ref.dtype)


def _nsa_attention(proj, gates_t, cmp_kv, batch, seq, d_model, *, tq=128, tq_sel=256, tk=512):
    assert tk % tq_sel == 0 and seq % tk == 0
    t = proj.shape[0]
    groups = NSA_KV_GROUPS
    hpg = d_model // HEAD_DIM // groups
    gw = hpg * HEAD_DIM
    nq = seq // tq
    nq_sel = seq // tq_sel
    nc = cmp_kv.shape[3]
    n_sel = seq // NSA_SEL_BLOCK
    n_top = min(NSA_TOP_N, n_sel)
    ng = gates_t.shape[0]
    span = NSA_WINDOW + tq
    qcb = d_model // LANES

    ci = jnp.arange(nc)[None, :] * NSA_CMP_STRIDE
    sj = jnp.arange(n_sel)[:, None] * NSA_SEL_BLOCK
    mt = ((ci < sj + NSA_SEL_BLOCK) & (ci + NSA_CMP_BLOCK > sj)
          & (jnp.arange(nc)[None, :] < nc - 1)).astype(BF16)

    q_spec = pl.BlockSpec((tq, gw), lambda b, g, i: (b * nq + i, g))
    gt_spec = pl.BlockSpec((ng, tq), lambda b, g, i: (0, b * nq + i))
    o_spec = pl.BlockSpec((tq, gw), lambda b, g, i: (b * nq + i, g))
    o_shape = jax.ShapeDtypeStruct((t, d_model), BF16)
    sem = ("arbitrary", "arbitrary", "arbitrary")
    qm_scr = pltpu.VMEM((hpg * tq, LANES), BF16)

    o_cmp, sel_mask = pl.pallas_call(
        functools.partial(_nsa_cmp_kernel, tq=tq, hpg=hpg, n_top=n_top),
        grid=(batch, groups, nq),
        in_specs=[
            q_spec,
            pl.BlockSpec((1, 1, 1, nc, LANES), lambda b, g, i: (b, 0, g, 0, 0)),
            pl.BlockSpec((1, 1, 1, nc, LANES), lambda b, g, i: (b, 1, g, 0, 0)),
            pl.BlockSpec((n_sel, nc), lambda b, g, i: (0, 0)),
            gt_spec,
        ],
        out_specs=[o_spec, pl.BlockSpec((1, 1, n_sel, tq), lambda b, g, i: (b, g, 0, i))],
        out_shape=[o_shape, jax.ShapeDtypeStruct((batch, groups, n_sel, seq), F32)],
        scratch_shapes=[qm_scr],
        compiler_params=_cparams(*sem),
        name="nsa_cmp",
    )(proj, cmp_kv, cmp_kv, mt, gates_t)

    o_win = pl.pallas_call(
        functools.partial(_nsa_win_kernel, tq=tq, hpg=hpg, span=span),
        grid=(batch, groups, nq),
        in_specs=[
            q_spec,
            pl.BlockSpec((seq, LANES), lambda b, g, i: (b, qcb + 5 + g)),
            pl.BlockSpec((seq, LANES), lambda b, g, i: (b, qcb + 7)),
            gt_spec,
        ],
        out_specs=o_spec,
        out_shape=o_shape,
        scratch_shapes=[qm_scr, pltpu.VMEM((LANES, hpg * tq), F32)],
        compiler_params=_cparams(*sem),
        name="nsa_win",
    )(proj, proj, proj, gates_t)

    rows_sel = pl.BlockSpec((tq_sel, gw), lambda b, g, i: (b * nq_sel + i, g))
    return pl.pallas_call(
        functools.partial(_nsa_sel_kernel, tq=tq_sel, tk=tk, hpg=hpg),
        grid=(batch, groups, nq_sel),
        in_specs=[
            rows_sel,
            pl.BlockSpec((seq, LANES), lambda b, g, i: (b, qcb + 2 + g)),
            pl.BlockSpec((seq, LANES), lambda b, g, i: (b, qcb + 4)),
            pl.BlockSpec((1, 1, n_sel, tq_sel), lambda b, g, i: (b, g, 0, i)),
            pl.BlockSpec((ng, tq_sel), lambda b, g, i: (0, b * nq_sel + i)),
            rows_sel,
            rows_sel,
        ],
        out_specs=rows_sel,
        out_shape=o_shape,
        scratch_shapes=[pltpu.VMEM((hpg * tq_sel, LANES), BF16)] + _attn_scratch(tk, hpg * tq_sel, LANES),
        compiler_params=_cparams(*sem),
        name="nsa_sel",
    )(proj, proj, proj, sel_mask, gates_t, o_cmp, o_win)


def _diff_in_weights(w_in, d_model):
    scale = HEAD_DIM ** -0.5 * LOG2E
    wq = w_in[:, :d_model] * scale
    return jnp.concatenate([wq, w_in[:, d_model:]], axis=1).astype(BF16)


def _nsa_in_weights(w_in, d_model):
    g, hd = NSA_KV_GROUPS, HEAD_DIM
    kvw = g * hd
    scale = hd ** -0.5 * LOG2E
    q = w_in[:, :d_model] * scale
    off = d_model
    sec = [w_in[:, off + i * kvw: off + (i + 1) * kvw] for i in range(6)]
    dup = lambda w: jnp.concatenate([w[:, i * hd:(i + 1) * hd] for i in range(g) for _ in range(2)], axis=1)
    w_main = jnp.concatenate([q, sec[0], sec[1], dup(sec[2]), sec[3], dup(sec[4]), sec[5]], axis=1)
    w_gates = w_in[:, off + 6 * kvw:].T
    return w_main.astype(BF16), w_gates.astype(BF16)


def _router_weights(w_grp, b_grp, w_exp, b_exp):
    d = w_grp.shape[0]
    n_exp = w_exp.shape[0] * w_exp.shape[2]
    we = jnp.transpose(w_exp, (1, 0, 2)).reshape(d, n_exp)
    pad = LANES - n_exp - w_grp.shape[1]
    wr = jnp.concatenate([we, w_grp, jnp.zeros((d, pad), F32)], axis=1).astype(F32)
    br = jnp.concatenate([b_exp.reshape(-1), b_grp, jnp.zeros((pad,), F32)])[None, :].astype(F32)
    return wr, br, n_exp


def kernel(x, c, positions, ada_w, ada_b, norm_g, final_g, diff_w_in, diff_w_out, diff_lambda,
           diff_subln_g, nsa_w_in, nsa_w_out, nsa_cmp_pos, nsa_cmp_w1, nsa_cmp_b1, nsa_cmp_w2,
           moe_w_group, moe_b_group, moe_w_expert, moe_b_expert, moe_w_gate, moe_w_up, moe_w_down):
    batch, seq, d = x.shape
    depth = ada_w.shape[0]
    x2d = x.reshape(batch * seq, d)
    tables = _rope_tables(positions)
    mod = _adaln_mod(c, ada_w, ada_b)

    for i in range(depth):
        sh1, sc1, g1, sh2, sc2, g2 = jnp.split(mod[i], 6, axis=-1)
        j = i // N_MIXERS
        if i % N_MIXERS == 0:
            lambda_init = 0.8 - 0.6 * math.exp(-0.3 * i)
            qkv = _norm_proj(x2d, norm_g[i, 0], sc1, sh1, _diff_in_weights(diff_w_in[j], d), tables,
                             seq, tn=512, plain_tiles=range(2 * d // 512, 3 * d // 512))
            o = _diff_attention(qkv, diff_lambda[j], diff_subln_g[j], batch, seq, lambda_init)
            w_out = diff_w_out[j]
        else:
            w_main, w_gates = _nsa_in_weights(nsa_w_in[j], d)
            proj, gates_t = _norm_proj(x2d, norm_g[i, 0], sc1, sh1, w_main, tables, seq,
                                       tn=256, plain_tiles=(d // 256,), w_gates=w_gates)
            groups = NSA_KV_GROUPS
            kcvc = proj[:, d:d + 2 * groups * HEAD_DIM]
            ch = kcvc.reshape(batch, seq, 2, groups, HEAD_DIM).transpose(0, 2, 3, 1, 4)
            ch = ch.reshape(batch, 2, groups, seq // NSA_CMP_STRIDE, NSA_CMP_STRIDE * HEAD_DIM)
            cmp_kv = _nsa_compress(ch, nsa_cmp_pos[j], nsa_cmp_w1[j], nsa_cmp_b1[j], nsa_cmp_w2[j])
            o = _nsa_attention(proj, gates_t, cmp_kv, batch, seq, d)
            w_out = nsa_w_out[j]
        wr, br, n_exp = _router_weights(moe_w_group[i], moe_b_group[i], moe_w_expert[i], moe_b_expert[i])
        x2d, h2, comb, gid_t = _outproj_router(o, w_out.astype(BF16), x2d, g1, norm_g[i, 1], sc2, sh2,
                                               wr, br, seq, n_exp)
        wgu = jnp.concatenate([moe_w_gate[i], moe_w_up[i]], axis=-1).astype(BF16)
        x2d = _moe(h2, wgu, moe_w_down[i].astype(BF16), comb, gid_t, x2d, g2, final_g, seq,
                   final_norm=(i == depth - 1))
    return x2d.reshape(batch, seq, d)
```

```python
import functools
import math

import jax
import jax.numpy as jnp
from jax import lax
from jax.experimental import pallas as pl
from jax.experimental.pallas import tpu as pltpu

F32 = jnp.float32
BF16 = jnp.bfloat16
HIGHEST = lax.Precision.HIGHEST

ROPE_THETA = 500000.0
ROT_DIM = 16
NORM_EPS = 1e-6
NEG_INF = -1e30
HEAD_DIM = 64
DA_V_DIM = 2 * HEAD_DIM
NSA_KV_GROUPS = 2
NSA_CMP_BLOCK = 32
NSA_CMP_STRIDE = 16
NSA_SEL_BLOCK = 64
NSA_TOP_N = 16
NSA_WINDOW = 512
NSA_FORCE = 1e4
MOE_GROUPS = 4
MOE_EXPERTS_PER_GROUP = 8
MOE_TOP_K = 2
N_MIXERS = 2
LOG2E = 1.4426950408889634

LANES = 128
VMEM_LIMIT_BYTES = 56 * 1024 * 1024

NT_DIMS = (((1,), (1,)), ((), ()))
TN_DIMS = (((0,), (0,)), ((), ()))


def _cparams(*sem):
    return pltpu.CompilerParams(dimension_semantics=sem, vmem_limit_bytes=VMEM_LIMIT_BYTES)


def _mod_kernel(c_ref, w_ref, b_ref, o_ref):
    c = c_ref[...]
    ca = c * jax.nn.sigmoid(c)
    o_ref[0] = jnp.dot(ca, w_ref[0], precision=HIGHEST, preferred_element_type=F32) + b_ref[0]


def _adaln_mod(c, ada_w, ada_b):
    depth, d, n = ada_w.shape
    b = c.shape[0]
    rows = 8
    tn = 1536
    c_pad = jnp.zeros((rows, d), F32).at[:b].set(c)
    out = pl.pallas_call(
        _mod_kernel,
        grid=(depth, n // tn),
        in_specs=[
            pl.BlockSpec((rows, d), lambda i, j: (0, 0)),
            pl.BlockSpec((1, d, tn), lambda i, j: (i, 0, j)),
            pl.BlockSpec((1, 1, tn), lambda i, j: (i, 0, j)),
        ],
        out_specs=pl.BlockSpec((1, rows, tn), lambda i, j: (i, 0, j)),
        out_shape=jax.ShapeDtypeStruct((depth, rows, n), F32),
        compiler_params=_cparams("arbitrary", "arbitrary"),
        name="adaln_mod",
    )(c_pad, ada_w, ada_b.reshape(depth, 1, n))
    return out[:, :b, :]


def _rope_tables(positions):
    half = ROT_DIM // 2
    inv_freq = ROPE_THETA ** (-jnp.arange(0, ROT_DIM, 2, dtype=F32) / ROT_DIM)
    ang = positions.astype(F32).reshape(-1)[:, None] * inv_freq
    cos, sin = jnp.cos(ang), jnp.sin(ang)
    t = ang.shape[0]
    rest = HEAD_DIM - ROT_DIM
    z8 = jnp.zeros((t, half), F32)
    cos_h = jnp.concatenate([cos, cos, jnp.ones((t, rest), F32)], axis=-1)
    sina_h = jnp.concatenate([-sin, z8, jnp.zeros((t, rest), F32)], axis=-1)
    sinb_h = jnp.concatenate([z8, sin, jnp.zeros((t, rest), F32)], axis=-1)
    rep = LANES // HEAD_DIM
    return tuple(jnp.tile(a, (1, rep)) for a in (cos_h, sina_h, sinb_h))


def _rms_modulate(x, g, sc, sh):
    ms = jnp.mean(x * x, axis=-1, keepdims=True)
    y = x * lax.rsqrt(ms + NORM_EPS)
    return (y * g) * (1.0 + sc) + sh


def _proj_kernel(*refs, plain_tiles, with_gates):
    if with_gates:
        (x_ref, g_ref, sc_ref, sh_ref, w_ref, cos_ref, sa_ref, sb_ref, wg_ref,
         o_ref, gt_ref, h_scr) = refs
    else:
        x_ref, g_ref, sc_ref, sh_ref, w_ref, cos_ref, sa_ref, sb_ref, o_ref, h_scr = refs
    j = pl.program_id(1)
    tn = o_ref.shape[1]

    @pl.when(j == 0)
    def _():
        h = _rms_modulate(x_ref[...], g_ref[...], sc_ref[0], sh_ref[0])
        h_scr[...] = h.astype(BF16)
        if with_gates:
            logits = lax.dot_general(wg_ref[...], h_scr[...], NT_DIMS, preferred_element_type=F32)
            gt_ref[...] = jax.nn.sigmoid(logits)

    acc = jnp.dot(h_scr[...], w_ref[...], preferred_element_type=F32)

    is_plain = functools.reduce(jnp.logical_or, [j == t for t in plain_tiles], j < 0)
    cos = jnp.where(is_plain, 1.0, cos_ref[...])
    sa = jnp.where(is_plain, 0.0, sa_ref[...])
    sb = jnp.where(is_plain, 0.0, sb_ref[...])
    shift_up = pltpu.roll(acc, tn - ROT_DIM // 2, 1)
    shift_dn = pltpu.roll(acc, ROT_DIM // 2, 1)
    for c in range(tn // LANES):
        sl = slice(c * LANES, (c + 1) * LANES)
        y = acc[:, sl] * cos + shift_up[:, sl] * sa + shift_dn[:, sl] * sb
        o_ref[:, sl] = y.astype(o_ref.dtype)


def _norm_proj(x2d, g, sc, sh, w, tables, seq, *, tn, plain_tiles, w_gates=None, tm=1024):
    t, d = x2d.shape
    n = w.shape[1]
    with_gates = w_gates is not None
    per_b = seq // tm
    in_specs = [
        pl.BlockSpec((tm, d), lambda i, j: (i, 0)),
        pl.BlockSpec((1, d), lambda i, j: (0, 0)),
        pl.BlockSpec((1, 1, d), lambda i, j: (i // per_b, 0, 0)),
        pl.BlockSpec((1, 1, d), lambda i, j: (i // per_b, 0, 0)),
        pl.BlockSpec((d, tn), lambda i, j: (0, j)),
        pl.BlockSpec((tm, LANES), lambda i, j: (i, 0)),
        pl.BlockSpec((tm, LANES), lambda i, j: (i, 0)),
        pl.BlockSpec((tm, LANES), lambda i, j: (i, 0)),
    ]
    args = [x2d, g.reshape(1, d), sc[:, None, :], sh[:, None, :], w, *tables]
    out_specs = [pl.BlockSpec((tm, tn), lambda i, j: (i, j))]
    out_shape = [jax.ShapeDtypeStruct((t, n), BF16)]
    if with_gates:
        ng = w_gates.shape[0]
        in_specs.append(pl.BlockSpec((ng, d), lambda i, j: (0, 0)))
        args.append(w_gates)
        out_specs.append(pl.BlockSpec((ng, tm), lambda i, j: (0, i)))
        out_shape.append(jax.ShapeDtypeStruct((ng, t), F32))
    res = pl.pallas_call(
        functools.partial(_proj_kernel, plain_tiles=tuple(plain_tiles), with_gates=with_gates),
        grid=(t // tm, n // tn),
        in_specs=in_specs,
        out_specs=out_specs,
        out_shape=out_shape,
        scratch_shapes=[pltpu.VMEM((tm, d), BF16)],
        compiler_params=_cparams("arbitrary", "arbitrary"),
        name="norm_proj",
    )(*args)
    return res if with_gates else res[0]


def _attn_scratch(tk, nq, dv):
    return [
        pltpu.VMEM((2, tk, nq), F32),
        pltpu.VMEM((2, 1, nq), F32),
        pltpu.VMEM((2, tk, nq), BF16),
        pltpu.VMEM((2, 1, nq), F32),
        pltpu.VMEM((1, nq), F32),
        pltpu.VMEM((1, nq), F32),
        pltpu.VMEM((dv, nq), F32),
    ]


def _pipelined_attention(n_full, scores_full, scores_diag, v_tile, scratch, sum_in_v=False):
    s_scr, mx_scr, p_scr, al_scr, m_ref, l_ref, acc_ref = scratch

    def stage_scores(fn, tile, slot):
        s = fn(tile)
        s_scr[slot] = s
        mx_scr[slot] = jnp.max(s, axis=0, keepdims=True)

    def stage_softmax(slot):
        m_prev = m_ref[...]
        m_new = jnp.maximum(m_prev, mx_scr[slot])
        alpha = jnp.exp2(m_prev - m_new)
        p = jnp.exp2(s_scr[slot] - m_new)
        if not sum_in_v:
            l_ref[...] = alpha * l_ref[...] + jnp.sum(p, axis=0, keepdims=True)
        m_ref[...] = m_new
        al_scr[slot] = alpha
        p_scr[slot] = p.astype(BF16)

    def stage_pv(pos, slot):
        tile = jnp.where(pos == 0, n_full, jnp.maximum(pos - 1, 0))
        pv = lax.dot_general(v_tile(tile), p_scr[slot], TN_DIMS, preferred_element_type=F32)
        acc_ref[...] = acc_ref[...] * al_scr[slot] + pv

    def step(j, slot):
        stage_scores(scores_full, j, 1 - slot)
        stage_softmax(slot)
        stage_pv(j - 1, 1 - slot)

    m_ref[...] = jnp.full(m_ref.shape, NEG_INF, F32)
    l_ref[...] = jnp.zeros(l_ref.shape, F32)
    acc_ref[...] = jnp.zeros(acc_ref.shape, F32)
    p_scr[1] = jnp.zeros(p_scr.shape[1:], BF16)
    al_scr[1] = jnp.ones(al_scr.shape[1:], F32)
    stage_scores(scores_diag, n_full, 0)

    def pair(jj, carry):
        step(2 * jj, 0)
        step(2 * jj + 1, 1)
        return carry

    lax.fori_loop(0, n_full // 2, pair, 0)
    odd = n_full % 2 == 1

    @pl.when(odd)
    def _():
        step(n_full - 1, 0)
        stage_softmax(1)
        stage_pv(n_full - 1, 0)
        stage_pv(n_full, 1)

    @pl.when(jnp.logical_not(odd))
    def _():
        stage_softmax(0)
        stage_pv(n_full - 1, 1)
        stage_pv(n_full, 0)


def _split_halves(q, rows):
    lane = lax.broadcasted_iota(jnp.int32, q.shape, 1)
    zero = jnp.zeros_like(q)
    return jnp.where(lane < HEAD_DIM, q, zero), jnp.where(lane >= HEAD_DIM, q, zero)


def _diff_attn_kernel(q_ref, k_ref, v_ref, lam_ref, g_ref, o_ref, qm, *scratch,
                      tq, tk, lambda_init):
    qi = pl.program_id(2)
    lo, hi = _split_halves(q_ref[...], tq)
    qm[0:tq, :] = lo
    qm[tq:2 * tq, :] = hi
    q0 = qi * tq
    acc_ref, l_ref = scratch[-1], scratch[-2]

    def scores_full(t):
        k = k_ref[pl.ds(pl.multiple_of(t * tk, tk), tk), :]
        return lax.dot_general(k, qm[...], NT_DIMS, preferred_element_type=F32)

    def scores_diag(t):
        s = scores_full(t)
        k_idx = t * tk + lax.broadcasted_iota(jnp.int32, s.shape, 0)
        q_idx = q0 + lax.broadcasted_iota(jnp.int32, s.shape, 1) % tq
        return jnp.where(k_idx <= q_idx, s, NEG_INF)

    def v_tile(t):
        return v_ref[pl.ds(pl.multiple_of(t * tk, tk), tk), :]

    _pipelined_attention(q0 // tk, scores_full, scores_diag, v_tile, scratch)

    lam = lam_ref[...]
    lam_full = (jnp.exp(jnp.sum(lam[0:1] * lam[1:2], axis=-1, keepdims=True))
                - jnp.exp(jnp.sum(lam[2:3] * lam[3:4], axis=-1, keepdims=True)) + lambda_init)
    acc = acc_ref[...]
    l = jnp.maximum(l_ref[...], 1e-30)
    ot = acc[:, 0:tq] / l[:, 0:tq] - lam_full * (acc[:, tq:2 * tq] / l[:, tq:2 * tq])
    ms = jnp.mean(ot * ot, axis=0, keepdims=True)
    ot = (ot * lax.rsqrt(ms + NORM_EPS)) * g_ref[...] * (1.0 - lambda_init)
    o_ref[...] = ot.T.astype(o_ref.dtype)


def _diff_attention(qkv, lam, subln_g, batch, seq, lambda_init, *, tq=1024, tk=1024):
    assert tk % tq == 0 and seq % tk == 0
    t, n3 = qkv.shape
    d = n3 // 3
    heads = d // DA_V_DIM
    nq = seq // tq
    cb = d // LANES
    g_b = jnp.broadcast_to(subln_g.astype(F32)[:, None], (DA_V_DIM, tq))
    return pl.pallas_call(
        functools.partial(_diff_attn_kernel, tq=tq, tk=tk, lambda_init=lambda_init),
        grid=(batch, heads, nq),
        in_specs=[
            pl.BlockSpec((tq, LANES), lambda b, h, i: (b * nq + i, h)),
            pl.BlockSpec((seq, LANES), lambda b, h, i: (b, cb + h)),
            pl.BlockSpec((seq, LANES), lambda b, h, i: (b, 2 * cb + h)),
            pl.BlockSpec(lam.shape, lambda b, h, i: (0, 0)),
            pl.BlockSpec((DA_V_DIM, tq), lambda b, h, i: (0, 0)),
        ],
        out_specs=pl.BlockSpec((tq, LANES), lambda b, h, i: (b * nq + i, h)),
        out_shape=jax.ShapeDtypeStruct((t, d), BF16),
        scratch_shapes=[pltpu.VMEM((2 * tq, LANES), BF16)] + _attn_scratch(tk, 2 * tq, DA_V_DIM),
        compiler_params=_cparams("arbitrary", "arbitrary", "arbitrary"),
        name="diff_attn",
    )(qkv, qkv, qkv, lam.astype(F32), g_b)


def _route(logits, n_exp):
    lane = lax.broadcasted_iota(jnp.int32, logits.shape, 1)
    lane_f = lane.astype(F32)
    big = float(LANES)
    is_grp = (lane >= n_exp) & (lane < n_exp + MOE_GROUPS)
    gl = jnp.where(is_grp, logits, -jnp.inf)
    ge = jnp.where(is_grp, jnp.exp(gl - jnp.max(gl, axis=-1, keepdims=True)), 0.0)
    pg = ge / jnp.sum(ge, axis=-1, keepdims=True)
    p_top = jnp.max(pg, axis=-1, keepdims=True)
    grp = jnp.min(jnp.where(is_grp & (pg == p_top), lane_f, big), axis=-1, keepdims=True) - n_exp
    in_grp = (lane < n_exp) & ((lane // MOE_EXPERTS_PER_GROUP).astype(F32) == grp)
    sel = jnp.where(in_grp, logits, -jnp.inf)
    v1 = jnp.max(sel, axis=-1, keepdims=True)
    i1 = jnp.min(jnp.where(sel == v1, lane_f, big), axis=-1, keepdims=True)
    sel2 = jnp.where(lane_f == i1, -jnp.inf, sel)
    v2 = jnp.max(sel2, axis=-1, keepdims=True)
    i2 = jnp.min(jnp.where(sel2 == v2, lane_f, big), axis=-1, keepdims=True)
    e2 = jnp.exp(v2 - v1)
    den = 1.0 + e2
    w1 = (1.0 / den) * p_top
    w2 = (e2 / den) * p_top
    return jnp.where(lane_f == i1, w1, 0.0) + jnp.where(lane_f == i2, w2, 0.0), grp


def _outproj_router_kernel(o_ref, w_ref, x_ref, g1_ref, ng_ref, sc_ref, sh_ref, wr_ref, br_ref,
                           xo_ref, h_ref, comb_ref, gid_ref, *, n_exp):
    mix = jnp.dot(o_ref[...], w_ref[...], preferred_element_type=F32)
    x = x_ref[...] + g1_ref[0] * mix
    xo_ref[...] = x
    h = _rms_modulate(x, ng_ref[...], sc_ref[0], sh_ref[0])
    h_ref[...] = h.astype(BF16)
    logits = jnp.dot(h, wr_ref[...], precision=HIGHEST, preferred_element_type=F32) + br_ref[...]
    comb, grp = _route(logits, n_exp)
    comb_ref[...] = comb[:, 0:n_exp]
    gid_ref[...] = jnp.broadcast_to(grp, logits.shape).T[0:gid_ref.shape[0], :]


def _outproj_router(o, w_out, x2d, g1, ng, sc, sh, wr, br, seq, n_exp, *, tm=512):
    t, d = x2d.shape
    per_b = seq // tm
    bvec = lambda i: (i // per_b, 0, 0)
    return pl.pallas_call(
        functools.partial(_outproj_router_kernel, n_exp=n_exp),
        grid=(t // tm,),
        in_specs=[
            pl.BlockSpec((tm, d), lambda i: (i, 0)),
            pl.BlockSpec((d, d), lambda i: (0, 0)),
            pl.BlockSpec((tm, d), lambda i: (i, 0)),
            pl.BlockSpec((1, 1, d), bvec),
            pl.BlockSpec((1, d), lambda i: (0, 0)),
            pl.BlockSpec((1, 1, d), bvec),
            pl.BlockSpec((1, 1, d), bvec),
            pl.BlockSpec((d, LANES), lambda i: (0, 0)),
            pl.BlockSpec((1, LANES), lambda i: (0, 0)),
        ],
        out_specs=[
            pl.BlockSpec((tm, d), lambda i: (i, 0)),
            pl.BlockSpec((tm, d), lambda i: (i, 0)),
            pl.BlockSpec((tm, n_exp), lambda i: (i, 0)),
            pl.BlockSpec((8, tm), lambda i: (0, i)),
        ],
        out_shape=[
            jax.ShapeDtypeStruct((t, d), F32),
            jax.ShapeDtypeStruct((t, d), BF16),
            jax.ShapeDtypeStruct((t, n_exp), F32),
            jax.ShapeDtypeStruct((8, t), F32),
        ],
        compiler_params=_cparams("arbitrary"),
        name="outproj_router",
    )(o, w_out, x2d, g1[:, None, :], ng.reshape(1, d), sc[:, None, :], sh[:, None, :], wr, br)


MOE_CHUNK = 128
MOE_EXPERTS_PER_STEP = 4


def _moe_kernel(h_ref, wgu_ref, wd_ref, comb_ref, gid_ref, tri_ref, x_ref, g2_ref, fg_ref, o_ref,
                perm_scr, hs_scr, cs_scr, ys_scr, off_smem, nch_smem, *, hidden, final_norm):
    step = pl.program_id(1)
    tm = h_ref.shape[0]
    rows = perm_scr.shape[0]
    n_exp = comb_ref.shape[1]
    steps_per_group = MOE_EXPERTS_PER_GROUP // MOE_EXPERTS_PER_STEP

    @pl.when(step == 0)
    def _():
        gid = gid_ref[0:1, :]
        grow = lax.broadcasted_iota(jnp.int32, (8, tm), 0).astype(F32)
        onehot = jnp.where(grow == gid, 1.0, 0.0)
        rank = jnp.dot(onehot.astype(BF16), tri_ref[...], preferred_element_type=F32)
        off = jnp.int32(0)
        off_col = jnp.zeros((8, 1), F32)
        grow_col = lax.broadcasted_iota(jnp.int32, (8, 1), 0)
        for g in range(MOE_GROUPS):
            cnt = jnp.sum(onehot[g:g + 1, :]).astype(jnp.int32)
            nch = (cnt + (MOE_CHUNK - 1)) // MOE_CHUNK
            off_smem[g] = off
            nch_smem[g] = nch
            off_col = jnp.where(grow_col == g, off.astype(F32), off_col)
            off = off + nch * MOE_CHUNK
        dest = jnp.sum(onehot * (rank + off_col), axis=0, keepdims=True)
        row_id = lax.broadcasted_iota(jnp.int32, (rows, tm), 0)
        perm = jnp.where(row_id == dest.astype(jnp.int32), 1.0, 0.0).astype(BF16)
        perm_scr[...] = perm
        hs_scr[...] = jnp.dot(perm, h_ref[...], preferred_element_type=F32).astype(BF16)
        comb = comb_ref[...]
        comb_hi = comb.astype(BF16)
        comb_lo = (comb - comb_hi.astype(F32)).astype(BF16)
        cs_scr[...] = (jnp.dot(perm, comb_hi, preferred_element_type=F32)
                       + jnp.dot(perm, comb_lo, preferred_element_type=F32))
        ys_scr[...] = jnp.zeros(ys_scr.shape, F32)

    g = step // steps_per_group
    off = off_smem[g]
    lane = lax.broadcasted_iota(jnp.int32, (MOE_CHUNK, n_exp), 1)

    def chunk(c, carry):
        r0 = pl.multiple_of(off + c * MOE_CHUNK, MOE_CHUNK)
        xs = hs_scr[pl.ds(r0, MOE_CHUNK), :]
        cs = cs_scr[pl.ds(r0, MOE_CHUNK), :]
        y = jnp.zeros((MOE_CHUNK, xs.shape[1]), F32)
        for k in range(MOE_EXPERTS_PER_STEP):
            gu = jnp.dot(xs, wgu_ref[k], preferred_element_type=F32)
            gate, up = gu[:, 0:hidden], gu[:, hidden:2 * hidden]
            a = (gate * jax.nn.sigmoid(gate)) * up
            e = step * MOE_EXPERTS_PER_STEP + k
            ce = jnp.sum(jnp.where(lane == e, cs, 0.0), axis=-1, keepdims=True)
            y = y + jnp.dot((a * ce).astype(BF16), wd_ref[k], preferred_element_type=F32)
        ys_scr[pl.ds(r0, MOE_CHUNK), :] += y
        return carry

    lax.fori_loop(0, nch_smem[g], chunk, 0)

    @pl.when(step == pl.num_programs(1) - 1)
    def _():
        y = lax.dot_general(perm_scr[...], ys_scr[...].astype(BF16), TN_DIMS,
                            preferred_element_type=F32)
        x = x_ref[...] + g2_ref[0] * y
        if final_norm:
            ms = jnp.mean(x * x, axis=-1, keepdims=True)
            x = (x * lax.rsqrt(ms + NORM_EPS)) * fg_ref[...]
        o_ref[...] = x


def _moe(h, wgu, wd, comb, gid_t, x2d, g2, final_g, seq, *, final_norm, tm=1024):
    t, d = x2d.shape
    n_exp, _, two_h = wgu.shape
    hidden = two_h // 2
    per_b = seq // tm
    eps = MOE_EXPERTS_PER_STEP
    rows = tm + MOE_GROUPS * MOE_CHUNK
    idx = jnp.arange(tm)
    tri = (idx[:, None] < idx[None, :]).astype(BF16)
    return pl.pallas_call(
        functools.partial(_moe_kernel, hidden=hidden, final_norm=final_norm),
        grid=(t // tm, n_exp // eps),
        in_specs=[
            pl.BlockSpec((tm, d), lambda i, s: (i, 0)),
            pl.BlockSpec((eps, d, two_h), lambda i, s: (s, 0, 0)),
            pl.BlockSpec((eps, hidden, d), lambda i, s: (s, 0, 0)),
            pl.BlockSpec((tm, n_exp), lambda i, s: (i, 0)),
            pl.BlockSpec((8, tm), lambda i, s: (0, i)),
            pl.BlockSpec((tm, tm), lambda i, s: (0, 0)),
            pl.BlockSpec((tm, d), lambda i, s: (i, 0)),
            pl.BlockSpec((1, 1, d), lambda i, s: (i // per_b, 0, 0)),
            pl.BlockSpec((1, d), lambda i, s: (0, 0)),
        ],
        out_specs=pl.BlockSpec((tm, d), lambda i, s: (i, 0)),
        out_shape=jax.ShapeDtypeStruct((t, d), F32),
        scratch_shapes=[
            pltpu.VMEM((rows, tm), BF16),
            pltpu.VMEM((rows, d), BF16),
            pltpu.VMEM((rows, n_exp), F32),
            pltpu.VMEM((rows, d), F32),
            pltpu.SMEM((MOE_GROUPS,), jnp.int32),
            pltpu.SMEM((MOE_GROUPS,), jnp.int32),
        ],
        compiler_params=_cparams("arbitrary", "arbitrary"),
        name="moe",
    )(h, wgu, wd, comb, gid_t, tri, x2d, g2[:, None, :], final_g.reshape(1, d))


def _compress_kernel(ch_ref, w1a_ref, w1b_ref, pos_ref, w1f_ref, b1_ref, w2_ref, o_ref):
    ch = ch_ref[0, 0, 0]
    n = ch.shape[0]
    a = jnp.dot(ch, w1a_ref[0], preferred_element_type=F32)
    b = jnp.dot(ch, w1b_ref[0], preferred_element_type=F32)
    const = jnp.dot(pos_ref[0], w1f_ref[0], precision=HIGHEST, preferred_element_type=F32)[0:1]
    hid = a + pltpu.roll(b, n - 1, 0) + (const + b1_ref[0])
    act = jax.nn.gelu(hid, approximate=True)
    o_ref[0, 0, 0] = jnp.dot(act.astype(BF16), w2_ref[0], preferred_element_type=F32).astype(o_ref.dtype)


def _nsa_compress(ch, cmp_pos, cmp_w1, cmp_b1, cmp_w2):
    b, two, g, nc, f = ch.shape
    hid = cmp_w1.shape[-1]
    w1a = cmp_w1[:, :f, :].astype(BF16)
    w1b = cmp_w1[:, f:, :].astype(BF16)
    pos = jnp.zeros((two, 8, 2 * f), F32).at[:, 0, :].set(cmp_pos.reshape(two, 2 * f))
    w2 = jnp.concatenate([cmp_w2, cmp_w2], axis=-1).astype(BF16)
    return pl.pallas_call(
        _compress_kernel,
        grid=(b, two, g),
        in_specs=[
            pl.BlockSpec((1, 1, 1, nc, f), lambda i, j, k: (i, j, k, 0, 0)),
            pl.BlockSpec((1, f, hid), lambda i, j, k: (j, 0, 0)),
            pl.BlockSpec((1, f, hid), lambda i, j, k: (j, 0, 0)),
            pl.BlockSpec((1, 8, 2 * f), lambda i, j, k: (j, 0, 0)),
            pl.BlockSpec((1, 2 * f, hid), lambda i, j, k: (j, 0, 0)),
            pl.BlockSpec((1, 1, hid), lambda i, j, k: (j, 0, 0)),
            pl.BlockSpec((1, hid, LANES), lambda i, j, k: (j, 0, 0)),
        ],
        out_specs=pl.BlockSpec((1, 1, 1, nc, LANES), lambda i, j, k: (i, j, k, 0, 0)),
        out_shape=jax.ShapeDtypeStruct((b, two, g, nc, LANES), BF16),
        compiler_params=_cparams("arbitrary", "arbitrary", "arbitrary"),
        name="nsa_compress",
    )(ch, w1a, w1b, pos, cmp_w1.astype(F32), cmp_b1[:, None, :].astype(F32), w2)


def _stack_group_queries(q_ref, qm, tq):
    for j in range(q_ref.shape[1] // LANES):
        lo, hi = _split_halves(q_ref[:, j * LANES:(j + 1) * LANES], tq)
        qm[(2 * j) * tq:(2 * j + 1) * tq, :] = lo
        qm[(2 * j + 1) * tq:(2 * j + 2) * tq, :] = hi


def _own_group_values_with_ones(v, g):
    lane = lax.broadcasted_iota(jnp.int32, v.shape, 1)
    own_half = (lane >= HEAD_DIM) == (g == 1)
    return jnp.where(own_half, v, jnp.ones_like(v))


def _gated_pairs_to_token_major(ot, gt_ref, branch, tq, hpg):
    g = pl.program_id(1)
    tiles = []
    for j in range(hpg // 2):
        parts = []
        for h in (2 * j, 2 * j + 1):
            row = gt_ref[pl.ds((g * hpg + h) * 3 + branch, 1), :]
            parts.append(ot[:, h * tq:(h + 1) * tq] * row)
        tiles.append(jnp.concatenate(parts, axis=0).T)
    return tiles


NSA_CMP_CLASSES = 4


def _nsa_window_tiles(q0, qm, kw_ref, vw_ref, gt_ref, acc_ref, *, tq, hpg):
    g = pl.program_id(1)
    span = NSA_WINDOW + tq
    start = pl.multiple_of(jnp.maximum(q0 - NSA_WINDOW, 0), tq)
    k = kw_ref[pl.ds(start, span), :]
    v = vw_ref[pl.ds(start, span), :]
    s = lax.dot_general(k, qm[...], NT_DIMS, preferred_element_type=F32)
    k_idx = start + lax.broadcasted_iota(jnp.int32, (span, tq), 0)
    q_idx = q0 + lax.broadcasted_iota(jnp.int32, (span, tq), 1)
    dist = q_idx - k_idx
    bias = jnp.where(dist >= 0, jnp.where(dist < NSA_WINDOW, 0.0, NEG_INF), NEG_INF)
    s = s + jnp.concatenate([bias] * hpg, axis=1)
    m = jnp.max(s, axis=0, keepdims=True)
    e = jnp.exp2(s - m)
    r = 1.0 / jnp.maximum(jnp.sum(e, axis=0, keepdims=True), 1e-30)
    acc_ref[...] = lax.dot_general(v, e.astype(BF16), TN_DIMS, preferred_element_type=F32) * r
    ot = acc_ref[pl.ds(pl.multiple_of(g * HEAD_DIM, HEAD_DIM), HEAD_DIM), :]
    return _gated_pairs_to_token_major(ot, gt_ref, 2, tq, hpg)


def _nsa_cmp_body(q0, qm, kc_ref, vc_ref, mt_ref, gt_ref, kw_ref, vw_ref, o_ref, mask_ref, acc_ref,
                  *, tq, hpg, n_top, n_cmp, n_blk):
    win_tiles = _nsa_window_tiles(q0, qm, kw_ref, vw_ref, gt_ref, acc_ref, tq=tq, hpg=hpg)
    kc = kc_ref[0, 0, 0, 0:n_cmp, :]
    s = lax.dot_general(kc, qm[...], NT_DIMS, preferred_element_type=F32)
    n_idx = lax.broadcasted_iota(jnp.int32, (n_cmp, tq), 0)
    q_idx = q0 + lax.broadcasted_iota(jnp.int32, (n_cmp, tq), 1)
    valid = n_idx * NSA_CMP_STRIDE + (NSA_CMP_BLOCK - 1) <= q_idx
    bias = jnp.where(valid, 0.0, NEG_INF)
    s = s + jnp.concatenate([bias] * hpg, axis=1)
    m = jnp.maximum(jnp.max(s, axis=0, keepdims=True), 0.1 * NEG_INF)
    e = jnp.exp2(s - m)
    r = 1.0 / jnp.maximum(jnp.sum(e, axis=0, keepdims=True), 1e-30)
    p = e * r
    ot = lax.dot_general(vc_ref[0, 0, 0, 0:n_cmp, :], p.astype(BF16), TN_DIMS,
                         preferred_element_type=F32)
    tiles = _gated_pairs_to_token_major(ot[0:HEAD_DIM], gt_ref, 0, tq, hpg)
    for j, (tile, win) in enumerate(zip(tiles, win_tiles)):
        o_ref[:, j * LANES:(j + 1) * LANES] = (tile + win).astype(o_ref.dtype)

    psum = p[:, 0:tq]
    for h in range(1, hpg):
        psum = psum + p[:, h * tq:(h + 1) * tq]
    p_hi = psum.astype(BF16)
    p_lo = (psum - p_hi.astype(F32)).astype(BF16)
    mt = mt_ref[0:n_blk, 0:n_cmp]
    imp = (jnp.dot(mt, p_hi, preferred_element_type=F32)
           + jnp.dot(mt, p_lo, preferred_element_type=F32))
    j_idx = lax.broadcasted_iota(jnp.int32, imp.shape, 0)
    j_f = j_idx.astype(F32)
    q_blk = (q0 + lax.broadcasted_iota(jnp.int32, imp.shape, 1)) // NSA_SEL_BLOCK
    forced = (j_idx == 0) | (j_idx == q_blk) | (j_idx == q_blk - 1)
    val = jnp.where(forced, NSA_FORCE, jnp.where(j_idx <= q_blk, imp, -NSA_FORCE))
    sel_bias = jnp.full(imp.shape, NEG_INF, F32)
    for _ in range(n_top):
        mx = jnp.max(val, axis=0, keepdims=True)
        first = jnp.min(jnp.where(val == mx, j_f, float(n_blk)), axis=0, keepdims=True)
        hit = j_f == first
        sel_bias = jnp.where(hit, 0.0, sel_bias)
        val = jnp.where(hit, -jnp.inf, val)
    n_sel = mask_ref.shape[2]
    mask_ref[0, 0, 0:n_blk, :] = sel_bias
    if n_blk < n_sel:
        mask_ref[0, 0, n_blk:n_sel, :] = jnp.full((n_sel - n_blk, tq), NEG_INF, F32)


def _nsa_cmp_win_kernel(q_ref, kc_ref, vc_ref, mt_ref, gt_ref, kw_ref, vw_ref, o_ref, mask_ref,
                        qm, acc_ref, *, tq, hpg, n_top):
    qi = pl.program_id(2)
    q0 = qi * tq
    _stack_group_queries(q_ref, qm, tq)
    nc = kc_ref.shape[3]
    n_sel = mask_ref.shape[2]
    seq = n_sel * NSA_SEL_BLOCK
    cls = (q0 + tq - 1) // (seq // NSA_CMP_CLASSES)
    for c in range(NSA_CMP_CLASSES):
        @pl.when(cls == c)
        def _(c=c):
            _nsa_cmp_body(q0, qm, kc_ref, vc_ref, mt_ref, gt_ref, kw_ref, vw_ref, o_ref, mask_ref,
                          acc_ref, tq=tq, hpg=hpg,
                          n_top=min(n_top, (c + 1) * n_sel // NSA_CMP_CLASSES),
                          n_cmp=(c + 1) * nc // NSA_CMP_CLASSES,
                          n_blk=(c + 1) * n_sel // NSA_CMP_CLASSES)


def _nsa_sel_kernel(q_ref, k_ref, v_ref, mask_ref, gt_ref, ocw_ref, o_ref,
                    qm, *scratch, tq, tk, hpg):
    g = pl.program_id(1)
    qi = pl.program_id(2)
    q0 = qi * tq
    _stack_group_queries(q_ref, qm, tq)
    acc_ref, l_ref = scratch[-1], scratch[-2]
    blocks = tk // NSA_SEL_BLOCK

    def scores_full(t):
        k = k_ref[pl.ds(pl.multiple_of(t * tk, tk), tk), :]
        s = lax.dot_general(k, qm[...], NT_DIMS, preferred_element_type=F32)
        rows = mask_ref[0, 0, pl.ds(pl.multiple_of(t * blocks, blocks), blocks), :]
        exp_rows = [jnp.broadcast_to(rows[r:r + 1, :], (NSA_SEL_BLOCK, tq)) for r in range(blocks)]
        bias = jnp.concatenate(exp_rows, axis=0)
        return s + jnp.concatenate([bias] * hpg, axis=1)

    def scores_diag(t):
        s = scores_full(t)
        k_idx = t * tk + lax.broadcasted_iota(jnp.int32, s.shape, 0)
        q_idx = q0 + lax.broadcasted_iota(jnp.int32, s.shape, 1) % tq
        return jnp.where(k_idx <= q_idx, s, NEG_INF)

    def v_tile(t):
        return _own_group_values_with_ones(v_ref[pl.ds(pl.multiple_of(t * tk, tk), tk), :], g)

    _pipelined_attention(q0 // tk, scores_full, scores_diag, v_tile, scratch, sum_in_v=True)

    own = pl.multiple_of(g * HEAD_DIM, HEAD_DIM)
    other = pl.multiple_of((1 - g) * HEAD_DIM, HEAD_DIM)
    inv_l = 1.0 / jnp.maximum(acc_ref[pl.ds(other, 1), :], 1e-30)
    ot = acc_ref[pl.ds(own, HEAD_DIM), :] * inv_l
    tiles = _gated_pairs_to_token_major(ot, gt_ref, 1, tq, hpg)
    for j, tile in enumerate(tiles):
        sl = slice(j * LANES, (j + 1) * LANES)
        total = tile + ocw_ref[:, sl].astype(F32)
        o_ref[:, sl] = total.astype(o_ref.dtype)


def _nsa_attention(proj, gates_t, cmp_kv, batch, seq, d_model, *, tq=128, tq_sel=512, tk=512):
    assert tk % tq_sel == 0 and seq % tk == 0
    t = proj.shape[0]
    groups = NSA_KV_GROUPS
    hpg = d_model // HEAD_DIM // groups
    gw = hpg * HEAD_DIM
    nq = seq // tq
    nq_sel = seq // tq_sel
    nc = cmp_kv.shape[3]
    n_sel = seq // NSA_SEL_BLOCK
    n_top = min(NSA_TOP_N, n_sel)
    ng = gates_t.shape[0]
    qcb = d_model // LANES

    ci = jnp.arange(nc)[None, :] * NSA_CMP_STRIDE
    sj = jnp.arange(n_sel)[:, None] * NSA_SEL_BLOCK
    mt = ((ci < sj + NSA_SEL_BLOCK) & (ci + NSA_CMP_BLOCK > sj)
          & (jnp.arange(nc)[None, :] < nc - 1)).astype(BF16)

    q_spec = pl.BlockSpec((tq, gw), lambda b, g, i: (b * nq + i, g))
    gt_spec = pl.BlockSpec((ng, tq), lambda b, g, i: (0, b * nq + i))
    o_spec = pl.BlockSpec((tq, gw), lambda b, g, i: (b * nq + i, g))
    o_shape = jax.ShapeDtypeStruct((t, d_model), BF16)
    sem = ("arbitrary", "arbitrary", "arbitrary")
    qm_scr = pltpu.VMEM((hpg * tq, LANES), BF16)

    o_cmp_win, sel_mask = pl.pallas_call(
        functools.partial(_nsa_cmp_win_kernel, tq=tq, hpg=hpg, n_top=n_top),
        grid=(batch, groups, nq),
        in_specs=[
            q_spec,
            pl.BlockSpec((1, 1, 1, nc, LANES), lambda b, g, i: (b, 0, g, 0, 0)),
            pl.BlockSpec((1, 1, 1, nc, LANES), lambda b, g, i: (b, 1, g, 0, 0)),
            pl.BlockSpec((n_sel, nc), lambda b, g, i: (0, 0)),
            gt_spec,
            pl.BlockSpec((seq, LANES), lambda b, g, i: (b, qcb + 5 + g)),
            pl.BlockSpec((seq, LANES), lambda b, g, i: (b, qcb + 7)),
        ],
        out_specs=[o_spec, pl.BlockSpec((1, 1, n_sel, tq), lambda b, g, i: (b, g, 0, i))],
        out_shape=[o_shape, jax.ShapeDtypeStruct((batch, groups, n_sel, seq), F32)],
        scratch_shapes=[qm_scr, pltpu.VMEM((LANES, hpg * tq), F32)],
        compiler_params=_cparams(*sem),
        name="nsa_cmp_win",
    )(proj, cmp_kv, cmp_kv, mt, gates_t, proj, proj)

    rows_sel = pl.BlockSpec((tq_sel, gw), lambda b, g, i: (b * nq_sel + i, g))
    return pl.pallas_call(
        functools.partial(_nsa_sel_kernel, tq=tq_sel, tk=tk, hpg=hpg),
        grid=(batch, groups, nq_sel),
        in_specs=[
            rows_sel,
            pl.BlockSpec((seq, LANES), lambda b, g, i: (b, qcb + 2 + g)),
            pl.BlockSpec((seq, LANES), lambda b, g, i: (b, qcb + 4)),
            pl.BlockSpec((1, 1, n_sel, tq_sel), lambda b, g, i: (b, g, 0, i)),
            pl.BlockSpec((ng, tq_sel), lambda b, g, i: (0, b * nq_sel + i)),
            rows_sel,
        ],
        out_specs=rows_sel,
        out_shape=o_shape,
        scratch_shapes=[pltpu.VMEM((hpg * tq_sel, LANES), BF16)] + _attn_scratch(tk, hpg * tq_sel, LANES),
        compiler_params=_cparams(*sem),
        name="nsa_sel",
    )(proj, proj, proj, sel_mask, gates_t, o_cmp_win)


def _diff_in_weights(w_in, d_model):
    scale = HEAD_DIM ** -0.5 * LOG2E
    wq = w_in[:, :d_model] * scale
    return jnp.concatenate([wq, w_in[:, d_model:]], axis=1).astype(BF16)


def _nsa_in_weights(w_in, d_model):
    g, hd = NSA_KV_GROUPS, HEAD_DIM
    kvw = g * hd
    scale = hd ** -0.5 * LOG2E
    q = w_in[:, :d_model] * scale
    off = d_model
    sec = [w_in[:, off + i * kvw: off + (i + 1) * kvw] for i in range(6)]
    dup = lambda w: jnp.concatenate([w[:, i * hd:(i + 1) * hd] for i in range(g) for _ in range(2)], axis=1)
    w_main = jnp.concatenate([q, sec[0], sec[1], dup(sec[2]), sec[3], dup(sec[4]), sec[5]], axis=1)
    w_gates = w_in[:, off + 6 * kvw:].T
    return w_main.astype(BF16), w_gates.astype(BF16)


def _router_weights(w_grp, b_grp, w_exp, b_exp):
    d = w_grp.shape[0]
    n_exp = w_exp.shape[0] * w_exp.shape[2]
    we = jnp.transpose(w_exp, (1, 0, 2)).reshape(d, n_exp)
    pad = LANES - n_exp - w_grp.shape[1]
    wr = jnp.concatenate([we, w_grp, jnp.zeros((d, pad), F32)], axis=1).astype(F32)
    br = jnp.concatenate([b_exp.reshape(-1), b_grp, jnp.zeros((pad,), F32)])[None, :].astype(F32)
    return wr, br, n_exp


def kernel(x, c, positions, ada_w, ada_b, norm_g, final_g, diff_w_in, diff_w_out, diff_lambda,
           diff_subln_g, nsa_w_in, nsa_w_out, nsa_cmp_pos, nsa_cmp_w1, nsa_cmp_b1, nsa_cmp_w2,
           moe_w_group, moe_b_group, moe_w_expert, moe_b_expert, moe_w_gate, moe_w_up, moe_w_down):
    batch, seq, d = x.shape
    depth = ada_w.shape[0]
    x2d = x.reshape(batch * seq, d)
    tables = _rope_tables(positions)
    mod = _adaln_mod(c, ada_w, ada_b)

    for i in range(depth):
        sh1, sc1, g1, sh2, sc2, g2 = jnp.split(mod[i], 6, axis=-1)
        j = i // N_MIXERS
        if i % N_MIXERS == 0:
            lambda_init = 0.8 - 0.6 * math.exp(-0.3 * i)
            qkv = _norm_proj(x2d, norm_g[i, 0], sc1, sh1, _diff_in_weights(diff_w_in[j], d), tables,
                             seq, tn=512, plain_tiles=range(2 * d // 512, 3 * d // 512))
            o = _diff_attention(qkv, diff_lambda[j], diff_subln_g[j], batch, seq, lambda_init)
            w_out = diff_w_out[j]
        else:
            w_main, w_gates = _nsa_in_weights(nsa_w_in[j], d)
            proj, gates_t = _norm_proj(x2d, norm_g[i, 0], sc1, sh1, w_main, tables, seq,
                                       tn=256, plain_tiles=(d // 256,), w_gates=w_gates)
            groups = NSA_KV_GROUPS
            kcvc = proj[:, d:d + 2 * groups * HEAD_DIM]
            ch = kcvc.reshape(batch, seq, 2, groups, HEAD_DIM).transpose(0, 2, 3, 1, 4)
            ch = ch.reshape(batch, 2, groups, seq // NSA_CMP_STRIDE, NSA_CMP_STRIDE * HEAD_DIM)
            cmp_kv = _nsa_compress(ch, nsa_cmp_pos[j], nsa_cmp_w1[j], nsa_cmp_b1[j], nsa_cmp_w2[j])
            o = _nsa_attention(proj, gates_t, cmp_kv, batch, seq, d)
            w_out = nsa_w_out[j]
        wr, br, n_exp = _router_weights(moe_w_group[i], moe_b_group[i], moe_w_expert[i], moe_b_expert[i])
        x2d, h2, comb, gid_t = _outproj_router(o, w_out.astype(BF16), x2d, g1, norm_g[i, 1], sc2, sh2,
                                               wr, br, seq, n_exp)
        wgu = jnp.concatenate([moe_w_gate[i], moe_w_up[i]], axis=-1).astype(BF16)
        x2d = _moe(h2, wgu, moe_w_down[i].astype(BF16), comb, gid_t, x2d, g2, final_g, seq,
                   final_norm=(i == depth - 1))
    return x2d.reshape(batch, seq, d)
```

```python
import functools
import math

import jax
import jax.numpy as jnp
from jax import lax
from jax.experimental import pallas as pl
from jax.experimental.pallas import tpu as pltpu

F32 = jnp.float32
BF16 = jnp.bfloat16
HIGHEST = lax.Precision.HIGHEST

ROPE_THETA = 500000.0
ROT_DIM = 16
NORM_EPS = 1e-6
NEG_INF = -1e30
HEAD_DIM = 64
DA_V_DIM = 2 * HEAD_DIM
NSA_KV_GROUPS = 2
NSA_CMP_BLOCK = 32
NSA_CMP_STRIDE = 16
NSA_SEL_BLOCK = 64
NSA_TOP_N = 16
NSA_WINDOW = 512
NSA_FORCE = 1e4
MOE_GROUPS = 4
MOE_EXPERTS_PER_GROUP = 8
MOE_TOP_K = 2
N_MIXERS = 2
LOG2E = 1.4426950408889634

LANES = 128
PROJ_ROW_CHUNK = 256
VMEM_LIMIT_BYTES = 56 * 1024 * 1024

NT_DIMS = (((1,), (1,)), ((), ()))
TN_DIMS = (((0,), (0,)), ((), ()))


def _cparams(*sem):
    return pltpu.CompilerParams(dimension_semantics=sem, vmem_limit_bytes=VMEM_LIMIT_BYTES)


def _mod_kernel(c_ref, w_ref, b_ref, o_ref):
    c = c_ref[...]
    ca = c * jax.nn.sigmoid(c)
    o_ref[0] = jnp.dot(ca, w_ref[0], precision=HIGHEST, preferred_element_type=F32) + b_ref[0]


def _adaln_mod(c, ada_w, ada_b):
    depth, d, n = ada_w.shape
    b = c.shape[0]
    rows = 8
    tn = 1536
    c_pad = jnp.zeros((rows, d), F32).at[:b].set(c)
    out = pl.pallas_call(
        _mod_kernel,
        grid=(depth, n // tn),
        in_specs=[
            pl.BlockSpec((rows, d), lambda i, j: (0, 0)),
            pl.BlockSpec((1, d, tn), lambda i, j: (i, 0, j)),
            pl.BlockSpec((1, 1, tn), lambda i, j: (i, 0, j)),
        ],
        out_specs=pl.BlockSpec((1, rows, tn), lambda i, j: (i, 0, j)),
        out_shape=jax.ShapeDtypeStruct((depth, rows, n), F32),
        compiler_params=_cparams("arbitrary", "arbitrary"),
        name="adaln_mod",
    )(c_pad, ada_w, ada_b.reshape(depth, 1, n))
    return out[:, :b, :]


def _rope_tables(positions):
    half = ROT_DIM // 2
    inv_freq = ROPE_THETA ** (-jnp.arange(0, ROT_DIM, 2, dtype=F32) / ROT_DIM)
    ang = positions.astype(F32).reshape(-1)[:, None] * inv_freq
    cos, sin = jnp.cos(ang), jnp.sin(ang)
    t = ang.shape[0]
    rest = HEAD_DIM - ROT_DIM
    z8 = jnp.zeros((t, half), F32)
    cos_h = jnp.concatenate([cos, cos, jnp.ones((t, rest), F32)], axis=-1)
    sina_h = jnp.concatenate([-sin, z8, jnp.zeros((t, rest), F32)], axis=-1)
    sinb_h = jnp.concatenate([z8, sin, jnp.zeros((t, rest), F32)], axis=-1)
    rep = LANES // HEAD_DIM
    return tuple(jnp.tile(a, (1, rep)) for a in (cos_h, sina_h, sinb_h))


def _rms_modulate(x, g, sc, sh):
    ms = jnp.mean(x * x, axis=-1, keepdims=True)
    y = x * lax.rsqrt(ms + NORM_EPS)
    return (y * g) * (1.0 + sc) + sh


def _proj_kernel(*refs, plain_tiles, with_gates):
    if with_gates:
        (x_ref, g_ref, sc_ref, sh_ref, w_ref, cos_ref, sa_ref, sb_ref, wg_ref,
         o_ref, gt_ref, h_scr) = refs
    else:
        x_ref, g_ref, sc_ref, sh_ref, w_ref, cos_ref, sa_ref, sb_ref, o_ref, h_scr = refs
    j = pl.program_id(1)
    tn = o_ref.shape[1]

    @pl.when(j == 0)
    def _():
        h = _rms_modulate(x_ref[...], g_ref[...], sc_ref[0], sh_ref[0])
        h_scr[...] = h.astype(BF16)
        if with_gates:
            logits = lax.dot_general(wg_ref[...], h_scr[...], NT_DIMS, preferred_element_type=F32)
            gt_ref[...] = jax.nn.sigmoid(logits)

    is_plain = functools.reduce(jnp.logical_or, [j == t for t in plain_tiles], j < 0)
    for r in range(h_scr.shape[0] // PROJ_ROW_CHUNK):
        rows = slice(r * PROJ_ROW_CHUNK, (r + 1) * PROJ_ROW_CHUNK)
        acc = jnp.dot(h_scr[rows, :], w_ref[...], preferred_element_type=F32)
        cos = jnp.where(is_plain, 1.0, cos_ref[rows, :])
        sa = jnp.where(is_plain, 0.0, sa_ref[rows, :])
        sb = jnp.where(is_plain, 0.0, sb_ref[rows, :])
        shift_up = pltpu.roll(acc, tn - ROT_DIM // 2, 1)
        shift_dn = pltpu.roll(acc, ROT_DIM // 2, 1)
        for c in range(tn // LANES):
            sl = slice(c * LANES, (c + 1) * LANES)
            y = acc[:, sl] * cos + shift_up[:, sl] * sa + shift_dn[:, sl] * sb
            o_ref[rows, sl] = y.astype(o_ref.dtype)


def _norm_proj(x2d, g, sc, sh, w, tables, seq, *, tn, plain_tiles, w_gates=None, tm=1024):
    t, d = x2d.shape
    n = w.shape[1]
    with_gates = w_gates is not None
    per_b = seq // tm
    in_specs = [
        pl.BlockSpec((tm, d), lambda i, j: (i, 0)),
        pl.BlockSpec((1, d), lambda i, j: (0, 0)),
        pl.BlockSpec((1, 1, d), lambda i, j: (i // per_b, 0, 0)),
        pl.BlockSpec((1, 1, d), lambda i, j: (i // per_b, 0, 0)),
        pl.BlockSpec((d, tn), lambda i, j: (0, j)),
        pl.BlockSpec((tm, LANES), lambda i, j: (i, 0)),
        pl.BlockSpec((tm, LANES), lambda i, j: (i, 0)),
        pl.BlockSpec((tm, LANES), lambda i, j: (i, 0)),
    ]
    args = [x2d, g.reshape(1, d), sc[:, None, :], sh[:, None, :], w, *tables]
    out_specs = [pl.BlockSpec((tm, tn), lambda i, j: (i, j))]
    out_shape = [jax.ShapeDtypeStruct((t, n), BF16)]
    if with_gates:
        ng = w_gates.shape[0]
        in_specs.append(pl.BlockSpec((ng, d), lambda i, j: (0, 0)))
        args.append(w_gates)
        out_specs.append(pl.BlockSpec((ng, tm), lambda i, j: (0, i)))
        out_shape.append(jax.ShapeDtypeStruct((ng, t), F32))
    res = pl.pallas_call(
        functools.partial(_proj_kernel, plain_tiles=tuple(plain_tiles), with_gates=with_gates),
        grid=(t // tm, n // tn),
        in_specs=in_specs,
        out_specs=out_specs,
        out_shape=out_shape,
        scratch_shapes=[pltpu.VMEM((tm, d), BF16)],
        compiler_params=_cparams("arbitrary", "arbitrary"),
        name="norm_proj",
    )(*args)
    return res if with_gates else res[0]


def _attn_scratch(tk, nq, dv):
    return [
        pltpu.VMEM((2, tk, nq), F32),
        pltpu.VMEM((2, 1, nq), F32),
        pltpu.VMEM((2, tk, nq), BF16),
        pltpu.VMEM((2, 1, nq), F32),
        pltpu.VMEM((1, nq), F32),
        pltpu.VMEM((1, nq), F32),
        pltpu.VMEM((dv, nq), F32),
    ]


def _pipelined_attention(n_full, scores_full, scores_diag, v_tile, scratch, sum_in_v=False):
    s_scr, mx_scr, p_scr, al_scr, m_ref, l_ref, acc_ref = scratch

    def stage_scores(fn, tile, slot):
        s = fn(tile)
        s_scr[slot] = s
        mx_scr[slot] = jnp.max(s, axis=0, keepdims=True)

    def stage_softmax(slot):
        m_prev = m_ref[...]
        m_new = jnp.maximum(m_prev, mx_scr[slot])
        alpha = jnp.exp2(m_prev - m_new)
        p = jnp.exp2(s_scr[slot] - m_new)
        if not sum_in_v:
            l_ref[...] = alpha * l_ref[...] + jnp.sum(p, axis=0, keepdims=True)
        m_ref[...] = m_new
        al_scr[slot] = alpha
        p_scr[slot] = p.astype(BF16)

    def stage_pv(pos, slot):
        tile = jnp.where(pos == 0, n_full, jnp.maximum(pos - 1, 0))
        pv = lax.dot_general(v_tile(tile), p_scr[slot], TN_DIMS, preferred_element_type=F32)
        acc_ref[...] = acc_ref[...] * al_scr[slot] + pv

    def step(j, slot):
        stage_scores(scores_full, j, 1 - slot)
        stage_softmax(slot)
        stage_pv(j - 1, 1 - slot)

    m_ref[...] = jnp.full(m_ref.shape, NEG_INF, F32)
    l_ref[...] = jnp.zeros(l_ref.shape, F32)
    acc_ref[...] = jnp.zeros(acc_ref.shape, F32)
    p_scr[1] = jnp.zeros(p_scr.shape[1:], BF16)
    al_scr[1] = jnp.ones(al_scr.shape[1:], F32)
    stage_scores(scores_diag, n_full, 0)

    def pair(jj, carry):
        step(2 * jj, 0)
        step(2 * jj + 1, 1)
        return carry

    lax.fori_loop(0, n_full // 2, pair, 0)
    odd = n_full % 2 == 1

    @pl.when(odd)
    def _():
        step(n_full - 1, 0)
        stage_softmax(1)
        stage_pv(n_full - 1, 0)
        stage_pv(n_full, 1)

    @pl.when(jnp.logical_not(odd))
    def _():
        stage_softmax(0)
        stage_pv(n_full - 1, 1)
        stage_pv(n_full, 0)


def _split_halves(q, rows):
    lane = lax.broadcasted_iota(jnp.int32, q.shape, 1)
    zero = jnp.zeros_like(q)
    return jnp.where(lane < HEAD_DIM, q, zero), jnp.where(lane >= HEAD_DIM, q, zero)


def _diff_attn_kernel(q_ref, k_ref, v_ref, lam_ref, g_ref, o_ref, qm, *scratch,
                      tq, tk, lambda_init):
    qi = pl.program_id(2)
    lo, hi = _split_halves(q_ref[...], tq)
    qm[0:tq, :] = lo
    qm[tq:2 * tq, :] = hi
    q0 = qi * tq
    acc_ref, l_ref = scratch[-1], scratch[-2]

    def scores_full(t):
        k = k_ref[pl.ds(pl.multiple_of(t * tk, tk), tk), :]
        return lax.dot_general(k, qm[...], NT_DIMS, preferred_element_type=F32)

    def scores_diag(t):
        s = scores_full(t)
        k_idx = t * tk + lax.broadcasted_iota(jnp.int32, s.shape, 0)
        q_idx = q0 + lax.broadcasted_iota(jnp.int32, s.shape, 1) % tq
        return jnp.where(k_idx <= q_idx, s, NEG_INF)

    def v_tile(t):
        return v_ref[pl.ds(pl.multiple_of(t * tk, tk), tk), :]

    _pipelined_attention(q0 // tk, scores_full, scores_diag, v_tile, scratch)

    lam = lam_ref[...]
    lam_full = (jnp.exp(jnp.sum(lam[0:1] * lam[1:2], axis=-1, keepdims=True))
                - jnp.exp(jnp.sum(lam[2:3] * lam[3:4], axis=-1, keepdims=True)) + lambda_init)
    acc = acc_ref[...]
    l = jnp.maximum(l_ref[...], 1e-30)
    ot = acc[:, 0:tq] / l[:, 0:tq] - lam_full * (acc[:, tq:2 * tq] / l[:, tq:2 * tq])
    ms = jnp.mean(ot * ot, axis=0, keepdims=True)
    ot = (ot * lax.rsqrt(ms + NORM_EPS)) * g_ref[...] * (1.0 - lambda_init)
    o_ref[...] = ot.T.astype(o_ref.dtype)


def _diff_attention(qkv, lam, subln_g, batch, seq, lambda_init, *, tq=1024, tk=1024):
    assert tk % tq == 0 and seq % tk == 0
    t, n3 = qkv.shape
    d = n3 // 3
    heads = d // DA_V_DIM
    nq = seq // tq
    cb = d // LANES
    g_b = jnp.broadcast_to(subln_g.astype(F32)[:, None], (DA_V_DIM, tq))
    return pl.pallas_call(
        functools.partial(_diff_attn_kernel, tq=tq, tk=tk, lambda_init=lambda_init),
        grid=(batch, heads, nq),
        in_specs=[
            pl.BlockSpec((tq, LANES), lambda b, h, i: (b * nq + i, h)),
            pl.BlockSpec((seq, LANES), lambda b, h, i: (b, cb + h)),
            pl.BlockSpec((seq, LANES), lambda b, h, i: (b, 2 * cb + h)),
            pl.BlockSpec(lam.shape, lambda b, h, i: (0, 0)),
            pl.BlockSpec((DA_V_DIM, tq), lambda b, h, i: (0, 0)),
        ],
        out_specs=pl.BlockSpec((tq, LANES), lambda b, h, i: (b * nq + i, h)),
        out_shape=jax.ShapeDtypeStruct((t, d), BF16),
        scratch_shapes=[pltpu.VMEM((2 * tq, LANES), BF16)] + _attn_scratch(tk, 2 * tq, DA_V_DIM),
        compiler_params=_cparams("arbitrary", "arbitrary", "arbitrary"),
        name="diff_attn",
    )(qkv, qkv, qkv, lam.astype(F32), g_b)


def _route(logits, n_exp):
    lane = lax.broadcasted_iota(jnp.int32, logits.shape, 1)
    lane_f = lane.astype(F32)
    big = float(LANES)
    is_grp = (lane >= n_exp) & (lane < n_exp + MOE_GROUPS)
    gl = jnp.where(is_grp, logits, -jnp.inf)
    ge = jnp.where(is_grp, jnp.exp(gl - jnp.max(gl, axis=-1, keepdims=True)), 0.0)
    pg = ge / jnp.sum(ge, axis=-1, keepdims=True)
    p_top = jnp.max(pg, axis=-1, keepdims=True)
    grp = jnp.min(jnp.where(is_grp & (pg == p_top), lane_f, big), axis=-1, keepdims=True) - n_exp
    in_grp = (lane < n_exp) & ((lane // MOE_EXPERTS_PER_GROUP).astype(F32) == grp)
    sel = jnp.where(in_grp, logits, -jnp.inf)
    v1 = jnp.max(sel, axis=-1, keepdims=True)
    i1 = jnp.min(jnp.where(sel == v1, lane_f, big), axis=-1, keepdims=True)
    sel2 = jnp.where(lane_f == i1, -jnp.inf, sel)
    v2 = jnp.max(sel2, axis=-1, keepdims=True)
    i2 = jnp.min(jnp.where(sel2 == v2, lane_f, big), axis=-1, keepdims=True)
    e2 = jnp.exp(v2 - v1)
    den = 1.0 + e2
    w1 = (1.0 / den) * p_top
    w2 = (e2 / den) * p_top
    return jnp.where(lane_f == i1, w1, 0.0) + jnp.where(lane_f == i2, w2, 0.0), grp


def _outproj_router_kernel(o_ref, w_ref, x_ref, g1_ref, ng_ref, sc_ref, sh_ref, wr_ref, br_ref,
                           xo_ref, h_ref, comb_ref, gid_ref, *, n_exp):
    mix = jnp.dot(o_ref[...], w_ref[...], preferred_element_type=F32)
    x = x_ref[...] + g1_ref[0] * mix
    xo_ref[...] = x
    h = _rms_modulate(x, ng_ref[...], sc_ref[0], sh_ref[0])
    h_hi = h.astype(BF16)
    h_ref[...] = h_hi
    h_lo = (h - h_hi.astype(F32)).astype(BF16)
    wr = wr_ref[...]
    hw = jnp.dot(h_hi, wr, preferred_element_type=F32)
    logits = (hw[:, 0:LANES] + hw[:, LANES:2 * LANES]
              + jnp.dot(h_lo, wr[:, 0:LANES], preferred_element_type=F32) + br_ref[...])
    comb, grp = _route(logits, n_exp)
    comb_ref[...] = comb[:, 0:n_exp]
    gid_ref[...] = jnp.broadcast_to(grp, logits.shape).T[0:gid_ref.shape[0], :]


def _outproj_router(o, w_out, x2d, g1, ng, sc, sh, wr, br, seq, n_exp, *, tm=1024):
    t, d = x2d.shape
    per_b = seq // tm
    bvec = lambda i: (i // per_b, 0, 0)
    return pl.pallas_call(
        functools.partial(_outproj_router_kernel, n_exp=n_exp),
        grid=(t // tm,),
        in_specs=[
            pl.BlockSpec((tm, d), lambda i: (i, 0)),
            pl.BlockSpec((d, d), lambda i: (0, 0)),
            pl.BlockSpec((tm, d), lambda i: (i, 0)),
            pl.BlockSpec((1, 1, d), bvec),
            pl.BlockSpec((1, d), lambda i: (0, 0)),
            pl.BlockSpec((1, 1, d), bvec),
            pl.BlockSpec((1, 1, d), bvec),
            pl.BlockSpec((d, 2 * LANES), lambda i: (0, 0)),
            pl.BlockSpec((1, LANES), lambda i: (0, 0)),
        ],
        out_specs=[
            pl.BlockSpec((tm, d), lambda i: (i, 0)),
            pl.BlockSpec((tm, d), lambda i: (i, 0)),
            pl.BlockSpec((tm, n_exp), lambda i: (i, 0)),
            pl.BlockSpec((8, tm), lambda i: (0, i)),
        ],
        out_shape=[
            jax.ShapeDtypeStruct((t, d), F32),
            jax.ShapeDtypeStruct((t, d), BF16),
            jax.ShapeDtypeStruct((t, n_exp), F32),
            jax.ShapeDtypeStruct((8, t), F32),
        ],
        compiler_params=_cparams("arbitrary"),
        name="outproj_router",
    )(o, w_out, x2d, g1[:, None, :], ng.reshape(1, d), sc[:, None, :], sh[:, None, :], wr, br)


MOE_CHUNK = 128
MOE_EXPERTS_PER_STEP = 4


def _moe_kernel(h_ref, wgu_ref, wd_ref, comb_ref, gid_ref, tri_ref, x_ref, g2_ref, fg_ref, o_ref,
                perm_scr, hs_scr, cs_scr, ys_scr, off_smem, nch_smem, *, hidden, final_norm):
    step = pl.program_id(1)
    tm = h_ref.shape[0]
    rows = perm_scr.shape[0]
    n_exp = comb_ref.shape[1]
    steps_per_group = MOE_EXPERTS_PER_GROUP // MOE_EXPERTS_PER_STEP

    @pl.when(step == 0)
    def _():
        gid = gid_ref[0:1, :]
        grow = lax.broadcasted_iota(jnp.int32, (8, tm), 0).astype(F32)
        onehot = jnp.where(grow == gid, 1.0, 0.0)
        rank = jnp.dot(onehot.astype(BF16), tri_ref[...], preferred_element_type=F32)
        off = jnp.int32(0)
        off_col = jnp.zeros((8, 1), F32)
        grow_col = lax.broadcasted_iota(jnp.int32, (8, 1), 0)
        for g in range(MOE_GROUPS):
            cnt = jnp.sum(onehot[g:g + 1, :]).astype(jnp.int32)
            nch = (cnt + (MOE_CHUNK - 1)) // MOE_CHUNK
            off_smem[g] = off
            nch_smem[g] = nch
            off_col = jnp.where(grow_col == g, off.astype(F32), off_col)
            off = off + nch * MOE_CHUNK
        dest = jnp.sum(onehot * (rank + off_col), axis=0, keepdims=True)
        row_id = lax.broadcasted_iota(jnp.int32, (rows, tm), 0)
        perm = jnp.where(row_id == dest.astype(jnp.int32), 1.0, 0.0).astype(BF16)
        perm_scr[...] = perm
        hs_scr[...] = jnp.dot(perm, h_ref[...], preferred_element_type=F32).astype(BF16)
        comb = comb_ref[...]
        comb_hi = comb.astype(BF16)
        comb_lo = (comb - comb_hi.astype(F32)).astype(BF16)
        cs_scr[...] = (jnp.dot(perm, comb_hi, preferred_element_type=F32)
                       + jnp.dot(perm, comb_lo, preferred_element_type=F32))
        ys_scr[...] = jnp.zeros(ys_scr.shape, F32)

    g = step // steps_per_group
    off = off_smem[g]
    lane = lax.broadcasted_iota(jnp.int32, (MOE_CHUNK, n_exp), 1)

    def chunk(c, carry):
        r0 = pl.multiple_of(off + c * MOE_CHUNK, MOE_CHUNK)
        xs = hs_scr[pl.ds(r0, MOE_CHUNK), :]
        cs = cs_scr[pl.ds(r0, MOE_CHUNK), :]
        y = jnp.zeros((MOE_CHUNK, xs.shape[1]), F32)
        for k in range(MOE_EXPERTS_PER_STEP):
            gu = jnp.dot(xs, wgu_ref[k], preferred_element_type=F32)
            gate, up = gu[:, 0:hidden], gu[:, hidden:2 * hidden]
            a = (gate * jax.nn.sigmoid(gate)) * up
            e = step * MOE_EXPERTS_PER_STEP + k
            ce = jnp.sum(jnp.where(lane == e, cs, 0.0), axis=-1, keepdims=True)
            y = y + jnp.dot((a * ce).astype(BF16), wd_ref[k], preferred_element_type=F32)
        ys_scr[pl.ds(r0, MOE_CHUNK), :] += y
        return carry

    lax.fori_loop(0, nch_smem[g], chunk, 0)

    @pl.when(step == pl.num_programs(1) - 1)
    def _():
        y = lax.dot_general(perm_scr[...], ys_scr[...].astype(BF16), TN_DIMS,
                            preferred_element_type=F32)
        x = x_ref[...] + g2_ref[0] * y
        if final_norm:
            ms = jnp.mean(x * x, axis=-1, keepdims=True)
            x = (x * lax.rsqrt(ms + NORM_EPS)) * fg_ref[...]
        o_ref[...] = x


def _moe(h, wgu, wd, comb, gid_t, x2d, g2, final_g, seq, *, final_norm, tm=1024):
    t, d = x2d.shape
    n_exp, _, two_h = wgu.shape
    hidden = two_h // 2
    per_b = seq // tm
    eps = MOE_EXPERTS_PER_STEP
    rows = tm + MOE_GROUPS * MOE_CHUNK
    idx = jnp.arange(tm)
    tri = (idx[:, None] < idx[None, :]).astype(BF16)
    return pl.pallas_call(
        functools.partial(_moe_kernel, hidden=hidden, final_norm=final_norm),
        grid=(t // tm, n_exp // eps),
        in_specs=[
            pl.BlockSpec((tm, d), lambda i, s: (i, 0)),
            pl.BlockSpec((eps, d, two_h), lambda i, s: (s, 0, 0)),
            pl.BlockSpec((eps, hidden, d), lambda i, s: (s, 0, 0)),
            pl.BlockSpec((tm, n_exp), lambda i, s: (i, 0)),
            pl.BlockSpec((8, tm), lambda i, s: (0, i)),
            pl.BlockSpec((tm, tm), lambda i, s: (0, 0)),
            pl.BlockSpec((tm, d), lambda i, s: (i, 0)),
            pl.BlockSpec((1, 1, d), lambda i, s: (i // per_b, 0, 0)),
            pl.BlockSpec((1, d), lambda i, s: (0, 0)),
        ],
        out_specs=pl.BlockSpec((tm, d), lambda i, s: (i, 0)),
        out_shape=jax.ShapeDtypeStruct((t, d), F32),
        scratch_shapes=[
            pltpu.VMEM((rows, tm), BF16),
            pltpu.VMEM((rows, d), BF16),
            pltpu.VMEM((rows, n_exp), F32),
            pltpu.VMEM((rows, d), F32),
            pltpu.SMEM((MOE_GROUPS,), jnp.int32),
            pltpu.SMEM((MOE_GROUPS,), jnp.int32),
        ],
        compiler_params=_cparams("arbitrary", "arbitrary"),
        name="moe",
    )(h, wgu, wd, comb, gid_t, tri, x2d, g2[:, None, :], final_g.reshape(1, d))


def _compress_kernel(ch_ref, w1a_ref, w1b_ref, pos_ref, w1f_ref, b1_ref, w2_ref, o_ref):
    ch = ch_ref[0, 0, 0]
    n = ch.shape[0]
    a = jnp.dot(ch, w1a_ref[0], preferred_element_type=F32)
    b = jnp.dot(ch, w1b_ref[0], preferred_element_type=F32)
    const = jnp.dot(pos_ref[0], w1f_ref[0], precision=HIGHEST, preferred_element_type=F32)[0:1]
    hid = a + pltpu.roll(b, n - 1, 0) + (const + b1_ref[0])
    act = jax.nn.gelu(hid, approximate=True)
    o_ref[0, 0, 0] = jnp.dot(act.astype(BF16), w2_ref[0], preferred_element_type=F32).astype(o_ref.dtype)


def _nsa_compress(ch, cmp_pos, cmp_w1, cmp_b1, cmp_w2):
    b, two, g, nc, f = ch.shape
    hid = cmp_w1.shape[-1]
    w1a = cmp_w1[:, :f, :].astype(BF16)
    w1b = cmp_w1[:, f:, :].astype(BF16)
    pos = jnp.zeros((two, 8, 2 * f), F32).at[:, 0, :].set(cmp_pos.reshape(two, 2 * f))
    w2 = jnp.concatenate([cmp_w2, cmp_w2], axis=-1).astype(BF16)
    return pl.pallas_call(
        _compress_kernel,
        grid=(b, two, g),
        in_specs=[
            pl.BlockSpec((1, 1, 1, nc, f), lambda i, j, k: (i, j, k, 0, 0)),
            pl.BlockSpec((1, f, hid), lambda i, j, k: (j, 0, 0)),
            pl.BlockSpec((1, f, hid), lambda i, j, k: (j, 0, 0)),
            pl.BlockSpec((1, 8, 2 * f), lambda i, j, k: (j, 0, 0)),
            pl.BlockSpec((1, 2 * f, hid), lambda i, j, k: (j, 0, 0)),
            pl.BlockSpec((1, 1, hid), lambda i, j, k: (j, 0, 0)),
            pl.BlockSpec((1, hid, LANES), lambda i, j, k: (j, 0, 0)),
        ],
        out_specs=pl.BlockSpec((1, 1, 1, nc, LANES), lambda i, j, k: (i, j, k, 0, 0)),
        out_shape=jax.ShapeDtypeStruct((b, two, g, nc, LANES), BF16),
        compiler_params=_cparams("arbitrary", "arbitrary", "arbitrary"),
        name="nsa_compress",
    )(ch, w1a, w1b, pos, cmp_w1.astype(F32), cmp_b1[:, None, :].astype(F32), w2)


def _stack_group_queries(q_ref, qm, tq):
    for j in range(q_ref.shape[1] // LANES):
        lo, hi = _split_halves(q_ref[:, j * LANES:(j + 1) * LANES], tq)
        qm[(2 * j) * tq:(2 * j + 1) * tq, :] = lo
        qm[(2 * j + 1) * tq:(2 * j + 2) * tq, :] = hi


def _own_group_values_with_ones(v, g):
    lane = lax.broadcasted_iota(jnp.int32, v.shape, 1)
    own_half = (lane >= HEAD_DIM) == (g == 1)
    return jnp.where(own_half, v, jnp.ones_like(v))


def _gated_pairs_to_token_major(ot, gt_ref, branch, tq, hpg):
    g = pl.program_id(1)
    tiles = []
    for j in range(hpg // 2):
        parts = []
        for h in (2 * j, 2 * j + 1):
            row = gt_ref[pl.ds((g * hpg + h) * 3 + branch, 1), :]
            parts.append(ot[:, h * tq:(h + 1) * tq] * row)
        tiles.append(jnp.concatenate(parts, axis=0).T)
    return tiles


NSA_CMP_CLASSES = 4


def _nsa_window_tiles(q0, qm, kw_ref, vw_ref, gt_ref, acc_ref, *, tq, hpg):
    g = pl.program_id(1)
    span = NSA_WINDOW + tq
    start = pl.multiple_of(jnp.maximum(q0 - NSA_WINDOW, 0), tq)
    k = kw_ref[pl.ds(start, span), :]
    v = vw_ref[pl.ds(start, span), :]
    s = lax.dot_general(k, qm[...], NT_DIMS, preferred_element_type=F32)
    k_idx = start + lax.broadcasted_iota(jnp.int32, (span, tq), 0)
    q_idx = q0 + lax.broadcasted_iota(jnp.int32, (span, tq), 1)
    dist = q_idx - k_idx
    bias = jnp.where(dist >= 0, jnp.where(dist < NSA_WINDOW, 0.0, NEG_INF), NEG_INF)
    s = s + jnp.concatenate([bias] * hpg, axis=1)
    m = jnp.max(s, axis=0, keepdims=True)
    e = jnp.exp2(s - m)
    r = 1.0 / jnp.maximum(jnp.sum(e, axis=0, keepdims=True), 1e-30)
    acc_ref[...] = lax.dot_general(v, e.astype(BF16), TN_DIMS, preferred_element_type=F32) * r
    ot = acc_ref[pl.ds(pl.multiple_of(g * HEAD_DIM, HEAD_DIM), HEAD_DIM), :]
    return _gated_pairs_to_token_major(ot, gt_ref, 2, tq, hpg)


def _nsa_cmp_body(q0, qm, kc_ref, vc_ref, mt_ref, gt_ref, kw_ref, vw_ref, o_ref, mask_ref, acc_ref,
                  *, tq, hpg, n_top, n_cmp, n_blk):
    win_tiles = _nsa_window_tiles(q0, qm, kw_ref, vw_ref, gt_ref, acc_ref, tq=tq, hpg=hpg)
    kc = kc_ref[0, 0, 0, 0:n_cmp, :]
    s = lax.dot_general(kc, qm[...], NT_DIMS, preferred_element_type=F32)
    n_idx = lax.broadcasted_iota(jnp.int32, (n_cmp, tq), 0)
    q_idx = q0 + lax.broadcasted_iota(jnp.int32, (n_cmp, tq), 1)
    valid = n_idx * NSA_CMP_STRIDE + (NSA_CMP_BLOCK - 1) <= q_idx
    bias = jnp.where(valid, 0.0, NEG_INF)
    s = s + jnp.concatenate([bias] * hpg, axis=1)
    m = jnp.maximum(jnp.max(s, axis=0, keepdims=True), 0.1 * NEG_INF)
    e = jnp.exp2(s - m)
    r = 1.0 / jnp.maximum(jnp.sum(e, axis=0, keepdims=True), 1e-30)
    p = e * r
    ot = lax.dot_general(vc_ref[0, 0, 0, 0:n_cmp, :], p.astype(BF16), TN_DIMS,
                         preferred_element_type=F32)
    tiles = _gated_pairs_to_token_major(ot[0:HEAD_DIM], gt_ref, 0, tq, hpg)
    for j, (tile, win) in enumerate(zip(tiles, win_tiles)):
        o_ref[:, j * LANES:(j + 1) * LANES] = (tile + win).astype(o_ref.dtype)

    psum = p[:, 0:tq]
    for h in range(1, hpg):
        psum = psum + p[:, h * tq:(h + 1) * tq]
    p_hi = psum.astype(BF16)
    p_lo = (psum - p_hi.astype(F32)).astype(BF16)
    mt = mt_ref[0:n_blk, 0:n_cmp]
    imp = (jnp.dot(mt, p_hi, preferred_element_type=F32)
           + jnp.dot(mt, p_lo, preferred_element_type=F32))
    j_idx = lax.broadcasted_iota(jnp.int32, imp.shape, 0)
    j_f = j_idx.astype(F32)
    q_blk = (q0 + lax.broadcasted_iota(jnp.int32, imp.shape, 1)) // NSA_SEL_BLOCK
    forced = (j_idx == 0) | (j_idx == q_blk) | (j_idx == q_blk - 1)
    val = jnp.where(forced, NSA_FORCE, jnp.where(j_idx <= q_blk, imp, -NSA_FORCE))
    sel_bias = jnp.full(imp.shape, NEG_INF, F32)
    for _ in range(n_top):
        mx = jnp.max(val, axis=0, keepdims=True)
        first = jnp.min(jnp.where(val == mx, j_f, float(n_blk)), axis=0, keepdims=True)
        hit = j_f == first
        sel_bias = jnp.where(hit, 0.0, sel_bias)
        val = jnp.where(hit, -jnp.inf, val)
    n_sel = mask_ref.shape[2]
    mask_ref[0, 0, 0:n_blk, :] = sel_bias
    if n_blk < n_sel:
        mask_ref[0, 0, n_blk:n_sel, :] = jnp.full((n_sel - n_blk, tq), NEG_INF, F32)


def _nsa_cmp_win_kernel(q_ref, kc_ref, vc_ref, mt_ref, gt_ref, kw_ref, vw_ref, o_ref, mask_ref,
                        qm, acc_ref, *, tq, hpg, n_top):
    qi = pl.program_id(2)
    q0 = qi * tq
    _stack_group_queries(q_ref, qm, tq)
    nc = kc_ref.shape[3]
    n_sel = mask_ref.shape[2]
    seq = n_sel * NSA_SEL_BLOCK
    cls = (q0 + tq - 1) // (seq // NSA_CMP_CLASSES)
    for c in range(NSA_CMP_CLASSES):
        @pl.when(cls == c)
        def _(c=c):
            _nsa_cmp_body(q0, qm, kc_ref, vc_ref, mt_ref, gt_ref, kw_ref, vw_ref, o_ref, mask_ref,
                          acc_ref, tq=tq, hpg=hpg,
                          n_top=min(n_top, (c + 1) * n_sel // NSA_CMP_CLASSES),
                          n_cmp=(c + 1) * nc // NSA_CMP_CLASSES,
                          n_blk=(c + 1) * n_sel // NSA_CMP_CLASSES)


def _nsa_sel_kernel(q_ref, k_ref, v_ref, mask_ref, gt_ref, ocw_ref, o_ref,
                    qm, *scratch, tq, tk, hpg):
    g = pl.program_id(1)
    qi = pl.program_id(2)
    q0 = qi * tq
    _stack_group_queries(q_ref, qm, tq)
    acc_ref, l_ref = scratch[-1], scratch[-2]
    blocks = tk // NSA_SEL_BLOCK

    def scores_full(t):
        k = k_ref[pl.ds(pl.multiple_of(t * tk, tk), tk), :]
        s = lax.dot_general(k, qm[...], NT_DIMS, preferred_element_type=F32)
        rows = mask_ref[0, 0, pl.ds(pl.multiple_of(t * blocks, blocks), blocks), :]
        exp_rows = [jnp.broadcast_to(rows[r:r + 1, :], (NSA_SEL_BLOCK, tq)) for r in range(blocks)]
        bias = jnp.concatenate(exp_rows, axis=0)
        return s + jnp.concatenate([bias] * hpg, axis=1)

    def scores_diag(t):
        s = scores_full(t)
        k_idx = t * tk + lax.broadcasted_iota(jnp.int32, s.shape, 0)
        q_idx = q0 + lax.broadcasted_iota(jnp.int32, s.shape, 1) % tq
        return jnp.where(k_idx <= q_idx, s, NEG_INF)

    def v_tile(t):
        return _own_group_values_with_ones(v_ref[pl.ds(pl.multiple_of(t * tk, tk), tk), :], g)

    _pipelined_attention(q0 // tk, scores_full, scores_diag, v_tile, scratch, sum_in_v=True)

    own = pl.multiple_of(g * HEAD_DIM, HEAD_DIM)
    other = pl.multiple_of((1 - g) * HEAD_DIM, HEAD_DIM)
    inv_l = 1.0 / jnp.maximum(acc_ref[pl.ds(other, 1), :], 1e-30)
    ot = acc_ref[pl.ds(own, HEAD_DIM), :] * inv_l
    tiles = _gated_pairs_to_token_major(ot, gt_ref, 1, tq, hpg)
    for j, tile in enumerate(tiles):
        sl = slice(j * LANES, (j + 1) * LANES)
        total = tile + ocw_ref[:, sl].astype(F32)
        o_ref[:, sl] = total.astype(o_ref.dtype)


def _nsa_attention(proj, gates_t, cmp_kv, batch, seq, d_model, *, tq=128, tq_sel=512, tk=512):
    assert tk % tq_sel == 0 and seq % tk == 0
    t = proj.shape[0]
    groups = NSA_KV_GROUPS
    hpg = d_model // HEAD_DIM // groups
    gw = hpg * HEAD_DIM
    nq = seq // tq
    nq_sel = seq // tq_sel
    nc = cmp_kv.shape[3]
    n_sel = seq // NSA_SEL_BLOCK
    n_top = min(NSA_TOP_N, n_sel)
    ng = gates_t.shape[0]
    qcb = d_model // LANES

    ci = jnp.arange(nc)[None, :] * NSA_CMP_STRIDE
    sj = jnp.arange(n_sel)[:, None] * NSA_SEL_BLOCK
    mt = ((ci < sj + NSA_SEL_BLOCK) & (ci + NSA_CMP_BLOCK > sj)
          & (jnp.arange(nc)[None, :] < nc - 1)).astype(BF16)

    q_spec = pl.BlockSpec((tq, gw), lambda b, g, i: (b * nq + i, g))
    gt_spec = pl.BlockSpec((ng, tq), lambda b, g, i: (0, b * nq + i))
    o_spec = pl.BlockSpec((tq, gw), lambda b, g, i: (b * nq + i, g))
    o_shape = jax.ShapeDtypeStruct((t, d_model), BF16)
    sem = ("arbitrary", "arbitrary", "arbitrary")
    qm_scr = pltpu.VMEM((hpg * tq, LANES), BF16)

    o_cmp_win, sel_mask = pl.pallas_call(
        functools.partial(_nsa_cmp_win_kernel, tq=tq, hpg=hpg, n_top=n_top),
        grid=(batch, groups, nq),
        in_specs=[
            q_spec,
            pl.BlockSpec((1, 1, 1, nc, LANES), lambda b, g, i: (b, 0, g, 0, 0)),
            pl.BlockSpec((1, 1, 1, nc, LANES), lambda b, g, i: (b, 1, g, 0, 0)),
            pl.BlockSpec((n_sel, nc), lambda b, g, i: (0, 0)),
            gt_spec,
            pl.BlockSpec((seq, LANES), lambda b, g, i: (b, qcb + 5 + g)),
            pl.BlockSpec((seq, LANES), lambda b, g, i: (b, qcb + 7)),
        ],
        out_specs=[o_spec, pl.BlockSpec((1, 1, n_sel, tq), lambda b, g, i: (b, g, 0, i))],
        out_shape=[o_shape, jax.ShapeDtypeStruct((batch, groups, n_sel, seq), F32)],
        scratch_shapes=[qm_scr, pltpu.VMEM((LANES, hpg * tq), F32)],
        compiler_params=_cparams(*sem),
        name="nsa_cmp_win",
    )(proj, cmp_kv, cmp_kv, mt, gates_t, proj, proj)

    rows_sel = pl.BlockSpec((tq_sel, gw), lambda b, g, i: (b * nq_sel + i, g))
    return pl.pallas_call(
        functools.partial(_nsa_sel_kernel, tq=tq_sel, tk=tk, hpg=hpg),
        grid=(batch, groups, nq_sel),
        in_specs=[
            rows_sel,
            pl.BlockSpec((seq, LANES), lambda b, g, i: (b, qcb + 2 + g)),
            pl.BlockSpec((seq, LANES), lambda b, g, i: (b, qcb + 4)),
            pl.BlockSpec((1, 1, n_sel, tq_sel), lambda b, g, i: (b, g, 0, i)),
            pl.BlockSpec((ng, tq_sel), lambda b, g, i: (0, b * nq_sel + i)),
            rows_sel,
        ],
        out_specs=rows_sel,
        out_shape=o_shape,
        scratch_shapes=[pltpu.VMEM((hpg * tq_sel, LANES), BF16)] + _attn_scratch(tk, hpg * tq_sel, LANES),
        compiler_params=_cparams(*sem),
        name="nsa_sel",
    )(proj, proj, proj, sel_mask, gates_t, o_cmp_win)


def _diff_in_weights(w_in, d_model):
    scale = HEAD_DIM ** -0.5 * LOG2E
    wq = w_in[:, :d_model] * scale
    return jnp.concatenate([wq, w_in[:, d_model:]], axis=1).astype(BF16)


def _nsa_in_weights(w_in, d_model):
    g, hd = NSA_KV_GROUPS, HEAD_DIM
    kvw = g * hd
    scale = hd ** -0.5 * LOG2E
    q = w_in[:, :d_model] * scale
    off = d_model
    sec = [w_in[:, off + i * kvw: off + (i + 1) * kvw] for i in range(6)]
    dup = lambda w: jnp.concatenate([w[:, i * hd:(i + 1) * hd] for i in range(g) for _ in range(2)], axis=1)
    w_main = jnp.concatenate([q, sec[0], sec[1], dup(sec[2]), sec[3], dup(sec[4]), sec[5]], axis=1)
    w_gates = w_in[:, off + 6 * kvw:].T
    return w_main.astype(BF16), w_gates.astype(BF16)


def _router_weights(w_grp, b_grp, w_exp, b_exp):
    d = w_grp.shape[0]
    n_exp = w_exp.shape[0] * w_exp.shape[2]
    we = jnp.transpose(w_exp, (1, 0, 2)).reshape(d, n_exp)
    pad = LANES - n_exp - w_grp.shape[1]
    wr = jnp.concatenate([we, w_grp, jnp.zeros((d, pad), F32)], axis=1).astype(F32)
    wr_hi = wr.astype(BF16)
    wr_lo = (wr - wr_hi.astype(F32)).astype(BF16)
    br = jnp.concatenate([b_exp.reshape(-1), b_grp, jnp.zeros((pad,), F32)])[None, :].astype(F32)
    return jnp.concatenate([wr_hi, wr_lo], axis=1), br, n_exp


def kernel(x, c, positions, ada_w, ada_b, norm_g, final_g, diff_w_in, diff_w_out, diff_lambda,
           diff_subln_g, nsa_w_in, nsa_w_out, nsa_cmp_pos, nsa_cmp_w1, nsa_cmp_b1, nsa_cmp_w2,
           moe_w_group, moe_b_group, moe_w_expert, moe_b_expert, moe_w_gate, moe_w_up, moe_w_down):
    batch, seq, d = x.shape
    depth = ada_w.shape[0]
    x2d = x.reshape(batch * seq, d)
    tables = _rope_tables(positions)
    mod = _adaln_mod(c, ada_w, ada_b)

    for i in range(depth):
        sh1, sc1, g1, sh2, sc2, g2 = jnp.split(mod[i], 6, axis=-1)
        j = i // N_MIXERS
        if i % N_MIXERS == 0:
            lambda_init = 0.8 - 0.6 * math.exp(-0.3 * i)
            qkv = _norm_proj(x2d, norm_g[i, 0], sc1, sh1, _diff_in_weights(diff_w_in[j], d), tables,
                             seq, tn=512, plain_tiles=range(2 * d // 512, 3 * d // 512))
            o = _diff_attention(qkv, diff_lambda[j], diff_subln_g[j], batch, seq, lambda_init)
            w_out = diff_w_out[j]
        else:
            w_main, w_gates = _nsa_in_weights(nsa_w_in[j], d)
            proj, gates_t = _norm_proj(x2d, norm_g[i, 0], sc1, sh1, w_main, tables, seq,
                                       tn=256, plain_tiles=(d // 256,), w_gates=w_gates)
            groups = NSA_KV_GROUPS
            kcvc = proj[:, d:d + 2 * groups * HEAD_DIM]
            ch = kcvc.reshape(batch, seq, 2, groups, HEAD_DIM).transpose(0, 2, 3, 1, 4)
            ch = ch.reshape(batch, 2, groups, seq // NSA_CMP_STRIDE, NSA_CMP_STRIDE * HEAD_DIM)
            cmp_kv = _nsa_compress(ch, nsa_cmp_pos[j], nsa_cmp_w1[j], nsa_cmp_b1[j], nsa_cmp_w2[j])
            o = _nsa_attention(proj, gates_t, cmp_kv, batch, seq, d)
            w_out = nsa_w_out[j]
        wr, br, n_exp = _router_weights(moe_w_group[i], moe_b_group[i], moe_w_expert[i], moe_b_expert[i])
        x2d, h2, comb, gid_t = _outproj_router(o, w_out.astype(BF16), x2d, g1, norm_g[i, 1], sc2, sh2,
                                               wr, br, seq, n_exp)
        wgu = jnp.concatenate([moe_w_gate[i], moe_w_up[i]], axis=-1).astype(BF16)
        x2d = _moe(h2, wgu, moe_w_down[i].astype(BF16), comb, gid_t, x2d, g2, final_g, seq,
                   final_norm=(i == depth - 1))
    return x2d.reshape(batch, seq, d)
```

```python
import functools
import math

import jax
import jax.numpy as jnp
from jax import lax
from jax.experimental import pallas as pl
from jax.experimental.pallas import tpu as pltpu

F32 = jnp.float32
BF16 = jnp.bfloat16
HIGHEST = lax.Precision.HIGHEST

ROPE_THETA = 500000.0
ROT_DIM = 16
NORM_EPS = 1e-6
NEG_INF = -1e30
HEAD_DIM = 64
DA_V_DIM = 2 * HEAD_DIM
NSA_KV_GROUPS = 2
NSA_CMP_BLOCK = 32
NSA_CMP_STRIDE = 16
NSA_SEL_BLOCK = 64
NSA_TOP_N = 16
NSA_WINDOW = 512
NSA_FORCE = 1e4
MOE_GROUPS = 4
MOE_EXPERTS_PER_GROUP = 8
MOE_TOP_K = 2
N_MIXERS = 2
LOG2E = 1.4426950408889634

LANES = 128
SUBLANES = 8
PROJ_ROW_CHUNK = 256
VMEM_LIMIT_BYTES = 56 * 1024 * 1024

NT_DIMS = (((1,), (1,)), ((), ()))
TN_DIMS = (((0,), (0,)), ((), ()))


def _cparams(*sem):
    return pltpu.CompilerParams(dimension_semantics=sem, vmem_limit_bytes=VMEM_LIMIT_BYTES)


def _mod_kernel(c_ref, w_ref, b_ref, o_ref):
    c = c_ref[...]
    ca = c * jax.nn.sigmoid(c)
    o_ref[0] = jnp.dot(ca, w_ref[0], precision=HIGHEST, preferred_element_type=F32) + b_ref[0]


def _adaln_mod(c, ada_w, ada_b):
    depth, d, n = ada_w.shape
    b = c.shape[0]
    rows = 8
    tn = 1536
    c_pad = jnp.zeros((rows, d), F32).at[:b].set(c)
    out = pl.pallas_call(
        _mod_kernel,
        grid=(depth, n // tn),
        in_specs=[
            pl.BlockSpec((rows, d), lambda i, j: (0, 0)),
            pl.BlockSpec((1, d, tn), lambda i, j: (i, 0, j)),
            pl.BlockSpec((1, 1, tn), lambda i, j: (i, 0, j)),
        ],
        out_specs=pl.BlockSpec((1, rows, tn), lambda i, j: (i, 0, j)),
        out_shape=jax.ShapeDtypeStruct((depth, rows, n), F32),
        compiler_params=_cparams("arbitrary", "arbitrary"),
        name="adaln_mod",
    )(c_pad, ada_w, ada_b.reshape(depth, 1, n))
    return out[:, :b, :]


def _rope_tables(positions):
    half = ROT_DIM // 2
    inv_freq = ROPE_THETA ** (-jnp.arange(0, ROT_DIM, 2, dtype=F32) / ROT_DIM)
    ang = positions.astype(F32).reshape(-1)[:, None] * inv_freq
    cos, sin = jnp.cos(ang), jnp.sin(ang)
    t = ang.shape[0]
    rest = HEAD_DIM - ROT_DIM
    z8 = jnp.zeros((t, half), F32)
    cos_h = jnp.concatenate([cos, cos, jnp.ones((t, rest), F32)], axis=-1)
    sina_h = jnp.concatenate([-sin, z8, jnp.zeros((t, rest), F32)], axis=-1)
    sinb_h = jnp.concatenate([z8, sin, jnp.zeros((t, rest), F32)], axis=-1)
    rep = LANES // HEAD_DIM
    return tuple(jnp.tile(a, (1, rep)) for a in (cos_h, sina_h, sinb_h))


def _rms_modulate(x, g, sc, sh):
    ms = jnp.mean(x * x, axis=-1, keepdims=True)
    y = x * lax.rsqrt(ms + NORM_EPS)
    return (y * g) * (1.0 + sc) + sh


def _proj_kernel(*refs, plain_tiles, with_gates):
    if with_gates:
        (x_ref, g_ref, sc_ref, sh_ref, w_ref, cos_ref, sa_ref, sb_ref, wg_ref,
         o_ref, gt_ref, h_scr) = refs
    else:
        x_ref, g_ref, sc_ref, sh_ref, w_ref, cos_ref, sa_ref, sb_ref, o_ref, h_scr = refs
    j = pl.program_id(1)
    tn = o_ref.shape[1]

    @pl.when(j == 0)
    def _():
        h = _rms_modulate(x_ref[...], g_ref[...], sc_ref[0], sh_ref[0])
        h_scr[...] = h.astype(BF16)
        if with_gates:
            logits = lax.dot_general(wg_ref[...], h_scr[...], NT_DIMS, preferred_element_type=F32)
            gt_ref[...] = jax.nn.sigmoid(logits)

    is_plain = functools.reduce(jnp.logical_or, [j == t for t in plain_tiles], j < 0)
    for r in range(h_scr.shape[0] // PROJ_ROW_CHUNK):
        rows = slice(r * PROJ_ROW_CHUNK, (r + 1) * PROJ_ROW_CHUNK)
        acc = jnp.dot(h_scr[rows, :], w_ref[...], preferred_element_type=F32)
        cos = jnp.where(is_plain, 1.0, cos_ref[rows, :])
        sa = jnp.where(is_plain, 0.0, sa_ref[rows, :])
        sb = jnp.where(is_plain, 0.0, sb_ref[rows, :])
        shift_up = pltpu.roll(acc, tn - ROT_DIM // 2, 1)
        shift_dn = pltpu.roll(acc, ROT_DIM // 2, 1)
        for c in range(tn // LANES):
            sl = slice(c * LANES, (c + 1) * LANES)
            y = acc[:, sl] * cos + shift_up[:, sl] * sa + shift_dn[:, sl] * sb
            o_ref[rows, sl] = y.astype(o_ref.dtype)


def _norm_proj(x2d, g, sc, sh, w, tables, seq, *, tn, plain_tiles, w_gates=None, tm=1024):
    t, d = x2d.shape
    n = w.shape[1]
    with_gates = w_gates is not None
    per_b = seq // tm
    in_specs = [
        pl.BlockSpec((tm, d), lambda i, j: (i, 0)),
        pl.BlockSpec((1, d), lambda i, j: (0, 0)),
        pl.BlockSpec((1, 1, d), lambda i, j: (i // per_b, 0, 0)),
        pl.BlockSpec((1, 1, d), lambda i, j: (i // per_b, 0, 0)),
        pl.BlockSpec((d, tn), lambda i, j: (0, j)),
        pl.BlockSpec((tm, LANES), lambda i, j: (i, 0)),
        pl.BlockSpec((tm, LANES), lambda i, j: (i, 0)),
        pl.BlockSpec((tm, LANES), lambda i, j: (i, 0)),
    ]
    args = [x2d, g.reshape(1, d), sc[:, None, :], sh[:, None, :], w, *tables]
    out_specs = [pl.BlockSpec((tm, tn), lambda i, j: (i, j))]
    out_shape = [jax.ShapeDtypeStruct((t, n), BF16)]
    if with_gates:
        ng = w_gates.shape[0]
        in_specs.append(pl.BlockSpec((ng, d), lambda i, j: (0, 0)))
        args.append(w_gates)
        out_specs.append(pl.BlockSpec((ng, tm), lambda i, j: (0, i)))
        out_shape.append(jax.ShapeDtypeStruct((ng, t), F32))
    res = pl.pallas_call(
        functools.partial(_proj_kernel, plain_tiles=tuple(plain_tiles), with_gates=with_gates),
        grid=(t // tm, n // tn),
        in_specs=in_specs,
        out_specs=out_specs,
        out_shape=out_shape,
        scratch_shapes=[pltpu.VMEM((tm, d), BF16)],
        compiler_params=_cparams("arbitrary", "arbitrary"),
        name="norm_proj",
    )(*args)
    return res if with_gates else res[0]


def _attn_scratch(tk, nq, dv):
    return [
        pltpu.VMEM((2, tk, nq), F32),
        pltpu.VMEM((2, 1, nq), F32),
        pltpu.VMEM((2, tk, nq), BF16),
        pltpu.VMEM((2, 1, nq), F32),
        pltpu.VMEM((1, nq), F32),
        pltpu.VMEM((1, nq), F32),
        pltpu.VMEM((dv, nq), F32),
    ]


def _pipelined_attention(n_full, n_diag, scores_full, scores_diag, v_tile, scratch, sum_in_v=False):
    s_scr, mx_scr, p_scr, al_scr, m_ref, l_ref, acc_ref = scratch

    def stage_scores(fn, tile, slot):
        s = fn(tile)
        s_scr[slot] = s
        mx_scr[slot] = jnp.max(s, axis=0, keepdims=True)

    def stage_softmax(slot):
        m_prev = m_ref[...]
        m_new = jnp.maximum(m_prev, mx_scr[slot])
        alpha = jnp.exp2(m_prev - m_new)
        p = jnp.exp2(s_scr[slot] - m_new)
        if not sum_in_v:
            l_ref[...] = alpha * l_ref[...] + jnp.sum(p, axis=0, keepdims=True)
        m_ref[...] = m_new
        al_scr[slot] = alpha
        p_scr[slot] = p.astype(BF16)

    def stage_pv(pos, slot):
        tile = jnp.where(pos < n_diag, n_full + jnp.maximum(pos, 0), pos - n_diag)
        pv = lax.dot_general(v_tile(tile), p_scr[slot], TN_DIMS, preferred_element_type=F32)
        acc_ref[...] = acc_ref[...] * al_scr[slot] + pv

    def step(fn, pos, slot):
        nxt = pos + 1
        stage_scores(fn, jnp.where(nxt < n_diag, n_full + nxt, nxt - n_diag), 1 - slot)
        stage_softmax(slot)
        stage_pv(pos - 1, 1 - slot)

    m_ref[...] = jnp.full(m_ref.shape, NEG_INF, F32)
    l_ref[...] = jnp.zeros(l_ref.shape, F32)
    acc_ref[...] = jnp.zeros(acc_ref.shape, F32)
    p_scr[1] = jnp.zeros(p_scr.shape[1:], BF16)
    al_scr[1] = jnp.ones(al_scr.shape[1:], F32)
    stage_scores(scores_diag, n_full, 0)
    for p in range(n_diag - 1):
        step(scores_diag, p, p % 2)

    first = n_diag - 1
    s0 = first % 2

    def pair(jj, carry):
        step(scores_full, first + 2 * jj, s0)
        step(scores_full, first + 2 * jj + 1, 1 - s0)
        return carry

    lax.fori_loop(0, n_full // 2, pair, 0)
    odd = n_full % 2 == 1
    last = first + n_full

    @pl.when(odd)
    def _():
        step(scores_full, last - 1, s0)
        stage_softmax(1 - s0)
        stage_pv(last - 1, s0)
        stage_pv(last, 1 - s0)

    @pl.when(jnp.logical_not(odd))
    def _():
        stage_softmax(s0)
        stage_pv(last - 1, 1 - s0)
        stage_pv(last, s0)


def _split_halves(q, rows):
    lane = lax.broadcasted_iota(jnp.int32, q.shape, 1)
    zero = jnp.zeros_like(q)
    return jnp.where(lane < HEAD_DIM, q, zero), jnp.where(lane >= HEAD_DIM, q, zero)


def _diff_attn_kernel(q_ref, k_ref, v_ref, lam_ref, g_ref, o_ref, qm, *scratch,
                      tq, tk, lambda_init):
    qi = pl.program_id(2)
    lo, hi = _split_halves(q_ref[...], tq)
    qm[0:tq, :] = lo
    qm[tq:2 * tq, :] = hi
    q0 = qi * tq
    acc_ref, l_ref = scratch[-1], scratch[-2]

    def scores_full(t):
        k = k_ref[pl.ds(pl.multiple_of(t * tk, tk), tk), :]
        return lax.dot_general(k, qm[...], NT_DIMS, preferred_element_type=F32)

    def scores_diag(t):
        s = scores_full(t)
        k_idx = t * tk + lax.broadcasted_iota(jnp.int32, s.shape, 0)
        q_idx = q0 + lax.broadcasted_iota(jnp.int32, s.shape, 1) % tq
        return jnp.where(k_idx <= q_idx, s, NEG_INF)

    def v_tile(t):
        return v_ref[pl.ds(pl.multiple_of(t * tk, tk), tk), :]

    _pipelined_attention(q0 // tk, tq // tk, scores_full, scores_diag, v_tile, scratch)

    lam = lam_ref[...]
    lam_full = (jnp.exp(jnp.sum(lam[0:1] * lam[1:2], axis=-1, keepdims=True))
                - jnp.exp(jnp.sum(lam[2:3] * lam[3:4], axis=-1, keepdims=True)) + lambda_init)
    acc = acc_ref[...]
    l = jnp.maximum(l_ref[...], 1e-30)
    ot = acc[:, 0:tq] / l[:, 0:tq] - lam_full * (acc[:, tq:2 * tq] / l[:, tq:2 * tq])
    ms = jnp.mean(ot * ot, axis=0, keepdims=True)
    ot = (ot * lax.rsqrt(ms + NORM_EPS)) * g_ref[...] * (1.0 - lambda_init)
    o_ref[...] = ot.T.astype(o_ref.dtype)


def _diff_attention(qkv, lam, subln_g, batch, seq, lambda_init, *, tq=2048, tk=512):
    assert tq % tk == 0 and seq % tq == 0
    t, n3 = qkv.shape
    d = n3 // 3
    heads = d // DA_V_DIM
    nq = seq // tq
    cb = d // LANES
    g_b = jnp.broadcast_to(subln_g.astype(F32)[:, None], (DA_V_DIM, tq))
    return pl.pallas_call(
        functools.partial(_diff_attn_kernel, tq=tq, tk=tk, lambda_init=lambda_init),
        grid=(batch, heads, nq),
        in_specs=[
            pl.BlockSpec((tq, LANES), lambda b, h, i: (b * nq + i, h)),
            pl.BlockSpec((seq, LANES), lambda b, h, i: (b, cb + h), pipeline_mode=pl.Buffered(1)),
            pl.BlockSpec((seq, LANES), lambda b, h, i: (b, 2 * cb + h), pipeline_mode=pl.Buffered(1)),
            pl.BlockSpec(lam.shape, lambda b, h, i: (0, 0)),
            pl.BlockSpec((DA_V_DIM, tq), lambda b, h, i: (0, 0)),
        ],
        out_specs=pl.BlockSpec((tq, LANES), lambda b, h, i: (b * nq + i, h)),
        out_shape=jax.ShapeDtypeStruct((t, d), BF16),
        scratch_shapes=[pltpu.VMEM((2 * tq, LANES), BF16)] + _attn_scratch(tk, 2 * tq, DA_V_DIM),
        compiler_params=_cparams("arbitrary", "arbitrary", "arbitrary"),
        name="diff_attn",
    )(qkv, qkv, qkv, lam.astype(F32), g_b)


def _route(logits, n_exp):
    lane = lax.broadcasted_iota(jnp.int32, logits.shape, 1)
    lane_f = lane.astype(F32)
    big = float(LANES)
    is_grp = (lane >= n_exp) & (lane < n_exp + MOE_GROUPS)
    gl = jnp.where(is_grp, logits, -jnp.inf)
    ge = jnp.where(is_grp, jnp.exp(gl - jnp.max(gl, axis=-1, keepdims=True)), 0.0)
    pg = ge / jnp.sum(ge, axis=-1, keepdims=True)
    p_top = jnp.max(pg, axis=-1, keepdims=True)
    grp = jnp.min(jnp.where(is_grp & (pg == p_top), lane_f, big), axis=-1, keepdims=True) - n_exp
    in_grp = (lane < n_exp) & ((lane // MOE_EXPERTS_PER_GROUP).astype(F32) == grp)
    sel = jnp.where(in_grp, logits, -jnp.inf)
    v1 = jnp.max(sel, axis=-1, keepdims=True)
    i1 = jnp.min(jnp.where(sel == v1, lane_f, big), axis=-1, keepdims=True)
    sel2 = jnp.where(lane_f == i1, -jnp.inf, sel)
    v2 = jnp.max(sel2, axis=-1, keepdims=True)
    i2 = jnp.min(jnp.where(sel2 == v2, lane_f, big), axis=-1, keepdims=True)
    e2 = jnp.exp(v2 - v1)
    den = 1.0 + e2
    w1 = (1.0 / den) * p_top
    w2 = (e2 / den) * p_top
    return jnp.where(lane_f == i1, w1, 0.0) + jnp.where(lane_f == i2, w2, 0.0), grp


def _outproj_router_kernel(o_ref, w_ref, x_ref, g1_ref, ng_ref, sc_ref, sh_ref, wr_ref, br_ref,
                           xo_ref, h_ref, comb_ref, gid_ref, *, n_exp):
    mix = jnp.dot(o_ref[...], w_ref[...], preferred_element_type=F32)
    x = x_ref[...] + g1_ref[0] * mix
    xo_ref[...] = x
    h = _rms_modulate(x, ng_ref[...], sc_ref[0], sh_ref[0])
    h_hi = h.astype(BF16)
    h_ref[...] = h_hi
    h_lo = (h - h_hi.astype(F32)).astype(BF16)
    wr = wr_ref[...]
    hw = jnp.dot(h_hi, wr, preferred_element_type=F32)
    logits = (hw[:, 0:LANES] + hw[:, LANES:2 * LANES]
              + jnp.dot(h_lo, wr[:, 0:LANES], preferred_element_type=F32) + br_ref[...])
    comb, grp = _route(logits, n_exp)
    comb_ref[...] = comb[:, 0:n_exp]
    gid_ref[...] = jnp.broadcast_to(grp, logits.shape).T[0:gid_ref.shape[0], :]


def _outproj_router(o, w_out, x2d, g1, ng, sc, sh, wr, br, seq, n_exp, *, tm=1024):
    t, d = x2d.shape
    per_b = seq // tm
    bvec = lambda i: (i // per_b, 0, 0)
    return pl.pallas_call(
        functools.partial(_outproj_router_kernel, n_exp=n_exp),
        grid=(t // tm,),
        in_specs=[
            pl.BlockSpec((tm, d), lambda i: (i, 0)),
            pl.BlockSpec((d, d), lambda i: (0, 0)),
            pl.BlockSpec((tm, d), lambda i: (i, 0)),
            pl.BlockSpec((1, 1, d), bvec),
            pl.BlockSpec((1, d), lambda i: (0, 0)),
            pl.BlockSpec((1, 1, d), bvec),
            pl.BlockSpec((1, 1, d), bvec),
            pl.BlockSpec((d, 2 * LANES), lambda i: (0, 0)),
            pl.BlockSpec((1, LANES), lambda i: (0, 0)),
        ],
        out_specs=[
            pl.BlockSpec((tm, d), lambda i: (i, 0)),
            pl.BlockSpec((tm, d), lambda i: (i, 0)),
            pl.BlockSpec((tm, n_exp), lambda i: (i, 0)),
            pl.BlockSpec((8, tm), lambda i: (0, i)),
        ],
        out_shape=[
            jax.ShapeDtypeStruct((t, d), F32),
            jax.ShapeDtypeStruct((t, d), BF16),
            jax.ShapeDtypeStruct((t, n_exp), F32),
            jax.ShapeDtypeStruct((8, t), F32),
        ],
        compiler_params=_cparams("arbitrary"),
        name="outproj_router",
    )(o, w_out, x2d, g1[:, None, :], ng.reshape(1, d), sc[:, None, :], sh[:, None, :], wr, br)


MOE_CHUNK = 128
MOE_EXPERTS_PER_STEP = 4


def _moe_kernel(h_ref, wgu_ref, wd_ref, comb_ref, gid_ref, tri_ref, x_ref, g2_ref, fg_ref, o_ref,
                perm_scr, hs_scr, cs_scr, ys_scr, off_smem, nch_smem, *, hidden, final_norm):
    step = pl.program_id(1)
    tm = h_ref.shape[0]
    rows = perm_scr.shape[0]
    n_exp = comb_ref.shape[1]
    steps_per_group = MOE_EXPERTS_PER_GROUP // MOE_EXPERTS_PER_STEP

    @pl.when(step == 0)
    def _():
        gid = gid_ref[0:1, :]
        grow = lax.broadcasted_iota(jnp.int32, (8, tm), 0).astype(F32)
        onehot = jnp.where(grow == gid, 1.0, 0.0)
        rank = jnp.dot(onehot.astype(BF16), tri_ref[...], preferred_element_type=F32)
        off = jnp.int32(0)
        off_col = jnp.zeros((8, 1), F32)
        grow_col = lax.broadcasted_iota(jnp.int32, (8, 1), 0)
        for g in range(MOE_GROUPS):
            cnt = jnp.sum(onehot[g:g + 1, :]).astype(jnp.int32)
            nch = (cnt + (MOE_CHUNK - 1)) // MOE_CHUNK
            off_smem[g] = off
            nch_smem[g] = nch
            off_col = jnp.where(grow_col == g, off.astype(F32), off_col)
            off = off + nch * MOE_CHUNK
        dest = jnp.sum(onehot * (rank + off_col), axis=0, keepdims=True)
        row_id = lax.broadcasted_iota(jnp.int32, (rows, tm), 0)
        perm = jnp.where(row_id == dest.astype(jnp.int32), 1.0, 0.0).astype(BF16)
        perm_scr[...] = perm
        hs_scr[...] = jnp.dot(perm, h_ref[...], preferred_element_type=F32).astype(BF16)
        comb = comb_ref[...]
        comb_hi = comb.astype(BF16)
        comb_lo = (comb - comb_hi.astype(F32)).astype(BF16)
        cs_scr[...] = (jnp.dot(perm, comb_hi, preferred_element_type=F32)
                       + jnp.dot(perm, comb_lo, preferred_element_type=F32))
        ys_scr[...] = jnp.zeros(ys_scr.shape, F32)

    g = step // steps_per_group
    off = off_smem[g]
    lane = lax.broadcasted_iota(jnp.int32, (MOE_CHUNK, n_exp), 1)

    def chunk(c, carry):
        r0 = pl.multiple_of(off + c * MOE_CHUNK, MOE_CHUNK)
        xs = hs_scr[pl.ds(r0, MOE_CHUNK), :]
        cs = cs_scr[pl.ds(r0, MOE_CHUNK), :]
        y = jnp.zeros((MOE_CHUNK, xs.shape[1]), F32)
        for k in range(MOE_EXPERTS_PER_STEP):
            gu = jnp.dot(xs, wgu_ref[k], preferred_element_type=F32)
            gate, up = gu[:, 0:hidden], gu[:, hidden:2 * hidden]
            a = (gate * jax.nn.sigmoid(gate)) * up
            e = step * MOE_EXPERTS_PER_STEP + k
            ce = jnp.sum(jnp.where(lane == e, cs, 0.0), axis=-1, keepdims=True)
            y = y + jnp.dot((a * ce).astype(BF16), wd_ref[k], preferred_element_type=F32)
        ys_scr[pl.ds(r0, MOE_CHUNK), :] += y
        return carry

    lax.fori_loop(0, nch_smem[g], chunk, 0)

    @pl.when(step == pl.num_programs(1) - 1)
    def _():
        y = lax.dot_general(perm_scr[...], ys_scr[...].astype(BF16), TN_DIMS,
                            preferred_element_type=F32)
        x = x_ref[...] + g2_ref[0] * y
        if final_norm:
            ms = jnp.mean(x * x, axis=-1, keepdims=True)
            x = (x * lax.rsqrt(ms + NORM_EPS)) * fg_ref[...]
        o_ref[...] = x


def _moe(h, wgu, wd, comb, gid_t, x2d, g2, final_g, seq, *, final_norm, tm=1024):
    t, d = x2d.shape
    n_exp, _, two_h = wgu.shape
    hidden = two_h // 2
    per_b = seq // tm
    eps = MOE_EXPERTS_PER_STEP
    rows = tm + MOE_GROUPS * MOE_CHUNK
    idx = jnp.arange(tm)
    tri = (idx[:, None] < idx[None, :]).astype(BF16)
    return pl.pallas_call(
        functools.partial(_moe_kernel, hidden=hidden, final_norm=final_norm),
        grid=(t // tm, n_exp // eps),
        in_specs=[
            pl.BlockSpec((tm, d), lambda i, s: (i, 0)),
            pl.BlockSpec((eps, d, two_h), lambda i, s: (s, 0, 0)),
            pl.BlockSpec((eps, hidden, d), lambda i, s: (s, 0, 0)),
            pl.BlockSpec((tm, n_exp), lambda i, s: (i, 0)),
            pl.BlockSpec((8, tm), lambda i, s: (0, i)),
            pl.BlockSpec((tm, tm), lambda i, s: (0, 0)),
            pl.BlockSpec((tm, d), lambda i, s: (i, 0)),
            pl.BlockSpec((1, 1, d), lambda i, s: (i // per_b, 0, 0)),
            pl.BlockSpec((1, d), lambda i, s: (0, 0)),
        ],
        out_specs=pl.BlockSpec((tm, d), lambda i, s: (i, 0)),
        out_shape=jax.ShapeDtypeStruct((t, d), F32),
        scratch_shapes=[
            pltpu.VMEM((rows, tm), BF16),
            pltpu.VMEM((rows, d), BF16),
            pltpu.VMEM((rows, n_exp), F32),
            pltpu.VMEM((rows, d), F32),
            pltpu.SMEM((MOE_GROUPS,), jnp.int32),
            pltpu.SMEM((MOE_GROUPS,), jnp.int32),
        ],
        compiler_params=_cparams("arbitrary", "arbitrary"),
        name="moe",
    )(h, wgu, wd, comb, gid_t, tri, x2d, g2[:, None, :], final_g.reshape(1, d))


def _compress_kernel(ch_ref, w1a_ref, w1b_ref, pos_ref, w1f_ref, b1_ref, w2_ref, o_ref):
    ch = ch_ref[0, 0, 0]
    n = ch.shape[0]
    a = jnp.dot(ch, w1a_ref[0], preferred_element_type=F32)
    b = jnp.dot(ch, w1b_ref[0], preferred_element_type=F32)
    const = jnp.dot(pos_ref[0], w1f_ref[0], precision=HIGHEST, preferred_element_type=F32)[0:1]
    hid = a + pltpu.roll(b, n - 1, 0) + (const + b1_ref[0])
    act = jax.nn.gelu(hid, approximate=True)
    o_ref[0, 0, 0] = jnp.dot(act.astype(BF16), w2_ref[0], preferred_element_type=F32).astype(o_ref.dtype)


def _nsa_compress(ch, cmp_pos, cmp_w1, cmp_b1, cmp_w2):
    b, two, g, nc, f = ch.shape
    hid = cmp_w1.shape[-1]
    w1a = cmp_w1[:, :f, :].astype(BF16)
    w1b = cmp_w1[:, f:, :].astype(BF16)
    pos = jnp.zeros((two, 8, 2 * f), F32).at[:, 0, :].set(cmp_pos.reshape(two, 2 * f))
    w2 = jnp.concatenate([cmp_w2, cmp_w2], axis=-1).astype(BF16)
    return pl.pallas_call(
        _compress_kernel,
        grid=(b, two, g),
        in_specs=[
            pl.BlockSpec((1, 1, 1, nc, f), lambda i, j, k: (i, j, k, 0, 0)),
            pl.BlockSpec((1, f, hid), lambda i, j, k: (j, 0, 0)),
            pl.BlockSpec((1, f, hid), lambda i, j, k: (j, 0, 0)),
            pl.BlockSpec((1, 8, 2 * f), lambda i, j, k: (j, 0, 0)),
            pl.BlockSpec((1, 2 * f, hid), lambda i, j, k: (j, 0, 0)),
            pl.BlockSpec((1, 1, hid), lambda i, j, k: (j, 0, 0)),
            pl.BlockSpec((1, hid, LANES), lambda i, j, k: (j, 0, 0)),
        ],
        out_specs=pl.BlockSpec((1, 1, 1, nc, LANES), lambda i, j, k: (i, j, k, 0, 0)),
        out_shape=jax.ShapeDtypeStruct((b, two, g, nc, LANES), BF16),
        compiler_params=_cparams("arbitrary", "arbitrary", "arbitrary"),
        name="nsa_compress",
    )(ch, w1a, w1b, pos, cmp_w1.astype(F32), cmp_b1[:, None, :].astype(F32), w2)


def _stack_group_queries(q_ref, qm, tq):
    for j in range(q_ref.shape[1] // LANES):
        lo, hi = _split_halves(q_ref[:, j * LANES:(j + 1) * LANES], tq)
        qm[(2 * j) * tq:(2 * j + 1) * tq, :] = lo
        qm[(2 * j + 1) * tq:(2 * j + 2) * tq, :] = hi


def _own_group_values_with_ones(v, g):
    lane = lax.broadcasted_iota(jnp.int32, v.shape, 1)
    own_half = (lane >= HEAD_DIM) == (g == 1)
    return jnp.where(own_half, v, jnp.ones_like(v))


def _gated_pairs_to_token_major(ot, gt_ref, branch, tq, hpg):
    g = pl.program_id(1)
    tiles = []
    for j in range(hpg // 2):
        parts = []
        for h in (2 * j, 2 * j + 1):
            row = gt_ref[pl.ds((g * hpg + h) * 3 + branch, 1), :]
            parts.append(ot[:, h * tq:(h + 1) * tq] * row)
        tiles.append(jnp.concatenate(parts, axis=0).T)
    return tiles


NSA_CMP_CLASSES = 4


def _nsa_window_tiles(q0, qm, kw_ref, vw_ref, gt_ref, acc_ref, *, tq, hpg):
    g = pl.program_id(1)
    span = NSA_WINDOW + tq
    start = pl.multiple_of(jnp.maximum(q0 - NSA_WINDOW, 0), tq)
    k = kw_ref[pl.ds(start, span), :]
    v = vw_ref[pl.ds(start, span), :]
    s = lax.dot_general(k, qm[...], NT_DIMS, preferred_element_type=F32)
    k_idx = start + lax.broadcasted_iota(jnp.int32, (span, tq), 0)
    q_idx = q0 + lax.broadcasted_iota(jnp.int32, (span, tq), 1)
    dist = q_idx - k_idx
    bias = jnp.where(dist >= 0, jnp.where(dist < NSA_WINDOW, 0.0, NEG_INF), NEG_INF)
    s = s + jnp.concatenate([bias] * hpg, axis=1)
    m = jnp.max(s, axis=0, keepdims=True)
    e = jnp.exp2(s - m)
    r = 1.0 / jnp.maximum(jnp.sum(e, axis=0, keepdims=True), 1e-30)
    acc_ref[...] = lax.dot_general(v, e.astype(BF16), TN_DIMS, preferred_element_type=F32) * r
    ot = acc_ref[pl.ds(pl.multiple_of(g * HEAD_DIM, HEAD_DIM), HEAD_DIM), :]
    return _gated_pairs_to_token_major(ot, gt_ref, 2, tq, hpg)


def _nsa_cmp_body(q0, qm, kc_ref, vc_ref, mt_ref, gt_ref, kw_ref, vw_ref, o_ref, mask_ref, acc_ref,
                  *, tq, hpg, n_top, n_cmp, n_blk):
    win_tiles = _nsa_window_tiles(q0, qm, kw_ref, vw_ref, gt_ref, acc_ref, tq=tq, hpg=hpg)
    kc = kc_ref[0, 0, 0, 0:n_cmp, :]
    s = lax.dot_general(kc, qm[...], NT_DIMS, preferred_element_type=F32)
    n_idx = lax.broadcasted_iota(jnp.int32, (n_cmp, tq), 0)
    q_idx = q0 + lax.broadcasted_iota(jnp.int32, (n_cmp, tq), 1)
    valid = n_idx * NSA_CMP_STRIDE + (NSA_CMP_BLOCK - 1) <= q_idx
    bias = jnp.where(valid, 0.0, NEG_INF)
    s = s + jnp.concatenate([bias] * hpg, axis=1)
    m = jnp.maximum(jnp.max(s, axis=0, keepdims=True), 0.1 * NEG_INF)
    e = jnp.exp2(s - m)
    r = 1.0 / jnp.maximum(jnp.sum(e, axis=0, keepdims=True), 1e-30)
    p = e * r
    ot = lax.dot_general(vc_ref[0, 0, 0, 0:n_cmp, :], p.astype(BF16), TN_DIMS,
                         preferred_element_type=F32)
    tiles = _gated_pairs_to_token_major(ot[0:HEAD_DIM], gt_ref, 0, tq, hpg)
    for j, (tile, win) in enumerate(zip(tiles, win_tiles)):
        o_ref[:, j * LANES:(j + 1) * LANES] = (tile + win).astype(o_ref.dtype)

    psum = p[:, 0:tq]
    for h in range(1, hpg):
        psum = psum + p[:, h * tq:(h + 1) * tq]
    p_hi = psum.astype(BF16)
    p_lo = (psum - p_hi.astype(F32)).astype(BF16)
    mt = mt_ref[0:n_blk, 0:n_cmp]
    imp = (jnp.dot(mt, p_hi, preferred_element_type=F32)
           + jnp.dot(mt, p_lo, preferred_element_type=F32))
    j_idx = lax.broadcasted_iota(jnp.int32, imp.shape, 0)
    j_f = j_idx.astype(F32)
    q_blk = (q0 + lax.broadcasted_iota(jnp.int32, imp.shape, 1)) // NSA_SEL_BLOCK
    forced = (j_idx == 0) | (j_idx == q_blk) | (j_idx == q_blk - 1)
    val = jnp.where(forced, -jnp.inf, jnp.where(j_idx <= q_blk, imp, -NSA_FORCE))
    sel_bias = jnp.where(forced, 0.0, NEG_INF)
    for _ in range(max(n_top - 3, 0)):
        mx = jnp.max(val, axis=0, keepdims=True)
        first = jnp.min(jnp.where(val == mx, j_f, float(n_blk)), axis=0, keepdims=True)
        hit = j_f == first
        sel_bias = jnp.where(hit, 0.0, sel_bias)
        val = jnp.where(hit, -jnp.inf, val)
    n_sel = mask_ref.shape[2]
    mask_ref[0, 0, 0:n_blk, :] = sel_bias
    if n_blk < n_sel:
        mask_ref[0, 0, n_blk:n_sel, :] = jnp.full((n_sel - n_blk, tq), NEG_INF, F32)


def _nsa_cmp_win_kernel(q_ref, kc_ref, vc_ref, mt_ref, gt_ref, kw_ref, vw_ref, o_ref, mask_ref,
                        qm, acc_ref, *, tq, hpg, n_top):
    qi = pl.program_id(2)
    q0 = qi * tq
    _stack_group_queries(q_ref, qm, tq)
    nc = kc_ref.shape[3]
    n_sel = mask_ref.shape[2]
    seq = n_sel * NSA_SEL_BLOCK
    cls = (q0 + tq - 1) // (seq // NSA_CMP_CLASSES)
    for c in range(NSA_CMP_CLASSES):
        @pl.when(cls == c)
        def _(c=c):
            _nsa_cmp_body(q0, qm, kc_ref, vc_ref, mt_ref, gt_ref, kw_ref, vw_ref, o_ref, mask_ref,
                          acc_ref, tq=tq, hpg=hpg,
                          n_top=min(n_top, (c + 1) * n_sel // NSA_CMP_CLASSES),
                          n_cmp=(c + 1) * nc // NSA_CMP_CLASSES,
                          n_blk=(c + 1) * n_sel // NSA_CMP_CLASSES)


def _nsa_sel_kernel(q_ref, k_ref, v_ref, mask_ref, gt_ref, ocw_ref, o_ref,
                    qm, *scratch, tq, tk, hpg):
    g = pl.program_id(1)
    qi = pl.program_id(2)
    q0 = qi * tq
    _stack_group_queries(q_ref, qm, tq)
    acc_ref, l_ref = scratch[-1], scratch[-2]
    blocks = tk // NSA_SEL_BLOCK

    def scores_full(t):
        k = k_ref[pl.ds(pl.multiple_of(t * tk, tk), tk), :]
        s = lax.dot_general(k, qm[...], NT_DIMS, preferred_element_type=F32)
        if blocks >= SUBLANES:
            rows = mask_ref[0, 0, pl.ds(pl.multiple_of(t * blocks, blocks), blocks), :]
        else:
            per = SUBLANES // blocks
            grp = mask_ref[0, 0, pl.ds(pl.multiple_of((t // per) * SUBLANES, SUBLANES), SUBLANES), :]
            rows = grp[0:blocks]
            for i in range(1, per):
                rows = jnp.where(t % per == i, grp[i * blocks:(i + 1) * blocks], rows)
        exp_rows = [jnp.broadcast_to(rows[r:r + 1, :], (NSA_SEL_BLOCK, tq)) for r in range(blocks)]
        bias = jnp.concatenate(exp_rows, axis=0)
        return s + jnp.concatenate([bias] * hpg, axis=1)

    def scores_diag(t):
        s = scores_full(t)
        k_idx = t * tk + lax.broadcasted_iota(jnp.int32, s.shape, 0)
        q_idx = q0 + lax.broadcasted_iota(jnp.int32, s.shape, 1) % tq
        return jnp.where(k_idx <= q_idx, s, NEG_INF)

    def v_tile(t):
        return _own_group_values_with_ones(v_ref[pl.ds(pl.multiple_of(t * tk, tk), tk), :], g)

    _pipelined_attention(q0 // tk, tq // tk, scores_full, scores_diag, v_tile, scratch,
                         sum_in_v=True)

    own = pl.multiple_of(g * HEAD_DIM, HEAD_DIM)
    other = pl.multiple_of((1 - g) * HEAD_DIM, HEAD_DIM)
    inv_l = 1.0 / jnp.maximum(acc_ref[pl.ds(other, 1), :], 1e-30)
    ot = acc_ref[pl.ds(own, HEAD_DIM), :] * inv_l
    tiles = _gated_pairs_to_token_major(ot, gt_ref, 1, tq, hpg)
    for j, tile in enumerate(tiles):
        sl = slice(j * LANES, (j + 1) * LANES)
        total = tile + ocw_ref[:, sl].astype(F32)
        o_ref[:, sl] = total.astype(o_ref.dtype)


def _nsa_attention(proj, gates_t, cmp_kv, batch, seq, d_model, *, tq=128, tq_sel=512, tk=512):
    assert tq_sel % tk == 0 and seq % tq_sel == 0
    t = proj.shape[0]
    groups = NSA_KV_GROUPS
    hpg = d_model // HEAD_DIM // groups
    gw = hpg * HEAD_DIM
    nq = seq // tq
    nq_sel = seq // tq_sel
    nc = cmp_kv.shape[3]
    n_sel = seq // NSA_SEL_BLOCK
    n_top = min(NSA_TOP_N, n_sel)
    ng = gates_t.shape[0]
    qcb = d_model // LANES

    ci = jnp.arange(nc)[None, :] * NSA_CMP_STRIDE
    sj = jnp.arange(n_sel)[:, None] * NSA_SEL_BLOCK
    mt = ((ci < sj + NSA_SEL_BLOCK) & (ci + NSA_CMP_BLOCK > sj)
          & (jnp.arange(nc)[None, :] < nc - 1)).astype(BF16)

    q_spec = pl.BlockSpec((tq, gw), lambda b, g, i: (b * nq + i, g))
    gt_spec = pl.BlockSpec((ng, tq), lambda b, g, i: (0, b * nq + i))
    o_spec = pl.BlockSpec((tq, gw), lambda b, g, i: (b * nq + i, g))
    o_shape = jax.ShapeDtypeStruct((t, d_model), BF16)
    sem = ("arbitrary", "arbitrary", "arbitrary")
    qm_scr = pltpu.VMEM((hpg * tq, LANES), BF16)

    o_cmp_win, sel_mask = pl.pallas_call(
        functools.partial(_nsa_cmp_win_kernel, tq=tq, hpg=hpg, n_top=n_top),
        grid=(batch, groups, nq),
        in_specs=[
            q_spec,
            pl.BlockSpec((1, 1, 1, nc, LANES), lambda b, g, i: (b, 0, g, 0, 0)),
            pl.BlockSpec((1, 1, 1, nc, LANES), lambda b, g, i: (b, 1, g, 0, 0)),
            pl.BlockSpec((n_sel, nc), lambda b, g, i: (0, 0)),
            gt_spec,
            pl.BlockSpec((seq, LANES), lambda b, g, i: (b, qcb + 5 + g)),
            pl.BlockSpec((seq, LANES), lambda b, g, i: (b, qcb + 7)),
        ],
        out_specs=[o_spec, pl.BlockSpec((1, 1, n_sel, tq), lambda b, g, i: (b, g, 0, i))],
        out_shape=[o_shape, jax.ShapeDtypeStruct((batch, groups, n_sel, seq), F32)],
        scratch_shapes=[qm_scr, pltpu.VMEM((LANES, hpg * tq), F32)],
        compiler_params=_cparams(*sem),
        name="nsa_cmp_win",
    )(proj, cmp_kv, cmp_kv, mt, gates_t, proj, proj)

    rows_sel = pl.BlockSpec((tq_sel, gw), lambda b, g, i: (b * nq_sel + i, g))
    return pl.pallas_call(
        functools.partial(_nsa_sel_kernel, tq=tq_sel, tk=tk, hpg=hpg),
        grid=(batch, groups, nq_sel),
        in_specs=[
            rows_sel,
            pl.BlockSpec((seq, LANES), lambda b, g, i: (b, qcb + 2 + g)),
            pl.BlockSpec((seq, LANES), lambda b, g, i: (b, qcb + 4)),
            pl.BlockSpec((1, 1, n_sel, tq_sel), lambda b, g, i: (b, g, 0, i)),
            pl.BlockSpec((ng, tq_sel), lambda b, g, i: (0, b * nq_sel + i)),
            rows_sel,
        ],
        out_specs=rows_sel,
        out_shape=o_shape,
        scratch_shapes=[pltpu.VMEM((hpg * tq_sel, LANES), BF16)] + _attn_scratch(tk, hpg * tq_sel, LANES),
        compiler_params=_cparams(*sem),
        name="nsa_sel",
    )(proj, proj, proj, sel_mask, gates_t, o_cmp_win)


def _diff_in_weights(w_in, d_model):
    scale = HEAD_DIM ** -0.5 * LOG2E
    wq = w_in[:, :d_model] * scale
    return jnp.concatenate([wq, w_in[:, d_model:]], axis=1).astype(BF16)


def _nsa_in_weights(w_in, d_model):
    g, hd = NSA_KV_GROUPS, HEAD_DIM
    kvw = g * hd
    scale = hd ** -0.5 * LOG2E
    q = w_in[:, :d_model] * scale
    off = d_model
    sec = [w_in[:, off + i * kvw: off + (i + 1) * kvw] for i in range(6)]
    dup = lambda w: jnp.concatenate([w[:, i * hd:(i + 1) * hd] for i in range(g) for _ in range(2)], axis=1)
    w_main = jnp.concatenate([q, sec[0], sec[1], dup(sec[2]), sec[3], dup(sec[4]), sec[5]], axis=1)
    w_gates = w_in[:, off + 6 * kvw:].T
    return w_main.astype(BF16), w_gates.astype(BF16)


def _router_weights(w_grp, b_grp, w_exp, b_exp):
    d = w_grp.shape[0]
    n_exp = w_exp.shape[0] * w_exp.shape[2]
    we = jnp.transpose(w_exp, (1, 0, 2)).reshape(d, n_exp)
    pad = LANES - n_exp - w_grp.shape[1]
    wr = jnp.concatenate([we, w_grp, jnp.zeros((d, pad), F32)], axis=1).astype(F32)
    wr_hi = wr.astype(BF16)
    wr_lo = (wr - wr_hi.astype(F32)).astype(BF16)
    br = jnp.concatenate([b_exp.reshape(-1), b_grp, jnp.zeros((pad,), F32)])[None, :].astype(F32)
    return jnp.concatenate([wr_hi, wr_lo], axis=1), br, n_exp


def kernel(x, c, positions, ada_w, ada_b, norm_g, final_g, diff_w_in, diff_w_out, diff_lambda,
           diff_subln_g, nsa_w_in, nsa_w_out, nsa_cmp_pos, nsa_cmp_w1, nsa_cmp_b1, nsa_cmp_w2,
           moe_w_group, moe_b_group, moe_w_expert, moe_b_expert, moe_w_gate, moe_w_up, moe_w_down):
    batch, seq, d = x.shape
    depth = ada_w.shape[0]
    x2d = x.reshape(batch * seq, d)
    tables = _rope_tables(positions)
    mod = _adaln_mod(c, ada_w, ada_b)

    for i in range(depth):
        sh1, sc1, g1, sh2, sc2, g2 = jnp.split(mod[i], 6, axis=-1)
        j = i // N_MIXERS
        if i % N_MIXERS == 0:
            lambda_init = 0.8 - 0.6 * math.exp(-0.3 * i)
            qkv = _norm_proj(x2d, norm_g[i, 0], sc1, sh1, _diff_in_weights(diff_w_in[j], d), tables,
                             seq, tn=512, plain_tiles=range(2 * d // 512, 3 * d // 512))
            o = _diff_attention(qkv, diff_lambda[j], diff_subln_g[j], batch, seq, lambda_init)
            w_out = diff_w_out[j]
        else:
            w_main, w_gates = _nsa_in_weights(nsa_w_in[j], d)
            proj, gates_t = _norm_proj(x2d, norm_g[i, 0], sc1, sh1, w_main, tables, seq,
                                       tn=256, plain_tiles=(d // 256,), w_gates=w_gates)
            groups = NSA_KV_GROUPS
            kcvc = proj[:, d:d + 2 * groups * HEAD_DIM]
            ch = kcvc.reshape(batch, seq, 2, groups, HEAD_DIM).transpose(0, 2, 3, 1, 4)
            ch = ch.reshape(batch, 2, groups, seq // NSA_CMP_STRIDE, NSA_CMP_STRIDE * HEAD_DIM)
            cmp_kv = _nsa_compress(ch, nsa_cmp_pos[j], nsa_cmp_w1[j], nsa_cmp_b1[j], nsa_cmp_w2[j])
            o = _nsa_attention(proj, gates_t, cmp_kv, batch, seq, d)
            w_out = nsa_w_out[j]
        wr, br, n_exp = _router_weights(moe_w_group[i], moe_b_group[i], moe_w_expert[i], moe_b_expert[i])
        x2d, h2, comb, gid_t = _outproj_router(o, w_out.astype(BF16), x2d, g1, norm_g[i, 1], sc2, sh2,
                                               wr, br, seq, n_exp)
        wgu = jnp.concatenate([moe_w_gate[i], moe_w_up[i]], axis=-1).astype(BF16)
        x2d = _moe(h2, wgu, moe_w_down[i].astype(BF16), comb, gid_t, x2d, g2, final_g, seq,
                   final_norm=(i == depth - 1))
    return x2d.reshape(batch, seq, d)
```

```python
import functools
import math

import jax
import jax.numpy as jnp
from jax import lax
from jax.experimental import pallas as pl
from jax.experimental.pallas import tpu as pltpu

F32 = jnp.float32
BF16 = jnp.bfloat16
HIGHEST = lax.Precision.HIGHEST

ROPE_THETA = 500000.0
ROT_DIM = 16
NORM_EPS = 1e-6
NEG_INF = -1e30
HEAD_DIM = 64
DA_V_DIM = 2 * HEAD_DIM
NSA_KV_GROUPS = 2
NSA_CMP_BLOCK = 32
NSA_CMP_STRIDE = 16
NSA_SEL_BLOCK = 64
NSA_TOP_N = 16
NSA_WINDOW = 512
NSA_FORCE = 1e4
MOE_GROUPS = 4
MOE_EXPERTS_PER_GROUP = 8
MOE_TOP_K = 2
N_MIXERS = 2
LOG2E = 1.4426950408889634

LANES = 128
SUBLANES = 8
PROJ_ROW_CHUNK = 256
VMEM_LIMIT_BYTES = 56 * 1024 * 1024

NT_DIMS = (((1,), (1,)), ((), ()))
TN_DIMS = (((0,), (0,)), ((), ()))


def _cparams(*sem):
    return pltpu.CompilerParams(dimension_semantics=sem, vmem_limit_bytes=VMEM_LIMIT_BYTES)


def _mod_kernel(c_ref, w_ref, b_ref, o_ref):
    c = c_ref[...]
    ca = c * jax.nn.sigmoid(c)
    o_ref[0] = jnp.dot(ca, w_ref[0], precision=HIGHEST, preferred_element_type=F32) + b_ref[0]


def _adaln_mod(c, ada_w, ada_b):
    depth, d, n = ada_w.shape
    b = c.shape[0]
    rows = 8
    tn = 1536
    c_pad = jnp.zeros((rows, d), F32).at[:b].set(c)
    out = pl.pallas_call(
        _mod_kernel,
        grid=(depth, n // tn),
        in_specs=[
            pl.BlockSpec((rows, d), lambda i, j: (0, 0)),
            pl.BlockSpec((1, d, tn), lambda i, j: (i, 0, j)),
            pl.BlockSpec((1, 1, tn), lambda i, j: (i, 0, j)),
        ],
        out_specs=pl.BlockSpec((1, rows, tn), lambda i, j: (i, 0, j)),
        out_shape=jax.ShapeDtypeStruct((depth, rows, n), F32),
        compiler_params=_cparams("arbitrary", "arbitrary"),
        name="adaln_mod",
    )(c_pad, ada_w, ada_b.reshape(depth, 1, n))
    return out[:, :b, :]


def _rope_tables(positions):
    half = ROT_DIM // 2
    inv_freq = ROPE_THETA ** (-jnp.arange(0, ROT_DIM, 2, dtype=F32) / ROT_DIM)
    ang = positions.astype(F32).reshape(-1)[:, None] * inv_freq
    cos, sin = jnp.cos(ang), jnp.sin(ang)
    t = ang.shape[0]
    rest = HEAD_DIM - ROT_DIM
    z8 = jnp.zeros((t, half), F32)
    cos_h = jnp.concatenate([cos, cos, jnp.ones((t, rest), F32)], axis=-1)
    sina_h = jnp.concatenate([-sin, z8, jnp.zeros((t, rest), F32)], axis=-1)
    sinb_h = jnp.concatenate([z8, sin, jnp.zeros((t, rest), F32)], axis=-1)
    rep = LANES // HEAD_DIM
    return tuple(jnp.tile(a, (1, rep)) for a in (cos_h, sina_h, sinb_h))


def _rms_modulate(x, g, sc, sh):
    ms = jnp.mean(x * x, axis=-1, keepdims=True)
    y = x * lax.rsqrt(ms + NORM_EPS)
    return (y * g) * (1.0 + sc) + sh


def _proj_kernel(*refs, plain_tiles, with_gates):
    if with_gates:
        (x_ref, g_ref, sc_ref, sh_ref, w_ref, cos_ref, sa_ref, sb_ref, wg_ref,
         o_ref, gt_ref, h_scr) = refs
    else:
        x_ref, g_ref, sc_ref, sh_ref, w_ref, cos_ref, sa_ref, sb_ref, o_ref, h_scr = refs
    j = pl.program_id(1)
    tn = o_ref.shape[1]

    @pl.when(j == 0)
    def _():
        h = _rms_modulate(x_ref[...], g_ref[...], sc_ref[0], sh_ref[0])
        h_scr[...] = h.astype(BF16)
        if with_gates:
            logits = lax.dot_general(wg_ref[...], h_scr[...], NT_DIMS, preferred_element_type=F32)
            gt_ref[...] = jax.nn.sigmoid(logits)

    is_plain = functools.reduce(jnp.logical_or, [j == t for t in plain_tiles], j < 0)
    for r in range(h_scr.shape[0] // PROJ_ROW_CHUNK):
        rows = slice(r * PROJ_ROW_CHUNK, (r + 1) * PROJ_ROW_CHUNK)
        acc = jnp.dot(h_scr[rows, :], w_ref[...], preferred_element_type=F32)
        cos = jnp.where(is_plain, 1.0, cos_ref[rows, :])
        sa = jnp.where(is_plain, 0.0, sa_ref[rows, :])
        sb = jnp.where(is_plain, 0.0, sb_ref[rows, :])
        shift_up = pltpu.roll(acc, tn - ROT_DIM // 2, 1)
        shift_dn = pltpu.roll(acc, ROT_DIM // 2, 1)
        for c in range(tn // LANES):
            sl = slice(c * LANES, (c + 1) * LANES)
            y = acc[:, sl] * cos + shift_up[:, sl] * sa + shift_dn[:, sl] * sb
            o_ref[rows, sl] = y.astype(o_ref.dtype)


def _norm_proj(x2d, g, sc, sh, w, tables, seq, *, tn, plain_tiles, w_gates=None, tm=1024):
    t, d = x2d.shape
    n = w.shape[1]
    with_gates = w_gates is not None
    per_b = seq // tm
    in_specs = [
        pl.BlockSpec((tm, d), lambda i, j: (i, 0)),
        pl.BlockSpec((1, d), lambda i, j: (0, 0)),
        pl.BlockSpec((1, 1, d), lambda i, j: (i // per_b, 0, 0)),
        pl.BlockSpec((1, 1, d), lambda i, j: (i // per_b, 0, 0)),
        pl.BlockSpec((d, tn), lambda i, j: (0, j)),
        pl.BlockSpec((tm, LANES), lambda i, j: (i, 0)),
        pl.BlockSpec((tm, LANES), lambda i, j: (i, 0)),
        pl.BlockSpec((tm, LANES), lambda i, j: (i, 0)),
    ]
    args = [x2d, g.reshape(1, d), sc[:, None, :], sh[:, None, :], w, *tables]
    out_specs = [pl.BlockSpec((tm, tn), lambda i, j: (i, j))]
    out_shape = [jax.ShapeDtypeStruct((t, n), BF16)]
    if with_gates:
        ng = w_gates.shape[0]
        in_specs.append(pl.BlockSpec((ng, d), lambda i, j: (0, 0)))
        args.append(w_gates)
        out_specs.append(pl.BlockSpec((ng, tm), lambda i, j: (0, i)))
        out_shape.append(jax.ShapeDtypeStruct((ng, t), F32))
    res = pl.pallas_call(
        functools.partial(_proj_kernel, plain_tiles=tuple(plain_tiles), with_gates=with_gates),
        grid=(t // tm, n // tn),
        in_specs=in_specs,
        out_specs=out_specs,
        out_shape=out_shape,
        scratch_shapes=[pltpu.VMEM((tm, d), BF16)],
        compiler_params=_cparams("arbitrary", "arbitrary"),
        name="norm_proj",
    )(*args)
    return res if with_gates else res[0]


def _attn_scratch(tk, nq, dv):
    return [
        pltpu.VMEM((2, tk, nq), F32),
        pltpu.VMEM((2, 1, nq), F32),
        pltpu.VMEM((2, tk, nq), BF16),
        pltpu.VMEM((2, 1, nq), F32),
        pltpu.VMEM((1, nq), F32),
        pltpu.VMEM((1, nq), F32),
        pltpu.VMEM((dv, nq), F32),
    ]


def _pipelined_attention(n_full, n_diag, scores_full, scores_diag, v_tile, scratch, sum_in_v=False):
    s_scr, mx_scr, p_scr, al_scr, m_ref, l_ref, acc_ref = scratch

    def stage_scores(fn, tile, slot):
        s = fn(tile)
        s_scr[slot] = s
        mx_scr[slot] = jnp.max(s, axis=0, keepdims=True)

    def stage_softmax(slot):
        m_prev = m_ref[...]
        m_new = jnp.maximum(m_prev, mx_scr[slot])
        alpha = jnp.exp2(m_prev - m_new)
        p = jnp.exp2(s_scr[slot] - m_new)
        if not sum_in_v:
            l_ref[...] = alpha * l_ref[...] + jnp.sum(p, axis=0, keepdims=True)
        m_ref[...] = m_new
        al_scr[slot] = alpha
        p_scr[slot] = p.astype(BF16)

    def stage_pv(pos, slot):
        tile = jnp.where(pos < n_diag, n_full + jnp.maximum(pos, 0), pos - n_diag)
        pv = lax.dot_general(v_tile(tile), p_scr[slot], TN_DIMS, preferred_element_type=F32)
        acc_ref[...] = acc_ref[...] * al_scr[slot] + pv

    def step(fn, pos, slot):
        nxt = pos + 1
        stage_scores(fn, jnp.where(nxt < n_diag, n_full + nxt, nxt - n_diag), 1 - slot)
        stage_softmax(slot)
        stage_pv(pos - 1, 1 - slot)

    m_ref[...] = jnp.full(m_ref.shape, NEG_INF, F32)
    l_ref[...] = jnp.zeros(l_ref.shape, F32)
    acc_ref[...] = jnp.zeros(acc_ref.shape, F32)
    p_scr[1] = jnp.zeros(p_scr.shape[1:], BF16)
    al_scr[1] = jnp.ones(al_scr.shape[1:], F32)
    stage_scores(scores_diag, n_full, 0)
    for p in range(n_diag - 1):
        step(scores_diag, p, p % 2)

    first = n_diag - 1
    s0 = first % 2

    def pair(jj, carry):
        step(scores_full, first + 2 * jj, s0)
        step(scores_full, first + 2 * jj + 1, 1 - s0)
        return carry

    lax.fori_loop(0, n_full // 2, pair, 0)
    odd = n_full % 2 == 1
    last = first + n_full

    @pl.when(odd)
    def _():
        step(scores_full, last - 1, s0)
        stage_softmax(1 - s0)
        stage_pv(last - 1, s0)
        stage_pv(last, 1 - s0)

    @pl.when(jnp.logical_not(odd))
    def _():
        stage_softmax(s0)
        stage_pv(last - 1, 1 - s0)
        stage_pv(last, s0)


def _split_halves(q, rows):
    lane = lax.broadcasted_iota(jnp.int32, q.shape, 1)
    zero = jnp.zeros_like(q)
    return jnp.where(lane < HEAD_DIM, q, zero), jnp.where(lane >= HEAD_DIM, q, zero)


def _diff_attn_kernel(q_ref, k_ref, v_ref, lam_ref, g_ref, o_ref, qm, *scratch,
                      tq, tk, lambda_init):
    qi = pl.program_id(2)
    lo, hi = _split_halves(q_ref[...], tq)
    qm[0:tq, :] = lo
    qm[tq:2 * tq, :] = hi
    q0 = qi * tq
    acc_ref, l_ref = scratch[-1], scratch[-2]

    def scores_full(t):
        k = k_ref[pl.ds(pl.multiple_of(t * tk, tk), tk), :]
        return lax.dot_general(k, qm[...], NT_DIMS, preferred_element_type=F32)

    def scores_diag(t):
        s = scores_full(t)
        k_idx = t * tk + lax.broadcasted_iota(jnp.int32, s.shape, 0)
        q_idx = q0 + lax.broadcasted_iota(jnp.int32, s.shape, 1) % tq
        return jnp.where(k_idx <= q_idx, s, NEG_INF)

    def v_tile(t):
        return v_ref[pl.ds(pl.multiple_of(t * tk, tk), tk), :]

    _pipelined_attention(q0 // tk, tq // tk, scores_full, scores_diag, v_tile, scratch)

    lam = lam_ref[...]
    lam_full = (jnp.exp(jnp.sum(lam[0:1] * lam[1:2], axis=-1, keepdims=True))
                - jnp.exp(jnp.sum(lam[2:3] * lam[3:4], axis=-1, keepdims=True)) + lambda_init)
    acc = acc_ref[...]
    l = jnp.maximum(l_ref[...], 1e-30)
    ot = acc[:, 0:tq] / l[:, 0:tq] - lam_full * (acc[:, tq:2 * tq] / l[:, tq:2 * tq])
    ms = jnp.mean(ot * ot, axis=0, keepdims=True)
    ot = (ot * lax.rsqrt(ms + NORM_EPS)) * g_ref[...] * (1.0 - lambda_init)
    o_ref[...] = ot.T.astype(o_ref.dtype)


def _diff_attention(qkv, lam, subln_g, batch, seq, lambda_init, *, tq=1024, tk=1024):
    assert tq % tk == 0 and seq % tq == 0
    t, n3 = qkv.shape
    d = n3 // 3
    heads = d // DA_V_DIM
    nq = seq // tq
    cb = d // LANES
    g_b = jnp.broadcast_to(subln_g.astype(F32)[:, None], (DA_V_DIM, tq))
    return pl.pallas_call(
        functools.partial(_diff_attn_kernel, tq=tq, tk=tk, lambda_init=lambda_init),
        grid=(batch, heads, nq),
        in_specs=[
            pl.BlockSpec((tq, LANES), lambda b, h, i: (b * nq + i, h)),
            pl.BlockSpec((seq, LANES), lambda b, h, i: (b, cb + h)),
            pl.BlockSpec((seq, LANES), lambda b, h, i: (b, 2 * cb + h)),
            pl.BlockSpec(lam.shape, lambda b, h, i: (0, 0)),
            pl.BlockSpec((DA_V_DIM, tq), lambda b, h, i: (0, 0)),
        ],
        out_specs=pl.BlockSpec((tq, LANES), lambda b, h, i: (b * nq + i, h)),
        out_shape=jax.ShapeDtypeStruct((t, d), BF16),
        scratch_shapes=[pltpu.VMEM((2 * tq, LANES), BF16)] + _attn_scratch(tk, 2 * tq, DA_V_DIM),
        compiler_params=_cparams("arbitrary", "arbitrary", "arbitrary"),
        name="diff_attn",
    )(qkv, qkv, qkv, lam.astype(F32), g_b)


def _route(logits, n_exp):
    lane = lax.broadcasted_iota(jnp.int32, logits.shape, 1)
    lane_f = lane.astype(F32)
    big = float(LANES)
    is_grp = (lane >= n_exp) & (lane < n_exp + MOE_GROUPS)
    gl = jnp.where(is_grp, logits, -jnp.inf)
    ge = jnp.where(is_grp, jnp.exp(gl - jnp.max(gl, axis=-1, keepdims=True)), 0.0)
    pg = ge / jnp.sum(ge, axis=-1, keepdims=True)
    p_top = jnp.max(pg, axis=-1, keepdims=True)
    grp = jnp.min(jnp.where(is_grp & (pg == p_top), lane_f, big), axis=-1, keepdims=True) - n_exp
    in_grp = (lane < n_exp) & ((lane // MOE_EXPERTS_PER_GROUP).astype(F32) == grp)
    sel = jnp.where(in_grp, logits, -jnp.inf)
    v1 = jnp.max(sel, axis=-1, keepdims=True)
    i1 = jnp.min(jnp.where(sel == v1, lane_f, big), axis=-1, keepdims=True)
    sel2 = jnp.where(lane_f == i1, -jnp.inf, sel)
    v2 = jnp.max(sel2, axis=-1, keepdims=True)
    i2 = jnp.min(jnp.where(sel2 == v2, lane_f, big), axis=-1, keepdims=True)
    e2 = jnp.exp(v2 - v1)
    den = 1.0 + e2
    w1 = (1.0 / den) * p_top
    w2 = (e2 / den) * p_top
    return jnp.where(lane_f == i1, w1, 0.0) + jnp.where(lane_f == i2, w2, 0.0), grp


def _outproj_router_kernel(o_ref, w_ref, x_ref, g1_ref, ng_ref, sc_ref, sh_ref, wr_ref, br_ref,
                           xo_ref, h_ref, comb_ref, gid_ref, *, n_exp):
    mix = jnp.dot(o_ref[...], w_ref[...], preferred_element_type=F32)
    x = x_ref[...] + g1_ref[0] * mix
    xo_ref[...] = x
    h = _rms_modulate(x, ng_ref[...], sc_ref[0], sh_ref[0])
    h_hi = h.astype(BF16)
    h_ref[...] = h_hi
    h_lo = (h - h_hi.astype(F32)).astype(BF16)
    wr = wr_ref[...]
    hw = jnp.dot(h_hi, wr, preferred_element_type=F32)
    logits = (hw[:, 0:LANES] + hw[:, LANES:2 * LANES]
              + jnp.dot(h_lo, wr[:, 0:LANES], preferred_element_type=F32) + br_ref[...])
    comb, grp = _route(logits, n_exp)
    comb_ref[...] = comb[:, 0:n_exp]
    gid_ref[...] = jnp.broadcast_to(grp, logits.shape).T[0:gid_ref.shape[0], :]


def _outproj_router(o, w_out, x2d, g1, ng, sc, sh, wr, br, seq, n_exp, *, tm=1024):
    t, d = x2d.shape
    per_b = seq // tm
    bvec = lambda i: (i // per_b, 0, 0)
    return pl.pallas_call(
        functools.partial(_outproj_router_kernel, n_exp=n_exp),
        grid=(t // tm,),
        in_specs=[
            pl.BlockSpec((tm, d), lambda i: (i, 0)),
            pl.BlockSpec((d, d), lambda i: (0, 0)),
            pl.BlockSpec((tm, d), lambda i: (i, 0)),
            pl.BlockSpec((1, 1, d), bvec),
            pl.BlockSpec((1, d), lambda i: (0, 0)),
            pl.BlockSpec((1, 1, d), bvec),
            pl.BlockSpec((1, 1, d), bvec),
            pl.BlockSpec((d, 2 * LANES), lambda i: (0, 0)),
            pl.BlockSpec((1, LANES), lambda i: (0, 0)),
        ],
        out_specs=[
            pl.BlockSpec((tm, d), lambda i: (i, 0)),
            pl.BlockSpec((tm, d), lambda i: (i, 0)),
            pl.BlockSpec((tm, n_exp), lambda i: (i, 0)),
            pl.BlockSpec((8, tm), lambda i: (0, i)),
        ],
        out_shape=[
            jax.ShapeDtypeStruct((t, d), F32),
            jax.ShapeDtypeStruct((t, d), BF16),
            jax.ShapeDtypeStruct((t, n_exp), F32),
            jax.ShapeDtypeStruct((8, t), F32),
        ],
        compiler_params=_cparams("arbitrary"),
        name="outproj_router",
    )(o, w_out, x2d, g1[:, None, :], ng.reshape(1, d), sc[:, None, :], sh[:, None, :], wr, br)


MOE_CHUNK = 128
MOE_EXPERTS_PER_STEP = 4


def _moe_kernel(h_ref, wg_ref, wu_ref, wd_ref, comb_ref, gid_ref, tri_ref, x_ref, g2_ref, fg_ref,
                o_ref, perm_scr, hs_scr, cs_scr, ys_scr, off_smem, nch_smem, *, final_norm):
    step = pl.program_id(1)
    tm = h_ref.shape[0]
    rows = perm_scr.shape[0]
    n_exp = comb_ref.shape[1]
    steps_per_group = MOE_EXPERTS_PER_GROUP // MOE_EXPERTS_PER_STEP

    @pl.when(step == 0)
    def _():
        gid = gid_ref[0:1, :]
        grow = lax.broadcasted_iota(jnp.int32, (8, tm), 0).astype(F32)
        onehot = jnp.where(grow == gid, 1.0, 0.0)
        rank = jnp.dot(onehot.astype(BF16), tri_ref[...], preferred_element_type=F32)
        off = jnp.int32(0)
        off_col = jnp.zeros((8, 1), F32)
        grow_col = lax.broadcasted_iota(jnp.int32, (8, 1), 0)
        for g in range(MOE_GROUPS):
            cnt = jnp.sum(onehot[g:g + 1, :]).astype(jnp.int32)
            nch = (cnt + (MOE_CHUNK - 1)) // MOE_CHUNK
            off_smem[g] = off
            nch_smem[g] = nch
            off_col = jnp.where(grow_col == g, off.astype(F32), off_col)
            off = off + nch * MOE_CHUNK
        dest = jnp.sum(onehot * (rank + off_col), axis=0, keepdims=True)
        row_id = lax.broadcasted_iota(jnp.int32, (rows, tm), 0)
        perm = jnp.where(row_id == dest.astype(jnp.int32), 1.0, 0.0).astype(BF16)
        perm_scr[...] = perm
        hs_scr[...] = jnp.dot(perm, h_ref[...], preferred_element_type=F32).astype(BF16)
        comb = comb_ref[...]
        comb_hi = comb.astype(BF16)
        comb_lo = (comb - comb_hi.astype(F32)).astype(BF16)
        cs_scr[...] = (jnp.dot(perm, comb_hi, preferred_element_type=F32)
                       + jnp.dot(perm, comb_lo, preferred_element_type=F32))
        ys_scr[...] = jnp.zeros(ys_scr.shape, F32)

    g = step // steps_per_group
    off = off_smem[g]
    lane = lax.broadcasted_iota(jnp.int32, (MOE_CHUNK, n_exp), 1)

    def chunk(c, carry):
        r0 = pl.multiple_of(off + c * MOE_CHUNK, MOE_CHUNK)
        xs = hs_scr[pl.ds(r0, MOE_CHUNK), :]
        cs = cs_scr[pl.ds(r0, MOE_CHUNK), :]
        y = jnp.zeros((MOE_CHUNK, xs.shape[1]), F32)
        for k in range(MOE_EXPERTS_PER_STEP):
            gate = jnp.dot(xs, wg_ref[k], preferred_element_type=F32)
            up = jnp.dot(xs, wu_ref[k], preferred_element_type=F32)
            a = (gate * jax.nn.sigmoid(gate)) * up
            e = step * MOE_EXPERTS_PER_STEP + k
            ce = jnp.sum(jnp.where(lane == e, cs, 0.0), axis=-1, keepdims=True)
            y = y + jnp.dot((a * ce).astype(BF16), wd_ref[k], preferred_element_type=F32)
        ys_scr[pl.ds(r0, MOE_CHUNK), :] += y
        return carry

    lax.fori_loop(0, nch_smem[g], chunk, 0)

    @pl.when(step == pl.num_programs(1) - 1)
    def _():
        y = lax.dot_general(perm_scr[...], ys_scr[...].astype(BF16), TN_DIMS,
                            preferred_element_type=F32)
        x = x_ref[...] + g2_ref[0] * y
        if final_norm:
            ms = jnp.mean(x * x, axis=-1, keepdims=True)
            x = (x * lax.rsqrt(ms + NORM_EPS)) * fg_ref[...]
        o_ref[...] = x


def _moe(h, w_gate, w_up, w_down, layer, comb, gid_t, x2d, g2, final_g, seq, *, final_norm, tm=1024):
    t, d = x2d.shape
    _, n_exp, _, hidden = w_gate.shape
    per_b = seq // tm
    eps = MOE_EXPERTS_PER_STEP
    rows = tm + MOE_GROUPS * MOE_CHUNK
    idx = jnp.arange(tm)
    tri = (idx[:, None] < idx[None, :]).astype(BF16)
    return pl.pallas_call(
        functools.partial(_moe_kernel, final_norm=final_norm),
        grid=(t // tm, n_exp // eps),
        in_specs=[
            pl.BlockSpec((tm, d), lambda i, s: (i, 0)),
            pl.BlockSpec((None, eps, d, hidden), lambda i, s: (layer, s, 0, 0)),
            pl.BlockSpec((None, eps, d, hidden), lambda i, s: (layer, s, 0, 0)),
            pl.BlockSpec((None, eps, hidden, d), lambda i, s: (layer, s, 0, 0)),
            pl.BlockSpec((tm, n_exp), lambda i, s: (i, 0)),
            pl.BlockSpec((8, tm), lambda i, s: (0, i)),
            pl.BlockSpec((tm, tm), lambda i, s: (0, 0)),
            pl.BlockSpec((tm, d), lambda i, s: (i, 0)),
            pl.BlockSpec((1, 1, d), lambda i, s: (i // per_b, 0, 0)),
            pl.BlockSpec((1, d), lambda i, s: (0, 0)),
        ],
        out_specs=pl.BlockSpec((tm, d), lambda i, s: (i, 0)),
        out_shape=jax.ShapeDtypeStruct((t, d), F32),
        scratch_shapes=[
            pltpu.VMEM((rows, tm), BF16),
            pltpu.VMEM((rows, d), BF16),
            pltpu.VMEM((rows, n_exp), F32),
            pltpu.VMEM((rows, d), F32),
            pltpu.SMEM((MOE_GROUPS,), jnp.int32),
            pltpu.SMEM((MOE_GROUPS,), jnp.int32),
        ],
        compiler_params=_cparams("arbitrary", "arbitrary"),
        name="moe",
    )(h, w_gate, w_up, w_down, comb, gid_t, tri, x2d, g2[:, None, :], final_g.reshape(1, d))


def _compress_kernel(ch_ref, w1a_ref, w1b_ref, pos_ref, w1f_ref, b1_ref, w2_ref, o_ref):
    ch = ch_ref[0, 0, 0]
    n = ch.shape[0]
    a = jnp.dot(ch, w1a_ref[0], preferred_element_type=F32)
    b = jnp.dot(ch, w1b_ref[0], preferred_element_type=F32)
    const = jnp.dot(pos_ref[0], w1f_ref[0], precision=HIGHEST, preferred_element_type=F32)[0:1]
    hid = a + pltpu.roll(b, n - 1, 0) + (const + b1_ref[0])
    act = jax.nn.gelu(hid, approximate=True)
    o_ref[0, 0, 0] = jnp.dot(act.astype(BF16), w2_ref[0], preferred_element_type=F32).astype(o_ref.dtype)


def _nsa_compress(ch, cmp_pos, cmp_w1, cmp_b1, cmp_w2):
    b, two, g, nc, f = ch.shape
    hid = cmp_w1.shape[-1]
    w1a = cmp_w1[:, :f, :].astype(BF16)
    w1b = cmp_w1[:, f:, :].astype(BF16)
    pos = jnp.zeros((two, 8, 2 * f), F32).at[:, 0, :].set(cmp_pos.reshape(two, 2 * f))
    w2 = jnp.concatenate([cmp_w2, cmp_w2], axis=-1).astype(BF16)
    return pl.pallas_call(
        _compress_kernel,
        grid=(b, two, g),
        in_specs=[
            pl.BlockSpec((1, 1, 1, nc, f), lambda i, j, k: (i, j, k, 0, 0)),
            pl.BlockSpec((1, f, hid), lambda i, j, k: (j, 0, 0)),
            pl.BlockSpec((1, f, hid), lambda i, j, k: (j, 0, 0)),
            pl.BlockSpec((1, 8, 2 * f), lambda i, j, k: (j, 0, 0)),
            pl.BlockSpec((1, 2 * f, hid), lambda i, j, k: (j, 0, 0)),
            pl.BlockSpec((1, 1, hid), lambda i, j, k: (j, 0, 0)),
            pl.BlockSpec((1, hid, LANES), lambda i, j, k: (j, 0, 0)),
        ],
        out_specs=pl.BlockSpec((1, 1, 1, nc, LANES), lambda i, j, k: (i, j, k, 0, 0)),
        out_shape=jax.ShapeDtypeStruct((b, two, g, nc, LANES), BF16),
        compiler_params=_cparams("arbitrary", "arbitrary", "arbitrary"),
        name="nsa_compress",
    )(ch, w1a, w1b, pos, cmp_w1.astype(F32), cmp_b1[:, None, :].astype(F32), w2)


def _stack_group_queries(q_ref, qm, tq):
    for j in range(q_ref.shape[1] // LANES):
        lo, hi = _split_halves(q_ref[:, j * LANES:(j + 1) * LANES], tq)
        qm[(2 * j) * tq:(2 * j + 1) * tq, :] = lo
        qm[(2 * j + 1) * tq:(2 * j + 2) * tq, :] = hi


def _own_group_values_with_ones(v, g):
    lane = lax.broadcasted_iota(jnp.int32, v.shape, 1)
    own_half = (lane >= HEAD_DIM) == (g == 1)
    return jnp.where(own_half, v, jnp.ones_like(v))


def _gated_pairs_to_token_major(ot, gt_ref, branch, tq, hpg):
    g = pl.program_id(1)
    tiles = []
    for j in range(hpg // 2):
        parts = []
        for h in (2 * j, 2 * j + 1):
            row = gt_ref[pl.ds((g * hpg + h) * 3 + branch, 1), :]
            parts.append(ot[:, h * tq:(h + 1) * tq] * row)
        tiles.append(jnp.concatenate(parts, axis=0).T)
    return tiles


NSA_CMP_CLASSES = 4


def _nsa_window_tiles(q0, qm, kw_ref, vw_ref, gt_ref, acc_ref, *, tq, hpg):
    g = pl.program_id(1)
    span = NSA_WINDOW + tq
    start = pl.multiple_of(jnp.maximum(q0 - NSA_WINDOW, 0), tq)
    k = kw_ref[pl.ds(start, span), :]
    v = vw_ref[pl.ds(start, span), :]
    s = lax.dot_general(k, qm[...], NT_DIMS, preferred_element_type=F32)
    k_idx = start + lax.broadcasted_iota(jnp.int32, (span, tq), 0)
    q_idx = q0 + lax.broadcasted_iota(jnp.int32, (span, tq), 1)
    dist = q_idx - k_idx
    bias = jnp.where(dist >= 0, jnp.where(dist < NSA_WINDOW, 0.0, NEG_INF), NEG_INF)
    s = s + jnp.concatenate([bias] * hpg, axis=1)
    m = jnp.max(s, axis=0, keepdims=True)
    e = jnp.exp2(s - m)
    r = 1.0 / jnp.maximum(jnp.sum(e, axis=0, keepdims=True), 1e-30)
    acc_ref[...] = lax.dot_general(v, e.astype(BF16), TN_DIMS, preferred_element_type=F32) * r
    ot = acc_ref[pl.ds(pl.multiple_of(g * HEAD_DIM, HEAD_DIM), HEAD_DIM), :]
    return _gated_pairs_to_token_major(ot, gt_ref, 2, tq, hpg)


def _nsa_cmp_body(q0, qm, kc_ref, vc_ref, mt_ref, gt_ref, kw_ref, vw_ref, o_ref, mask_ref, acc_ref,
                  *, tq, hpg, n_top, n_cmp, n_blk):
    win_tiles = _nsa_window_tiles(q0, qm, kw_ref, vw_ref, gt_ref, acc_ref, tq=tq, hpg=hpg)
    kc = kc_ref[0, 0, 0, 0:n_cmp, :]
    s = lax.dot_general(kc, qm[...], NT_DIMS, preferred_element_type=F32)
    n_idx = lax.broadcasted_iota(jnp.int32, (n_cmp, tq), 0)
    q_idx = q0 + lax.broadcasted_iota(jnp.int32, (n_cmp, tq), 1)
    valid = n_idx * NSA_CMP_STRIDE + (NSA_CMP_BLOCK - 1) <= q_idx
    bias = jnp.where(valid, 0.0, NEG_INF)
    s = s + jnp.concatenate([bias] * hpg, axis=1)
    m = jnp.maximum(jnp.max(s, axis=0, keepdims=True), 0.1 * NEG_INF)
    e = jnp.exp2(s - m)
    r = 1.0 / jnp.maximum(jnp.sum(e, axis=0, keepdims=True), 1e-30)
    p = e * r
    ot = lax.dot_general(vc_ref[0, 0, 0, 0:n_cmp, :], p.astype(BF16), TN_DIMS,
                         preferred_element_type=F32)
    tiles = _gated_pairs_to_token_major(ot[0:HEAD_DIM], gt_ref, 0, tq, hpg)
    for j, (tile, win) in enumerate(zip(tiles, win_tiles)):
        o_ref[:, j * LANES:(j + 1) * LANES] = (tile + win).astype(o_ref.dtype)

    psum = p[:, 0:tq]
    for h in range(1, hpg):
        psum = psum + p[:, h * tq:(h + 1) * tq]
    p_hi = psum.astype(BF16)
    p_lo = (psum - p_hi.astype(F32)).astype(BF16)
    mt = mt_ref[0:n_blk, 0:n_cmp]
    imp = (jnp.dot(mt, p_hi, preferred_element_type=F32)
           + jnp.dot(mt, p_lo, preferred_element_type=F32))
    j_idx = lax.broadcasted_iota(jnp.int32, imp.shape, 0)
    j_f = j_idx.astype(F32)
    q_blk = (q0 + lax.broadcasted_iota(jnp.int32, imp.shape, 1)) // NSA_SEL_BLOCK
    forced = (j_idx == 0) | (j_idx == q_blk) | (j_idx == q_blk - 1)
    val = jnp.where(forced, -jnp.inf, jnp.where(j_idx <= q_blk, imp, -NSA_FORCE))
    sel_bias = jnp.where(forced, 0.0, NEG_INF)
    for _ in range(max(n_top - 3, 0)):
        mx = jnp.max(val, axis=0, keepdims=True)
        first = jnp.min(jnp.where(val == mx, j_f, float(n_blk)), axis=0, keepdims=True)
        hit = j_f == first
        sel_bias = jnp.where(hit, 0.0, sel_bias)
        val = jnp.where(hit, -jnp.inf, val)
    n_sel = mask_ref.shape[2]
    mask_ref[0, 0, 0:n_blk, :] = sel_bias
    if n_blk < n_sel:
        mask_ref[0, 0, n_blk:n_sel, :] = jnp.full((n_sel - n_blk, tq), NEG_INF, F32)


def _nsa_cmp_win_kernel(q_ref, kc_ref, vc_ref, mt_ref, gt_ref, kw_ref, vw_ref, o_ref, mask_ref,
                        qm, acc_ref, *, tq, hpg, n_top):
    qi = pl.program_id(2)
    q0 = qi * tq
    _stack_group_queries(q_ref, qm, tq)
    nc = kc_ref.shape[3]
    n_sel = mask_ref.shape[2]
    seq = n_sel * NSA_SEL_BLOCK
    cls = (q0 + tq - 1) // (seq // NSA_CMP_CLASSES)
    for c in range(NSA_CMP_CLASSES):
        @pl.when(cls == c)
        def _(c=c):
            _nsa_cmp_body(q0, qm, kc_ref, vc_ref, mt_ref, gt_ref, kw_ref, vw_ref, o_ref, mask_ref,
                          acc_ref, tq=tq, hpg=hpg,
                          n_top=min(n_top, (c + 1) * n_sel // NSA_CMP_CLASSES),
                          n_cmp=(c + 1) * nc // NSA_CMP_CLASSES,
                          n_blk=(c + 1) * n_sel // NSA_CMP_CLASSES)


def _nsa_sel_kernel(q_ref, k_ref, v_ref, mask_ref, gt_ref, ocw_ref, o_ref,
                    qm, *scratch, tq, tk, hpg):
    g = pl.program_id(1)
    qi = pl.program_id(2)
    q0 = qi * tq
    _stack_group_queries(q_ref, qm, tq)
    acc_ref, l_ref = scratch[-1], scratch[-2]
    blocks = tk // NSA_SEL_BLOCK

    def scores_full(t):
        k = k_ref[pl.ds(pl.multiple_of(t * tk, tk), tk), :]
        s = lax.dot_general(k, qm[...], NT_DIMS, preferred_element_type=F32)
        if blocks >= SUBLANES:
            rows = mask_ref[0, 0, pl.ds(pl.multiple_of(t * blocks, blocks), blocks), :]
        else:
            per = SUBLANES // blocks
            grp = mask_ref[0, 0, pl.ds(pl.multiple_of((t // per) * SUBLANES, SUBLANES), SUBLANES), :]
            rows = grp[0:blocks]
            for i in range(1, per):
                rows = jnp.where(t % per == i, grp[i * blocks:(i + 1) * blocks], rows)
        exp_rows = [jnp.broadcast_to(rows[r:r + 1, :], (NSA_SEL_BLOCK, tq)) for r in range(blocks)]
        bias = jnp.concatenate(exp_rows, axis=0)
        return s + jnp.concatenate([bias] * hpg, axis=1)

    def scores_diag(t):
        s = scores_full(t)
        k_idx = t * tk + lax.broadcasted_iota(jnp.int32, s.shape, 0)
        q_idx = q0 + lax.broadcasted_iota(jnp.int32, s.shape, 1) % tq
        return jnp.where(k_idx <= q_idx, s, NEG_INF)

    def v_tile(t):
        return _own_group_values_with_ones(v_ref[pl.ds(pl.multiple_of(t * tk, tk), tk), :], g)

    _pipelined_attention(q0 // tk, tq // tk, scores_full, scores_diag, v_tile, scratch,
                         sum_in_v=True)

    own = pl.multiple_of(g * HEAD_DIM, HEAD_DIM)
    other = pl.multiple_of((1 - g) * HEAD_DIM, HEAD_DIM)
    inv_l = 1.0 / jnp.maximum(acc_ref[pl.ds(other, 1), :], 1e-30)
    ot = acc_ref[pl.ds(own, HEAD_DIM), :] * inv_l
    tiles = _gated_pairs_to_token_major(ot, gt_ref, 1, tq, hpg)
    for j, tile in enumerate(tiles):
        sl = slice(j * LANES, (j + 1) * LANES)
        total = tile + ocw_ref[:, sl].astype(F32)
        o_ref[:, sl] = total.astype(o_ref.dtype)


def _nsa_attention(proj, gates_t, cmp_kv, batch, seq, d_model, *, tq=128, tq_sel=512, tk=512):
    assert tq_sel % tk == 0 and seq % tq_sel == 0
    t = proj.shape[0]
    groups = NSA_KV_GROUPS
    hpg = d_model // HEAD_DIM // groups
    gw = hpg * HEAD_DIM
    nq = seq // tq
    nq_sel = seq // tq_sel
    nc = cmp_kv.shape[3]
    n_sel = seq // NSA_SEL_BLOCK
    n_top = min(NSA_TOP_N, n_sel)
    ng = gates_t.shape[0]
    qcb = d_model // LANES

    ci = jnp.arange(nc)[None, :] * NSA_CMP_STRIDE
    sj = jnp.arange(n_sel)[:, None] * NSA_SEL_BLOCK
    mt = ((ci < sj + NSA_SEL_BLOCK) & (ci + NSA_CMP_BLOCK > sj)
          & (jnp.arange(nc)[None, :] < nc - 1)).astype(BF16)

    q_spec = pl.BlockSpec((tq, gw), lambda b, g, i: (b * nq + i, g))
    gt_spec = pl.BlockSpec((ng, tq), lambda b, g, i: (0, b * nq + i))
    o_spec = pl.BlockSpec((tq, gw), lambda b, g, i: (b * nq + i, g))
    o_shape = jax.ShapeDtypeStruct((t, d_model), BF16)
    sem = ("arbitrary", "arbitrary", "arbitrary")
    qm_scr = pltpu.VMEM((hpg * tq, LANES), BF16)

    o_cmp_win, sel_mask = pl.pallas_call(
        functools.partial(_nsa_cmp_win_kernel, tq=tq, hpg=hpg, n_top=n_top),
        grid=(batch, groups, nq),
        in_specs=[
            q_spec,
            pl.BlockSpec((1, 1, 1, nc, LANES), lambda b, g, i: (b, 0, g, 0, 0)),
            pl.BlockSpec((1, 1, 1, nc, LANES), lambda b, g, i: (b, 1, g, 0, 0)),
            pl.BlockSpec((n_sel, nc), lambda b, g, i: (0, 0)),
            gt_spec,
            pl.BlockSpec((seq, LANES), lambda b, g, i: (b, qcb + 5 + g)),
            pl.BlockSpec((seq, LANES), lambda b, g, i: (b, qcb + 7)),
        ],
        out_specs=[o_spec, pl.BlockSpec((1, 1, n_sel, tq), lambda b, g, i: (b, g, 0, i))],
        out_shape=[o_shape, jax.ShapeDtypeStruct((batch, groups, n_sel, seq), F32)],
        scratch_shapes=[qm_scr, pltpu.VMEM((LANES, hpg * tq), F32)],
        compiler_params=_cparams(*sem),
        name="nsa_cmp_win",
    )(proj, cmp_kv, cmp_kv, mt, gates_t, proj, proj)

    rows_sel = pl.BlockSpec((tq_sel, gw), lambda b, g, i: (b * nq_sel + i, g))
    return pl.pallas_call(
        functools.partial(_nsa_sel_kernel, tq=tq_sel, tk=tk, hpg=hpg),
        grid=(batch, groups, nq_sel),
        in_specs=[
            rows_sel,
            pl.BlockSpec((seq, LANES), lambda b, g, i: (b, qcb + 2 + g)),
            pl.BlockSpec((seq, LANES), lambda b, g, i: (b, qcb + 4)),
            pl.BlockSpec((1, 1, n_sel, tq_sel), lambda b, g, i: (b, g, 0, i)),
            pl.BlockSpec((ng, tq_sel), lambda b, g, i: (0, b * nq_sel + i)),
            rows_sel,
        ],
        out_specs=rows_sel,
        out_shape=o_shape,
        scratch_shapes=[pltpu.VMEM((hpg * tq_sel, LANES), BF16)] + _attn_scratch(tk, hpg * tq_sel, LANES),
        compiler_params=_cparams(*sem),
        name="nsa_sel",
    )(proj, proj, proj, sel_mask, gates_t, o_cmp_win)


def _diff_in_weights(w_in, d_model):
    scale = HEAD_DIM ** -0.5 * LOG2E
    wq = w_in[:, :d_model] * scale
    return jnp.concatenate([wq, w_in[:, d_model:]], axis=1).astype(BF16)


def _nsa_in_weights(w_in, d_model):
    g, hd = NSA_KV_GROUPS, HEAD_DIM
    kvw = g * hd
    scale = hd ** -0.5 * LOG2E
    q = w_in[:, :d_model] * scale
    off = d_model
    sec = [w_in[:, off + i * kvw: off + (i + 1) * kvw] for i in range(6)]
    dup = lambda w: jnp.concatenate([w[:, i * hd:(i + 1) * hd] for i in range(g) for _ in range(2)], axis=1)
    w_main = jnp.concatenate([q, sec[0], sec[1], dup(sec[2]), sec[3], dup(sec[4]), sec[5]], axis=1)
    w_gates = w_in[:, off + 6 * kvw:].T
    return w_main.astype(BF16), w_gates.astype(BF16)


def _router_weights(w_grp, b_grp, w_exp, b_exp):
    d = w_grp.shape[0]
    n_exp = w_exp.shape[0] * w_exp.shape[2]
    we = jnp.transpose(w_exp, (1, 0, 2)).reshape(d, n_exp)
    pad = LANES - n_exp - w_grp.shape[1]
    wr = jnp.concatenate([we, w_grp, jnp.zeros((d, pad), F32)], axis=1).astype(F32)
    wr_hi = wr.astype(BF16)
    wr_lo = (wr - wr_hi.astype(F32)).astype(BF16)
    br = jnp.concatenate([b_exp.reshape(-1), b_grp, jnp.zeros((pad,), F32)])[None, :].astype(F32)
    return jnp.concatenate([wr_hi, wr_lo], axis=1), br, n_exp


def kernel(x, c, positions, ada_w, ada_b, norm_g, final_g, diff_w_in, diff_w_out, diff_lambda,
           diff_subln_g, nsa_w_in, nsa_w_out, nsa_cmp_pos, nsa_cmp_w1, nsa_cmp_b1, nsa_cmp_w2,
           moe_w_group, moe_b_group, moe_w_expert, moe_b_expert, moe_w_gate, moe_w_up, moe_w_down):
    batch, seq, d = x.shape
    depth = ada_w.shape[0]
    x2d = x.reshape(batch * seq, d)
    tables = _rope_tables(positions)
    mod = _adaln_mod(c, ada_w, ada_b)
    w_gate, w_up, w_down = (w.astype(BF16) for w in (moe_w_gate, moe_w_up, moe_w_down))

    for i in range(depth):
        sh1, sc1, g1, sh2, sc2, g2 = jnp.split(mod[i], 6, axis=-1)
        j = i // N_MIXERS
        if i % N_MIXERS == 0:
            lambda_init = 0.8 - 0.6 * math.exp(-0.3 * i)
            qkv = _norm_proj(x2d, norm_g[i, 0], sc1, sh1, _diff_in_weights(diff_w_in[j], d), tables,
                             seq, tn=512, plain_tiles=range(2 * d // 512, 3 * d // 512))
            o = _diff_attention(qkv, diff_lambda[j], diff_subln_g[j], batch, seq, lambda_init)
            w_out = diff_w_out[j]
        else:
            w_main, w_gates = _nsa_in_weights(nsa_w_in[j], d)
            proj, gates_t = _norm_proj(x2d, norm_g[i, 0], sc1, sh1, w_main, tables, seq,
                                       tn=256, plain_tiles=(d // 256,), w_gates=w_gates)
            groups = NSA_KV_GROUPS
            kcvc = proj[:, d:d + 2 * groups * HEAD_DIM]
            ch = kcvc.reshape(batch, seq, 2, groups, HEAD_DIM).transpose(0, 2, 3, 1, 4)
            ch = ch.reshape(batch, 2, groups, seq // NSA_CMP_STRIDE, NSA_CMP_STRIDE * HEAD_DIM)
            cmp_kv = _nsa_compress(ch, nsa_cmp_pos[j], nsa_cmp_w1[j], nsa_cmp_b1[j], nsa_cmp_w2[j])
            o = _nsa_attention(proj, gates_t, cmp_kv, batch, seq, d)
            w_out = nsa_w_out[j]
        wr, br, n_exp = _router_weights(moe_w_group[i], moe_b_group[i], moe_w_expert[i], moe_b_expert[i])
        x2d, h2, comb, gid_t = _outproj_router(o, w_out.astype(BF16), x2d, g1, norm_g[i, 1], sc2, sh2,
                                               wr, br, seq, n_exp)
        x2d = _moe(h2, w_gate, w_up, w_down, i, comb, gid_t, x2d, g2, final_g, seq,
                   final_norm=(i == depth - 1))
    return x2d.reshape(batch, seq, d)
```

```python
import functools
import math

import jax
import jax.numpy as jnp
from jax import lax
from jax.experimental import pallas as pl
from jax.experimental.pallas import tpu as pltpu

F32 = jnp.float32
BF16 = jnp.bfloat16
HIGHEST = lax.Precision.HIGHEST

ROPE_THETA = 500000.0
ROT_DIM = 16
NORM_EPS = 1e-6
NEG_INF = -1e30
HEAD_DIM = 64
DA_V_DIM = 2 * HEAD_DIM
NSA_KV_GROUPS = 2
NSA_CMP_BLOCK = 32
NSA_CMP_STRIDE = 16
NSA_SEL_BLOCK = 64
NSA_TOP_N = 16
NSA_WINDOW = 512
NSA_FORCE = 1e4
MOE_GROUPS = 4
MOE_EXPERTS_PER_GROUP = 8
MOE_TOP_K = 2
N_MIXERS = 2
LOG2E = 1.4426950408889634

LANES = 128
PROJ_ROW_CHUNK = 256
VMEM_LIMIT_BYTES = 56 * 1024 * 1024

NT_DIMS = (((1,), (1,)), ((), ()))
TN_DIMS = (((0,), (0,)), ((), ()))


def _cparams(*sem):
    return pltpu.CompilerParams(dimension_semantics=sem, vmem_limit_bytes=VMEM_LIMIT_BYTES)


def _mod_kernel(c_ref, w_ref, b_ref, o_ref):
    c = c_ref[...]
    ca = c * jax.nn.sigmoid(c)
    o_ref[0] = jnp.dot(ca, w_ref[0], precision=HIGHEST, preferred_element_type=F32) + b_ref[0]


def _adaln_mod(c, ada_w, ada_b):
    depth, d, n = ada_w.shape
    b = c.shape[0]
    rows = 8
    tn = 1536
    c_pad = jnp.zeros((rows, d), F32).at[:b].set(c)
    out = pl.pallas_call(
        _mod_kernel,
        grid=(depth, n // tn),
        in_specs=[
            pl.BlockSpec((rows, d), lambda i, j: (0, 0)),
            pl.BlockSpec((1, d, tn), lambda i, j: (i, 0, j)),
            pl.BlockSpec((1, 1, tn), lambda i, j: (i, 0, j)),
        ],
        out_specs=pl.BlockSpec((1, rows, tn), lambda i, j: (i, 0, j)),
        out_shape=jax.ShapeDtypeStruct((depth, rows, n), F32),
        compiler_params=_cparams("arbitrary", "arbitrary"),
        name="adaln_mod",
    )(c_pad, ada_w, ada_b.reshape(depth, 1, n))
    return out[:, :b, :]


def _rope_tables(positions):
    half = ROT_DIM // 2
    inv_freq = ROPE_THETA ** (-jnp.arange(0, ROT_DIM, 2, dtype=F32) / ROT_DIM)
    ang = positions.astype(F32).reshape(-1)[:, None] * inv_freq
    cos, sin = jnp.cos(ang), jnp.sin(ang)
    t = ang.shape[0]
    rest = HEAD_DIM - ROT_DIM
    z8 = jnp.zeros((t, half), F32)
    cos_h = jnp.concatenate([cos, cos, jnp.ones((t, rest), F32)], axis=-1)
    sina_h = jnp.concatenate([-sin, z8, jnp.zeros((t, rest), F32)], axis=-1)
    sinb_h = jnp.concatenate([z8, sin, jnp.zeros((t, rest), F32)], axis=-1)
    rep = LANES // HEAD_DIM
    return tuple(jnp.tile(a, (1, rep)) for a in (cos_h, sina_h, sinb_h))


def _rms_modulate(x, g, sc, sh):
    ms = jnp.mean(x * x, axis=-1, keepdims=True)
    y = x * lax.rsqrt(ms + NORM_EPS)
    return (y * g) * (1.0 + sc) + sh


def _proj_kernel(*refs, plain_tiles, with_gates):
    if with_gates:
        (x_ref, g_ref, sc_ref, sh_ref, w_ref, cos_ref, sa_ref, sb_ref, wg_ref,
         o_ref, gt_ref, h_scr) = refs
    else:
        x_ref, g_ref, sc_ref, sh_ref, w_ref, cos_ref, sa_ref, sb_ref, o_ref, h_scr = refs
    j = pl.program_id(1)
    tn = o_ref.shape[1]

    @pl.when(j == 0)
    def _():
        h = _rms_modulate(x_ref[...], g_ref[...], sc_ref[0], sh_ref[0])
        h_scr[...] = h.astype(BF16)
        if with_gates:
            logits = lax.dot_general(wg_ref[...], h_scr[...], NT_DIMS, preferred_element_type=F32)
            gt_ref[...] = jax.nn.sigmoid(logits)

    is_plain = functools.reduce(jnp.logical_or, [j == t for t in plain_tiles], j < 0)
    for r in range(h_scr.shape[0] // PROJ_ROW_CHUNK):
        rows = slice(r * PROJ_ROW_CHUNK, (r + 1) * PROJ_ROW_CHUNK)
        acc = jnp.dot(h_scr[rows, :], w_ref[...], preferred_element_type=F32)
        cos = jnp.where(is_plain, 1.0, cos_ref[rows, :])
        sa = jnp.where(is_plain, 0.0, sa_ref[rows, :])
        sb = jnp.where(is_plain, 0.0, sb_ref[rows, :])
        shift_up = pltpu.roll(acc, tn - ROT_DIM // 2, 1)
        shift_dn = pltpu.roll(acc, ROT_DIM // 2, 1)
        for c in range(tn // LANES):
            sl = slice(c * LANES, (c + 1) * LANES)
            y = acc[:, sl] * cos + shift_up[:, sl] * sa + shift_dn[:, sl] * sb
            o_ref[rows, sl] = y.astype(o_ref.dtype)


def _norm_proj(x2d, g, sc, sh, w, tables, seq, *, tn, plain_tiles, w_gates=None, tm=1024):
    t, d = x2d.shape
    n = w.shape[1]
    with_gates = w_gates is not None
    per_b = seq // tm
    in_specs = [
        pl.BlockSpec((tm, d), lambda i, j: (i, 0)),
        pl.BlockSpec((1, d), lambda i, j: (0, 0)),
        pl.BlockSpec((1, 1, d), lambda i, j: (i // per_b, 0, 0)),
        pl.BlockSpec((1, 1, d), lambda i, j: (i // per_b, 0, 0)),
        pl.BlockSpec((d, tn), lambda i, j: (0, j)),
        pl.BlockSpec((tm, LANES), lambda i, j: (i, 0)),
        pl.BlockSpec((tm, LANES), lambda i, j: (i, 0)),
        pl.BlockSpec((tm, LANES), lambda i, j: (i, 0)),
    ]
    args = [x2d, g.reshape(1, d), sc[:, None, :], sh[:, None, :], w, *tables]
    out_specs = [pl.BlockSpec((tm, tn), lambda i, j: (i, j))]
    out_shape = [jax.ShapeDtypeStruct((t, n), BF16)]
    if with_gates:
        ng = w_gates.shape[0]
        in_specs.append(pl.BlockSpec((ng, d), lambda i, j: (0, 0)))
        args.append(w_gates)
        out_specs.append(pl.BlockSpec((ng, tm), lambda i, j: (0, i)))
        out_shape.append(jax.ShapeDtypeStruct((ng, t), F32))
    res = pl.pallas_call(
        functools.partial(_proj_kernel, plain_tiles=tuple(plain_tiles), with_gates=with_gates),
        grid=(t // tm, n // tn),
        in_specs=in_specs,
        out_specs=out_specs,
        out_shape=out_shape,
        scratch_shapes=[pltpu.VMEM((tm, d), BF16)],
        compiler_params=_cparams("arbitrary", "arbitrary"),
        name="norm_proj",
    )(*args)
    return res if with_gates else res[0]


def _attn_scratch(tk, nq, dv):
    return [
        pltpu.VMEM((2, tk, nq), F32),
        pltpu.VMEM((2, 1, nq), F32),
        pltpu.VMEM((2, tk, nq), BF16),
        pltpu.VMEM((2, 1, nq), F32),
        pltpu.VMEM((1, nq), F32),
        pltpu.VMEM((1, nq), F32),
        pltpu.VMEM((dv, nq), F32),
    ]


def _pipelined_attention(n_full, n_diag, scores_full, scores_diag, v_tile, scratch, sum_in_v=False):
    s_scr, mx_scr, p_scr, al_scr, m_ref, l_ref, acc_ref = scratch

    def stage_scores(fn, tile, slot):
        s = fn(tile)
        s_scr[slot] = s
        mx_scr[slot] = jnp.max(s, axis=0, keepdims=True)

    def stage_softmax(slot):
        m_prev = m_ref[...]
        m_new = jnp.maximum(m_prev, mx_scr[slot])
        alpha = jnp.exp2(m_prev - m_new)
        p = jnp.exp2(s_scr[slot] - m_new)
        if not sum_in_v:
            l_ref[...] = alpha * l_ref[...] + jnp.sum(p, axis=0, keepdims=True)
        m_ref[...] = m_new
        al_scr[slot] = alpha
        p_scr[slot] = p.astype(BF16)

    def stage_pv(pos, slot):
        tile = jnp.where(pos < n_diag, n_full + jnp.maximum(pos, 0), pos - n_diag)
        pv = lax.dot_general(v_tile(tile), p_scr[slot], TN_DIMS, preferred_element_type=F32)
        acc_ref[...] = acc_ref[...] * al_scr[slot] + pv

    def step(fn, pos, slot):
        nxt = pos + 1
        stage_scores(fn, jnp.where(nxt < n_diag, n_full + nxt, nxt - n_diag), 1 - slot)
        stage_softmax(slot)
        stage_pv(pos - 1, 1 - slot)

    m_ref[...] = jnp.full(m_ref.shape, NEG_INF, F32)
    l_ref[...] = jnp.zeros(l_ref.shape, F32)
    acc_ref[...] = jnp.zeros(acc_ref.shape, F32)
    p_scr[1] = jnp.zeros(p_scr.shape[1:], BF16)
    al_scr[1] = jnp.ones(al_scr.shape[1:], F32)
    stage_scores(scores_diag, n_full, 0)
    for p in range(n_diag - 1):
        step(scores_diag, p, p % 2)

    first = n_diag - 1
    s0 = first % 2

    def pair(jj, carry):
        step(scores_full, first + 2 * jj, s0)
        step(scores_full, first + 2 * jj + 1, 1 - s0)
        return carry

    lax.fori_loop(0, n_full // 2, pair, 0)
    odd = n_full % 2 == 1
    last = first + n_full

    @pl.when(odd)
    def _():
        step(scores_full, last - 1, s0)
        stage_softmax(1 - s0)
        stage_pv(last - 1, s0)
        stage_pv(last, 1 - s0)

    @pl.when(jnp.logical_not(odd))
    def _():
        stage_softmax(s0)
        stage_pv(last - 1, 1 - s0)
        stage_pv(last, s0)


def _split_halves(q, rows):
    lane = lax.broadcasted_iota(jnp.int32, q.shape, 1)
    zero = jnp.zeros_like(q)
    return jnp.where(lane < HEAD_DIM, q, zero), jnp.where(lane >= HEAD_DIM, q, zero)


def _diff_attn_kernel(q_ref, k_ref, v_ref, lam_ref, g_ref, o_ref, qm, *scratch,
                      tq, tk, lambda_init):
    qi = pl.program_id(2)
    lo, hi = _split_halves(q_ref[...], tq)
    qm[0:tq, :] = lo
    qm[tq:2 * tq, :] = hi
    q0 = qi * tq
    acc_ref, l_ref = scratch[-1], scratch[-2]

    def scores_full(t):
        k = k_ref[pl.ds(pl.multiple_of(t * tk, tk), tk), :]
        return lax.dot_general(k, qm[...], NT_DIMS, preferred_element_type=F32)

    def scores_diag(t):
        s = scores_full(t)
        k_idx = t * tk + lax.broadcasted_iota(jnp.int32, s.shape, 0)
        q_idx = q0 + lax.broadcasted_iota(jnp.int32, s.shape, 1) % tq
        return jnp.where(k_idx <= q_idx, s, NEG_INF)

    def v_tile(t):
        return v_ref[pl.ds(pl.multiple_of(t * tk, tk), tk), :]

    _pipelined_attention(q0 // tk, tq // tk, scores_full, scores_diag, v_tile, scratch)

    lam = lam_ref[...]
    lam_full = (jnp.exp(jnp.sum(lam[0:1] * lam[1:2], axis=-1, keepdims=True))
                - jnp.exp(jnp.sum(lam[2:3] * lam[3:4], axis=-1, keepdims=True)) + lambda_init)
    acc = acc_ref[...]
    l = jnp.maximum(l_ref[...], 1e-30)
    ot = acc[:, 0:tq] / l[:, 0:tq] - lam_full * (acc[:, tq:2 * tq] / l[:, tq:2 * tq])
    ms = jnp.mean(ot * ot, axis=0, keepdims=True)
    ot = (ot * lax.rsqrt(ms + NORM_EPS)) * g_ref[...] * (1.0 - lambda_init)
    o_ref[...] = ot.T.astype(o_ref.dtype)


def _diff_attention(qkv, lam, subln_g, batch, seq, lambda_init, *, tq=1024, tk=1024):
    assert tq % tk == 0 and seq % tq == 0
    t, n3 = qkv.shape
    d = n3 // 3
    heads = d // DA_V_DIM
    nq = seq // tq
    cb = d // LANES
    g_b = jnp.broadcast_to(subln_g.astype(F32)[:, None], (DA_V_DIM, tq))
    return pl.pallas_call(
        functools.partial(_diff_attn_kernel, tq=tq, tk=tk, lambda_init=lambda_init),
        grid=(batch, heads, nq),
        in_specs=[
            pl.BlockSpec((tq, LANES), lambda b, h, i: (b * nq + i, h)),
            pl.BlockSpec((seq, LANES), lambda b, h, i: (b, cb + h)),
            pl.BlockSpec((seq, LANES), lambda b, h, i: (b, 2 * cb + h)),
            pl.BlockSpec(lam.shape, lambda b, h, i: (0, 0)),
            pl.BlockSpec((DA_V_DIM, tq), lambda b, h, i: (0, 0)),
        ],
        out_specs=pl.BlockSpec((tq, LANES), lambda b, h, i: (b * nq + i, h)),
        out_shape=jax.ShapeDtypeStruct((t, d), BF16),
        scratch_shapes=[pltpu.VMEM((2 * tq, LANES), BF16)] + _attn_scratch(tk, 2 * tq, DA_V_DIM),
        compiler_params=_cparams("arbitrary", "arbitrary", "arbitrary"),
        name="diff_attn",
    )(qkv, qkv, qkv, lam.astype(F32), g_b)


def _route(logits, n_exp):
    lane = lax.broadcasted_iota(jnp.int32, logits.shape, 1)
    lane_f = lane.astype(F32)
    big = float(LANES)
    is_grp = (lane >= n_exp) & (lane < n_exp + MOE_GROUPS)
    gl = jnp.where(is_grp, logits, -jnp.inf)
    ge = jnp.where(is_grp, jnp.exp(gl - jnp.max(gl, axis=-1, keepdims=True)), 0.0)
    pg = ge / jnp.sum(ge, axis=-1, keepdims=True)
    p_top = jnp.max(pg, axis=-1, keepdims=True)
    grp = jnp.min(jnp.where(is_grp & (pg == p_top), lane_f, big), axis=-1, keepdims=True) - n_exp
    in_grp = (lane < n_exp) & ((lane // MOE_EXPERTS_PER_GROUP).astype(F32) == grp)
    sel = jnp.where(in_grp, logits, -jnp.inf)
    v1 = jnp.max(sel, axis=-1, keepdims=True)
    i1 = jnp.min(jnp.where(sel == v1, lane_f, big), axis=-1, keepdims=True)
    sel2 = jnp.where(lane_f == i1, -jnp.inf, sel)
    v2 = jnp.max(sel2, axis=-1, keepdims=True)
    i2 = jnp.min(jnp.where(sel2 == v2, lane_f, big), axis=-1, keepdims=True)
    e2 = jnp.exp(v2 - v1)
    den = 1.0 + e2
    w1 = (1.0 / den) * p_top
    w2 = (e2 / den) * p_top
    return jnp.where(lane_f == i1, w1, 0.0) + jnp.where(lane_f == i2, w2, 0.0), grp


def _outproj_router_kernel(o_ref, w_ref, x_ref, g1_ref, ng_ref, sc_ref, sh_ref, wr_ref, br_ref,
                           xo_ref, h_ref, comb_ref, gid_ref, *, n_exp):
    mix = jnp.dot(o_ref[...], w_ref[...], preferred_element_type=F32)
    x = x_ref[...] + g1_ref[0] * mix
    xo_ref[...] = x
    h = _rms_modulate(x, ng_ref[...], sc_ref[0], sh_ref[0])
    h_hi = h.astype(BF16)
    h_ref[...] = h_hi
    h_lo = (h - h_hi.astype(F32)).astype(BF16)
    wr = wr_ref[...]
    hw = jnp.dot(h_hi, wr, preferred_element_type=F32)
    logits = (hw[:, 0:LANES] + hw[:, LANES:2 * LANES]
              + jnp.dot(h_lo, wr[:, 0:LANES], preferred_element_type=F32) + br_ref[...])
    comb, grp = _route(logits, n_exp)
    comb_ref[...] = comb[:, 0:n_exp]
    gid_ref[...] = jnp.broadcast_to(grp, logits.shape).T[0:gid_ref.shape[0], :]


def _outproj_router(o, w_out, x2d, g1, ng, sc, sh, wr, br, seq, n_exp, *, tm=1024):
    t, d = x2d.shape
    per_b = seq // tm
    bvec = lambda i: (i // per_b, 0, 0)
    return pl.pallas_call(
        functools.partial(_outproj_router_kernel, n_exp=n_exp),
        grid=(t // tm,),
        in_specs=[
            pl.BlockSpec((tm, d), lambda i: (i, 0)),
            pl.BlockSpec((d, d), lambda i: (0, 0)),
            pl.BlockSpec((tm, d), lambda i: (i, 0)),
            pl.BlockSpec((1, 1, d), bvec),
            pl.BlockSpec((1, d), lambda i: (0, 0)),
            pl.BlockSpec((1, 1, d), bvec),
            pl.BlockSpec((1, 1, d), bvec),
            pl.BlockSpec((d, 2 * LANES), lambda i: (0, 0)),
            pl.BlockSpec((1, LANES), lambda i: (0, 0)),
        ],
        out_specs=[
            pl.BlockSpec((tm, d), lambda i: (i, 0)),
            pl.BlockSpec((tm, d), lambda i: (i, 0)),
            pl.BlockSpec((tm, n_exp), lambda i: (i, 0)),
            pl.BlockSpec((8, tm), lambda i: (0, i)),
        ],
        out_shape=[
            jax.ShapeDtypeStruct((t, d), F32),
            jax.ShapeDtypeStruct((t, d), BF16),
            jax.ShapeDtypeStruct((t, n_exp), F32),
            jax.ShapeDtypeStruct((8, t), F32),
        ],
        compiler_params=_cparams("arbitrary"),
        name="outproj_router",
    )(o, w_out, x2d, g1[:, None, :], ng.reshape(1, d), sc[:, None, :], sh[:, None, :], wr, br)


MOE_CHUNK = 128
MOE_EXPERTS_PER_STEP = 4


def _moe_kernel(h_ref, wg_ref, wu_ref, wd_ref, comb_ref, gid_ref, tri_ref, x_ref, g2_ref, fg_ref,
                o_ref, perm_scr, hs_scr, cs_scr, ys_scr, off_smem, nch_smem, *, final_norm):
    step = pl.program_id(1)
    tm = h_ref.shape[0]
    rows = perm_scr.shape[0]
    n_exp = comb_ref.shape[1]
    steps_per_group = MOE_EXPERTS_PER_GROUP // MOE_EXPERTS_PER_STEP

    @pl.when(step == 0)
    def _():
        gid = gid_ref[0:1, :]
        grow = lax.broadcasted_iota(jnp.int32, (8, tm), 0).astype(F32)
        onehot = jnp.where(grow == gid, 1.0, 0.0)
        rank = jnp.dot(onehot.astype(BF16), tri_ref[...], preferred_element_type=F32)
        off = jnp.int32(0)
        off_col = jnp.zeros((8, 1), F32)
        grow_col = lax.broadcasted_iota(jnp.int32, (8, 1), 0)
        for g in range(MOE_GROUPS):
            cnt = jnp.sum(onehot[g:g + 1, :]).astype(jnp.int32)
            nch = (cnt + (MOE_CHUNK - 1)) // MOE_CHUNK
            off_smem[g] = off
            nch_smem[g] = nch
            off_col = jnp.where(grow_col == g, off.astype(F32), off_col)
            off = off + nch * MOE_CHUNK
        dest = jnp.sum(onehot * (rank + off_col), axis=0, keepdims=True)
        row_id = lax.broadcasted_iota(jnp.int32, (rows, tm), 0)
        perm = jnp.where(row_id == dest.astype(jnp.int32), 1.0, 0.0).astype(BF16)
        perm_scr[...] = perm
        hs_scr[...] = jnp.dot(perm, h_ref[...], preferred_element_type=F32).astype(BF16)
        comb = comb_ref[...]
        comb_hi = comb.astype(BF16)
        comb_lo = (comb - comb_hi.astype(F32)).astype(BF16)
        cs_scr[...] = (jnp.dot(perm, comb_hi, preferred_element_type=F32)
                       + jnp.dot(perm, comb_lo, preferred_element_type=F32))
        ys_scr[...] = jnp.zeros(ys_scr.shape, F32)

    g = step // steps_per_group
    off = off_smem[g]
    lane = lax.broadcasted_iota(jnp.int32, (MOE_CHUNK, n_exp), 1)

    def chunk(c, carry):
        r0 = pl.multiple_of(off + c * MOE_CHUNK, MOE_CHUNK)
        xs = hs_scr[pl.ds(r0, MOE_CHUNK), :]
        cs = cs_scr[pl.ds(r0, MOE_CHUNK), :]
        y = jnp.zeros((MOE_CHUNK, xs.shape[1]), F32)
        for k in range(MOE_EXPERTS_PER_STEP):
            gate = jnp.dot(xs, wg_ref[k], preferred_element_type=F32)
            up = jnp.dot(xs, wu_ref[k], preferred_element_type=F32)
            a = (gate * jax.nn.sigmoid(gate)) * up
            e = step * MOE_EXPERTS_PER_STEP + k
            ce = jnp.sum(jnp.where(lane == e, cs, 0.0), axis=-1, keepdims=True)
            y = y + jnp.dot((a * ce).astype(BF16), wd_ref[k], preferred_element_type=F32)
        ys_scr[pl.ds(r0, MOE_CHUNK), :] += y
        return carry

    lax.fori_loop(0, nch_smem[g], chunk, 0)

    @pl.when(step == pl.num_programs(1) - 1)
    def _():
        y = lax.dot_general(perm_scr[...], ys_scr[...].astype(BF16), TN_DIMS,
                            preferred_element_type=F32)
        x = x_ref[...] + g2_ref[0] * y
        if final_norm:
            ms = jnp.mean(x * x, axis=-1, keepdims=True)
            x = (x * lax.rsqrt(ms + NORM_EPS)) * fg_ref[...]
        o_ref[...] = x


def _moe(h, w_gate, w_up, w_down, layer, comb, gid_t, x2d, g2, final_g, seq, *, final_norm, tm=1024):
    t, d = x2d.shape
    _, n_exp, _, hidden = w_gate.shape
    per_b = seq // tm
    eps = MOE_EXPERTS_PER_STEP
    rows = tm + MOE_GROUPS * MOE_CHUNK
    idx = jnp.arange(tm)
    tri = (idx[:, None] < idx[None, :]).astype(BF16)
    return pl.pallas_call(
        functools.partial(_moe_kernel, final_norm=final_norm),
        grid=(t // tm, n_exp // eps),
        in_specs=[
            pl.BlockSpec((tm, d), lambda i, s: (i, 0)),
            pl.BlockSpec((None, eps, d, hidden), lambda i, s: (layer, s, 0, 0)),
            pl.BlockSpec((None, eps, d, hidden), lambda i, s: (layer, s, 0, 0)),
            pl.BlockSpec((None, eps, hidden, d), lambda i, s: (layer, s, 0, 0)),
            pl.BlockSpec((tm, n_exp), lambda i, s: (i, 0)),
            pl.BlockSpec((8, tm), lambda i, s: (0, i)),
            pl.BlockSpec((tm, tm), lambda i, s: (0, 0)),
            pl.BlockSpec((tm, d), lambda i, s: (i, 0)),
            pl.BlockSpec((1, 1, d), lambda i, s: (i // per_b, 0, 0)),
            pl.BlockSpec((1, d), lambda i, s: (0, 0)),
        ],
        out_specs=pl.BlockSpec((tm, d), lambda i, s: (i, 0)),
        out_shape=jax.ShapeDtypeStruct((t, d), F32),
        scratch_shapes=[
            pltpu.VMEM((rows, tm), BF16),
            pltpu.VMEM((rows, d), BF16),
            pltpu.VMEM((rows, n_exp), F32),
            pltpu.VMEM((rows, d), F32),
            pltpu.SMEM((MOE_GROUPS,), jnp.int32),
            pltpu.SMEM((MOE_GROUPS,), jnp.int32),
        ],
        compiler_params=_cparams("arbitrary", "arbitrary"),
        name="moe",
    )(h, w_gate, w_up, w_down, comb, gid_t, tri, x2d, g2[:, None, :], final_g.reshape(1, d))


def _compress_kernel(ch_ref, w1a_ref, w1b_ref, pos_ref, w1f_ref, b1_ref, w2_ref, o_ref):
    ch = ch_ref[0, 0, 0]
    n = ch.shape[0]
    a = jnp.dot(ch, w1a_ref[0], preferred_element_type=F32)
    b = jnp.dot(ch, w1b_ref[0], preferred_element_type=F32)
    const = jnp.dot(pos_ref[0], w1f_ref[0], precision=HIGHEST, preferred_element_type=F32)[0:1]
    hid = a + pltpu.roll(b, n - 1, 0) + (const + b1_ref[0])
    act = jax.nn.gelu(hid, approximate=True)
    o_ref[0, 0, 0] = jnp.dot(act.astype(BF16), w2_ref[0], preferred_element_type=F32).astype(o_ref.dtype)


def _nsa_compress(ch, cmp_pos, cmp_w1, cmp_b1, cmp_w2):
    b, two, g, nc, f = ch.shape
    hid = cmp_w1.shape[-1]
    w1a = cmp_w1[:, :f, :].astype(BF16)
    w1b = cmp_w1[:, f:, :].astype(BF16)
    pos = jnp.zeros((two, 8, 2 * f), F32).at[:, 0, :].set(cmp_pos.reshape(two, 2 * f))
    w2 = jnp.concatenate([cmp_w2, cmp_w2], axis=-1).astype(BF16)
    return pl.pallas_call(
        _compress_kernel,
        grid=(b, two, g),
        in_specs=[
            pl.BlockSpec((1, 1, 1, nc, f), lambda i, j, k: (i, j, k, 0, 0)),
            pl.BlockSpec((1, f, hid), lambda i, j, k: (j, 0, 0)),
            pl.BlockSpec((1, f, hid), lambda i, j, k: (j, 0, 0)),
            pl.BlockSpec((1, 8, 2 * f), lambda i, j, k: (j, 0, 0)),
            pl.BlockSpec((1, 2 * f, hid), lambda i, j, k: (j, 0, 0)),
            pl.BlockSpec((1, 1, hid), lambda i, j, k: (j, 0, 0)),
            pl.BlockSpec((1, hid, LANES), lambda i, j, k: (j, 0, 0)),
        ],
        out_specs=pl.BlockSpec((1, 1, 1, nc, LANES), lambda i, j, k: (i, j, k, 0, 0)),
        out_shape=jax.ShapeDtypeStruct((b, two, g, nc, LANES), BF16),
        compiler_params=_cparams("arbitrary", "arbitrary", "arbitrary"),
        name="nsa_compress",
    )(ch, w1a, w1b, pos, cmp_w1.astype(F32), cmp_b1[:, None, :].astype(F32), w2)


def _stack_group_queries(q_ref, qm, tq):
    for j in range(q_ref.shape[1] // LANES):
        lo, hi = _split_halves(q_ref[:, j * LANES:(j + 1) * LANES], tq)
        qm[(2 * j) * tq:(2 * j + 1) * tq, 0:LANES] = lo
        qm[(2 * j + 1) * tq:(2 * j + 2) * tq, 0:LANES] = hi


def _own_group_values_with_ones(v, g):
    lane = lax.broadcasted_iota(jnp.int32, v.shape, 1)
    own_half = (lane >= HEAD_DIM) == (g == 1)
    return jnp.where(own_half, v, jnp.ones_like(v))


def _gated_pairs_to_token_major(ot, gt_ref, branch, tq, hpg):
    g = pl.program_id(1)
    tiles = []
    for j in range(hpg // 2):
        parts = []
        for h in (2 * j, 2 * j + 1):
            row = gt_ref[pl.ds((g * hpg + h) * 3 + branch, 1), :]
            parts.append(ot[:, h * tq:(h + 1) * tq] * row)
        tiles.append(jnp.concatenate(parts, axis=0).T)
    return tiles


NSA_CMP_CLASSES = 4


def _nsa_window_tiles(q0, qm, kw_ref, vw_ref, gt_ref, acc_ref, *, tq, hpg):
    g = pl.program_id(1)
    span = NSA_WINDOW + tq
    start = pl.multiple_of(jnp.maximum(q0 - NSA_WINDOW, 0), tq)
    k = kw_ref[pl.ds(start, span), :]
    v = vw_ref[pl.ds(start, span), :]
    s = lax.dot_general(k, qm[...], NT_DIMS, preferred_element_type=F32)
    k_idx = start + lax.broadcasted_iota(jnp.int32, (span, tq), 0)
    q_idx = q0 + lax.broadcasted_iota(jnp.int32, (span, tq), 1)
    dist = q_idx - k_idx
    bias = jnp.where(dist >= 0, jnp.where(dist < NSA_WINDOW, 0.0, NEG_INF), NEG_INF)
    s = s + jnp.concatenate([bias] * hpg, axis=1)
    m = jnp.max(s, axis=0, keepdims=True)
    e = jnp.exp2(s - m)
    r = 1.0 / jnp.maximum(jnp.sum(e, axis=0, keepdims=True), 1e-30)
    acc_ref[...] = lax.dot_general(v, e.astype(BF16), TN_DIMS, preferred_element_type=F32) * r
    ot = acc_ref[pl.ds(pl.multiple_of(g * HEAD_DIM, HEAD_DIM), HEAD_DIM), :]
    return _gated_pairs_to_token_major(ot, gt_ref, 2, tq, hpg)


def _nsa_cmp_body(q0, qm, kc_ref, vc_ref, mt_ref, gt_ref, kw_ref, vw_ref, o_ref, mask_ref, acc_ref,
                  *, tq, hpg, n_top, n_cmp, n_blk):
    win_tiles = _nsa_window_tiles(q0, qm, kw_ref, vw_ref, gt_ref, acc_ref, tq=tq, hpg=hpg)
    kc = kc_ref[0, 0, 0, 0:n_cmp, :]
    s = lax.dot_general(kc, qm[...], NT_DIMS, preferred_element_type=F32)
    n_idx = lax.broadcasted_iota(jnp.int32, (n_cmp, tq), 0)
    q_idx = q0 + lax.broadcasted_iota(jnp.int32, (n_cmp, tq), 1)
    valid = n_idx * NSA_CMP_STRIDE + (NSA_CMP_BLOCK - 1) <= q_idx
    bias = jnp.where(valid, 0.0, NEG_INF)
    s = s + jnp.concatenate([bias] * hpg, axis=1)
    m = jnp.maximum(jnp.max(s, axis=0, keepdims=True), 0.1 * NEG_INF)
    e = jnp.exp2(s - m)
    r = 1.0 / jnp.maximum(jnp.sum(e, axis=0, keepdims=True), 1e-30)
    p = e * r
    ot = lax.dot_general(vc_ref[0, 0, 0, 0:n_cmp, :], p.astype(BF16), TN_DIMS,
                         preferred_element_type=F32)
    tiles = _gated_pairs_to_token_major(ot[0:HEAD_DIM], gt_ref, 0, tq, hpg)
    for j, (tile, win) in enumerate(zip(tiles, win_tiles)):
        o_ref[:, j * LANES:(j + 1) * LANES] = (tile + win).astype(o_ref.dtype)

    psum = p[:, 0:tq]
    for h in range(1, hpg):
        psum = psum + p[:, h * tq:(h + 1) * tq]
    p_hi = psum.astype(BF16)
    p_lo = (psum - p_hi.astype(F32)).astype(BF16)
    mt = mt_ref[0:n_blk, 0:n_cmp]
    imp = (jnp.dot(mt, p_hi, preferred_element_type=F32)
           + jnp.dot(mt, p_lo, preferred_element_type=F32))
    j_idx = lax.broadcasted_iota(jnp.int32, imp.shape, 0)
    j_f = j_idx.astype(F32)
    q_blk = (q0 + lax.broadcasted_iota(jnp.int32, imp.shape, 1)) // NSA_SEL_BLOCK
    forced = (j_idx == 0) | (j_idx == q_blk) | (j_idx == q_blk - 1)
    val = jnp.where(forced, -jnp.inf, jnp.where(j_idx <= q_blk, imp, -NSA_FORCE))
    sel_bias = jnp.where(forced, 0.0, NEG_INF)
    for _ in range(max(n_top - 3, 0)):
        mx = jnp.max(val, axis=0, keepdims=True)
        first = jnp.min(jnp.where(val == mx, j_f, float(n_blk)), axis=0, keepdims=True)
        hit = j_f == first
        sel_bias = jnp.where(hit, 0.0, sel_bias)
        val = jnp.where(hit, -jnp.inf, val)
    n_lane = mask_ref.shape[3]
    if n_blk < n_lane:
        sel_bias = jnp.concatenate([sel_bias, jnp.full((n_lane - n_blk, tq), NEG_INF, F32)], axis=0)
    mask_ref[0, 0] = sel_bias.T.astype(mask_ref.dtype)


def _nsa_cmp_win_kernel(q_ref, kc_ref, vc_ref, mt_ref, gt_ref, kw_ref, vw_ref, o_ref, mask_ref,
                        qm, acc_ref, *, tq, hpg, n_top, n_sel):
    qi = pl.program_id(2)
    q0 = qi * tq
    _stack_group_queries(q_ref, qm, tq)
    nc = kc_ref.shape[3]
    seq = n_sel * NSA_SEL_BLOCK
    cls = (q0 + tq - 1) // (seq // NSA_CMP_CLASSES)
    for c in range(NSA_CMP_CLASSES):
        @pl.when(cls == c)
        def _(c=c):
            _nsa_cmp_body(q0, qm, kc_ref, vc_ref, mt_ref, gt_ref, kw_ref, vw_ref, o_ref, mask_ref,
                          acc_ref, tq=tq, hpg=hpg,
                          n_top=min(n_top, (c + 1) * n_sel // NSA_CMP_CLASSES),
                          n_cmp=(c + 1) * nc // NSA_CMP_CLASSES,
                          n_blk=(c + 1) * n_sel // NSA_CMP_CLASSES)


def _nsa_sel_kernel(q_ref, k_ref, v_ref, mask_ref, gt_ref, ocw_ref, o_ref,
                    qm, *scratch, tq, tk, hpg):
    g = pl.program_id(1)
    qi = pl.program_id(2)
    q0 = qi * tq
    _stack_group_queries(q_ref, qm, tq)
    acc_ref, l_ref = scratch[-1], scratch[-2]
    blocks = tk // NSA_SEL_BLOCK

    def scores_full(t):
        first_blk = t * blocks
        window = pl.multiple_of((first_blk // LANES) * LANES, LANES)
        bias_win = mask_ref[0, 0, :, pl.ds(window, LANES)]
        for h in range(hpg):
            qm[h * tq:(h + 1) * tq, LANES:2 * LANES] = bias_win
        lane = lax.broadcasted_iota(jnp.int32, (tk, LANES), 1)
        blk = first_blk - window + lax.broadcasted_iota(jnp.int32, (tk, LANES), 0) // NSA_SEL_BLOCK
        onehot = jnp.where(lane == blk, 1.0, 0.0).astype(BF16)
        k = k_ref[pl.ds(pl.multiple_of(t * tk, tk), tk), :]
        kx = jnp.concatenate([k, onehot], axis=1)
        return lax.dot_general(kx, qm[...], NT_DIMS, preferred_element_type=F32)

    def scores_diag(t):
        s = scores_full(t)
        k_idx = t * tk + lax.broadcasted_iota(jnp.int32, s.shape, 0)
        q_idx = q0 + lax.broadcasted_iota(jnp.int32, s.shape, 1) % tq
        return jnp.where(k_idx <= q_idx, s, NEG_INF)

    def v_tile(t):
        return _own_group_values_with_ones(v_ref[pl.ds(pl.multiple_of(t * tk, tk), tk), :], g)

    _pipelined_attention(q0 // tk, tq // tk, scores_full, scores_diag, v_tile, scratch,
                         sum_in_v=True)

    own = pl.multiple_of(g * HEAD_DIM, HEAD_DIM)
    other = pl.multiple_of((1 - g) * HEAD_DIM, HEAD_DIM)
    inv_l = 1.0 / jnp.maximum(acc_ref[pl.ds(other, 1), :], 1e-30)
    ot = acc_ref[pl.ds(own, HEAD_DIM), :] * inv_l
    tiles = _gated_pairs_to_token_major(ot, gt_ref, 1, tq, hpg)
    for j, tile in enumerate(tiles):
        sl = slice(j * LANES, (j + 1) * LANES)
        total = tile + ocw_ref[:, sl].astype(F32)
        o_ref[:, sl] = total.astype(o_ref.dtype)


def _nsa_attention(proj, gates_t, cmp_kv, batch, seq, d_model, *, tq=128, tq_sel=512, tk=512):
    assert tq_sel % tk == 0 and seq % tq_sel == 0
    t = proj.shape[0]
    groups = NSA_KV_GROUPS
    hpg = d_model // HEAD_DIM // groups
    gw = hpg * HEAD_DIM
    nq = seq // tq
    nq_sel = seq // tq_sel
    nc = cmp_kv.shape[3]
    n_sel = seq // NSA_SEL_BLOCK
    n_top = min(NSA_TOP_N, n_sel)
    ng = gates_t.shape[0]
    qcb = d_model // LANES

    ci = jnp.arange(nc)[None, :] * NSA_CMP_STRIDE
    sj = jnp.arange(n_sel)[:, None] * NSA_SEL_BLOCK
    mt = ((ci < sj + NSA_SEL_BLOCK) & (ci + NSA_CMP_BLOCK > sj)
          & (jnp.arange(nc)[None, :] < nc - 1)).astype(BF16)

    q_spec = pl.BlockSpec((tq, gw), lambda b, g, i: (b * nq + i, g))
    gt_spec = pl.BlockSpec((ng, tq), lambda b, g, i: (0, b * nq + i))
    o_spec = pl.BlockSpec((tq, gw), lambda b, g, i: (b * nq + i, g))
    o_shape = jax.ShapeDtypeStruct((t, d_model), BF16)
    sem = ("arbitrary", "arbitrary", "arbitrary")
    qm_scr = pltpu.VMEM((hpg * tq, LANES), BF16)

    n_lane = max(n_sel, LANES)
    o_cmp_win, sel_mask = pl.pallas_call(
        functools.partial(_nsa_cmp_win_kernel, tq=tq, hpg=hpg, n_top=n_top, n_sel=n_sel),
        grid=(batch, groups, nq),
        in_specs=[
            q_spec,
            pl.BlockSpec((1, 1, 1, nc, LANES), lambda b, g, i: (b, 0, g, 0, 0)),
            pl.BlockSpec((1, 1, 1, nc, LANES), lambda b, g, i: (b, 1, g, 0, 0)),
            pl.BlockSpec((n_sel, nc), lambda b, g, i: (0, 0)),
            gt_spec,
            pl.BlockSpec((seq, LANES), lambda b, g, i: (b, qcb + 5 + g)),
            pl.BlockSpec((seq, LANES), lambda b, g, i: (b, qcb + 7)),
        ],
        out_specs=[o_spec, pl.BlockSpec((1, 1, tq, n_lane), lambda b, g, i: (b, g, i, 0))],
        out_shape=[o_shape, jax.ShapeDtypeStruct((batch, groups, seq, n_lane), BF16)],
        scratch_shapes=[qm_scr, pltpu.VMEM((LANES, hpg * tq), F32)],
        compiler_params=_cparams(*sem),
        name="nsa_cmp_win",
    )(proj, cmp_kv, cmp_kv, mt, gates_t, proj, proj)

    rows_sel = pl.BlockSpec((tq_sel, gw), lambda b, g, i: (b * nq_sel + i, g))
    return pl.pallas_call(
        functools.partial(_nsa_sel_kernel, tq=tq_sel, tk=tk, hpg=hpg),
        grid=(batch, groups, nq_sel),
        in_specs=[
            rows_sel,
            pl.BlockSpec((seq, LANES), lambda b, g, i: (b, qcb + 2 + g)),
            pl.BlockSpec((seq, LANES), lambda b, g, i: (b, qcb + 4)),
            pl.BlockSpec((1, 1, tq_sel, n_lane), lambda b, g, i: (b, g, i, 0)),
            pl.BlockSpec((ng, tq_sel), lambda b, g, i: (0, b * nq_sel + i)),
            rows_sel,
        ],
        out_specs=rows_sel,
        out_shape=o_shape,
        scratch_shapes=([pltpu.VMEM((hpg * tq_sel, 2 * LANES), BF16)]
                        + _attn_scratch(tk, hpg * tq_sel, LANES)),
        compiler_params=_cparams(*sem),
        name="nsa_sel",
    )(proj, proj, proj, sel_mask, gates_t, o_cmp_win)


def _diff_in_weights(w_in, d_model):
    scale = HEAD_DIM ** -0.5 * LOG2E
    wq = w_in[:, :d_model] * scale
    return jnp.concatenate([wq, w_in[:, d_model:]], axis=1).astype(BF16)


def _nsa_in_weights(w_in, d_model):
    g, hd = NSA_KV_GROUPS, HEAD_DIM
    kvw = g * hd
    scale = hd ** -0.5 * LOG2E
    q = w_in[:, :d_model] * scale
    off = d_model
    sec = [w_in[:, off + i * kvw: off + (i + 1) * kvw] for i in range(6)]
    dup = lambda w: jnp.concatenate([w[:, i * hd:(i + 1) * hd] for i in range(g) for _ in range(2)], axis=1)
    w_main = jnp.concatenate([q, sec[0], sec[1], dup(sec[2]), sec[3], dup(sec[4]), sec[5]], axis=1)
    w_gates = w_in[:, off + 6 * kvw:].T
    return w_main.astype(BF16), w_gates.astype(BF16)


def _router_weights(w_grp, b_grp, w_exp, b_exp):
    d = w_grp.shape[0]
    n_exp = w_exp.shape[0] * w_exp.shape[2]
    we = jnp.transpose(w_exp, (1, 0, 2)).reshape(d, n_exp)
    pad = LANES - n_exp - w_grp.shape[1]
    wr = jnp.concatenate([we, w_grp, jnp.zeros((d, pad), F32)], axis=1).astype(F32)
    wr_hi = wr.astype(BF16)
    wr_lo = (wr - wr_hi.astype(F32)).astype(BF16)
    br = jnp.concatenate([b_exp.reshape(-1), b_grp, jnp.zeros((pad,), F32)])[None, :].astype(F32)
    return jnp.concatenate([wr_hi, wr_lo], axis=1), br, n_exp


def kernel(x, c, positions, ada_w, ada_b, norm_g, final_g, diff_w_in, diff_w_out, diff_lambda,
           diff_subln_g, nsa_w_in, nsa_w_out, nsa_cmp_pos, nsa_cmp_w1, nsa_cmp_b1, nsa_cmp_w2,
           moe_w_group, moe_b_group, moe_w_expert, moe_b_expert, moe_w_gate, moe_w_up, moe_w_down):
    batch, seq, d = x.shape
    depth = ada_w.shape[0]
    x2d = x.reshape(batch * seq, d)
    tables = _rope_tables(positions)
    mod = _adaln_mod(c, ada_w, ada_b)
    w_gate, w_up, w_down = (w.astype(BF16) for w in (moe_w_gate, moe_w_up, moe_w_down))

    for i in range(depth):
        sh1, sc1, g1, sh2, sc2, g2 = jnp.split(mod[i], 6, axis=-1)
        j = i // N_MIXERS
        if i % N_MIXERS == 0:
            lambda_init = 0.8 - 0.6 * math.exp(-0.3 * i)
            qkv = _norm_proj(x2d, norm_g[i, 0], sc1, sh1, _diff_in_weights(diff_w_in[j], d), tables,
                             seq, tn=512, plain_tiles=range(2 * d // 512, 3 * d // 512))
            o = _diff_attention(qkv, diff_lambda[j], diff_subln_g[j], batch, seq, lambda_init)
            w_out = diff_w_out[j]
        else:
            w_main, w_gates = _nsa_in_weights(nsa_w_in[j], d)
            proj, gates_t = _norm_proj(x2d, norm_g[i, 0], sc1, sh1, w_main, tables, seq,
                                       tn=256, plain_tiles=(d // 256,), w_gates=w_gates)
            groups = NSA_KV_GROUPS
            kcvc = proj[:, d:d + 2 * groups * HEAD_DIM]
            ch = kcvc.reshape(batch, seq, 2, groups, HEAD_DIM).transpose(0, 2, 3, 1, 4)
            ch = ch.reshape(batch, 2, groups, seq // NSA_CMP_STRIDE, NSA_CMP_STRIDE * HEAD_DIM)
            cmp_kv = _nsa_compress(ch, nsa_cmp_pos[j], nsa_cmp_w1[j], nsa_cmp_b1[j], nsa_cmp_w2[j])
            o = _nsa_attention(proj, gates_t, cmp_kv, batch, seq, d)
            w_out = nsa_w_out[j]
        wr, br, n_exp = _router_weights(moe_w_group[i], moe_b_group[i], moe_w_expert[i], moe_b_expert[i])
        x2d, h2, comb, gid_t = _outproj_router(o, w_out.astype(BF16), x2d, g1, norm_g[i, 1], sc2, sh2,
                                               wr, br, seq, n_exp)
        x2d = _moe(h2, w_gate, w_up, w_down, i, comb, gid_t, x2d, g2, final_g, seq,
                   final_norm=(i == depth - 1))
    return x2d.reshape(batch, seq, d)
```

```python
import functools
import math

import jax
import jax.numpy as jnp
from jax import lax
from jax.experimental import pallas as pl
from jax.experimental.pallas import tpu as pltpu

F32 = jnp.float32
BF16 = jnp.bfloat16
HIGHEST = lax.Precision.HIGHEST

ROPE_THETA = 500000.0
ROT_DIM = 16
NORM_EPS = 1e-6
NEG_INF = -1e30
HEAD_DIM = 64
DA_V_DIM = 2 * HEAD_DIM
NSA_KV_GROUPS = 2
NSA_CMP_BLOCK = 32
NSA_CMP_STRIDE = 16
NSA_SEL_BLOCK = 64
NSA_TOP_N = 16
NSA_WINDOW = 512
NSA_FORCE = 1e4
MOE_GROUPS = 4
MOE_EXPERTS_PER_GROUP = 8
MOE_TOP_K = 2
N_MIXERS = 2
LOG2E = 1.4426950408889634

LANES = 128
PROJ_ROW_CHUNK = 256
VMEM_LIMIT_BYTES = 56 * 1024 * 1024

NT_DIMS = (((1,), (1,)), ((), ()))
TN_DIMS = (((0,), (0,)), ((), ()))


def _cparams(*sem):
    return pltpu.CompilerParams(dimension_semantics=sem, vmem_limit_bytes=VMEM_LIMIT_BYTES)


def _mod_kernel(c_ref, w_ref, b_ref, o_ref):
    c = c_ref[...]
    ca = c * jax.nn.sigmoid(c)
    o_ref[0] = jnp.dot(ca, w_ref[0], precision=HIGHEST, preferred_element_type=F32) + b_ref[0]


def _adaln_mod(c, ada_w, ada_b):
    depth, d, n = ada_w.shape
    b = c.shape[0]
    rows = 8
    tn = 1536
    c_pad = jnp.zeros((rows, d), F32).at[:b].set(c)
    out = pl.pallas_call(
        _mod_kernel,
        grid=(depth, n // tn),
        in_specs=[
            pl.BlockSpec((rows, d), lambda i, j: (0, 0)),
            pl.BlockSpec((1, d, tn), lambda i, j: (i, 0, j)),
            pl.BlockSpec((1, 1, tn), lambda i, j: (i, 0, j)),
        ],
        out_specs=pl.BlockSpec((1, rows, tn), lambda i, j: (i, 0, j)),
        out_shape=jax.ShapeDtypeStruct((depth, rows, n), F32),
        compiler_params=_cparams("arbitrary", "arbitrary"),
        name="adaln_mod",
    )(c_pad, ada_w, ada_b.reshape(depth, 1, n))
    return out[:, :b, :]


def _rope_tables(positions):
    half = ROT_DIM // 2
    inv_freq = ROPE_THETA ** (-jnp.arange(0, ROT_DIM, 2, dtype=F32) / ROT_DIM)
    ang = positions.astype(F32).reshape(-1)[:, None] * inv_freq
    cos, sin = jnp.cos(ang), jnp.sin(ang)
    t = ang.shape[0]
    rest = HEAD_DIM - ROT_DIM
    z8 = jnp.zeros((t, half), F32)
    cos_h = jnp.concatenate([cos, cos, jnp.ones((t, rest), F32)], axis=-1)
    sina_h = jnp.concatenate([-sin, z8, jnp.zeros((t, rest), F32)], axis=-1)
    sinb_h = jnp.concatenate([z8, sin, jnp.zeros((t, rest), F32)], axis=-1)
    rep = LANES // HEAD_DIM
    return tuple(jnp.tile(a, (1, rep)) for a in (cos_h, sina_h, sinb_h))


def _rms_modulate(x, g, sc, sh):
    ms = jnp.mean(x * x, axis=-1, keepdims=True)
    y = x * lax.rsqrt(ms + NORM_EPS)
    return (y * g) * (1.0 + sc) + sh


def _proj_kernel(*refs, plain_tiles, with_gates):
    if with_gates:
        (x_ref, g_ref, sc_ref, sh_ref, w_ref, cos_ref, sa_ref, sb_ref, wg_ref,
         o_ref, gt_ref, h_scr) = refs
    else:
        x_ref, g_ref, sc_ref, sh_ref, w_ref, cos_ref, sa_ref, sb_ref, o_ref, h_scr = refs
    j = pl.program_id(1)
    tn = o_ref.shape[1]

    @pl.when(j == 0)
    def _():
        h = _rms_modulate(x_ref[...], g_ref[...], sc_ref[0], sh_ref[0])
        h_scr[...] = h.astype(BF16)
        if with_gates:
            logits = lax.dot_general(wg_ref[...], h_scr[...], NT_DIMS, preferred_element_type=F32)
            gt_ref[...] = jax.nn.sigmoid(logits)

    is_plain = functools.reduce(jnp.logical_or, [j == t for t in plain_tiles], j < 0)
    for r in range(h_scr.shape[0] // PROJ_ROW_CHUNK):
        rows = slice(r * PROJ_ROW_CHUNK, (r + 1) * PROJ_ROW_CHUNK)
        acc = jnp.dot(h_scr[rows, :], w_ref[...], preferred_element_type=F32)
        cos = jnp.where(is_plain, 1.0, cos_ref[rows, :])
        sa = jnp.where(is_plain, 0.0, sa_ref[rows, :])
        sb = jnp.where(is_plain, 0.0, sb_ref[rows, :])
        shift_up = pltpu.roll(acc, tn - ROT_DIM // 2, 1)
        shift_dn = pltpu.roll(acc, ROT_DIM // 2, 1)
        for c in range(tn // LANES):
            sl = slice(c * LANES, (c + 1) * LANES)
            y = acc[:, sl] * cos + shift_up[:, sl] * sa + shift_dn[:, sl] * sb
            o_ref[rows, sl] = y.astype(o_ref.dtype)


def _norm_proj(x2d, g, sc, sh, w, tables, seq, *, tn, plain_tiles, w_gates=None, tm=1024):
    t, d = x2d.shape
    n = w.shape[1]
    with_gates = w_gates is not None
    per_b = seq // tm
    in_specs = [
        pl.BlockSpec((tm, d), lambda i, j: (i, 0)),
        pl.BlockSpec((1, d), lambda i, j: (0, 0)),
        pl.BlockSpec((1, 1, d), lambda i, j: (i // per_b, 0, 0)),
        pl.BlockSpec((1, 1, d), lambda i, j: (i // per_b, 0, 0)),
        pl.BlockSpec((d, tn), lambda i, j: (0, j)),
        pl.BlockSpec((tm, LANES), lambda i, j: (i, 0)),
        pl.BlockSpec((tm, LANES), lambda i, j: (i, 0)),
        pl.BlockSpec((tm, LANES), lambda i, j: (i, 0)),
    ]
    args = [x2d, g.reshape(1, d), sc[:, None, :], sh[:, None, :], w, *tables]
    out_specs = [pl.BlockSpec((tm, tn), lambda i, j: (i, j))]
    out_shape = [jax.ShapeDtypeStruct((t, n), BF16)]
    if with_gates:
        ng = w_gates.shape[0]
        in_specs.append(pl.BlockSpec((ng, d), lambda i, j: (0, 0)))
        args.append(w_gates)
        out_specs.append(pl.BlockSpec((ng, tm), lambda i, j: (0, i)))
        out_shape.append(jax.ShapeDtypeStruct((ng, t), F32))
    res = pl.pallas_call(
        functools.partial(_proj_kernel, plain_tiles=tuple(plain_tiles), with_gates=with_gates),
        grid=(t // tm, n // tn),
        in_specs=in_specs,
        out_specs=out_specs,
        out_shape=out_shape,
        scratch_shapes=[pltpu.VMEM((tm, d), BF16)],
        compiler_params=_cparams("arbitrary", "arbitrary"),
        name="norm_proj",
    )(*args)
    return res if with_gates else res[0]


def _attn_scratch(tk, nq, dv):
    return [
        pltpu.VMEM((2, tk, nq), F32),
        pltpu.VMEM((2, 1, nq), F32),
        pltpu.VMEM((2, tk, nq), BF16),
        pltpu.VMEM((2, 1, nq), F32),
        pltpu.VMEM((1, nq), F32),
        pltpu.VMEM((1, nq), F32),
        pltpu.VMEM((dv, nq), F32),
    ]


def _pipelined_attention(n_full, n_diag, scores_full, scores_diag, v_tile, scratch, sum_in_v=False):
    s_scr, mx_scr, p_scr, al_scr, m_ref, l_ref, acc_ref = scratch

    def stage_scores(fn, tile, slot):
        s = fn(tile)
        s_scr[slot] = s
        mx_scr[slot] = jnp.max(s, axis=0, keepdims=True)

    def stage_softmax(slot):
        m_prev = m_ref[...]
        m_new = jnp.maximum(m_prev, mx_scr[slot])
        alpha = jnp.exp2(m_prev - m_new)
        p = jnp.exp2(s_scr[slot] - m_new)
        if not sum_in_v:
            l_ref[...] = alpha * l_ref[...] + jnp.sum(p, axis=0, keepdims=True)
        m_ref[...] = m_new
        al_scr[slot] = alpha
        p_scr[slot] = p.astype(BF16)

    def stage_pv(pos, slot):
        tile = jnp.where(pos < n_diag, n_full + jnp.maximum(pos, 0), pos - n_diag)
        pv = lax.dot_general(v_tile(tile), p_scr[slot], TN_DIMS, preferred_element_type=F32)
        acc_ref[...] = acc_ref[...] * al_scr[slot] + pv

    def step(fn, pos, slot):
        nxt = pos + 1
        stage_scores(fn, jnp.where(nxt < n_diag, n_full + nxt, nxt - n_diag), 1 - slot)
        stage_softmax(slot)
        stage_pv(pos - 1, 1 - slot)

    m_ref[...] = jnp.full(m_ref.shape, NEG_INF, F32)
    l_ref[...] = jnp.zeros(l_ref.shape, F32)
    acc_ref[...] = jnp.zeros(acc_ref.shape, F32)
    p_scr[1] = jnp.zeros(p_scr.shape[1:], BF16)
    al_scr[1] = jnp.ones(al_scr.shape[1:], F32)
    stage_scores(scores_diag, n_full, 0)
    for p in range(n_diag - 1):
        step(scores_diag, p, p % 2)

    first = n_diag - 1
    s0 = first % 2

    def pair(jj, carry):
        step(scores_full, first + 2 * jj, s0)
        step(scores_full, first + 2 * jj + 1, 1 - s0)
        return carry

    lax.fori_loop(0, n_full // 2, pair, 0)
    odd = n_full % 2 == 1
    last = first + n_full

    @pl.when(odd)
    def _():
        step(scores_full, last - 1, s0)
        stage_softmax(1 - s0)
        stage_pv(last - 1, s0)
        stage_pv(last, 1 - s0)

    @pl.when(jnp.logical_not(odd))
    def _():
        stage_softmax(s0)
        stage_pv(last - 1, 1 - s0)
        stage_pv(last, s0)


def _split_halves(q, rows):
    lane = lax.broadcasted_iota(jnp.int32, q.shape, 1)
    zero = jnp.zeros_like(q)
    return jnp.where(lane < HEAD_DIM, q, zero), jnp.where(lane >= HEAD_DIM, q, zero)


def _diff_attn_kernel(q_ref, k_ref, v_ref, lam_ref, g_ref, o_ref, qm, *scratch,
                      tq, tk, lambda_init):
    qi = pl.program_id(2)
    lo, hi = _split_halves(q_ref[...], tq)
    qm[0:tq, :] = lo
    qm[tq:2 * tq, :] = hi
    q0 = qi * tq
    acc_ref, l_ref = scratch[-1], scratch[-2]

    def scores_full(t):
        k = k_ref[pl.ds(pl.multiple_of(t * tk, tk), tk), :]
        return lax.dot_general(k, qm[...], NT_DIMS, preferred_element_type=F32)

    def scores_diag(t):
        s = scores_full(t)
        k_idx = t * tk + lax.broadcasted_iota(jnp.int32, s.shape, 0)
        q_idx = q0 + lax.broadcasted_iota(jnp.int32, s.shape, 1) % tq
        return jnp.where(k_idx <= q_idx, s, NEG_INF)

    def v_tile(t):
        v = v_ref[pl.ds(pl.multiple_of(t * tk, tk), tk), :]
        return jnp.concatenate([v, jnp.ones((tk, 16), BF16)], axis=1)

    _pipelined_attention(q0 // tk, tq // tk, scores_full, scores_diag, v_tile, scratch, sum_in_v=True)

    lam = lam_ref[...]
    lam_full = (jnp.exp(jnp.sum(lam[0:1] * lam[1:2], axis=-1, keepdims=True))
                - jnp.exp(jnp.sum(lam[2:3] * lam[3:4], axis=-1, keepdims=True)) + lambda_init)
    acc = acc_ref[0:DA_V_DIM, :]
    l = jnp.maximum(acc_ref[DA_V_DIM:DA_V_DIM + 1, :], 1e-30)
    ot = acc[:, 0:tq] / l[:, 0:tq] - lam_full * (acc[:, tq:2 * tq] / l[:, tq:2 * tq])
    ms = jnp.mean(ot * ot, axis=0, keepdims=True)
    ot = (ot * lax.rsqrt(ms + NORM_EPS)) * g_ref[...] * (1.0 - lambda_init)
    o_ref[...] = ot.T.astype(o_ref.dtype)


def _diff_attention(qkv, lam, subln_g, batch, seq, lambda_init, *, tq=1024, tk=1024):
    assert tq % tk == 0 and seq % tq == 0
    t, n3 = qkv.shape
    d = n3 // 3
    heads = d // DA_V_DIM
    nq = seq // tq
    cb = d // LANES
    g_b = jnp.broadcast_to(subln_g.astype(F32)[:, None], (DA_V_DIM, tq))
    return pl.pallas_call(
        functools.partial(_diff_attn_kernel, tq=tq, tk=tk, lambda_init=lambda_init),
        grid=(batch, heads, nq),
        in_specs=[
            pl.BlockSpec((tq, LANES), lambda b, h, i: (b * nq + i, h)),
            pl.BlockSpec((seq, LANES), lambda b, h, i: (b, cb + h)),
            pl.BlockSpec((seq, LANES), lambda b, h, i: (b, 2 * cb + h)),
            pl.BlockSpec(lam.shape, lambda b, h, i: (0, 0)),
            pl.BlockSpec((DA_V_DIM, tq), lambda b, h, i: (0, 0)),
        ],
        out_specs=pl.BlockSpec((tq, LANES), lambda b, h, i: (b * nq + i, h)),
        out_shape=jax.ShapeDtypeStruct((t, d), BF16),
        scratch_shapes=[pltpu.VMEM((2 * tq, LANES), BF16)] + _attn_scratch(tk, 2 * tq, DA_V_DIM + 16),
        compiler_params=_cparams("arbitrary", "arbitrary", "arbitrary"),
        name="diff_attn",
    )(qkv, qkv, qkv, lam.astype(F32), g_b)


def _route(logits, n_exp):
    lane = lax.broadcasted_iota(jnp.int32, logits.shape, 1)
    lane_f = lane.astype(F32)
    big = float(LANES)
    is_grp = (lane >= n_exp) & (lane < n_exp + MOE_GROUPS)
    gl = jnp.where(is_grp, logits, -jnp.inf)
    ge = jnp.where(is_grp, jnp.exp(gl - jnp.max(gl, axis=-1, keepdims=True)), 0.0)
    pg = ge / jnp.sum(ge, axis=-1, keepdims=True)
    p_top = jnp.max(pg, axis=-1, keepdims=True)
    grp = jnp.min(jnp.where(is_grp & (pg == p_top), lane_f, big), axis=-1, keepdims=True) - n_exp
    in_grp = (lane < n_exp) & ((lane // MOE_EXPERTS_PER_GROUP).astype(F32) == grp)
    sel = jnp.where(in_grp, logits, -jnp.inf)
    v1 = jnp.max(sel, axis=-1, keepdims=True)
    i1 = jnp.min(jnp.where(sel == v1, lane_f, big), axis=-1, keepdims=True)
    sel2 = jnp.where(lane_f == i1, -jnp.inf, sel)
    v2 = jnp.max(sel2, axis=-1, keepdims=True)
    i2 = jnp.min(jnp.where(sel2 == v2, lane_f, big), axis=-1, keepdims=True)
    e2 = jnp.exp(v2 - v1)
    den = 1.0 + e2
    w1 = (1.0 / den) * p_top
    w2 = (e2 / den) * p_top
    return jnp.where(lane_f == i1, w1, 0.0) + jnp.where(lane_f == i2, w2, 0.0), grp


def _outproj_router_kernel(o_ref, w_ref, x_ref, g1_ref, ng_ref, sc_ref, sh_ref, wr_ref, br_ref,
                           xo_ref, h_ref, comb_ref, gid_ref, *, n_exp):
    mix = jnp.dot(o_ref[...], w_ref[...], preferred_element_type=F32)
    x = x_ref[...] + g1_ref[0] * mix
    xo_ref[...] = x
    h = _rms_modulate(x, ng_ref[...], sc_ref[0], sh_ref[0])
    h_hi = h.astype(BF16)
    h_ref[...] = h_hi
    h_lo = (h - h_hi.astype(F32)).astype(BF16)
    wr = wr_ref[...]
    hw = jnp.dot(h_hi, wr, preferred_element_type=F32)
    logits = (hw[:, 0:LANES] + hw[:, LANES:2 * LANES]
              + jnp.dot(h_lo, wr[:, 0:LANES], preferred_element_type=F32) + br_ref[...])
    comb, grp = _route(logits, n_exp)
    comb_ref[...] = comb[:, 0:n_exp]
    gid_ref[...] = jnp.broadcast_to(grp, logits.shape).T[0:gid_ref.shape[0], :]


def _outproj_router(o, w_out, x2d, g1, ng, sc, sh, wr, br, seq, n_exp, *, tm=1024):
    t, d = x2d.shape
    per_b = seq // tm
    bvec = lambda i: (i // per_b, 0, 0)
    return pl.pallas_call(
        functools.partial(_outproj_router_kernel, n_exp=n_exp),
        grid=(t // tm,),
        in_specs=[
            pl.BlockSpec((tm, d), lambda i: (i, 0)),
            pl.BlockSpec((d, d), lambda i: (0, 0)),
            pl.BlockSpec((tm, d), lambda i: (i, 0)),
            pl.BlockSpec((1, 1, d), bvec),
            pl.BlockSpec((1, d), lambda i: (0, 0)),
            pl.BlockSpec((1, 1, d), bvec),
            pl.BlockSpec((1, 1, d), bvec),
            pl.BlockSpec((d, 2 * LANES), lambda i: (0, 0)),
            pl.BlockSpec((1, LANES), lambda i: (0, 0)),
        ],
        out_specs=[
            pl.BlockSpec((tm, d), lambda i: (i, 0)),
            pl.BlockSpec((tm, d), lambda i: (i, 0)),
            pl.BlockSpec((tm, n_exp), lambda i: (i, 0)),
            pl.BlockSpec((8, tm), lambda i: (0, i)),
        ],
        out_shape=[
            jax.ShapeDtypeStruct((t, d), F32),
            jax.ShapeDtypeStruct((t, d), BF16),
            jax.ShapeDtypeStruct((t, n_exp), F32),
            jax.ShapeDtypeStruct((8, t), F32),
        ],
        compiler_params=_cparams("arbitrary"),
        name="outproj_router",
    )(o, w_out, x2d, g1[:, None, :], ng.reshape(1, d), sc[:, None, :], sh[:, None, :], wr, br)


MOE_CHUNK = 128
MOE_EXPERTS_PER_STEP = 4


def _moe_kernel(h_ref, wg_ref, wu_ref, wd_ref, comb_ref, gid_ref, tri_ref, x_ref, g2_ref, fg_ref,
                o_ref, perm_scr, hs_scr, cs_scr, ys_scr, off_smem, nch_smem, *, final_norm):
    step = pl.program_id(1)
    tm = h_ref.shape[0]
    rows = perm_scr.shape[0]
    n_exp = comb_ref.shape[1]
    steps_per_group = MOE_EXPERTS_PER_GROUP // MOE_EXPERTS_PER_STEP

    @pl.when(step == 0)
    def _():
        gid = gid_ref[0:1, :]
        grow = lax.broadcasted_iota(jnp.int32, (8, tm), 0).astype(F32)
        onehot = jnp.where(grow == gid, 1.0, 0.0)
        rank = jnp.dot(onehot.astype(BF16), tri_ref[...], preferred_element_type=F32)
        off = jnp.int32(0)
        off_col = jnp.zeros((8, 1), F32)
        grow_col = lax.broadcasted_iota(jnp.int32, (8, 1), 0)
        for g in range(MOE_GROUPS):
            cnt = jnp.sum(onehot[g:g + 1, :]).astype(jnp.int32)
            nch = (cnt + (MOE_CHUNK - 1)) // MOE_CHUNK
            off_smem[g] = off
            nch_smem[g] = nch
            off_col = jnp.where(grow_col == g, off.astype(F32), off_col)
            off = off + nch * MOE_CHUNK
        dest = jnp.sum(onehot * (rank + off_col), axis=0, keepdims=True)
        row_id = lax.broadcasted_iota(jnp.int32, (rows, tm), 0)
        perm = jnp.where(row_id == dest.astype(jnp.int32), 1.0, 0.0).astype(BF16)
        perm_scr[...] = perm
        hs_scr[...] = jnp.dot(perm, h_ref[...], preferred_element_type=F32).astype(BF16)
        comb = comb_ref[...]
        comb_hi = comb.astype(BF16)
        comb_lo = (comb - comb_hi.astype(F32)).astype(BF16)
        cs_scr[...] = (jnp.dot(perm, comb_hi, preferred_element_type=F32)
                       + jnp.dot(perm, comb_lo, preferred_element_type=F32))
        ys_scr[...] = jnp.zeros(ys_scr.shape, F32)

    g = step // steps_per_group
    off = off_smem[g]
    lane = lax.broadcasted_iota(jnp.int32, (MOE_CHUNK, n_exp), 1)

    def chunk(c, carry):
        r0 = pl.multiple_of(off + c * MOE_CHUNK, MOE_CHUNK)
        xs = hs_scr[pl.ds(r0, MOE_CHUNK), :]
        cs = cs_scr[pl.ds(r0, MOE_CHUNK), :]
        y = jnp.zeros((MOE_CHUNK, xs.shape[1]), F32)
        for k in range(MOE_EXPERTS_PER_STEP):
            gate = jnp.dot(xs, wg_ref[k], preferred_element_type=F32)
            up = jnp.dot(xs, wu_ref[k], preferred_element_type=F32)
            a = (gate * jax.nn.sigmoid(gate)) * up
            e = step * MOE_EXPERTS_PER_STEP + k
            ce = jnp.sum(jnp.where(lane == e, cs, 0.0), axis=-1, keepdims=True)
            y = y + jnp.dot((a * ce).astype(BF16), wd_ref[k], preferred_element_type=F32)
        ys_scr[pl.ds(r0, MOE_CHUNK), :] += y
        return carry

    lax.fori_loop(0, nch_smem[g], chunk, 0)

    @pl.when(step == pl.num_programs(1) - 1)
    def _():
        y = lax.dot_general(perm_scr[...], ys_scr[...].astype(BF16), TN_DIMS,
                            preferred_element_type=F32)
        x = x_ref[...] + g2_ref[0] * y
        if final_norm:
            ms = jnp.mean(x * x, axis=-1, keepdims=True)
            x = (x * lax.rsqrt(ms + NORM_EPS)) * fg_ref[...]
        o_ref[...] = x


def _moe(h, w_gate, w_up, w_down, layer, comb, gid_t, x2d, g2, final_g, seq, *, final_norm, tm=1024):
    t, d = x2d.shape
    _, n_exp, _, hidden = w_gate.shape
    per_b = seq // tm
    eps = MOE_EXPERTS_PER_STEP
    rows = tm + MOE_GROUPS * MOE_CHUNK
    idx = jnp.arange(tm)
    tri = (idx[:, None] < idx[None, :]).astype(BF16)
    return pl.pallas_call(
        functools.partial(_moe_kernel, final_norm=final_norm),
        grid=(t // tm, n_exp // eps),
        in_specs=[
            pl.BlockSpec((tm, d), lambda i, s: (i, 0)),
            pl.BlockSpec((None, eps, d, hidden), lambda i, s: (layer, s, 0, 0)),
            pl.BlockSpec((None, eps, d, hidden), lambda i, s: (layer, s, 0, 0)),
            pl.BlockSpec((None, eps, hidden, d), lambda i, s: (layer, s, 0, 0)),
            pl.BlockSpec((tm, n_exp), lambda i, s: (i, 0)),
            pl.BlockSpec((8, tm), lambda i, s: (0, i)),
            pl.BlockSpec((tm, tm), lambda i, s: (0, 0)),
            pl.BlockSpec((tm, d), lambda i, s: (i, 0)),
            pl.BlockSpec((1, 1, d), lambda i, s: (i // per_b, 0, 0)),
            pl.BlockSpec((1, d), lambda i, s: (0, 0)),
        ],
        out_specs=pl.BlockSpec((tm, d), lambda i, s: (i, 0)),
        out_shape=jax.ShapeDtypeStruct((t, d), F32),
        scratch_shapes=[
            pltpu.VMEM((rows, tm), BF16),
            pltpu.VMEM((rows, d), BF16),
            pltpu.VMEM((rows, n_exp), F32),
            pltpu.VMEM((rows, d), F32),
            pltpu.SMEM((MOE_GROUPS,), jnp.int32),
            pltpu.SMEM((MOE_GROUPS,), jnp.int32),
        ],
        compiler_params=_cparams("arbitrary", "arbitrary"),
        name="moe",
    )(h, w_gate, w_up, w_down, comb, gid_t, tri, x2d, g2[:, None, :], final_g.reshape(1, d))


def _compress_kernel(ch_ref, w1a_ref, w1b_ref, pos_ref, w1f_ref, b1_ref, w2_ref, o_ref):
    ch = ch_ref[0, 0, 0]
    n = ch.shape[0]
    a = jnp.dot(ch, w1a_ref[0], preferred_element_type=F32)
    b = jnp.dot(ch, w1b_ref[0], preferred_element_type=F32)
    const = jnp.dot(pos_ref[0], w1f_ref[0], precision=HIGHEST, preferred_element_type=F32)[0:1]
    hid = a + pltpu.roll(b, n - 1, 0) + (const + b1_ref[0])
    act = jax.nn.gelu(hid, approximate=True)
    o_ref[0, 0, 0] = jnp.dot(act.astype(BF16), w2_ref[0], preferred_element_type=F32).astype(o_ref.dtype)


def _nsa_compress(ch, cmp_pos, cmp_w1, cmp_b1, cmp_w2):
    b, two, g, nc, f = ch.shape
    hid = cmp_w1.shape[-1]
    w1a = cmp_w1[:, :f, :].astype(BF16)
    w1b = cmp_w1[:, f:, :].astype(BF16)
    pos = jnp.zeros((two, 8, 2 * f), F32).at[:, 0, :].set(cmp_pos.reshape(two, 2 * f))
    w2 = jnp.concatenate([cmp_w2, cmp_w2], axis=-1).astype(BF16)
    return pl.pallas_call(
        _compress_kernel,
        grid=(b, two, g),
        in_specs=[
            pl.BlockSpec((1, 1, 1, nc, f), lambda i, j, k: (i, j, k, 0, 0)),
            pl.BlockSpec((1, f, hid), lambda i, j, k: (j, 0, 0)),
            pl.BlockSpec((1, f, hid), lambda i, j, k: (j, 0, 0)),
            pl.BlockSpec((1, 8, 2 * f), lambda i, j, k: (j, 0, 0)),
            pl.BlockSpec((1, 2 * f, hid), lambda i, j, k: (j, 0, 0)),
            pl.BlockSpec((1, 1, hid), lambda i, j, k: (j, 0, 0)),
            pl.BlockSpec((1, hid, LANES), lambda i, j, k: (j, 0, 0)),
        ],
        out_specs=pl.BlockSpec((1, 1, 1, nc, LANES), lambda i, j, k: (i, j, k, 0, 0)),
        out_shape=jax.ShapeDtypeStruct((b, two, g, nc, LANES), BF16),
        compiler_params=_cparams("arbitrary", "arbitrary", "arbitrary"),
        name="nsa_compress",
    )(ch, w1a, w1b, pos, cmp_w1.astype(F32), cmp_b1[:, None, :].astype(F32), w2)


def _stack_group_queries(q_ref, qm, tq):
    for j in range(q_ref.shape[1] // LANES):
        lo, hi = _split_halves(q_ref[:, j * LANES:(j + 1) * LANES], tq)
        qm[(2 * j) * tq:(2 * j + 1) * tq, 0:LANES] = lo
        qm[(2 * j + 1) * tq:(2 * j + 2) * tq, 0:LANES] = hi


def _own_group_values_with_ones(v, g):
    lane = lax.broadcasted_iota(jnp.int32, v.shape, 1)
    own_half = (lane >= HEAD_DIM) == (g == 1)
    return jnp.where(own_half, v, jnp.ones_like(v))


def _gated_pairs_to_token_major(ot, gt_ref, branch, tq, hpg):
    g = pl.program_id(1)
    tiles = []
    for j in range(hpg // 2):
        parts = []
        for h in (2 * j, 2 * j + 1):
            row = gt_ref[pl.ds((g * hpg + h) * 3 + branch, 1), :]
            parts.append(ot[:, h * tq:(h + 1) * tq] * row)
        tiles.append(jnp.concatenate(parts, axis=0).T)
    return tiles


NSA_CMP_CLASSES = 4


def _nsa_window_tiles(q0, qm, kw_ref, vw_ref, gt_ref, acc_ref, *, tq, hpg):
    g = pl.program_id(1)
    span = NSA_WINDOW + tq
    start = pl.multiple_of(jnp.maximum(q0 - NSA_WINDOW, 0), tq)
    k = kw_ref[pl.ds(start, span), :]
    v = vw_ref[pl.ds(start, span), :]
    s = lax.dot_general(k, qm[...], NT_DIMS, preferred_element_type=F32)
    k_idx = start + lax.broadcasted_iota(jnp.int32, (span, tq), 0)
    q_idx = q0 + lax.broadcasted_iota(jnp.int32, (span, tq), 1)
    dist = q_idx - k_idx
    bias = jnp.where(dist >= 0, jnp.where(dist < NSA_WINDOW, 0.0, NEG_INF), NEG_INF)
    s = s + jnp.concatenate([bias] * hpg, axis=1)
    m = jnp.max(s, axis=0, keepdims=True)
    e = jnp.exp2(s - m)
    r = 1.0 / jnp.maximum(jnp.sum(e, axis=0, keepdims=True), 1e-30)
    acc_ref[...] = lax.dot_general(v, e.astype(BF16), TN_DIMS, preferred_element_type=F32) * r
    ot = acc_ref[pl.ds(pl.multiple_of(g * HEAD_DIM, HEAD_DIM), HEAD_DIM), :]
    return _gated_pairs_to_token_major(ot, gt_ref, 2, tq, hpg)


def _nsa_cmp_body(q0, qm, kc_ref, vc_ref, mt_ref, gt_ref, kw_ref, vw_ref, o_ref, mask_ref, acc_ref,
                  *, tq, hpg, n_top, n_cmp, n_blk):
    win_tiles = _nsa_window_tiles(q0, qm, kw_ref, vw_ref, gt_ref, acc_ref, tq=tq, hpg=hpg)
    kc = kc_ref[0, 0, 0, 0:n_cmp, :]
    s = lax.dot_general(kc, qm[...], NT_DIMS, preferred_element_type=F32)
    n_idx = lax.broadcasted_iota(jnp.int32, (n_cmp, tq), 0)
    q_idx = q0 + lax.broadcasted_iota(jnp.int32, (n_cmp, tq), 1)
    valid = n_idx * NSA_CMP_STRIDE + (NSA_CMP_BLOCK - 1) <= q_idx
    bias = jnp.where(valid, 0.0, NEG_INF)
    s = s + jnp.concatenate([bias] * hpg, axis=1)
    m = jnp.maximum(jnp.max(s, axis=0, keepdims=True), 0.1 * NEG_INF)
    e = jnp.exp2(s - m)
    r = 1.0 / jnp.maximum(jnp.sum(e, axis=0, keepdims=True), 1e-30)
    p = e * r
    ot = lax.dot_general(vc_ref[0, 0, 0, 0:n_cmp, :], p.astype(BF16), TN_DIMS,
                         preferred_element_type=F32)
    tiles = _gated_pairs_to_token_major(ot[0:HEAD_DIM], gt_ref, 0, tq, hpg)
    for j, (tile, win) in enumerate(zip(tiles, win_tiles)):
        o_ref[:, j * LANES:(j + 1) * LANES] = (tile + win).astype(o_ref.dtype)

    psum = p[:, 0:tq]
    for h in range(1, hpg):
        psum = psum + p[:, h * tq:(h + 1) * tq]
    p_hi = psum.astype(BF16)
    p_lo = (psum - p_hi.astype(F32)).astype(BF16)
    mt = mt_ref[0:n_blk, 0:n_cmp]
    imp = (jnp.dot(mt, p_hi, preferred_element_type=F32)
           + jnp.dot(mt, p_lo, preferred_element_type=F32))
    j_idx = lax.broadcasted_iota(jnp.int32, imp.shape, 0)
    j_f = j_idx.astype(F32)
    q_blk = (q0 + lax.broadcasted_iota(jnp.int32, imp.shape, 1)) // NSA_SEL_BLOCK
    forced = (j_idx == 0) | (j_idx == q_blk) | (j_idx == q_blk - 1)
    val = jnp.where(forced, -jnp.inf, jnp.where(j_idx <= q_blk, imp, -NSA_FORCE))
    sel_bias = jnp.where(forced, 0.0, NEG_INF)
    for _ in range(max(n_top - 3, 0)):
        mx = jnp.max(val, axis=0, keepdims=True)
        first = jnp.min(jnp.where(val == mx, j_f, float(n_blk)), axis=0, keepdims=True)
        hit = j_f == first
        sel_bias = jnp.where(hit, 0.0, sel_bias)
        val = jnp.where(hit, -jnp.inf, val)
    n_lane = mask_ref.shape[3]
    if n_blk < n_lane:
        sel_bias = jnp.concatenate([sel_bias, jnp.full((n_lane - n_blk, tq), NEG_INF, F32)], axis=0)
    mask_ref[0, 0] = sel_bias.T.astype(mask_ref.dtype)


def _nsa_cmp_win_kernel(q_ref, kc_ref, vc_ref, mt_ref, gt_ref, kw_ref, vw_ref, o_ref, mask_ref,
                        qm, acc_ref, *, tq, hpg, n_top, n_sel):
    qi = pl.program_id(2)
    q0 = qi * tq
    _stack_group_queries(q_ref, qm, tq)
    nc = kc_ref.shape[3]
    seq = n_sel * NSA_SEL_BLOCK
    cls = (q0 + tq - 1) // (seq // NSA_CMP_CLASSES)
    for c in range(NSA_CMP_CLASSES):
        @pl.when(cls == c)
        def _(c=c):
            _nsa_cmp_body(q0, qm, kc_ref, vc_ref, mt_ref, gt_ref, kw_ref, vw_ref, o_ref, mask_ref,
                          acc_ref, tq=tq, hpg=hpg,
                          n_top=min(n_top, (c + 1) * n_sel // NSA_CMP_CLASSES),
                          n_cmp=(c + 1) * nc // NSA_CMP_CLASSES,
                          n_blk=(c + 1) * n_sel // NSA_CMP_CLASSES)


def _nsa_sel_kernel(q_ref, k_ref, v_ref, mask_ref, gt_ref, ocw_ref, o_ref,
                    qm, *scratch, tq, tk, hpg):
    g = pl.program_id(1)
    qi = pl.program_id(2)
    q0 = qi * tq
    _stack_group_queries(q_ref, qm, tq)
    acc_ref, l_ref = scratch[-1], scratch[-2]
    blocks = tk // NSA_SEL_BLOCK

    def scores_full(t):
        first_blk = t * blocks
        window = pl.multiple_of((first_blk // LANES) * LANES, LANES)
        bias_win = mask_ref[0, 0, :, pl.ds(window, LANES)]
        for h in range(hpg):
            qm[h * tq:(h + 1) * tq, LANES:2 * LANES] = bias_win
        lane = lax.broadcasted_iota(jnp.int32, (tk, LANES), 1)
        blk = first_blk - window + lax.broadcasted_iota(jnp.int32, (tk, LANES), 0) // NSA_SEL_BLOCK
        onehot = jnp.where(lane == blk, 1.0, 0.0).astype(BF16)
        k = k_ref[pl.ds(pl.multiple_of(t * tk, tk), tk), :]
        kx = jnp.concatenate([k, onehot], axis=1)
        return lax.dot_general(kx, qm[...], NT_DIMS, preferred_element_type=F32)

    def scores_diag(t):
        s = scores_full(t)
        k_idx = t * tk + lax.broadcasted_iota(jnp.int32, s.shape, 0)
        q_idx = q0 + lax.broadcasted_iota(jnp.int32, s.shape, 1) % tq
        return jnp.where(k_idx <= q_idx, s, NEG_INF)

    def v_tile(t):
        return _own_group_values_with_ones(v_ref[pl.ds(pl.multiple_of(t * tk, tk), tk), :], g)

    _pipelined_attention(q0 // tk, tq // tk, scores_full, scores_diag, v_tile, scratch,
                         sum_in_v=True)

    own = pl.multiple_of(g * HEAD_DIM, HEAD_DIM)
    other = pl.multiple_of((1 - g) * HEAD_DIM, HEAD_DIM)
    inv_l = 1.0 / jnp.maximum(acc_ref[pl.ds(other, 1), :], 1e-30)
    ot = acc_ref[pl.ds(own, HEAD_DIM), :] * inv_l
    tiles = _gated_pairs_to_token_major(ot, gt_ref, 1, tq, hpg)
    for j, tile in enumerate(tiles):
        sl = slice(j * LANES, (j + 1) * LANES)
        total = tile + ocw_ref[:, sl].astype(F32)
        o_ref[:, sl] = total.astype(o_ref.dtype)


def _nsa_attention(proj, gates_t, cmp_kv, batch, seq, d_model, *, tq=128, tq_sel=512, tk=512):
    assert tq_sel % tk == 0 and seq % tq_sel == 0
    t = proj.shape[0]
    groups = NSA_KV_GROUPS
    hpg = d_model // HEAD_DIM // groups
    gw = hpg * HEAD_DIM
    nq = seq // tq
    nq_sel = seq // tq_sel
    nc = cmp_kv.shape[3]
    n_sel = seq // NSA_SEL_BLOCK
    n_top = min(NSA_TOP_N, n_sel)
    ng = gates_t.shape[0]
    qcb = d_model // LANES

    ci = jnp.arange(nc)[None, :] * NSA_CMP_STRIDE
    sj = jnp.arange(n_sel)[:, None] * NSA_SEL_BLOCK
    mt = ((ci < sj + NSA_SEL_BLOCK) & (ci + NSA_CMP_BLOCK > sj)
          & (jnp.arange(nc)[None, :] < nc - 1)).astype(BF16)

    q_spec = pl.BlockSpec((tq, gw), lambda b, g, i: (b * nq + i, g))
    gt_spec = pl.BlockSpec((ng, tq), lambda b, g, i: (0, b * nq + i))
    o_spec = pl.BlockSpec((tq, gw), lambda b, g, i: (b * nq + i, g))
    o_shape = jax.ShapeDtypeStruct((t, d_model), BF16)
    sem = ("arbitrary", "arbitrary", "arbitrary")
    qm_scr = pltpu.VMEM((hpg * tq, LANES), BF16)

    n_lane = max(n_sel, LANES)
    o_cmp_win, sel_mask = pl.pallas_call(
        functools.partial(_nsa_cmp_win_kernel, tq=tq, hpg=hpg, n_top=n_top, n_sel=n_sel),
        grid=(batch, groups, nq),
        in_specs=[
            q_spec,
            pl.BlockSpec((1, 1, 1, nc, LANES), lambda b, g, i: (b, 0, g, 0, 0)),
            pl.BlockSpec((1, 1, 1, nc, LANES), lambda b, g, i: (b, 1, g, 0, 0)),
            pl.BlockSpec((n_sel, nc), lambda b, g, i: (0, 0)),
            gt_spec,
            pl.BlockSpec((seq, LANES), lambda b, g, i: (b, qcb + 5 + g)),
            pl.BlockSpec((seq, LANES), lambda b, g, i: (b, qcb + 7)),
        ],
        out_specs=[o_spec, pl.BlockSpec((1, 1, tq, n_lane), lambda b, g, i: (b, g, i, 0))],
        out_shape=[o_shape, jax.ShapeDtypeStruct((batch, groups, seq, n_lane), BF16)],
        scratch_shapes=[qm_scr, pltpu.VMEM((LANES, hpg * tq), F32)],
        compiler_params=_cparams(*sem),
        name="nsa_cmp_win",
    )(proj, cmp_kv, cmp_kv, mt, gates_t, proj, proj)

    rows_sel = pl.BlockSpec((tq_sel, gw), lambda b, g, i: (b * nq_sel + i, g))
    return pl.pallas_call(
        functools.partial(_nsa_sel_kernel, tq=tq_sel, tk=tk, hpg=hpg),
        grid=(batch, groups, nq_sel),
        in_specs=[
            rows_sel,
            pl.BlockSpec((seq, LANES), lambda b, g, i: (b, qcb + 2 + g)),
            pl.BlockSpec((seq, LANES), lambda b, g, i: (b, qcb + 4)),
            pl.BlockSpec((1, 1, tq_sel, n_lane), lambda b, g, i: (b, g, i, 0)),
            pl.BlockSpec((ng, tq_sel), lambda b, g, i: (0, b * nq_sel + i)),
            rows_sel,
        ],
        out_specs=rows_sel,
        out_shape=o_shape,
        scratch_shapes=([pltpu.VMEM((hpg * tq_sel, 2 * LANES), BF16)]
                        + _attn_scratch(tk, hpg * tq_sel, LANES)),
        compiler_params=_cparams(*sem),
        name="nsa_sel",
    )(proj, proj, proj, sel_mask, gates_t, o_cmp_win)


def _diff_in_weights(w_in, d_model):
    scale = HEAD_DIM ** -0.5 * LOG2E
    wq = w_in[:, :d_model] * scale
    return jnp.concatenate([wq, w_in[:, d_model:]], axis=1).astype(BF16)


def _nsa_in_weights(w_in, d_model):
    g, hd = NSA_KV_GROUPS, HEAD_DIM
    kvw = g * hd
    scale = hd ** -0.5 * LOG2E
    q = w_in[:, :d_model] * scale
    off = d_model
    sec = [w_in[:, off + i * kvw: off + (i + 1) * kvw] for i in range(6)]
    dup = lambda w: jnp.concatenate([w[:, i * hd:(i + 1) * hd] for i in range(g) for _ in range(2)], axis=1)
    w_main = jnp.concatenate([q, sec[0], sec[1], dup(sec[2]), sec[3], dup(sec[4]), sec[5]], axis=1)
    w_gates = w_in[:, off + 6 * kvw:].T
    return w_main.astype(BF16), w_gates.astype(BF16)


def _router_weights(w_grp, b_grp, w_exp, b_exp):
    d = w_grp.shape[0]
    n_exp = w_exp.shape[0] * w_exp.shape[2]
    we = jnp.transpose(w_exp, (1, 0, 2)).reshape(d, n_exp)
    pad = LANES - n_exp - w_grp.shape[1]
    wr = jnp.concatenate([we, w_grp, jnp.zeros((d, pad), F32)], axis=1).astype(F32)
    wr_hi = wr.astype(BF16)
    wr_lo = (wr - wr_hi.astype(F32)).astype(BF16)
    br = jnp.concatenate([b_exp.reshape(-1), b_grp, jnp.zeros((pad,), F32)])[None, :].astype(F32)
    return jnp.concatenate([wr_hi, wr_lo], axis=1), br, n_exp


def kernel(x, c, positions, ada_w, ada_b, norm_g, final_g, diff_w_in, diff_w_out, diff_lambda,
           diff_subln_g, nsa_w_in, nsa_w_out, nsa_cmp_pos, nsa_cmp_w1, nsa_cmp_b1, nsa_cmp_w2,
           moe_w_group, moe_b_group, moe_w_expert, moe_b_expert, moe_w_gate, moe_w_up, moe_w_down):
    batch, seq, d = x.shape
    depth = ada_w.shape[0]
    x2d = x.reshape(batch * seq, d)
    tables = _rope_tables(positions)
    mod = _adaln_mod(c, ada_w, ada_b)
    w_gate, w_up, w_down = (w.astype(BF16) for w in (moe_w_gate, moe_w_up, moe_w_down))

    for i in range(depth):
        sh1, sc1, g1, sh2, sc2, g2 = jnp.split(mod[i], 6, axis=-1)
        j = i // N_MIXERS
        if i % N_MIXERS == 0:
            lambda_init = 0.8 - 0.6 * math.exp(-0.3 * i)
            qkv = _norm_proj(x2d, norm_g[i, 0], sc1, sh1, _diff_in_weights(diff_w_in[j], d), tables,
                             seq, tn=512, plain_tiles=range(2 * d // 512, 3 * d // 512))
            o = _diff_attention(qkv, diff_lambda[j], diff_subln_g[j], batch, seq, lambda_init)
            w_out = diff_w_out[j]
        else:
            w_main, w_gates = _nsa_in_weights(nsa_w_in[j], d)
            proj, gates_t = _norm_proj(x2d, norm_g[i, 0], sc1, sh1, w_main, tables, seq,
                                       tn=256, plain_tiles=(d // 256,), w_gates=w_gates)
            groups = NSA_KV_GROUPS
            kcvc = proj[:, d:d + 2 * groups * HEAD_DIM]
            ch = kcvc.reshape(batch, seq, 2, groups, HEAD_DIM).transpose(0, 2, 3, 1, 4)
            ch = ch.reshape(batch, 2, groups, seq // NSA_CMP_STRIDE, NSA_CMP_STRIDE * HEAD_DIM)
            cmp_kv = _nsa_compress(ch, nsa_cmp_pos[j], nsa_cmp_w1[j], nsa_cmp_b1[j], nsa_cmp_w2[j])
            o = _nsa_attention(proj, gates_t, cmp_kv, batch, seq, d)
            w_out = nsa_w_out[j]
        wr, br, n_exp = _router_weights(moe_w_group[i], moe_b_group[i], moe_w_expert[i], moe_b_expert[i])
        x2d, h2, comb, gid_t = _outproj_router(o, w_out.astype(BF16), x2d, g1, norm_g[i, 1], sc2, sh2,
                                               wr, br, seq, n_exp)
        x2d = _moe(h2, w_gate, w_up, w_down, i, comb, gid_t, x2d, g2, final_g, seq,
                   final_norm=(i == depth - 1))
    return x2d.reshape(batch, seq, d)
```

```python
import functools
import math

import jax
import jax.numpy as jnp
from jax import lax
from jax.experimental import pallas as pl
from jax.experimental.pallas import tpu as pltpu

F32 = jnp.float32
BF16 = jnp.bfloat16
HIGHEST = lax.Precision.HIGHEST

ROPE_THETA = 500000.0
ROT_DIM = 16
NORM_EPS = 1e-6
NEG_INF = -1e30
HEAD_DIM = 64
DA_V_DIM = 2 * HEAD_DIM
NSA_KV_GROUPS = 2
NSA_CMP_BLOCK = 32
NSA_CMP_STRIDE = 16
NSA_SEL_BLOCK = 64
NSA_TOP_N = 16
NSA_WINDOW = 512
NSA_FORCE = 1e4
MOE_GROUPS = 4
MOE_EXPERTS_PER_GROUP = 8
MOE_TOP_K = 2
N_MIXERS = 2
LOG2E = 1.4426950408889634

LANES = 128
PROJ_ROW_CHUNK = 256
VMEM_LIMIT_BYTES = 56 * 1024 * 1024

NT_DIMS = (((1,), (1,)), ((), ()))
TN_DIMS = (((0,), (0,)), ((), ()))


def _cparams(*sem):
    return pltpu.CompilerParams(dimension_semantics=sem, vmem_limit_bytes=VMEM_LIMIT_BYTES)


def _mod_kernel(c_ref, w_ref, b_ref, o_ref):
    c = c_ref[...]
    ca = c * jax.nn.sigmoid(c)
    o_ref[0] = jnp.dot(ca, w_ref[0], precision=HIGHEST, preferred_element_type=F32) + b_ref[0]


def _adaln_mod(c, ada_w, ada_b):
    depth, d, n = ada_w.shape
    b = c.shape[0]
    rows = 8
    tn = 1536
    c_pad = jnp.zeros((rows, d), F32).at[:b].set(c)
    out = pl.pallas_call(
        _mod_kernel,
        grid=(depth, n // tn),
        in_specs=[
            pl.BlockSpec((rows, d), lambda i, j: (0, 0)),
            pl.BlockSpec((1, d, tn), lambda i, j: (i, 0, j)),
            pl.BlockSpec((1, 1, tn), lambda i, j: (i, 0, j)),
        ],
        out_specs=pl.BlockSpec((1, rows, tn), lambda i, j: (i, 0, j)),
        out_shape=jax.ShapeDtypeStruct((depth, rows, n), F32),
        compiler_params=_cparams("arbitrary", "arbitrary"),
        name="adaln_mod",
    )(c_pad, ada_w, ada_b.reshape(depth, 1, n))
    return out[:, :b, :]


def _rope_tables(positions):
    half = ROT_DIM // 2
    inv_freq = ROPE_THETA ** (-jnp.arange(0, ROT_DIM, 2, dtype=F32) / ROT_DIM)
    ang = positions.astype(F32).reshape(-1)[:, None] * inv_freq
    cos, sin = jnp.cos(ang), jnp.sin(ang)
    t = ang.shape[0]
    rest = HEAD_DIM - ROT_DIM
    z8 = jnp.zeros((t, half), F32)
    cos_h = jnp.concatenate([cos, cos, jnp.ones((t, rest), F32)], axis=-1)
    sina_h = jnp.concatenate([-sin, z8, jnp.zeros((t, rest), F32)], axis=-1)
    sinb_h = jnp.concatenate([z8, sin, jnp.zeros((t, rest), F32)], axis=-1)
    rep = LANES // HEAD_DIM
    return tuple(jnp.tile(a, (1, rep)) for a in (cos_h, sina_h, sinb_h))


def _rms_modulate(x, g, sc, sh):
    ms = jnp.mean(x * x, axis=-1, keepdims=True)
    y = x * lax.rsqrt(ms + NORM_EPS)
    return (y * g) * (1.0 + sc) + sh


def _proj_kernel(*refs, plain_tiles, with_gates):
    if with_gates:
        (x_ref, g_ref, sc_ref, sh_ref, w_ref, cos_ref, sa_ref, sb_ref, wg_ref,
         o_ref, gt_ref, h_scr) = refs
    else:
        x_ref, g_ref, sc_ref, sh_ref, w_ref, cos_ref, sa_ref, sb_ref, o_ref, h_scr = refs
    j = pl.program_id(1)
    tn = o_ref.shape[1]

    @pl.when(j == 0)
    def _():
        h = _rms_modulate(x_ref[...], g_ref[...], sc_ref[0], sh_ref[0])
        h_scr[...] = h.astype(BF16)
        if with_gates:
            logits = lax.dot_general(wg_ref[...], h_scr[...], NT_DIMS, preferred_element_type=F32)
            gt_ref[...] = jax.nn.sigmoid(logits)

    is_plain = functools.reduce(jnp.logical_or, [j == t for t in plain_tiles], j < 0)
    for r in range(h_scr.shape[0] // PROJ_ROW_CHUNK):
        rows = slice(r * PROJ_ROW_CHUNK, (r + 1) * PROJ_ROW_CHUNK)
        acc = jnp.dot(h_scr[rows, :], w_ref[...], preferred_element_type=F32)
        cos = jnp.where(is_plain, 1.0, cos_ref[rows, :])
        sa = jnp.where(is_plain, 0.0, sa_ref[rows, :])
        sb = jnp.where(is_plain, 0.0, sb_ref[rows, :])
        shift_up = pltpu.roll(acc, tn - ROT_DIM // 2, 1)
        shift_dn = pltpu.roll(acc, ROT_DIM // 2, 1)
        for c in range(tn // LANES):
            sl = slice(c * LANES, (c + 1) * LANES)
            y = acc[:, sl] * cos + shift_up[:, sl] * sa + shift_dn[:, sl] * sb
            o_ref[rows, sl] = y.astype(o_ref.dtype)


def _norm_proj(x2d, g, sc, sh, w, tables, seq, *, tn, plain_tiles, w_gates=None, tm=1024):
    t, d = x2d.shape
    n = w.shape[1]
    with_gates = w_gates is not None
    per_b = seq // tm
    in_specs = [
        pl.BlockSpec((tm, d), lambda i, j: (i, 0)),
        pl.BlockSpec((1, d), lambda i, j: (0, 0)),
        pl.BlockSpec((1, 1, d), lambda i, j: (i // per_b, 0, 0)),
        pl.BlockSpec((1, 1, d), lambda i, j: (i // per_b, 0, 0)),
        pl.BlockSpec((d, tn), lambda i, j: (0, j)),
        pl.BlockSpec((tm, LANES), lambda i, j: (i, 0)),
        pl.BlockSpec((tm, LANES), lambda i, j: (i, 0)),
        pl.BlockSpec((tm, LANES), lambda i, j: (i, 0)),
    ]
    args = [x2d, g.reshape(1, d), sc[:, None, :], sh[:, None, :], w, *tables]
    out_specs = [pl.BlockSpec((tm, tn), lambda i, j: (i, j))]
    out_shape = [jax.ShapeDtypeStruct((t, n), BF16)]
    if with_gates:
        ng = w_gates.shape[0]
        in_specs.append(pl.BlockSpec((ng, d), lambda i, j: (0, 0)))
        args.append(w_gates)
        out_specs.append(pl.BlockSpec((ng, tm), lambda i, j: (0, i)))
        out_shape.append(jax.ShapeDtypeStruct((ng, t), F32))
    res = pl.pallas_call(
        functools.partial(_proj_kernel, plain_tiles=tuple(plain_tiles), with_gates=with_gates),
        grid=(t // tm, n // tn),
        in_specs=in_specs,
        out_specs=out_specs,
        out_shape=out_shape,
        scratch_shapes=[pltpu.VMEM((tm, d), BF16)],
        compiler_params=_cparams("arbitrary", "arbitrary"),
        name="norm_proj",
    )(*args)
    return res if with_gates else res[0]


def _attn_scratch(tk, nq, dv):
    return [
        pltpu.VMEM((2, tk, nq), F32),
        pltpu.VMEM((2, 1, nq), F32),
        pltpu.VMEM((2, tk, nq), BF16),
        pltpu.VMEM((2, 1, nq), F32),
        pltpu.VMEM((1, nq), F32),
        pltpu.VMEM((1, nq), F32),
        pltpu.VMEM((dv, nq), F32),
    ]


def _pipelined_attention(n_full, n_diag, scores_full, scores_diag, v_tile, scratch, sum_in_v=False):
    s_scr, mx_scr, p_scr, al_scr, m_ref, l_ref, acc_ref = scratch

    def stage_scores(fn, tile, slot):
        s = fn(tile)
        s_scr[slot] = s
        mx_scr[slot] = jnp.max(s, axis=0, keepdims=True)

    def stage_softmax(slot):
        m_prev = m_ref[...]
        m_new = jnp.maximum(m_prev, mx_scr[slot])
        alpha = jnp.exp2(m_prev - m_new)
        p = jnp.exp2(s_scr[slot] - m_new)
        if not sum_in_v:
            l_ref[...] = alpha * l_ref[...] + jnp.sum(p, axis=0, keepdims=True)
        m_ref[...] = m_new
        al_scr[slot] = alpha
        p_scr[slot] = p.astype(BF16)

    def stage_pv(pos, slot):
        tile = jnp.where(pos < n_diag, n_full + jnp.maximum(pos, 0), pos - n_diag)
        pv = lax.dot_general(v_tile(tile), p_scr[slot], TN_DIMS, preferred_element_type=F32)
        acc_ref[...] = acc_ref[...] * al_scr[slot] + pv

    def step(fn, pos, slot):
        nxt = pos + 1
        stage_scores(fn, jnp.where(nxt < n_diag, n_full + nxt, nxt - n_diag), 1 - slot)
        stage_softmax(slot)
        stage_pv(pos - 1, 1 - slot)

    m_ref[...] = jnp.full(m_ref.shape, NEG_INF, F32)
    l_ref[...] = jnp.zeros(l_ref.shape, F32)
    acc_ref[...] = jnp.zeros(acc_ref.shape, F32)
    p_scr[1] = jnp.zeros(p_scr.shape[1:], BF16)
    al_scr[1] = jnp.ones(al_scr.shape[1:], F32)
    stage_scores(scores_diag, n_full, 0)
    for p in range(n_diag - 1):
        step(scores_diag, p, p % 2)

    first = n_diag - 1
    s0 = first % 2

    def pair(jj, carry):
        step(scores_full, first + 2 * jj, s0)
        step(scores_full, first + 2 * jj + 1, 1 - s0)
        return carry

    lax.fori_loop(0, n_full // 2, pair, 0)
    odd = n_full % 2 == 1
    last = first + n_full

    @pl.when(odd)
    def _():
        step(scores_full, last - 1, s0)
        stage_softmax(1 - s0)
        stage_pv(last - 1, s0)
        stage_pv(last, 1 - s0)

    @pl.when(jnp.logical_not(odd))
    def _():
        stage_softmax(s0)
        stage_pv(last - 1, 1 - s0)
        stage_pv(last, s0)


def _split_halves(q, rows):
    lane = lax.broadcasted_iota(jnp.int32, q.shape, 1)
    zero = jnp.zeros_like(q)
    return jnp.where(lane < HEAD_DIM, q, zero), jnp.where(lane >= HEAD_DIM, q, zero)


def _diff_attn_kernel(q_ref, k_ref, v_ref, lam_ref, g_ref, o_ref, qm, *scratch,
                      tq, tk, lambda_init):
    qi = pl.program_id(2)
    lo, hi = _split_halves(q_ref[...], tq)
    qm[0:tq, :] = lo
    qm[tq:2 * tq, :] = hi
    q0 = qi * tq
    acc_ref, l_ref = scratch[-1], scratch[-2]

    def scores_full(t):
        k = k_ref[pl.ds(pl.multiple_of(t * tk, tk), tk), :]
        return lax.dot_general(k, qm[...], NT_DIMS, preferred_element_type=F32)

    def scores_diag(t):
        s = scores_full(t)
        k_idx = t * tk + lax.broadcasted_iota(jnp.int32, s.shape, 0)
        q_idx = q0 + lax.broadcasted_iota(jnp.int32, s.shape, 1) % tq
        return jnp.where(k_idx <= q_idx, s, NEG_INF)

    def v_tile(t):
        v = v_ref[pl.ds(pl.multiple_of(t * tk, tk), tk), :]
        return jnp.concatenate([v, jnp.ones((tk, 16), BF16)], axis=1)

    _pipelined_attention(q0 // tk, tq // tk, scores_full, scores_diag, v_tile, scratch, sum_in_v=True)

    lam = lam_ref[...]
    lam_full = (jnp.exp(jnp.sum(lam[0:1] * lam[1:2], axis=-1, keepdims=True))
                - jnp.exp(jnp.sum(lam[2:3] * lam[3:4], axis=-1, keepdims=True)) + lambda_init)
    acc = acc_ref[0:DA_V_DIM, :]
    l = jnp.maximum(acc_ref[DA_V_DIM:DA_V_DIM + 1, :], 1e-30)
    ot = acc[:, 0:tq] / l[:, 0:tq] - lam_full * (acc[:, tq:2 * tq] / l[:, tq:2 * tq])
    ms = jnp.mean(ot * ot, axis=0, keepdims=True)
    ot = (ot * lax.rsqrt(ms + NORM_EPS)) * g_ref[...] * (1.0 - lambda_init)
    o_ref[...] = ot.T.astype(o_ref.dtype)


def _diff_attention(qkv, lam, subln_g, batch, seq, lambda_init, *, tq=1024, tk=1024):
    assert tq % tk == 0 and seq % tq == 0
    t, n3 = qkv.shape
    d = n3 // 3
    heads = d // DA_V_DIM
    nq = seq // tq
    cb = d // LANES
    g_b = jnp.broadcast_to(subln_g.astype(F32)[:, None], (DA_V_DIM, tq))
    return pl.pallas_call(
        functools.partial(_diff_attn_kernel, tq=tq, tk=tk, lambda_init=lambda_init),
        grid=(batch, heads, nq),
        in_specs=[
            pl.BlockSpec((tq, LANES), lambda b, h, i: (b * nq + i, h)),
            pl.BlockSpec((seq, LANES), lambda b, h, i: (b, cb + h)),
            pl.BlockSpec((seq, LANES), lambda b, h, i: (b, 2 * cb + h)),
            pl.BlockSpec(lam.shape, lambda b, h, i: (0, 0)),
            pl.BlockSpec((DA_V_DIM, tq), lambda b, h, i: (0, 0)),
        ],
        out_specs=pl.BlockSpec((tq, LANES), lambda b, h, i: (b * nq + i, h)),
        out_shape=jax.ShapeDtypeStruct((t, d), BF16),
        scratch_shapes=[pltpu.VMEM((2 * tq, LANES), BF16)] + _attn_scratch(tk, 2 * tq, DA_V_DIM + 16),
        compiler_params=_cparams("arbitrary", "arbitrary", "arbitrary"),
        name="diff_attn",
    )(qkv, qkv, qkv, lam.astype(F32), g_b)


def _route(logits, n_exp):
    lane = lax.broadcasted_iota(jnp.int32, logits.shape, 1)
    lane_f = lane.astype(F32)
    big = float(LANES)
    is_grp = (lane >= n_exp) & (lane < n_exp + MOE_GROUPS)
    gl = jnp.where(is_grp, logits, -jnp.inf)
    ge = jnp.where(is_grp, jnp.exp(gl - jnp.max(gl, axis=-1, keepdims=True)), 0.0)
    pg = ge / jnp.sum(ge, axis=-1, keepdims=True)
    p_top = jnp.max(pg, axis=-1, keepdims=True)
    grp = jnp.min(jnp.where(is_grp & (pg == p_top), lane_f, big), axis=-1, keepdims=True) - n_exp
    in_grp = (lane < n_exp) & ((lane // MOE_EXPERTS_PER_GROUP).astype(F32) == grp)
    sel = jnp.where(in_grp, logits, -jnp.inf)
    v1 = jnp.max(sel, axis=-1, keepdims=True)
    i1 = jnp.min(jnp.where(sel == v1, lane_f, big), axis=-1, keepdims=True)
    sel2 = jnp.where(lane_f == i1, -jnp.inf, sel)
    v2 = jnp.max(sel2, axis=-1, keepdims=True)
    i2 = jnp.min(jnp.where(sel2 == v2, lane_f, big), axis=-1, keepdims=True)
    e2 = jnp.exp(v2 - v1)
    den = 1.0 + e2
    w1 = (1.0 / den) * p_top
    w2 = (e2 / den) * p_top
    return jnp.where(lane_f == i1, w1, 0.0) + jnp.where(lane_f == i2, w2, 0.0), grp


def _outproj_router_kernel(o_ref, w_ref, x_ref, g1_ref, ng_ref, sc_ref, sh_ref, wr_ref, br_ref,
                           xo_ref, h_ref, comb_ref, gid_ref, *, n_exp):
    mix = jnp.dot(o_ref[...], w_ref[...], preferred_element_type=F32)
    x = x_ref[...] + g1_ref[0] * mix
    xo_ref[...] = x
    h = _rms_modulate(x, ng_ref[...], sc_ref[0], sh_ref[0])
    h_hi = h.astype(BF16)
    h_ref[...] = h_hi
    h_lo = (h - h_hi.astype(F32)).astype(BF16)
    wr = wr_ref[...]
    hw = jnp.dot(h_hi, wr, preferred_element_type=F32)
    logits = (hw[:, 0:LANES] + hw[:, LANES:2 * LANES]
              + jnp.dot(h_lo, wr[:, 0:LANES], preferred_element_type=F32) + br_ref[...])
    comb, grp = _route(logits, n_exp)
    comb_ref[...] = comb[:, 0:n_exp]
    gid_ref[...] = jnp.broadcast_to(grp, logits.shape).T[0:gid_ref.shape[0], :]


def _outproj_router(o, w_out, x2d, g1, ng, sc, sh, wr, br, seq, n_exp, *, tm=1024):
    t, d = x2d.shape
    per_b = seq // tm
    bvec = lambda i: (i // per_b, 0, 0)
    return pl.pallas_call(
        functools.partial(_outproj_router_kernel, n_exp=n_exp),
        grid=(t // tm,),
        in_specs=[
            pl.BlockSpec((tm, d), lambda i: (i, 0)),
            pl.BlockSpec((d, d), lambda i: (0, 0)),
            pl.BlockSpec((tm, d), lambda i: (i, 0)),
            pl.BlockSpec((1, 1, d), bvec),
            pl.BlockSpec((1, d), lambda i: (0, 0)),
            pl.BlockSpec((1, 1, d), bvec),
            pl.BlockSpec((1, 1, d), bvec),
            pl.BlockSpec((d, 2 * LANES), lambda i: (0, 0)),
            pl.BlockSpec((1, LANES), lambda i: (0, 0)),
        ],
        out_specs=[
            pl.BlockSpec((tm, d), lambda i: (i, 0)),
            pl.BlockSpec((tm, d), lambda i: (i, 0)),
            pl.BlockSpec((tm, n_exp), lambda i: (i, 0)),
            pl.BlockSpec((8, tm), lambda i: (0, i)),
        ],
        out_shape=[
            jax.ShapeDtypeStruct((t, d), F32),
            jax.ShapeDtypeStruct((t, d), BF16),
            jax.ShapeDtypeStruct((t, n_exp), F32),
            jax.ShapeDtypeStruct((8, t), F32),
        ],
        compiler_params=_cparams("arbitrary"),
        name="outproj_router",
    )(o, w_out, x2d, g1[:, None, :], ng.reshape(1, d), sc[:, None, :], sh[:, None, :], wr, br)


MOE_CHUNK = 128
MOE_EXPERTS_PER_STEP = 4


def _moe_kernel(h_ref, wg_ref, wu_ref, wd_ref, comb_ref, gid_ref, tri_ref, x_ref, g2_ref, fg_ref,
                o_ref, perm_scr, hs_scr, cs_scr, ys_scr, off_smem, nch_smem, *, final_norm):
    step = pl.program_id(1)
    tm = h_ref.shape[0]
    rows = perm_scr.shape[0]
    n_exp = comb_ref.shape[1]
    steps_per_group = MOE_EXPERTS_PER_GROUP // MOE_EXPERTS_PER_STEP

    @pl.when(step == 0)
    def _():
        gid = gid_ref[0:1, :]
        grow = lax.broadcasted_iota(jnp.int32, (8, tm), 0).astype(F32)
        onehot = jnp.where(grow == gid, 1.0, 0.0)
        rank = jnp.dot(onehot.astype(BF16), tri_ref[...], preferred_element_type=F32)
        off = jnp.int32(0)
        off_col = jnp.zeros((8, 1), F32)
        grow_col = lax.broadcasted_iota(jnp.int32, (8, 1), 0)
        for g in range(MOE_GROUPS):
            cnt = jnp.sum(onehot[g:g + 1, :]).astype(jnp.int32)
            nch = (cnt + (MOE_CHUNK - 1)) // MOE_CHUNK
            off_smem[g] = off
            nch_smem[g] = nch
            off_col = jnp.where(grow_col == g, off.astype(F32), off_col)
            off = off + nch * MOE_CHUNK
        dest = jnp.sum(onehot * (rank + off_col), axis=0, keepdims=True)
        row_id = lax.broadcasted_iota(jnp.int32, (rows, tm), 0)
        perm = jnp.where(row_id == dest.astype(jnp.int32), 1.0, 0.0).astype(BF16)
        perm_scr[...] = perm
        hs_scr[...] = jnp.dot(perm, h_ref[...], preferred_element_type=F32).astype(BF16)
        comb = comb_ref[...]
        comb_hi = comb.astype(BF16)
        comb_lo = (comb - comb_hi.astype(F32)).astype(BF16)
        cs_scr[...] = (jnp.dot(perm, comb_hi, preferred_element_type=F32)
                       + jnp.dot(perm, comb_lo, preferred_element_type=F32))
        ys_scr[...] = jnp.zeros(ys_scr.shape, F32)

    g = step // steps_per_group
    off = off_smem[g]
    nch = nch_smem[g]

    def run_experts(r0, rows_n):
        lane = lax.broadcasted_iota(jnp.int32, (rows_n, n_exp), 1)
        xs = hs_scr[pl.ds(r0, rows_n), :]
        cs = cs_scr[pl.ds(r0, rows_n), :]
        y = jnp.zeros((rows_n, xs.shape[1]), F32)
        for k in range(MOE_EXPERTS_PER_STEP):
            gate = jnp.dot(xs, wg_ref[k], preferred_element_type=F32)
            up = jnp.dot(xs, wu_ref[k], preferred_element_type=F32)
            a = (gate * jax.nn.sigmoid(gate)) * up
            e = step * MOE_EXPERTS_PER_STEP + k
            ce = jnp.sum(jnp.where(lane == e, cs, 0.0), axis=-1, keepdims=True)
            y = y + jnp.dot((a * ce).astype(BF16), wd_ref[k], preferred_element_type=F32)
        ys_scr[pl.ds(r0, rows_n), :] += y

    def chunk_pair(c, carry):
        run_experts(pl.multiple_of(off + c * (2 * MOE_CHUNK), MOE_CHUNK), 2 * MOE_CHUNK)
        return carry

    lax.fori_loop(0, nch // 2, chunk_pair, 0)

    @pl.when(nch % 2 == 1)
    def _():
        run_experts(pl.multiple_of(off + (nch - 1) * MOE_CHUNK, MOE_CHUNK), MOE_CHUNK)

    @pl.when(step == pl.num_programs(1) - 1)
    def _():
        y = lax.dot_general(perm_scr[...], ys_scr[...].astype(BF16), TN_DIMS,
                            preferred_element_type=F32)
        x = x_ref[...] + g2_ref[0] * y
        if final_norm:
            ms = jnp.mean(x * x, axis=-1, keepdims=True)
            x = (x * lax.rsqrt(ms + NORM_EPS)) * fg_ref[...]
        o_ref[...] = x


def _moe(h, w_gate, w_up, w_down, layer, comb, gid_t, x2d, g2, final_g, seq, *, final_norm, tm=1024):
    t, d = x2d.shape
    _, n_exp, _, hidden = w_gate.shape
    per_b = seq // tm
    eps = MOE_EXPERTS_PER_STEP
    rows = tm + MOE_GROUPS * MOE_CHUNK
    idx = jnp.arange(tm)
    tri = (idx[:, None] < idx[None, :]).astype(BF16)
    return pl.pallas_call(
        functools.partial(_moe_kernel, final_norm=final_norm),
        grid=(t // tm, n_exp // eps),
        in_specs=[
            pl.BlockSpec((tm, d), lambda i, s: (i, 0)),
            pl.BlockSpec((None, eps, d, hidden), lambda i, s: (layer, s, 0, 0)),
            pl.BlockSpec((None, eps, d, hidden), lambda i, s: (layer, s, 0, 0)),
            pl.BlockSpec((None, eps, hidden, d), lambda i, s: (layer, s, 0, 0)),
            pl.BlockSpec((tm, n_exp), lambda i, s: (i, 0)),
            pl.BlockSpec((8, tm), lambda i, s: (0, i)),
            pl.BlockSpec((tm, tm), lambda i, s: (0, 0)),
            pl.BlockSpec((tm, d), lambda i, s: (i, 0)),
            pl.BlockSpec((1, 1, d), lambda i, s: (i // per_b, 0, 0)),
            pl.BlockSpec((1, d), lambda i, s: (0, 0)),
        ],
        out_specs=pl.BlockSpec((tm, d), lambda i, s: (i, 0)),
        out_shape=jax.ShapeDtypeStruct((t, d), F32),
        scratch_shapes=[
            pltpu.VMEM((rows, tm), BF16),
            pltpu.VMEM((rows, d), BF16),
            pltpu.VMEM((rows, n_exp), F32),
            pltpu.VMEM((rows, d), F32),
            pltpu.SMEM((MOE_GROUPS,), jnp.int32),
            pltpu.SMEM((MOE_GROUPS,), jnp.int32),
        ],
        compiler_params=_cparams("arbitrary", "arbitrary"),
        name="moe",
    )(h, w_gate, w_up, w_down, comb, gid_t, tri, x2d, g2[:, None, :], final_g.reshape(1, d))


def _compress_kernel(ch_ref, w1a_ref, w1b_ref, pos_ref, w1f_ref, b1_ref, w2_ref, o_ref):
    ch = ch_ref[0, 0, 0]
    n = ch.shape[0]
    a = jnp.dot(ch, w1a_ref[0], preferred_element_type=F32)
    b = jnp.dot(ch, w1b_ref[0], preferred_element_type=F32)
    const = jnp.dot(pos_ref[0], w1f_ref[0], precision=HIGHEST, preferred_element_type=F32)[0:1]
    hid = a + pltpu.roll(b, n - 1, 0) + (const + b1_ref[0])
    act = jax.nn.gelu(hid, approximate=True)
    o_ref[0, 0, 0] = jnp.dot(act.astype(BF16), w2_ref[0], preferred_element_type=F32).astype(o_ref.dtype)


def _nsa_compress(ch, cmp_pos, cmp_w1, cmp_b1, cmp_w2):
    b, two, g, nc, f = ch.shape
    hid = cmp_w1.shape[-1]
    w1a = cmp_w1[:, :f, :].astype(BF16)
    w1b = cmp_w1[:, f:, :].astype(BF16)
    pos = jnp.zeros((two, 8, 2 * f), F32).at[:, 0, :].set(cmp_pos.reshape(two, 2 * f))
    w2 = jnp.concatenate([cmp_w2, cmp_w2], axis=-1).astype(BF16)
    return pl.pallas_call(
        _compress_kernel,
        grid=(b, two, g),
        in_specs=[
            pl.BlockSpec((1, 1, 1, nc, f), lambda i, j, k: (i, j, k, 0, 0)),
            pl.BlockSpec((1, f, hid), lambda i, j, k: (j, 0, 0)),
            pl.BlockSpec((1, f, hid), lambda i, j, k: (j, 0, 0)),
            pl.BlockSpec((1, 8, 2 * f), lambda i, j, k: (j, 0, 0)),
            pl.BlockSpec((1, 2 * f, hid), lambda i, j, k: (j, 0, 0)),
            pl.BlockSpec((1, 1, hid), lambda i, j, k: (j, 0, 0)),
            pl.BlockSpec((1, hid, LANES), lambda i, j, k: (j, 0, 0)),
        ],
        out_specs=pl.BlockSpec((1, 1, 1, nc, LANES), lambda i, j, k: (i, j, k, 0, 0)),
        out_shape=jax.ShapeDtypeStruct((b, two, g, nc, LANES), BF16),
        compiler_params=_cparams("arbitrary", "arbitrary", "arbitrary"),
        name="nsa_compress",
    )(ch, w1a, w1b, pos, cmp_w1.astype(F32), cmp_b1[:, None, :].astype(F32), w2)


def _stack_group_queries(q_ref, qm, tq):
    for j in range(q_ref.shape[1] // LANES):
        lo, hi = _split_halves(q_ref[:, j * LANES:(j + 1) * LANES], tq)
        qm[(2 * j) * tq:(2 * j + 1) * tq, 0:LANES] = lo
        qm[(2 * j + 1) * tq:(2 * j + 2) * tq, 0:LANES] = hi


def _own_group_values_with_ones(v, g):
    lane = lax.broadcasted_iota(jnp.int32, v.shape, 1)
    own_half = (lane >= HEAD_DIM) == (g == 1)
    return jnp.where(own_half, v, jnp.ones_like(v))


def _gated_pairs_to_token_major(ot, gt_ref, branch, tq, hpg):
    g = pl.program_id(1)
    tiles = []
    for j in range(hpg // 2):
        parts = []
        for h in (2 * j, 2 * j + 1):
            row = gt_ref[pl.ds((g * hpg + h) * 3 + branch, 1), :]
            parts.append(ot[:, h * tq:(h + 1) * tq] * row)
        tiles.append(jnp.concatenate(parts, axis=0).T)
    return tiles


NSA_CMP_CLASSES = 4


def _nsa_window_tiles(q0, qm, kw_ref, vw_ref, gt_ref, acc_ref, *, tq, hpg):
    g = pl.program_id(1)
    span = NSA_WINDOW + tq
    start = pl.multiple_of(jnp.maximum(q0 - NSA_WINDOW, 0), tq)
    k = kw_ref[pl.ds(start, span), :]
    v = vw_ref[pl.ds(start, span), :]
    s = lax.dot_general(k, qm[...], NT_DIMS, preferred_element_type=F32)
    k_idx = start + lax.broadcasted_iota(jnp.int32, (span, tq), 0)
    q_idx = q0 + lax.broadcasted_iota(jnp.int32, (span, tq), 1)
    dist = q_idx - k_idx
    bias = jnp.where(dist >= 0, jnp.where(dist < NSA_WINDOW, 0.0, NEG_INF), NEG_INF)
    s = s + jnp.concatenate([bias] * hpg, axis=1)
    m = jnp.max(s, axis=0, keepdims=True)
    e = jnp.exp2(s - m)
    r = 1.0 / jnp.maximum(jnp.sum(e, axis=0, keepdims=True), 1e-30)
    acc_ref[...] = lax.dot_general(v, e.astype(BF16), TN_DIMS, preferred_element_type=F32) * r
    ot = acc_ref[pl.ds(pl.multiple_of(g * HEAD_DIM, HEAD_DIM), HEAD_DIM), :]
    return _gated_pairs_to_token_major(ot, gt_ref, 2, tq, hpg)


def _nsa_cmp_body(q0, qm, kc_ref, vc_ref, mt_ref, gt_ref, kw_ref, vw_ref, o_ref, mask_ref, acc_ref,
                  *, tq, hpg, n_top, n_cmp, n_blk):
    win_tiles = _nsa_window_tiles(q0, qm, kw_ref, vw_ref, gt_ref, acc_ref, tq=tq, hpg=hpg)
    kc = kc_ref[0, 0, 0, 0:n_cmp, :]
    s = lax.dot_general(kc, qm[...], NT_DIMS, preferred_element_type=F32)
    n_idx = lax.broadcasted_iota(jnp.int32, (n_cmp, tq), 0)
    q_idx = q0 + lax.broadcasted_iota(jnp.int32, (n_cmp, tq), 1)
    valid = n_idx * NSA_CMP_STRIDE + (NSA_CMP_BLOCK - 1) <= q_idx
    bias = jnp.where(valid, 0.0, NEG_INF)
    s = s + jnp.concatenate([bias] * hpg, axis=1)
    m = jnp.maximum(jnp.max(s, axis=0, keepdims=True), 0.1 * NEG_INF)
    e = jnp.exp2(s - m)
    r = 1.0 / jnp.maximum(jnp.sum(e, axis=0, keepdims=True), 1e-30)
    p = e * r
    ot = lax.dot_general(vc_ref[0, 0, 0, 0:n_cmp, :], p.astype(BF16), TN_DIMS,
                         preferred_element_type=F32)
    tiles = _gated_pairs_to_token_major(ot[0:HEAD_DIM], gt_ref, 0, tq, hpg)
    for j, (tile, win) in enumerate(zip(tiles, win_tiles)):
        o_ref[:, j * LANES:(j + 1) * LANES] = (tile + win).astype(o_ref.dtype)

    psum = p[:, 0:tq]
    for h in range(1, hpg):
        psum = psum + p[:, h * tq:(h + 1) * tq]
    p_hi = psum.astype(BF16)
    p_lo = (psum - p_hi.astype(F32)).astype(BF16)
    mt = mt_ref[0:n_blk, 0:n_cmp]
    imp = (jnp.dot(mt, p_hi, preferred_element_type=F32)
           + jnp.dot(mt, p_lo, preferred_element_type=F32))
    j_idx = lax.broadcasted_iota(jnp.int32, imp.shape, 0)
    j_f = j_idx.astype(F32)
    q_blk = (q0 + lax.broadcasted_iota(jnp.int32, imp.shape, 1)) // NSA_SEL_BLOCK
    forced = (j_idx == 0) | (j_idx == q_blk) | (j_idx == q_blk - 1)
    val = jnp.where(forced, -jnp.inf, jnp.where(j_idx <= q_blk, imp, -NSA_FORCE))
    sel_bias = jnp.where(forced, 0.0, NEG_INF)
    for _ in range(max(n_top - 3, 0)):
        mx = jnp.max(val, axis=0, keepdims=True)
        first = jnp.min(jnp.where(val == mx, j_f, float(n_blk)), axis=0, keepdims=True)
        hit = j_f == first
        sel_bias = jnp.where(hit, 0.0, sel_bias)
        val = jnp.where(hit, -jnp.inf, val)
    n_lane = mask_ref.shape[3]
    if n_blk < n_lane:
        sel_bias = jnp.concatenate([sel_bias, jnp.full((n_lane - n_blk, tq), NEG_INF, F32)], axis=0)
    mask_ref[0, 0] = sel_bias.T.astype(mask_ref.dtype)


def _nsa_cmp_win_kernel(q_ref, kc_ref, vc_ref, mt_ref, gt_ref, kw_ref, vw_ref, o_ref, mask_ref,
                        qm, acc_ref, *, tq, hpg, n_top, n_sel):
    qi = pl.program_id(2)
    q0 = qi * tq
    _stack_group_queries(q_ref, qm, tq)
    nc = kc_ref.shape[3]
    seq = n_sel * NSA_SEL_BLOCK
    cls = (q0 + tq - 1) // (seq // NSA_CMP_CLASSES)
    for c in range(NSA_CMP_CLASSES):
        @pl.when(cls == c)
        def _(c=c):
            _nsa_cmp_body(q0, qm, kc_ref, vc_ref, mt_ref, gt_ref, kw_ref, vw_ref, o_ref, mask_ref,
                          acc_ref, tq=tq, hpg=hpg,
                          n_top=min(n_top, (c + 1) * n_sel // NSA_CMP_CLASSES),
                          n_cmp=(c + 1) * nc // NSA_CMP_CLASSES,
                          n_blk=(c + 1) * n_sel // NSA_CMP_CLASSES)


def _nsa_sel_kernel(q_ref, k_ref, v_ref, mask_ref, gt_ref, ocw_ref, o_ref,
                    qm, *scratch, tq, tk, hpg):
    g = pl.program_id(1)
    qi = pl.program_id(2)
    q0 = qi * tq
    _stack_group_queries(q_ref, qm, tq)
    acc_ref, l_ref = scratch[-1], scratch[-2]
    blocks = tk // NSA_SEL_BLOCK

    def scores_full(t):
        first_blk = t * blocks
        window = pl.multiple_of((first_blk // LANES) * LANES, LANES)
        bias_win = mask_ref[0, 0, :, pl.ds(window, LANES)]
        for h in range(hpg):
            qm[h * tq:(h + 1) * tq, LANES:2 * LANES] = bias_win
        lane = lax.broadcasted_iota(jnp.int32, (tk, LANES), 1)
        blk = first_blk - window + lax.broadcasted_iota(jnp.int32, (tk, LANES), 0) // NSA_SEL_BLOCK
        onehot = jnp.where(lane == blk, 1.0, 0.0).astype(BF16)
        k = k_ref[pl.ds(pl.multiple_of(t * tk, tk), tk), :]
        kx = jnp.concatenate([k, onehot], axis=1)
        return lax.dot_general(kx, qm[...], NT_DIMS, preferred_element_type=F32)

    def scores_diag(t):
        s = scores_full(t)
        k_idx = t * tk + lax.broadcasted_iota(jnp.int32, s.shape, 0)
        q_idx = q0 + lax.broadcasted_iota(jnp.int32, s.shape, 1) % tq
        return jnp.where(k_idx <= q_idx, s, NEG_INF)

    def v_tile(t):
        return _own_group_values_with_ones(v_ref[pl.ds(pl.multiple_of(t * tk, tk), tk), :], g)

    _pipelined_attention(q0 // tk, tq // tk, scores_full, scores_diag, v_tile, scratch,
                         sum_in_v=True)

    own = pl.multiple_of(g * HEAD_DIM, HEAD_DIM)
    other = pl.multiple_of((1 - g) * HEAD_DIM, HEAD_DIM)
    inv_l = 1.0 / jnp.maximum(acc_ref[pl.ds(other, 1), :], 1e-30)
    ot = acc_ref[pl.ds(own, HEAD_DIM), :] * inv_l
    tiles = _gated_pairs_to_token_major(ot, gt_ref, 1, tq, hpg)
    for j, tile in enumerate(tiles):
        sl = slice(j * LANES, (j + 1) * LANES)
        total = tile + ocw_ref[:, sl].astype(F32)
        o_ref[:, sl] = total.astype(o_ref.dtype)


def _nsa_attention(proj, gates_t, cmp_kv, batch, seq, d_model, *, tq=128, tq_sel=512, tk=512):
    assert tq_sel % tk == 0 and seq % tq_sel == 0
    t = proj.shape[0]
    groups = NSA_KV_GROUPS
    hpg = d_model // HEAD_DIM // groups
    gw = hpg * HEAD_DIM
    nq = seq // tq
    nq_sel = seq // tq_sel
    nc = cmp_kv.shape[3]
    n_sel = seq // NSA_SEL_BLOCK
    n_top = min(NSA_TOP_N, n_sel)
    ng = gates_t.shape[0]
    qcb = d_model // LANES

    ci = jnp.arange(nc)[None, :] * NSA_CMP_STRIDE
    sj = jnp.arange(n_sel)[:, None] * NSA_SEL_BLOCK
    mt = ((ci < sj + NSA_SEL_BLOCK) & (ci + NSA_CMP_BLOCK > sj)
          & (jnp.arange(nc)[None, :] < nc - 1)).astype(BF16)

    q_spec = pl.BlockSpec((tq, gw), lambda b, g, i: (b * nq + i, g))
    gt_spec = pl.BlockSpec((ng, tq), lambda b, g, i: (0, b * nq + i))
    o_spec = pl.BlockSpec((tq, gw), lambda b, g, i: (b * nq + i, g))
    o_shape = jax.ShapeDtypeStruct((t, d_model), BF16)
    sem = ("arbitrary", "arbitrary", "arbitrary")
    qm_scr = pltpu.VMEM((hpg * tq, LANES), BF16)

    n_lane = max(n_sel, LANES)
    o_cmp_win, sel_mask = pl.pallas_call(
        functools.partial(_nsa_cmp_win_kernel, tq=tq, hpg=hpg, n_top=n_top, n_sel=n_sel),
        grid=(batch, groups, nq),
        in_specs=[
            q_spec,
            pl.BlockSpec((1, 1, 1, nc, LANES), lambda b, g, i: (b, 0, g, 0, 0)),
            pl.BlockSpec((1, 1, 1, nc, LANES), lambda b, g, i: (b, 1, g, 0, 0)),
            pl.BlockSpec((n_sel, nc), lambda b, g, i: (0, 0)),
            gt_spec,
            pl.BlockSpec((seq, LANES), lambda b, g, i: (b, qcb + 5 + g)),
            pl.BlockSpec((seq, LANES), lambda b, g, i: (b, qcb + 7)),
        ],
        out_specs=[o_spec, pl.BlockSpec((1, 1, tq, n_lane), lambda b, g, i: (b, g, i, 0))],
        out_shape=[o_shape, jax.ShapeDtypeStruct((batch, groups, seq, n_lane), BF16)],
        scratch_shapes=[qm_scr, pltpu.VMEM((LANES, hpg * tq), F32)],
        compiler_params=_cparams(*sem),
        name="nsa_cmp_win",
    )(proj, cmp_kv, cmp_kv, mt, gates_t, proj, proj)

    rows_sel = pl.BlockSpec((tq_sel, gw), lambda b, g, i: (b * nq_sel + i, g))
    return pl.pallas_call(
        functools.partial(_nsa_sel_kernel, tq=tq_sel, tk=tk, hpg=hpg),
        grid=(batch, groups, nq_sel),
        in_specs=[
            rows_sel,
            pl.BlockSpec((seq, LANES), lambda b, g, i: (b, qcb + 2 + g)),
            pl.BlockSpec((seq, LANES), lambda b, g, i: (b, qcb + 4)),
            pl.BlockSpec((1, 1, tq_sel, n_lane), lambda b, g, i: (b, g, i, 0)),
            pl.BlockSpec((ng, tq_sel), lambda b, g, i: (0, b * nq_sel + i)),
            rows_sel,
        ],
        out_specs=rows_sel,
        out_shape=o_shape,
        scratch_shapes=([pltpu.VMEM((hpg * tq_sel, 2 * LANES), BF16)]
                        + _attn_scratch(tk, hpg * tq_sel, LANES)),
        compiler_params=_cparams(*sem),
        name="nsa_sel",
    )(proj, proj, proj, sel_mask, gates_t, o_cmp_win)


def _diff_in_weights(w_in, d_model):
    scale = HEAD_DIM ** -0.5 * LOG2E
    wq = w_in[:, :d_model] * scale
    return jnp.concatenate([wq, w_in[:, d_model:]], axis=1).astype(BF16)


def _nsa_in_weights(w_in, d_model):
    g, hd = NSA_KV_GROUPS, HEAD_DIM
    kvw = g * hd
    scale = hd ** -0.5 * LOG2E
    q = w_in[:, :d_model] * scale
    off = d_model
    sec = [w_in[:, off + i * kvw: off + (i + 1) * kvw] for i in range(6)]
    dup = lambda w: jnp.concatenate([w[:, i * hd:(i + 1) * hd] for i in range(g) for _ in range(2)], axis=1)
    w_main = jnp.concatenate([q, sec[0], sec[1], dup(sec[2]), sec[3], dup(sec[4]), sec[5]], axis=1)
    w_gates = w_in[:, off + 6 * kvw:].T
    return w_main.astype(BF16), w_gates.astype(BF16)


def _router_weights(w_grp, b_grp, w_exp, b_exp):
    d = w_grp.shape[0]
    n_exp = w_exp.shape[0] * w_exp.shape[2]
    we = jnp.transpose(w_exp, (1, 0, 2)).reshape(d, n_exp)
    pad = LANES - n_exp - w_grp.shape[1]
    wr = jnp.concatenate([we, w_grp, jnp.zeros((d, pad), F32)], axis=1).astype(F32)
    wr_hi = wr.astype(BF16)
    wr_lo = (wr - wr_hi.astype(F32)).astype(BF16)
    br = jnp.concatenate([b_exp.reshape(-1), b_grp, jnp.zeros((pad,), F32)])[None, :].astype(F32)
    return jnp.concatenate([wr_hi, wr_lo], axis=1), br, n_exp


def kernel(x, c, positions, ada_w, ada_b, norm_g, final_g, diff_w_in, diff_w_out, diff_lambda,
           diff_subln_g, nsa_w_in, nsa_w_out, nsa_cmp_pos, nsa_cmp_w1, nsa_cmp_b1, nsa_cmp_w2,
           moe_w_group, moe_b_group, moe_w_expert, moe_b_expert, moe_w_gate, moe_w_up, moe_w_down):
    batch, seq, d = x.shape
    depth = ada_w.shape[0]
    x2d = x.reshape(batch * seq, d)
    tables = _rope_tables(positions)
    mod = _adaln_mod(c, ada_w, ada_b)
    w_gate, w_up, w_down = (w.astype(BF16) for w in (moe_w_gate, moe_w_up, moe_w_down))

    for i in range(depth):
        sh1, sc1, g1, sh2, sc2, g2 = jnp.split(mod[i], 6, axis=-1)
        j = i // N_MIXERS
        if i % N_MIXERS == 0:
            lambda_init = 0.8 - 0.6 * math.exp(-0.3 * i)
            qkv = _norm_proj(x2d, norm_g[i, 0], sc1, sh1, _diff_in_weights(diff_w_in[j], d), tables,
                             seq, tn=512, plain_tiles=range(2 * d // 512, 3 * d // 512))
            o = _diff_attention(qkv, diff_lambda[j], diff_subln_g[j], batch, seq, lambda_init)
            w_out = diff_w_out[j]
        else:
            w_main, w_gates = _nsa_in_weights(nsa_w_in[j], d)
            proj, gates_t = _norm_proj(x2d, norm_g[i, 0], sc1, sh1, w_main, tables, seq,
                                       tn=256, plain_tiles=(d // 256,), w_gates=w_gates)
            groups = NSA_KV_GROUPS
            kcvc = proj[:, d:d + 2 * groups * HEAD_DIM]
            ch = kcvc.reshape(batch, seq, 2, groups, HEAD_DIM).transpose(0, 2, 3, 1, 4)
            ch = ch.reshape(batch, 2, groups, seq // NSA_CMP_STRIDE, NSA_CMP_STRIDE * HEAD_DIM)
            cmp_kv = _nsa_compress(ch, nsa_cmp_pos[j], nsa_cmp_w1[j], nsa_cmp_b1[j], nsa_cmp_w2[j])
            o = _nsa_attention(proj, gates_t, cmp_kv, batch, seq, d)
            w_out = nsa_w_out[j]
        wr, br, n_exp = _router_weights(moe_w_group[i], moe_b_group[i], moe_w_expert[i], moe_b_expert[i])
        x2d, h2, comb, gid_t = _outproj_router(o, w_out.astype(BF16), x2d, g1, norm_g[i, 1], sc2, sh2,
                                               wr, br, seq, n_exp)
        x2d = _moe(h2, w_gate, w_up, w_down, i, comb, gid_t, x2d, g2, final_g, seq,
                   final_norm=(i == depth - 1))
    return x2d.reshape(batch, seq, d)
```

```python
import functools
import math

import jax
import jax.numpy as jnp
from jax import lax
from jax.experimental import pallas as pl
from jax.experimental.pallas import tpu as pltpu

F32 = jnp.float32
BF16 = jnp.bfloat16
HIGHEST = lax.Precision.HIGHEST

ROPE_THETA = 500000.0
ROT_DIM = 16
NORM_EPS = 1e-6
NEG_INF = -1e30
HEAD_DIM = 64
DA_V_DIM = 2 * HEAD_DIM
NSA_KV_GROUPS = 2
NSA_CMP_BLOCK = 32
NSA_CMP_STRIDE = 16
NSA_SEL_BLOCK = 64
NSA_TOP_N = 16
NSA_WINDOW = 512
NSA_FORCE = 1e4
MOE_GROUPS = 4
MOE_EXPERTS_PER_GROUP = 8
MOE_TOP_K = 2
N_MIXERS = 2
LOG2E = 1.4426950408889634

LANES = 128
PROJ_ROW_CHUNK = 256
VMEM_LIMIT_BYTES = 56 * 1024 * 1024

NT_DIMS = (((1,), (1,)), ((), ()))
TN_DIMS = (((0,), (0,)), ((), ()))


def _cparams(*sem):
    return pltpu.CompilerParams(dimension_semantics=sem, vmem_limit_bytes=VMEM_LIMIT_BYTES)


def _mod_kernel(c_ref, w_ref, b_ref, o_ref):
    c = c_ref[...]
    ca = c * jax.nn.sigmoid(c)
    o_ref[0] = jnp.dot(ca, w_ref[0], precision=HIGHEST, preferred_element_type=F32) + b_ref[0]


def _adaln_mod(c, ada_w, ada_b):
    depth, d, n = ada_w.shape
    b = c.shape[0]
    rows = 8
    tn = 1536
    c_pad = jnp.zeros((rows, d), F32).at[:b].set(c)
    out = pl.pallas_call(
        _mod_kernel,
        grid=(depth, n // tn),
        in_specs=[
            pl.BlockSpec((rows, d), lambda i, j: (0, 0)),
            pl.BlockSpec((1, d, tn), lambda i, j: (i, 0, j)),
            pl.BlockSpec((1, 1, tn), lambda i, j: (i, 0, j)),
        ],
        out_specs=pl.BlockSpec((1, rows, tn), lambda i, j: (i, 0, j)),
        out_shape=jax.ShapeDtypeStruct((depth, rows, n), F32),
        compiler_params=_cparams("arbitrary", "arbitrary"),
        name="adaln_mod",
    )(c_pad, ada_w, ada_b.reshape(depth, 1, n))
    return out[:, :b, :]


def _rope_tables(positions):
    half = ROT_DIM // 2
    inv_freq = ROPE_THETA ** (-jnp.arange(0, ROT_DIM, 2, dtype=F32) / ROT_DIM)
    ang = positions.astype(F32).reshape(-1)[:, None] * inv_freq
    cos, sin = jnp.cos(ang), jnp.sin(ang)
    t = ang.shape[0]
    rest = HEAD_DIM - ROT_DIM
    z8 = jnp.zeros((t, half), F32)
    cos_h = jnp.concatenate([cos, cos, jnp.ones((t, rest), F32)], axis=-1)
    sina_h = jnp.concatenate([-sin, z8, jnp.zeros((t, rest), F32)], axis=-1)
    sinb_h = jnp.concatenate([z8, sin, jnp.zeros((t, rest), F32)], axis=-1)
    rep = LANES // HEAD_DIM
    return tuple(jnp.tile(a, (1, rep)) for a in (cos_h, sina_h, sinb_h))


def _rms_modulate(x, g, sc, sh):
    ms = jnp.mean(x * x, axis=-1, keepdims=True)
    y = x * lax.rsqrt(ms + NORM_EPS)
    return (y * g) * (1.0 + sc) + sh


def _proj_kernel(*refs, plain_tiles, with_gates):
    if with_gates:
        (x_ref, g_ref, sc_ref, sh_ref, w_ref, cos_ref, sa_ref, sb_ref, wg_ref,
         o_ref, gt_ref, h_scr) = refs
    else:
        x_ref, g_ref, sc_ref, sh_ref, w_ref, cos_ref, sa_ref, sb_ref, o_ref, h_scr = refs
    j = pl.program_id(1)
    tn = o_ref.shape[1]

    @pl.when(j == 0)
    def _():
        h = _rms_modulate(x_ref[...], g_ref[...], sc_ref[0], sh_ref[0])
        h_scr[...] = h.astype(BF16)
        if with_gates:
            logits = lax.dot_general(wg_ref[...], h_scr[...], NT_DIMS, preferred_element_type=F32)
            gt_ref[...] = jax.nn.sigmoid(logits)

    is_plain = functools.reduce(jnp.logical_or, [j == t for t in plain_tiles], j < 0)
    for r in range(h_scr.shape[0] // PROJ_ROW_CHUNK):
        rows = slice(r * PROJ_ROW_CHUNK, (r + 1) * PROJ_ROW_CHUNK)
        acc = jnp.dot(h_scr[rows, :], w_ref[...], preferred_element_type=F32)
        cos = jnp.where(is_plain, 1.0, cos_ref[rows, :])
        sa = jnp.where(is_plain, 0.0, sa_ref[rows, :])
        sb = jnp.where(is_plain, 0.0, sb_ref[rows, :])
        shift_up = pltpu.roll(acc, tn - ROT_DIM // 2, 1)
        shift_dn = pltpu.roll(acc, ROT_DIM // 2, 1)
        for c in range(tn // LANES):
            sl = slice(c * LANES, (c + 1) * LANES)
            y = acc[:, sl] * cos + shift_up[:, sl] * sa + shift_dn[:, sl] * sb
            o_ref[rows, sl] = y.astype(o_ref.dtype)


def _norm_proj(x2d, g, sc, sh, w, tables, seq, *, tn, plain_tiles, w_gates=None, tm=1024):
    t, d = x2d.shape
    n = w.shape[1]
    with_gates = w_gates is not None
    per_b = seq // tm
    in_specs = [
        pl.BlockSpec((tm, d), lambda i, j: (i, 0)),
        pl.BlockSpec((1, d), lambda i, j: (0, 0)),
        pl.BlockSpec((1, 1, d), lambda i, j: (i // per_b, 0, 0)),
        pl.BlockSpec((1, 1, d), lambda i, j: (i // per_b, 0, 0)),
        pl.BlockSpec((d, tn), lambda i, j: (0, j)),
        pl.BlockSpec((tm, LANES), lambda i, j: (i, 0)),
        pl.BlockSpec((tm, LANES), lambda i, j: (i, 0)),
        pl.BlockSpec((tm, LANES), lambda i, j: (i, 0)),
    ]
    args = [x2d, g.reshape(1, d), sc[:, None, :], sh[:, None, :], w, *tables]
    out_specs = [pl.BlockSpec((tm, tn), lambda i, j: (i, j))]
    out_shape = [jax.ShapeDtypeStruct((t, n), BF16)]
    if with_gates:
        ng = w_gates.shape[0]
        in_specs.append(pl.BlockSpec((ng, d), lambda i, j: (0, 0)))
        args.append(w_gates)
        out_specs.append(pl.BlockSpec((ng, tm), lambda i, j: (0, i)))
        out_shape.append(jax.ShapeDtypeStruct((ng, t), F32))
    res = pl.pallas_call(
        functools.partial(_proj_kernel, plain_tiles=tuple(plain_tiles), with_gates=with_gates),
        grid=(t // tm, n // tn),
        in_specs=in_specs,
        out_specs=out_specs,
        out_shape=out_shape,
        scratch_shapes=[pltpu.VMEM((tm, d), BF16)],
        compiler_params=_cparams("arbitrary", "arbitrary"),
        name="norm_proj",
    )(*args)
    return res if with_gates else res[0]


def _attn_scratch(tk, nq, dv):
    return [
        pltpu.VMEM((2, tk, nq), F32),
        pltpu.VMEM((2, 1, nq), F32),
        pltpu.VMEM((2, tk, nq), BF16),
        pltpu.VMEM((2, 1, nq), F32),
        pltpu.VMEM((1, nq), F32),
        pltpu.VMEM((1, nq), F32),
        pltpu.VMEM((dv, nq), F32),
    ]


def _pipelined_attention(n_full, n_diag, scores_full, scores_diag, v_tile, scratch, sum_in_v=False):
    s_scr, mx_scr, p_scr, al_scr, m_ref, l_ref, acc_ref = scratch

    def stage_scores(fn, tile, slot):
        s = fn(tile)
        s_scr[slot] = s
        mx_scr[slot] = jnp.max(s, axis=0, keepdims=True)

    def stage_softmax(slot):
        m_prev = m_ref[...]
        m_new = jnp.maximum(m_prev, mx_scr[slot])
        alpha = jnp.exp2(m_prev - m_new)
        p = jnp.exp2(s_scr[slot] - m_new)
        if not sum_in_v:
            l_ref[...] = alpha * l_ref[...] + jnp.sum(p, axis=0, keepdims=True)
        m_ref[...] = m_new
        al_scr[slot] = alpha
        p_scr[slot] = p.astype(BF16)

    def stage_pv(pos, slot):
        tile = jnp.where(pos < n_diag, n_full + jnp.maximum(pos, 0), pos - n_diag)
        pv = lax.dot_general(v_tile(tile), p_scr[slot], TN_DIMS, preferred_element_type=F32)
        acc_ref[...] = acc_ref[...] * al_scr[slot] + pv

    def step(fn, pos, slot):
        nxt = pos + 1
        stage_scores(fn, jnp.where(nxt < n_diag, n_full + nxt, nxt - n_diag), 1 - slot)
        stage_softmax(slot)
        stage_pv(pos - 1, 1 - slot)

    m_ref[...] = jnp.full(m_ref.shape, NEG_INF, F32)
    l_ref[...] = jnp.zeros(l_ref.shape, F32)
    acc_ref[...] = jnp.zeros(acc_ref.shape, F32)
    p_scr[1] = jnp.zeros(p_scr.shape[1:], BF16)
    al_scr[1] = jnp.ones(al_scr.shape[1:], F32)
    stage_scores(scores_diag, n_full, 0)
    for p in range(n_diag - 1):
        step(scores_diag, p, p % 2)

    first = n_diag - 1
    s0 = first % 2

    def pair(jj, carry):
        step(scores_full, first + 2 * jj, s0)
        step(scores_full, first + 2 * jj + 1, 1 - s0)
        return carry

    lax.fori_loop(0, n_full // 2, pair, 0)
    odd = n_full % 2 == 1
    last = first + n_full

    @pl.when(odd)
    def _():
        step(scores_full, last - 1, s0)
        stage_softmax(1 - s0)
        stage_pv(last - 1, s0)
        stage_pv(last, 1 - s0)

    @pl.when(jnp.logical_not(odd))
    def _():
        stage_softmax(s0)
        stage_pv(last - 1, 1 - s0)
        stage_pv(last, s0)


def _split_halves(q, rows):
    lane = lax.broadcasted_iota(jnp.int32, q.shape, 1)
    zero = jnp.zeros_like(q)
    return jnp.where(lane < HEAD_DIM, q, zero), jnp.where(lane >= HEAD_DIM, q, zero)


def _diff_attn_kernel(q_ref, k_ref, v_ref, lam_ref, g_ref, o_ref, qm, *scratch,
                      tq, tk, lambda_init):
    qi = pl.program_id(2)
    lo, hi = _split_halves(q_ref[...], tq)
    qm[0:tq, :] = lo
    qm[tq:2 * tq, :] = hi
    q0 = qi * tq
    acc_ref, l_ref = scratch[-1], scratch[-2]

    def scores_full(t):
        k = k_ref[pl.ds(pl.multiple_of(t * tk, tk), tk), :]
        return lax.dot_general(k, qm[...], NT_DIMS, preferred_element_type=F32)

    def scores_diag(t):
        s = scores_full(t)
        k_idx = t * tk + lax.broadcasted_iota(jnp.int32, s.shape, 0)
        q_idx = q0 + lax.broadcasted_iota(jnp.int32, s.shape, 1) % tq
        return jnp.where(k_idx <= q_idx, s, NEG_INF)

    def v_tile(t):
        v = v_ref[pl.ds(pl.multiple_of(t * tk, tk), tk), :]
        return jnp.concatenate([v, jnp.ones((tk, 16), BF16)], axis=1)

    _pipelined_attention(q0 // tk, tq // tk, scores_full, scores_diag, v_tile, scratch, sum_in_v=True)

    lam = lam_ref[...]
    lam_full = (jnp.exp(jnp.sum(lam[0:1] * lam[1:2], axis=-1, keepdims=True))
                - jnp.exp(jnp.sum(lam[2:3] * lam[3:4], axis=-1, keepdims=True)) + lambda_init)
    acc = acc_ref[0:DA_V_DIM, :]
    l = jnp.maximum(acc_ref[DA_V_DIM:DA_V_DIM + 1, :], 1e-30)
    ot = acc[:, 0:tq] / l[:, 0:tq] - lam_full * (acc[:, tq:2 * tq] / l[:, tq:2 * tq])
    ms = jnp.mean(ot * ot, axis=0, keepdims=True)
    ot = (ot * lax.rsqrt(ms + NORM_EPS)) * g_ref[...] * (1.0 - lambda_init)
    o_ref[...] = ot.T.astype(o_ref.dtype)


def _diff_attention(qkv, lam, subln_g, batch, seq, lambda_init, *, tq=1024, tk=1024):
    assert tq % tk == 0 and seq % tq == 0
    t, n3 = qkv.shape
    d = n3 // 3
    heads = d // DA_V_DIM
    nq = seq // tq
    cb = d // LANES
    g_b = jnp.broadcast_to(subln_g.astype(F32)[:, None], (DA_V_DIM, tq))
    return pl.pallas_call(
        functools.partial(_diff_attn_kernel, tq=tq, tk=tk, lambda_init=lambda_init),
        grid=(batch, heads, nq),
        in_specs=[
            pl.BlockSpec((tq, LANES), lambda b, h, i: (b * nq + i, h)),
            pl.BlockSpec((seq, LANES), lambda b, h, i: (b, cb + h)),
            pl.BlockSpec((seq, LANES), lambda b, h, i: (b, 2 * cb + h)),
            pl.BlockSpec(lam.shape, lambda b, h, i: (0, 0)),
            pl.BlockSpec((DA_V_DIM, tq), lambda b, h, i: (0, 0)),
        ],
        out_specs=pl.BlockSpec((tq, LANES), lambda b, h, i: (b * nq + i, h)),
        out_shape=jax.ShapeDtypeStruct((t, d), BF16),
        scratch_shapes=[pltpu.VMEM((2 * tq, LANES), BF16)] + _attn_scratch(tk, 2 * tq, DA_V_DIM + 16),
        compiler_params=_cparams("arbitrary", "arbitrary", "arbitrary"),
        name="diff_attn",
    )(qkv, qkv, qkv, lam.astype(F32), g_b)


def _route(logits, n_exp):
    lane = lax.broadcasted_iota(jnp.int32, logits.shape, 1)
    lane_f = lane.astype(F32)
    big = float(LANES)
    is_grp = (lane >= n_exp) & (lane < n_exp + MOE_GROUPS)
    gl = jnp.where(is_grp, logits, -jnp.inf)
    ge = jnp.where(is_grp, jnp.exp(gl - jnp.max(gl, axis=-1, keepdims=True)), 0.0)
    pg = ge / jnp.sum(ge, axis=-1, keepdims=True)
    p_top = jnp.max(pg, axis=-1, keepdims=True)
    grp = jnp.min(jnp.where(is_grp & (pg == p_top), lane_f, big), axis=-1, keepdims=True) - n_exp
    in_grp = (lane < n_exp) & ((lane // MOE_EXPERTS_PER_GROUP).astype(F32) == grp)
    sel = jnp.where(in_grp, logits, -jnp.inf)
    v1 = jnp.max(sel, axis=-1, keepdims=True)
    i1 = jnp.min(jnp.where(sel == v1, lane_f, big), axis=-1, keepdims=True)
    sel2 = jnp.where(lane_f == i1, -jnp.inf, sel)
    v2 = jnp.max(sel2, axis=-1, keepdims=True)
    i2 = jnp.min(jnp.where(sel2 == v2, lane_f, big), axis=-1, keepdims=True)
    e2 = jnp.exp(v2 - v1)
    den = 1.0 + e2
    w1 = (1.0 / den) * p_top
    w2 = (e2 / den) * p_top
    return jnp.where(lane_f == i1, w1, 0.0) + jnp.where(lane_f == i2, w2, 0.0), grp


def _outproj_router_kernel(o_ref, w_ref, x_ref, g1_ref, ng_ref, sc_ref, sh_ref, wr_ref, br_ref,
                           xo_ref, h_ref, comb_ref, gid_ref, *, n_exp):
    mix = jnp.dot(o_ref[...], w_ref[...], preferred_element_type=F32)
    x = x_ref[...] + g1_ref[0] * mix
    xo_ref[...] = x
    h = _rms_modulate(x, ng_ref[...], sc_ref[0], sh_ref[0])
    h_hi = h.astype(BF16)
    h_ref[...] = h_hi
    h_lo = (h - h_hi.astype(F32)).astype(BF16)
    wr = wr_ref[...]
    hw = jnp.dot(h_hi, wr, preferred_element_type=F32)
    logits = (hw[:, 0:LANES] + hw[:, LANES:2 * LANES]
              + jnp.dot(h_lo, wr[:, 0:LANES], preferred_element_type=F32) + br_ref[...])
    comb, grp = _route(logits, n_exp)
    comb_ref[...] = comb[:, 0:n_exp]
    gid_ref[...] = jnp.broadcast_to(grp, logits.shape).T[0:gid_ref.shape[0], :]


def _outproj_router(o, w_out, x2d, g1, ng, sc, sh, wr, br, seq, n_exp, *, tm=1024):
    t, d = x2d.shape
    per_b = seq // tm
    bvec = lambda i: (i // per_b, 0, 0)
    return pl.pallas_call(
        functools.partial(_outproj_router_kernel, n_exp=n_exp),
        grid=(t // tm,),
        in_specs=[
            pl.BlockSpec((tm, d), lambda i: (i, 0)),
            pl.BlockSpec((d, d), lambda i: (0, 0)),
            pl.BlockSpec((tm, d), lambda i: (i, 0)),
            pl.BlockSpec((1, 1, d), bvec),
            pl.BlockSpec((1, d), lambda i: (0, 0)),
            pl.BlockSpec((1, 1, d), bvec),
            pl.BlockSpec((1, 1, d), bvec),
            pl.BlockSpec((d, 2 * LANES), lambda i: (0, 0)),
            pl.BlockSpec((1, LANES), lambda i: (0, 0)),
        ],
        out_specs=[
            pl.BlockSpec((tm, d), lambda i: (i, 0)),
            pl.BlockSpec((tm, d), lambda i: (i, 0)),
            pl.BlockSpec((tm, n_exp), lambda i: (i, 0)),
            pl.BlockSpec((8, tm), lambda i: (0, i)),
        ],
        out_shape=[
            jax.ShapeDtypeStruct((t, d), F32),
            jax.ShapeDtypeStruct((t, d), BF16),
            jax.ShapeDtypeStruct((t, n_exp), F32),
            jax.ShapeDtypeStruct((8, t), F32),
        ],
        compiler_params=_cparams("arbitrary"),
        name="outproj_router",
    )(o, w_out, x2d, g1[:, None, :], ng.reshape(1, d), sc[:, None, :], sh[:, None, :], wr, br)


MOE_CHUNK = 144
MOE_EXPERTS_PER_STEP = 4


def _moe_kernel(h_ref, wg_ref, wu_ref, wd_ref, comb_ref, gid_ref, tri_ref, x_ref, g2_ref, fg_ref,
                o_ref, perm_scr, hs_scr, cs_scr, ys_scr, off_smem, nch_smem, *, final_norm):
    step = pl.program_id(1)
    tm = h_ref.shape[0]
    rows = perm_scr.shape[0]
    n_exp = comb_ref.shape[1]
    steps_per_group = MOE_EXPERTS_PER_GROUP // MOE_EXPERTS_PER_STEP

    @pl.when(step == 0)
    def _():
        gid = gid_ref[0:1, :]
        grow = lax.broadcasted_iota(jnp.int32, (8, tm), 0).astype(F32)
        onehot = jnp.where(grow == gid, 1.0, 0.0)
        rank = jnp.dot(onehot.astype(BF16), tri_ref[...], preferred_element_type=F32)
        off = jnp.int32(0)
        off_col = jnp.zeros((8, 1), F32)
        grow_col = lax.broadcasted_iota(jnp.int32, (8, 1), 0)
        for g in range(MOE_GROUPS):
            cnt = jnp.sum(onehot[g:g + 1, :]).astype(jnp.int32)
            nch = (cnt + (MOE_CHUNK - 1)) // MOE_CHUNK
            off_smem[g] = off
            nch_smem[g] = nch
            off_col = jnp.where(grow_col == g, off.astype(F32), off_col)
            off = off + nch * MOE_CHUNK
        dest = jnp.sum(onehot * (rank + off_col), axis=0, keepdims=True)
        row_id = lax.broadcasted_iota(jnp.int32, (rows, tm), 0)
        perm = jnp.where(row_id == dest.astype(jnp.int32), 1.0, 0.0).astype(BF16)
        perm_scr[...] = perm
        hs_scr[...] = jnp.dot(perm, h_ref[...], preferred_element_type=F32).astype(BF16)
        comb = comb_ref[...]
        comb_hi = comb.astype(BF16)
        comb_lo = (comb - comb_hi.astype(F32)).astype(BF16)
        cs_scr[...] = (jnp.dot(perm, comb_hi, preferred_element_type=F32)
                       + jnp.dot(perm, comb_lo, preferred_element_type=F32))
        ys_scr[...] = jnp.zeros(ys_scr.shape, F32)

    g = step // steps_per_group
    off = off_smem[g]
    nch = nch_smem[g]

    def run_experts(r0, rows_n):
        lane = lax.broadcasted_iota(jnp.int32, (rows_n, n_exp), 1)
        xs = hs_scr[pl.ds(r0, rows_n), :]
        cs = cs_scr[pl.ds(r0, rows_n), :]
        y = jnp.zeros((rows_n, xs.shape[1]), F32)
        for k in range(MOE_EXPERTS_PER_STEP):
            gate = jnp.dot(xs, wg_ref[k], preferred_element_type=F32)
            up = jnp.dot(xs, wu_ref[k], preferred_element_type=F32)
            a = (gate * jax.nn.sigmoid(gate)) * up
            e = step * MOE_EXPERTS_PER_STEP + k
            ce = jnp.sum(jnp.where(lane == e, cs, 0.0), axis=-1, keepdims=True)
            y = y + jnp.dot((a * ce).astype(BF16), wd_ref[k], preferred_element_type=F32)
        ys_scr[pl.ds(r0, rows_n), :] += y

    def chunk_pair(c, carry):
        run_experts(pl.multiple_of(off + c * (2 * MOE_CHUNK), MOE_CHUNK), 2 * MOE_CHUNK)
        return carry

    lax.fori_loop(0, nch // 2, chunk_pair, 0)

    @pl.when(nch % 2 == 1)
    def _():
        run_experts(pl.multiple_of(off + (nch - 1) * MOE_CHUNK, MOE_CHUNK), MOE_CHUNK)

    @pl.when(step == pl.num_programs(1) - 1)
    def _():
        y = lax.dot_general(perm_scr[...], ys_scr[...].astype(BF16), TN_DIMS,
                            preferred_element_type=F32)
        x = x_ref[...] + g2_ref[0] * y
        if final_norm:
            ms = jnp.mean(x * x, axis=-1, keepdims=True)
            x = (x * lax.rsqrt(ms + NORM_EPS)) * fg_ref[...]
        o_ref[...] = x


def _moe(h, w_gate, w_up, w_down, layer, comb, gid_t, x2d, g2, final_g, seq, *, final_norm, tm=1024):
    t, d = x2d.shape
    _, n_exp, _, hidden = w_gate.shape
    per_b = seq // tm
    eps = MOE_EXPERTS_PER_STEP
    rows = tm + MOE_GROUPS * MOE_CHUNK
    idx = jnp.arange(tm)
    tri = (idx[:, None] < idx[None, :]).astype(BF16)
    return pl.pallas_call(
        functools.partial(_moe_kernel, final_norm=final_norm),
        grid=(t // tm, n_exp // eps),
        in_specs=[
            pl.BlockSpec((tm, d), lambda i, s: (i, 0)),
            pl.BlockSpec((None, eps, d, hidden), lambda i, s: (layer, s, 0, 0)),
            pl.BlockSpec((None, eps, d, hidden), lambda i, s: (layer, s, 0, 0)),
            pl.BlockSpec((None, eps, hidden, d), lambda i, s: (layer, s, 0, 0)),
            pl.BlockSpec((tm, n_exp), lambda i, s: (i, 0)),
            pl.BlockSpec((8, tm), lambda i, s: (0, i)),
            pl.BlockSpec((tm, tm), lambda i, s: (0, 0)),
            pl.BlockSpec((tm, d), lambda i, s: (i, 0)),
            pl.BlockSpec((1, 1, d), lambda i, s: (i // per_b, 0, 0)),
            pl.BlockSpec((1, d), lambda i, s: (0, 0)),
        ],
        out_specs=pl.BlockSpec((tm, d), lambda i, s: (i, 0)),
        out_shape=jax.ShapeDtypeStruct((t, d), F32),
        scratch_shapes=[
            pltpu.VMEM((rows, tm), BF16),
            pltpu.VMEM((rows, d), BF16),
            pltpu.VMEM((rows, n_exp), F32),
            pltpu.VMEM((rows, d), F32),
            pltpu.SMEM((MOE_GROUPS,), jnp.int32),
            pltpu.SMEM((MOE_GROUPS,), jnp.int32),
        ],
        compiler_params=_cparams("arbitrary", "arbitrary"),
        name="moe",
    )(h, w_gate, w_up, w_down, comb, gid_t, tri, x2d, g2[:, None, :], final_g.reshape(1, d))


def _compress_kernel(ch_ref, w1a_ref, w1b_ref, pos_ref, w1f_ref, b1_ref, w2_ref, o_ref):
    ch = ch_ref[0, 0, 0]
    n = ch.shape[0]
    a = jnp.dot(ch, w1a_ref[0], preferred_element_type=F32)
    b = jnp.dot(ch, w1b_ref[0], preferred_element_type=F32)
    const = jnp.dot(pos_ref[0], w1f_ref[0], precision=HIGHEST, preferred_element_type=F32)[0:1]
    hid = a + pltpu.roll(b, n - 1, 0) + (const + b1_ref[0])
    act = jax.nn.gelu(hid, approximate=True)
    o_ref[0, 0, 0] = jnp.dot(act.astype(BF16), w2_ref[0], preferred_element_type=F32).astype(o_ref.dtype)


def _nsa_compress(ch, cmp_pos, cmp_w1, cmp_b1, cmp_w2):
    b, two, g, nc, f = ch.shape
    hid = cmp_w1.shape[-1]
    w1a = cmp_w1[:, :f, :].astype(BF16)
    w1b = cmp_w1[:, f:, :].astype(BF16)
    pos = jnp.zeros((two, 8, 2 * f), F32).at[:, 0, :].set(cmp_pos.reshape(two, 2 * f))
    w2 = jnp.concatenate([cmp_w2, cmp_w2], axis=-1).astype(BF16)
    return pl.pallas_call(
        _compress_kernel,
        grid=(b, two, g),
        in_specs=[
            pl.BlockSpec((1, 1, 1, nc, f), lambda i, j, k: (i, j, k, 0, 0)),
            pl.BlockSpec((1, f, hid), lambda i, j, k: (j, 0, 0)),
            pl.BlockSpec((1, f, hid), lambda i, j, k: (j, 0, 0)),
            pl.BlockSpec((1, 8, 2 * f), lambda i, j, k: (j, 0, 0)),
            pl.BlockSpec((1, 2 * f, hid), lambda i, j, k: (j, 0, 0)),
            pl.BlockSpec((1, 1, hid), lambda i, j, k: (j, 0, 0)),
            pl.BlockSpec((1, hid, LANES), lambda i, j, k: (j, 0, 0)),
        ],
        out_specs=pl.BlockSpec((1, 1, 1, nc, LANES), lambda i, j, k: (i, j, k, 0, 0)),
        out_shape=jax.ShapeDtypeStruct((b, two, g, nc, LANES), BF16),
        compiler_params=_cparams("arbitrary", "arbitrary", "arbitrary"),
        name="nsa_compress",
    )(ch, w1a, w1b, pos, cmp_w1.astype(F32), cmp_b1[:, None, :].astype(F32), w2)


def _stack_group_queries(q_ref, qm, tq):
    for j in range(q_ref.shape[1] // LANES):
        lo, hi = _split_halves(q_ref[:, j * LANES:(j + 1) * LANES], tq)
        qm[(2 * j) * tq:(2 * j + 1) * tq, 0:LANES] = lo
        qm[(2 * j + 1) * tq:(2 * j + 2) * tq, 0:LANES] = hi


def _own_group_values_with_ones(v, g):
    lane = lax.broadcasted_iota(jnp.int32, v.shape, 1)
    own_half = (lane >= HEAD_DIM) == (g == 1)
    return jnp.where(own_half, v, jnp.ones_like(v))


def _gated_pairs_to_token_major(ot, gt_ref, branch, tq, hpg):
    g = pl.program_id(1)
    tiles = []
    for j in range(hpg // 2):
        parts = []
        for h in (2 * j, 2 * j + 1):
            row = gt_ref[pl.ds((g * hpg + h) * 3 + branch, 1), :]
            parts.append(ot[:, h * tq:(h + 1) * tq] * row)
        tiles.append(jnp.concatenate(parts, axis=0).T)
    return tiles


NSA_CMP_CLASSES = 4


def _nsa_window_tiles(q0, qm, kw_ref, vw_ref, gt_ref, acc_ref, *, tq, hpg):
    g = pl.program_id(1)
    span = NSA_WINDOW + tq
    start = pl.multiple_of(jnp.maximum(q0 - NSA_WINDOW, 0), tq)
    k = kw_ref[pl.ds(start, span), :]
    v = vw_ref[pl.ds(start, span), :]
    s = lax.dot_general(k, qm[...], NT_DIMS, preferred_element_type=F32)
    k_idx = start + lax.broadcasted_iota(jnp.int32, (span, tq), 0)
    q_idx = q0 + lax.broadcasted_iota(jnp.int32, (span, tq), 1)
    dist = q_idx - k_idx
    bias = jnp.where(dist >= 0, jnp.where(dist < NSA_WINDOW, 0.0, NEG_INF), NEG_INF)
    s = s + jnp.concatenate([bias] * hpg, axis=1)
    m = jnp.max(s, axis=0, keepdims=True)
    e = jnp.exp2(s - m)
    r = 1.0 / jnp.maximum(jnp.sum(e, axis=0, keepdims=True), 1e-30)
    acc_ref[...] = lax.dot_general(v, e.astype(BF16), TN_DIMS, preferred_element_type=F32) * r
    ot = acc_ref[pl.ds(pl.multiple_of(g * HEAD_DIM, HEAD_DIM), HEAD_DIM), :]
    return _gated_pairs_to_token_major(ot, gt_ref, 2, tq, hpg)


def _nsa_cmp_body(q0, qm, kc_ref, vc_ref, mt_ref, gt_ref, kw_ref, vw_ref, o_ref, mask_ref, acc_ref,
                  *, tq, hpg, n_top, n_cmp, n_blk):
    win_tiles = _nsa_window_tiles(q0, qm, kw_ref, vw_ref, gt_ref, acc_ref, tq=tq, hpg=hpg)
    kc = kc_ref[0, 0, 0, 0:n_cmp, :]
    s = lax.dot_general(kc, qm[...], NT_DIMS, preferred_element_type=F32)
    n_idx = lax.broadcasted_iota(jnp.int32, (n_cmp, tq), 0)
    q_idx = q0 + lax.broadcasted_iota(jnp.int32, (n_cmp, tq), 1)
    valid = n_idx * NSA_CMP_STRIDE + (NSA_CMP_BLOCK - 1) <= q_idx
    bias = jnp.where(valid, 0.0, NEG_INF)
    s = s + jnp.concatenate([bias] * hpg, axis=1)
    m = jnp.maximum(jnp.max(s, axis=0, keepdims=True), 0.1 * NEG_INF)
    e = jnp.exp2(s - m)
    r = 1.0 / jnp.maximum(jnp.sum(e, axis=0, keepdims=True), 1e-30)
    p = e * r
    ot = lax.dot_general(vc_ref[0, 0, 0, 0:n_cmp, :], p.astype(BF16), TN_DIMS,
                         preferred_element_type=F32)
    tiles = _gated_pairs_to_token_major(ot[0:HEAD_DIM], gt_ref, 0, tq, hpg)
    for j, (tile, win) in enumerate(zip(tiles, win_tiles)):
        o_ref[:, j * LANES:(j + 1) * LANES] = (tile + win).astype(o_ref.dtype)

    psum = p[:, 0:tq]
    for h in range(1, hpg):
        psum = psum + p[:, h * tq:(h + 1) * tq]
    p_hi = psum.astype(BF16)
    p_lo = (psum - p_hi.astype(F32)).astype(BF16)
    mt = mt_ref[0:n_blk, 0:n_cmp]
    imp = (jnp.dot(mt, p_hi, preferred_element_type=F32)
           + jnp.dot(mt, p_lo, preferred_element_type=F32))
    j_idx = lax.broadcasted_iota(jnp.int32, imp.shape, 0)
    j_f = j_idx.astype(F32)
    q_blk = (q0 + lax.broadcasted_iota(jnp.int32, imp.shape, 1)) // NSA_SEL_BLOCK
    forced = (j_idx == 0) | (j_idx == q_blk) | (j_idx == q_blk - 1)
    val = jnp.where(forced, -jnp.inf, jnp.where(j_idx <= q_blk, imp, -NSA_FORCE))
    sel_bias = jnp.where(forced, 0.0, NEG_INF)
    for _ in range(max(n_top - 3, 0)):
        mx = jnp.max(val, axis=0, keepdims=True)
        first = jnp.min(jnp.where(val == mx, j_f, float(n_blk)), axis=0, keepdims=True)
        hit = j_f == first
        sel_bias = jnp.where(hit, 0.0, sel_bias)
        val = jnp.where(hit, -jnp.inf, val)
    n_lane = mask_ref.shape[3]
    if n_blk < n_lane:
        sel_bias = jnp.concatenate([sel_bias, jnp.full((n_lane - n_blk, tq), NEG_INF, F32)], axis=0)
    mask_ref[0, 0] = sel_bias.T.astype(mask_ref.dtype)


def _nsa_cmp_win_kernel(q_ref, kc_ref, vc_ref, mt_ref, gt_ref, kw_ref, vw_ref, o_ref, mask_ref,
                        qm, acc_ref, *, tq, hpg, n_top, n_sel):
    qi = pl.program_id(2)
    q0 = qi * tq
    _stack_group_queries(q_ref, qm, tq)
    nc = kc_ref.shape[3]
    seq = n_sel * NSA_SEL_BLOCK
    cls = (q0 + tq - 1) // (seq // NSA_CMP_CLASSES)
    for c in range(NSA_CMP_CLASSES):
        @pl.when(cls == c)
        def _(c=c):
            _nsa_cmp_body(q0, qm, kc_ref, vc_ref, mt_ref, gt_ref, kw_ref, vw_ref, o_ref, mask_ref,
                          acc_ref, tq=tq, hpg=hpg,
                          n_top=min(n_top, (c + 1) * n_sel // NSA_CMP_CLASSES),
                          n_cmp=(c + 1) * nc // NSA_CMP_CLASSES,
                          n_blk=(c + 1) * n_sel // NSA_CMP_CLASSES)


def _nsa_sel_kernel(q_ref, k_ref, v_ref, mask_ref, gt_ref, ocw_ref, o_ref,
                    qm, *scratch, tq, tk, hpg):
    g = pl.program_id(1)
    qi = pl.program_id(2)
    q0 = qi * tq
    _stack_group_queries(q_ref, qm, tq)
    acc_ref, l_ref = scratch[-1], scratch[-2]
    blocks = tk // NSA_SEL_BLOCK

    def scores_full(t):
        first_blk = t * blocks
        window = pl.multiple_of((first_blk // LANES) * LANES, LANES)
        bias_win = mask_ref[0, 0, :, pl.ds(window, LANES)]
        for h in range(hpg):
            qm[h * tq:(h + 1) * tq, LANES:2 * LANES] = bias_win
        lane = lax.broadcasted_iota(jnp.int32, (tk, LANES), 1)
        blk = first_blk - window + lax.broadcasted_iota(jnp.int32, (tk, LANES), 0) // NSA_SEL_BLOCK
        onehot = jnp.where(lane == blk, 1.0, 0.0).astype(BF16)
        k = k_ref[pl.ds(pl.multiple_of(t * tk, tk), tk), :]
        kx = jnp.concatenate([k, onehot], axis=1)
        return lax.dot_general(kx, qm[...], NT_DIMS, preferred_element_type=F32)

    def scores_diag(t):
        s = scores_full(t)
        k_idx = t * tk + lax.broadcasted_iota(jnp.int32, s.shape, 0)
        q_idx = q0 + lax.broadcasted_iota(jnp.int32, s.shape, 1) % tq
        return jnp.where(k_idx <= q_idx, s, NEG_INF)

    def v_tile(t):
        return _own_group_values_with_ones(v_ref[pl.ds(pl.multiple_of(t * tk, tk), tk), :], g)

    _pipelined_attention(q0 // tk, tq // tk, scores_full, scores_diag, v_tile, scratch,
                         sum_in_v=True)

    own = pl.multiple_of(g * HEAD_DIM, HEAD_DIM)
    other = pl.multiple_of((1 - g) * HEAD_DIM, HEAD_DIM)
    inv_l = 1.0 / jnp.maximum(acc_ref[pl.ds(other, 1), :], 1e-30)
    ot = acc_ref[pl.ds(own, HEAD_DIM), :] * inv_l
    tiles = _gated_pairs_to_token_major(ot, gt_ref, 1, tq, hpg)
    for j, tile in enumerate(tiles):
        sl = slice(j * LANES, (j + 1) * LANES)
        total = tile + ocw_ref[:, sl].astype(F32)
        o_ref[:, sl] = total.astype(o_ref.dtype)


def _nsa_attention(proj, gates_t, cmp_kv, batch, seq, d_model, *, tq=128, tq_sel=512, tk=512):
    assert tq_sel % tk == 0 and seq % tq_sel == 0
    t = proj.shape[0]
    groups = NSA_KV_GROUPS
    hpg = d_model // HEAD_DIM // groups
    gw = hpg * HEAD_DIM
    nq = seq // tq
    nq_sel = seq // tq_sel
    nc = cmp_kv.shape[3]
    n_sel = seq // NSA_SEL_BLOCK
    n_top = min(NSA_TOP_N, n_sel)
    ng = gates_t.shape[0]
    qcb = d_model // LANES

    ci = jnp.arange(nc)[None, :] * NSA_CMP_STRIDE
    sj = jnp.arange(n_sel)[:, None] * NSA_SEL_BLOCK
    mt = ((ci < sj + NSA_SEL_BLOCK) & (ci + NSA_CMP_BLOCK > sj)
          & (jnp.arange(nc)[None, :] < nc - 1)).astype(BF16)

    q_spec = pl.BlockSpec((tq, gw), lambda b, g, i: (b * nq + i, g))
    gt_spec = pl.BlockSpec((ng, tq), lambda b, g, i: (0, b * nq + i))
    o_spec = pl.BlockSpec((tq, gw), lambda b, g, i: (b * nq + i, g))
    o_shape = jax.ShapeDtypeStruct((t, d_model), BF16)
    sem = ("arbitrary", "arbitrary", "arbitrary")
    qm_scr = pltpu.VMEM((hpg * tq, LANES), BF16)

    n_lane = max(n_sel, LANES)
    o_cmp_win, sel_mask = pl.pallas_call(
        functools.partial(_nsa_cmp_win_kernel, tq=tq, hpg=hpg, n_top=n_top, n_sel=n_sel),
        grid=(batch, groups, nq),
        in_specs=[
            q_spec,
            pl.BlockSpec((1, 1, 1, nc, LANES), lambda b, g, i: (b, 0, g, 0, 0)),
            pl.BlockSpec((1, 1, 1, nc, LANES), lambda b, g, i: (b, 1, g, 0, 0)),
            pl.BlockSpec((n_sel, nc), lambda b, g, i: (0, 0)),
            gt_spec,
            pl.BlockSpec((seq, LANES), lambda b, g, i: (b, qcb + 5 + g)),
            pl.BlockSpec((seq, LANES), lambda b, g, i: (b, qcb + 7)),
        ],
        out_specs=[o_spec, pl.BlockSpec((1, 1, tq, n_lane), lambda b, g, i: (b, g, i, 0))],
        out_shape=[o_shape, jax.ShapeDtypeStruct((batch, groups, seq, n_lane), BF16)],
        scratch_shapes=[qm_scr, pltpu.VMEM((LANES, hpg * tq), F32)],
        compiler_params=_cparams(*sem),
        name="nsa_cmp_win",
    )(proj, cmp_kv, cmp_kv, mt, gates_t, proj, proj)

    rows_sel = pl.BlockSpec((tq_sel, gw), lambda b, g, i: (b * nq_sel + i, g))
    return pl.pallas_call(
        functools.partial(_nsa_sel_kernel, tq=tq_sel, tk=tk, hpg=hpg),
        grid=(batch, groups, nq_sel),
        in_specs=[
            rows_sel,
            pl.BlockSpec((seq, LANES), lambda b, g, i: (b, qcb + 2 + g)),
            pl.BlockSpec((seq, LANES), lambda b, g, i: (b, qcb + 4)),
            pl.BlockSpec((1, 1, tq_sel, n_lane), lambda b, g, i: (b, g, i, 0)),
            pl.BlockSpec((ng, tq_sel), lambda b, g, i: (0, b * nq_sel + i)),
            rows_sel,
        ],
        out_specs=rows_sel,
        out_shape=o_shape,
        scratch_shapes=([pltpu.VMEM((hpg * tq_sel, 2 * LANES), BF16)]
                        + _attn_scratch(tk, hpg * tq_sel, LANES)),
        compiler_params=_cparams(*sem),
        name="nsa_sel",
    )(proj, proj, proj, sel_mask, gates_t, o_cmp_win)


def _diff_in_weights(w_in, d_model):
    scale = HEAD_DIM ** -0.5 * LOG2E
    wq = w_in[:, :d_model] * scale
    return jnp.concatenate([wq, w_in[:, d_model:]], axis=1).astype(BF16)


def _nsa_in_weights(w_in, d_model):
    g, hd = NSA_KV_GROUPS, HEAD_DIM
    kvw = g * hd
    scale = hd ** -0.5 * LOG2E
    q = w_in[:, :d_model] * scale
    off = d_model
    sec = [w_in[:, off + i * kvw: off + (i + 1) * kvw] for i in range(6)]
    dup = lambda w: jnp.concatenate([w[:, i * hd:(i + 1) * hd] for i in range(g) for _ in range(2)], axis=1)
    w_main = jnp.concatenate([q, sec[0], sec[1], dup(sec[2]), sec[3], dup(sec[4]), sec[5]], axis=1)
    w_gates = w_in[:, off + 6 * kvw:].T
    return w_main.astype(BF16), w_gates.astype(BF16)


def _router_weights(w_grp, b_grp, w_exp, b_exp):
    d = w_grp.shape[0]
    n_exp = w_exp.shape[0] * w_exp.shape[2]
    we = jnp.transpose(w_exp, (1, 0, 2)).reshape(d, n_exp)
    pad = LANES - n_exp - w_grp.shape[1]
    wr = jnp.concatenate([we, w_grp, jnp.zeros((d, pad), F32)], axis=1).astype(F32)
    wr_hi = wr.astype(BF16)
    wr_lo = (wr - wr_hi.astype(F32)).astype(BF16)
    br = jnp.concatenate([b_exp.reshape(-1), b_grp, jnp.zeros((pad,), F32)])[None, :].astype(F32)
    return jnp.concatenate([wr_hi, wr_lo], axis=1), br, n_exp


def kernel(x, c, positions, ada_w, ada_b, norm_g, final_g, diff_w_in, diff_w_out, diff_lambda,
           diff_subln_g, nsa_w_in, nsa_w_out, nsa_cmp_pos, nsa_cmp_w1, nsa_cmp_b1, nsa_cmp_w2,
           moe_w_group, moe_b_group, moe_w_expert, moe_b_expert, moe_w_gate, moe_w_up, moe_w_down):
    batch, seq, d = x.shape
    depth = ada_w.shape[0]
    x2d = x.reshape(batch * seq, d)
    tables = _rope_tables(positions)
    mod = _adaln_mod(c, ada_w, ada_b)
    w_gate, w_up, w_down = (w.astype(BF16) for w in (moe_w_gate, moe_w_up, moe_w_down))

    for i in range(depth):
        sh1, sc1, g1, sh2, sc2, g2 = jnp.split(mod[i], 6, axis=-1)
        j = i // N_MIXERS
        if i % N_MIXERS == 0:
            lambda_init = 0.8 - 0.6 * math.exp(-0.3 * i)
            qkv = _norm_proj(x2d, norm_g[i, 0], sc1, sh1, _diff_in_weights(diff_w_in[j], d), tables,
                             seq, tn=512, plain_tiles=range(2 * d // 512, 3 * d // 512))
            o = _diff_attention(qkv, diff_lambda[j], diff_subln_g[j], batch, seq, lambda_init)
            w_out = diff_w_out[j]
        else:
            w_main, w_gates = _nsa_in_weights(nsa_w_in[j], d)
            proj, gates_t = _norm_proj(x2d, norm_g[i, 0], sc1, sh1, w_main, tables, seq,
                                       tn=256, plain_tiles=(d // 256,), w_gates=w_gates)
            groups = NSA_KV_GROUPS
            kcvc = proj[:, d:d + 2 * groups * HEAD_DIM]
            ch = kcvc.reshape(batch, seq, 2, groups, HEAD_DIM).transpose(0, 2, 3, 1, 4)
            ch = ch.reshape(batch, 2, groups, seq // NSA_CMP_STRIDE, NSA_CMP_STRIDE * HEAD_DIM)
            cmp_kv = _nsa_compress(ch, nsa_cmp_pos[j], nsa_cmp_w1[j], nsa_cmp_b1[j], nsa_cmp_w2[j])
            o = _nsa_attention(proj, gates_t, cmp_kv, batch, seq, d)
            w_out = nsa_w_out[j]
        wr, br, n_exp = _router_weights(moe_w_group[i], moe_b_group[i], moe_w_expert[i], moe_b_expert[i])
        x2d, h2, comb, gid_t = _outproj_router(o, w_out.astype(BF16), x2d, g1, norm_g[i, 1], sc2, sh2,
                                               wr, br, seq, n_exp)
        x2d = _moe(h2, w_gate, w_up, w_down, i, comb, gid_t, x2d, g2, final_g, seq,
                   final_norm=(i == depth - 1))
    return x2d.reshape(batch, seq, d)
```

```python
import functools
import math

import jax
import jax.numpy as jnp
from jax import lax
from jax.experimental import pallas as pl
from jax.experimental.pallas import tpu as pltpu

F32 = jnp.float32
BF16 = jnp.bfloat16
HIGHEST = lax.Precision.HIGHEST

ROPE_THETA = 500000.0
ROT_DIM = 16
NORM_EPS = 1e-6
NEG_INF = -1e30
HEAD_DIM = 64
DA_V_DIM = 2 * HEAD_DIM
DA_ONES_COLS = 16
NSA_KV_GROUPS = 2
NSA_CMP_BLOCK = 32
NSA_CMP_STRIDE = 16
NSA_SEL_BLOCK = 64
NSA_TOP_N = 16
NSA_WINDOW = 512
NSA_FORCE = 1e4
MOE_GROUPS = 4
MOE_EXPERTS_PER_GROUP = 8
N_MIXERS = 2
LOG2E = 1.4426950408889634

LANES = 128
PROJ_ROW_CHUNK = 256
VMEM_LIMIT_BYTES = 56 * 1024 * 1024

NT_DIMS = (((1,), (1,)), ((), ()))
TN_DIMS = (((0,), (0,)), ((), ()))


def _cparams(*sem):
    return pltpu.CompilerParams(dimension_semantics=sem, vmem_limit_bytes=VMEM_LIMIT_BYTES)


def _mod_kernel(c_ref, w_ref, b_ref, o_ref):
    c = c_ref[...]
    ca = c * jax.nn.sigmoid(c)
    o_ref[0] = jnp.dot(ca, w_ref[0], precision=HIGHEST, preferred_element_type=F32) + b_ref[0]


def _adaln_mod(c, ada_w, ada_b):
    depth, d, n = ada_w.shape
    b = c.shape[0]
    rows = 8
    tn = 1536
    c_pad = jnp.zeros((rows, d), F32).at[:b].set(c)
    out = pl.pallas_call(
        _mod_kernel,
        grid=(depth, n // tn),
        in_specs=[
            pl.BlockSpec((rows, d), lambda i, j: (0, 0)),
            pl.BlockSpec((1, d, tn), lambda i, j: (i, 0, j)),
            pl.BlockSpec((1, 1, tn), lambda i, j: (i, 0, j)),
        ],
        out_specs=pl.BlockSpec((1, rows, tn), lambda i, j: (i, 0, j)),
        out_shape=jax.ShapeDtypeStruct((depth, rows, n), F32),
        compiler_params=_cparams("arbitrary", "arbitrary"),
        name="adaln_mod",
    )(c_pad, ada_w, ada_b.reshape(depth, 1, n))
    return out[:, :b, :]


def _rope_tables(positions):
    half = ROT_DIM // 2
    inv_freq = ROPE_THETA ** (-jnp.arange(0, ROT_DIM, 2, dtype=F32) / ROT_DIM)
    ang = positions.astype(F32).reshape(-1)[:, None] * inv_freq
    cos, sin = jnp.cos(ang), jnp.sin(ang)
    t = ang.shape[0]
    rest = HEAD_DIM - ROT_DIM
    z8 = jnp.zeros((t, half), F32)
    cos_h = jnp.concatenate([cos, cos, jnp.ones((t, rest), F32)], axis=-1)
    sina_h = jnp.concatenate([-sin, z8, jnp.zeros((t, rest), F32)], axis=-1)
    sinb_h = jnp.concatenate([z8, sin, jnp.zeros((t, rest), F32)], axis=-1)
    rep = LANES // HEAD_DIM
    return tuple(jnp.tile(a, (1, rep)) for a in (cos_h, sina_h, sinb_h))


def _rms_modulate(x, g, sc, sh):
    ms = jnp.mean(x * x, axis=-1, keepdims=True)
    y = x * lax.rsqrt(ms + NORM_EPS)
    return (y * g) * (1.0 + sc) + sh


def _proj_kernel(*refs, plain_tiles, with_gates):
    if with_gates:
        (x_ref, g_ref, sc_ref, sh_ref, w_ref, cos_ref, sa_ref, sb_ref, wg_ref,
         o_ref, gt_ref, h_scr) = refs
    else:
        x_ref, g_ref, sc_ref, sh_ref, w_ref, cos_ref, sa_ref, sb_ref, o_ref, h_scr = refs
    j = pl.program_id(1)
    tn = o_ref.shape[1]

    @pl.when(j == 0)
    def _():
        h = _rms_modulate(x_ref[...], g_ref[...], sc_ref[0], sh_ref[0])
        h_scr[...] = h.astype(BF16)
        if with_gates:
            logits = lax.dot_general(wg_ref[...], h_scr[...], NT_DIMS, preferred_element_type=F32)
            gt_ref[...] = jax.nn.sigmoid(logits)

    is_plain = functools.reduce(jnp.logical_or, [j == t for t in plain_tiles], j < 0)
    for r in range(h_scr.shape[0] // PROJ_ROW_CHUNK):
        rows = slice(r * PROJ_ROW_CHUNK, (r + 1) * PROJ_ROW_CHUNK)
        acc = jnp.dot(h_scr[rows, :], w_ref[...], preferred_element_type=F32)
        cos = jnp.where(is_plain, 1.0, cos_ref[rows, :])
        sa = jnp.where(is_plain, 0.0, sa_ref[rows, :])
        sb = jnp.where(is_plain, 0.0, sb_ref[rows, :])
        shift_up = pltpu.roll(acc, tn - ROT_DIM // 2, 1)
        shift_dn = pltpu.roll(acc, ROT_DIM // 2, 1)
        for c in range(tn // LANES):
            sl = slice(c * LANES, (c + 1) * LANES)
            y = acc[:, sl] * cos + shift_up[:, sl] * sa + shift_dn[:, sl] * sb
            o_ref[rows, sl] = y.astype(o_ref.dtype)


def _norm_proj(x2d, g, sc, sh, w, tables, seq, *, tn, plain_tiles, w_gates=None, tm=1024):
    t, d = x2d.shape
    n = w.shape[1]
    with_gates = w_gates is not None
    per_b = seq // tm
    in_specs = [
        pl.BlockSpec((tm, d), lambda i, j: (i, 0)),
        pl.BlockSpec((1, d), lambda i, j: (0, 0)),
        pl.BlockSpec((1, 1, d), lambda i, j: (i // per_b, 0, 0)),
        pl.BlockSpec((1, 1, d), lambda i, j: (i // per_b, 0, 0)),
        pl.BlockSpec((d, tn), lambda i, j: (0, j)),
        pl.BlockSpec((tm, LANES), lambda i, j: (i, 0)),
        pl.BlockSpec((tm, LANES), lambda i, j: (i, 0)),
        pl.BlockSpec((tm, LANES), lambda i, j: (i, 0)),
    ]
    args = [x2d, g.reshape(1, d), sc[:, None, :], sh[:, None, :], w, *tables]
    out_specs = [pl.BlockSpec((tm, tn), lambda i, j: (i, j))]
    out_shape = [jax.ShapeDtypeStruct((t, n), BF16)]
    if with_gates:
        ng = w_gates.shape[0]
        in_specs.append(pl.BlockSpec((ng, d), lambda i, j: (0, 0)))
        args.append(w_gates)
        out_specs.append(pl.BlockSpec((ng, tm), lambda i, j: (0, i)))
        out_shape.append(jax.ShapeDtypeStruct((ng, t), F32))
    res = pl.pallas_call(
        functools.partial(_proj_kernel, plain_tiles=tuple(plain_tiles), with_gates=with_gates),
        grid=(t // tm, n // tn),
        in_specs=in_specs,
        out_specs=out_specs,
        out_shape=out_shape,
        scratch_shapes=[pltpu.VMEM((tm, d), BF16)],
        compiler_params=_cparams("arbitrary", "arbitrary"),
        name="norm_proj",
    )(*args)
    return res if with_gates else res[0]


def _attn_scratch(tk, nq, dv):
    return [
        pltpu.VMEM((2, tk, nq), F32),
        pltpu.VMEM((2, 1, nq), F32),
        pltpu.VMEM((2, tk, nq), BF16),
        pltpu.VMEM((2, 1, nq), F32),
        pltpu.VMEM((1, nq), F32),
        pltpu.VMEM((dv, nq), F32),
    ]


def _pipelined_attention(n_full, n_diag, scores_full, scores_diag, v_tile, scratch):
    s_scr, mx_scr, p_scr, al_scr, m_ref, acc_ref = scratch

    def stage_scores(fn, tile, slot):
        s = fn(tile)
        s_scr[slot] = s
        mx_scr[slot] = jnp.max(s, axis=0, keepdims=True)

    def stage_softmax(slot):
        m_prev = m_ref[...]
        m_new = jnp.maximum(m_prev, mx_scr[slot])
        alpha = jnp.exp2(m_prev - m_new)
        m_ref[...] = m_new
        al_scr[slot] = alpha
        p_scr[slot] = jnp.exp2(s_scr[slot] - m_new).astype(BF16)

    def stage_pv(pos, slot):
        tile = jnp.where(pos < n_diag, n_full + jnp.maximum(pos, 0), pos - n_diag)
        pv = lax.dot_general(v_tile(tile), p_scr[slot], TN_DIMS, preferred_element_type=F32)
        acc_ref[...] = acc_ref[...] * al_scr[slot] + pv

    def step(fn, pos, slot):
        nxt = pos + 1
        stage_scores(fn, jnp.where(nxt < n_diag, n_full + nxt, nxt - n_diag), 1 - slot)
        stage_softmax(slot)
        stage_pv(pos - 1, 1 - slot)

    m_ref[...] = jnp.full(m_ref.shape, NEG_INF, F32)
    acc_ref[...] = jnp.zeros(acc_ref.shape, F32)
    p_scr[1] = jnp.zeros(p_scr.shape[1:], BF16)
    al_scr[1] = jnp.ones(al_scr.shape[1:], F32)
    stage_scores(scores_diag, n_full, 0)
    for p in range(n_diag - 1):
        step(scores_diag, p, p % 2)

    first = n_diag - 1
    s0 = first % 2

    def pair(jj, carry):
        step(scores_full, first + 2 * jj, s0)
        step(scores_full, first + 2 * jj + 1, 1 - s0)
        return carry

    lax.fori_loop(0, n_full // 2, pair, 0)
    odd = n_full % 2 == 1
    last = first + n_full

    @pl.when(odd)
    def _():
        step(scores_full, last - 1, s0)
        stage_softmax(1 - s0)
        stage_pv(last - 1, s0)
        stage_pv(last, 1 - s0)

    @pl.when(jnp.logical_not(odd))
    def _():
        stage_softmax(s0)
        stage_pv(last - 1, 1 - s0)
        stage_pv(last, s0)


def _split_halves(q):
    lane = lax.broadcasted_iota(jnp.int32, q.shape, 1)
    zero = jnp.zeros_like(q)
    return jnp.where(lane < HEAD_DIM, q, zero), jnp.where(lane >= HEAD_DIM, q, zero)


def _diff_attn_kernel(q_ref, k_ref, v_ref, lam_ref, g_ref, o_ref, qm, *scratch,
                      tq, tk, lambda_init):
    qi = pl.program_id(2)
    lo, hi = _split_halves(q_ref[...])
    qm[0:tq, :] = lo
    qm[tq:2 * tq, :] = hi
    q0 = qi * tq
    acc_ref = scratch[-1]

    def scores_full(t):
        k = k_ref[pl.ds(pl.multiple_of(t * tk, tk), tk), :]
        return lax.dot_general(k, qm[...], NT_DIMS, preferred_element_type=F32)

    def scores_diag(t):
        s = scores_full(t)
        k_idx = t * tk + lax.broadcasted_iota(jnp.int32, s.shape, 0)
        q_idx = q0 + lax.broadcasted_iota(jnp.int32, s.shape, 1) % tq
        return jnp.where(k_idx <= q_idx, s, NEG_INF)

    def v_tile(t):
        v = v_ref[pl.ds(pl.multiple_of(t * tk, tk), tk), :]
        return jnp.concatenate([v, jnp.ones((tk, DA_ONES_COLS), BF16)], axis=1)

    _pipelined_attention(q0 // tk, tq // tk, scores_full, scores_diag, v_tile, scratch)

    lam = lam_ref[...]
    lam_full = (jnp.exp(jnp.sum(lam[0:1] * lam[1:2], axis=-1, keepdims=True))
                - jnp.exp(jnp.sum(lam[2:3] * lam[3:4], axis=-1, keepdims=True)) + lambda_init)
    acc = acc_ref[0:DA_V_DIM, :]
    l = jnp.maximum(acc_ref[DA_V_DIM:DA_V_DIM + 1, :], 1e-30)
    ot = acc[:, 0:tq] / l[:, 0:tq] - lam_full * (acc[:, tq:2 * tq] / l[:, tq:2 * tq])
    ms = jnp.mean(ot * ot, axis=0, keepdims=True)
    ot = (ot * lax.rsqrt(ms + NORM_EPS)) * g_ref[...] * (1.0 - lambda_init)
    o_ref[...] = ot.T.astype(o_ref.dtype)


def _diff_attention(qkv, lam, subln_g, batch, seq, lambda_init, *, tq=1024, tk=1024):
    assert tq % tk == 0 and seq % tq == 0
    t, n3 = qkv.shape
    d = n3 // 3
    heads = d // DA_V_DIM
    nq = seq // tq
    cb = d // LANES
    g_b = jnp.broadcast_to(subln_g.astype(F32)[:, None], (DA_V_DIM, tq))
    return pl.pallas_call(
        functools.partial(_diff_attn_kernel, tq=tq, tk=tk, lambda_init=lambda_init),
        grid=(batch, heads, nq),
        in_specs=[
            pl.BlockSpec((tq, LANES), lambda b, h, i: (b * nq + i, h)),
            pl.BlockSpec((seq, LANES), lambda b, h, i: (b, cb + h)),
            pl.BlockSpec((seq, LANES), lambda b, h, i: (b, 2 * cb + h)),
            pl.BlockSpec(lam.shape, lambda b, h, i: (0, 0)),
            pl.BlockSpec((DA_V_DIM, tq), lambda b, h, i: (0, 0)),
        ],
        out_specs=pl.BlockSpec((tq, LANES), lambda b, h, i: (b * nq + i, h)),
        out_shape=jax.ShapeDtypeStruct((t, d), BF16),
        scratch_shapes=[pltpu.VMEM((2 * tq, LANES), BF16)] + _attn_scratch(tk, 2 * tq, DA_V_DIM + DA_ONES_COLS),
        compiler_params=_cparams("arbitrary", "arbitrary", "arbitrary"),
        name="diff_attn",
    )(qkv, qkv, qkv, lam.astype(F32), g_b)


def _route(logits, n_exp):
    lane = lax.broadcasted_iota(jnp.int32, logits.shape, 1)
    lane_f = lane.astype(F32)
    big = float(LANES)
    is_grp = (lane >= n_exp) & (lane < n_exp + MOE_GROUPS)
    gl = jnp.where(is_grp, logits, -jnp.inf)
    ge = jnp.where(is_grp, jnp.exp(gl - jnp.max(gl, axis=-1, keepdims=True)), 0.0)
    pg = ge / jnp.sum(ge, axis=-1, keepdims=True)
    p_top = jnp.max(pg, axis=-1, keepdims=True)
    grp = jnp.min(jnp.where(is_grp & (pg == p_top), lane_f, big), axis=-1, keepdims=True) - n_exp
    in_grp = (lane < n_exp) & ((lane // MOE_EXPERTS_PER_GROUP).astype(F32) == grp)
    sel = jnp.where(in_grp, logits, -jnp.inf)
    v1 = jnp.max(sel, axis=-1, keepdims=True)
    i1 = jnp.min(jnp.where(sel == v1, lane_f, big), axis=-1, keepdims=True)
    sel2 = jnp.where(lane_f == i1, -jnp.inf, sel)
    v2 = jnp.max(sel2, axis=-1, keepdims=True)
    i2 = jnp.min(jnp.where(sel2 == v2, lane_f, big), axis=-1, keepdims=True)
    e2 = jnp.exp(v2 - v1)
    den = 1.0 + e2
    w1 = (1.0 / den) * p_top
    w2 = (e2 / den) * p_top
    return jnp.where(lane_f == i1, w1, 0.0) + jnp.where(lane_f == i2, w2, 0.0), grp


def _outproj_router_kernel(o_ref, w_ref, x_ref, g1_ref, ng_ref, sc_ref, sh_ref, wr_ref, br_ref,
                           xo_ref, h_ref, comb_ref, gid_ref, *, n_exp):
    mix = jnp.dot(o_ref[...], w_ref[...], preferred_element_type=F32)
    x = x_ref[...] + g1_ref[0] * mix
    xo_ref[...] = x
    h = _rms_modulate(x, ng_ref[...], sc_ref[0], sh_ref[0])
    h_hi = h.astype(BF16)
    h_ref[...] = h_hi
    h_lo = (h - h_hi.astype(F32)).astype(BF16)
    wr = wr_ref[...]
    hw = jnp.dot(h_hi, wr, preferred_element_type=F32)
    logits = (hw[:, 0:LANES] + hw[:, LANES:2 * LANES]
              + jnp.dot(h_lo, wr[:, 0:LANES], preferred_element_type=F32) + br_ref[...])
    comb, grp = _route(logits, n_exp)
    comb_ref[...] = comb[:, 0:n_exp]
    gid_ref[...] = jnp.broadcast_to(grp, logits.shape).T[0:gid_ref.shape[0], :]


def _outproj_router(o, w_out, x2d, g1, ng, sc, sh, wr, br, seq, n_exp, *, tm=1024):
    t, d = x2d.shape
    per_b = seq // tm
    bvec = lambda i: (i // per_b, 0, 0)
    return pl.pallas_call(
        functools.partial(_outproj_router_kernel, n_exp=n_exp),
        grid=(t // tm,),
        in_specs=[
            pl.BlockSpec((tm, d), lambda i: (i, 0)),
            pl.BlockSpec((d, d), lambda i: (0, 0)),
            pl.BlockSpec((tm, d), lambda i: (i, 0)),
            pl.BlockSpec((1, 1, d), bvec),
            pl.BlockSpec((1, d), lambda i: (0, 0)),
            pl.BlockSpec((1, 1, d), bvec),
            pl.BlockSpec((1, 1, d), bvec),
            pl.BlockSpec((d, 2 * LANES), lambda i: (0, 0)),
            pl.BlockSpec((1, LANES), lambda i: (0, 0)),
        ],
        out_specs=[
            pl.BlockSpec((tm, d), lambda i: (i, 0)),
            pl.BlockSpec((tm, d), lambda i: (i, 0)),
            pl.BlockSpec((tm, n_exp), lambda i: (i, 0)),
            pl.BlockSpec((8, tm), lambda i: (0, i)),
        ],
        out_shape=[
            jax.ShapeDtypeStruct((t, d), F32),
            jax.ShapeDtypeStruct((t, d), BF16),
            jax.ShapeDtypeStruct((t, n_exp), F32),
            jax.ShapeDtypeStruct((8, t), F32),
        ],
        compiler_params=_cparams("arbitrary"),
        name="outproj_router",
    )(o, w_out, x2d, g1[:, None, :], ng.reshape(1, d), sc[:, None, :], sh[:, None, :], wr, br)


MOE_CHUNK = 128
MOE_EXPERTS_PER_STEP = 4


def _moe_kernel(h_ref, wg_ref, wu_ref, wd_ref, comb_ref, gid_ref, tri_ref, x_ref, g2_ref, fg_ref,
                o_ref, perm_scr, hs_scr, cs_scr, ys_scr, off_smem, nch_smem, *, final_norm):
    step = pl.program_id(1)
    tm = h_ref.shape[0]
    rows = perm_scr.shape[0]
    n_exp = comb_ref.shape[1]
    steps_per_group = MOE_EXPERTS_PER_GROUP // MOE_EXPERTS_PER_STEP

    @pl.when(step == 0)
    def _():
        gid = gid_ref[0:1, :]
        grow = lax.broadcasted_iota(jnp.int32, (8, tm), 0).astype(F32)
        onehot = jnp.where(grow == gid, 1.0, 0.0)
        rank = jnp.dot(onehot.astype(BF16), tri_ref[...], preferred_element_type=F32)
        off = jnp.int32(0)
        off_col = jnp.zeros((8, 1), F32)
        grow_col = lax.broadcasted_iota(jnp.int32, (8, 1), 0)
        for g in range(MOE_GROUPS):
            cnt = jnp.sum(onehot[g:g + 1, :]).astype(jnp.int32)
            nch = (cnt + (MOE_CHUNK - 1)) // MOE_CHUNK
            off_smem[g] = off
            nch_smem[g] = nch
            off_col = jnp.where(grow_col == g, off.astype(F32), off_col)
            off = off + nch * MOE_CHUNK
        dest = jnp.sum(onehot * (rank + off_col), axis=0, keepdims=True)
        row_id = lax.broadcasted_iota(jnp.int32, (rows, tm), 0)
        perm = jnp.where(row_id == dest.astype(jnp.int32), 1.0, 0.0).astype(BF16)
        perm_scr[...] = perm
        hs_scr[...] = jnp.dot(perm, h_ref[...], preferred_element_type=F32).astype(BF16)
        comb = comb_ref[...]
        comb_hi = comb.astype(BF16)
        comb_lo = (comb - comb_hi.astype(F32)).astype(BF16)
        cs_scr[...] = (jnp.dot(perm, comb_hi, preferred_element_type=F32)
                       + jnp.dot(perm, comb_lo, preferred_element_type=F32))
        ys_scr[...] = jnp.zeros(ys_scr.shape, F32)

    g = step // steps_per_group
    off = off_smem[g]
    nch = nch_smem[g]

    def run_experts(r0, rows_n):
        lane = lax.broadcasted_iota(jnp.int32, (rows_n, n_exp), 1)
        xs = hs_scr[pl.ds(r0, rows_n), :]
        cs = cs_scr[pl.ds(r0, rows_n), :]
        y = jnp.zeros((rows_n, xs.shape[1]), F32)
        for k in range(MOE_EXPERTS_PER_STEP):
            gate = jnp.dot(xs, wg_ref[k], preferred_element_type=F32)
            up = jnp.dot(xs, wu_ref[k], preferred_element_type=F32)
            a = (gate * jax.nn.sigmoid(gate)) * up
            e = step * MOE_EXPERTS_PER_STEP + k
            ce = jnp.sum(jnp.where(lane == e, cs, 0.0), axis=-1, keepdims=True)
            y = y + jnp.dot((a * ce).astype(BF16), wd_ref[k], preferred_element_type=F32)
        ys_scr[pl.ds(r0, rows_n), :] += y

    def chunk_pair(c, carry):
        run_experts(pl.multiple_of(off + c * (2 * MOE_CHUNK), MOE_CHUNK), 2 * MOE_CHUNK)
        return carry

    lax.fori_loop(0, nch // 2, chunk_pair, 0)

    @pl.when(nch % 2 == 1)
    def _():
        run_experts(pl.multiple_of(off + (nch - 1) * MOE_CHUNK, MOE_CHUNK), MOE_CHUNK)

    @pl.when(step == pl.num_programs(1) - 1)
    def _():
        y = lax.dot_general(perm_scr[...], ys_scr[...].astype(BF16), TN_DIMS,
                            preferred_element_type=F32)
        x = x_ref[...] + g2_ref[0] * y
        if final_norm:
            ms = jnp.mean(x * x, axis=-1, keepdims=True)
            x = (x * lax.rsqrt(ms + NORM_EPS)) * fg_ref[...]
        o_ref[...] = x


def _moe(h, w_gate, w_up, w_down, layer, comb, gid_t, x2d, g2, final_g, seq, *, final_norm, tm=1024):
    t, d = x2d.shape
    _, n_exp, _, hidden = w_gate.shape
    per_b = seq // tm
    eps = MOE_EXPERTS_PER_STEP
    rows = tm + MOE_GROUPS * MOE_CHUNK
    idx = jnp.arange(tm)
    tri = (idx[:, None] < idx[None, :]).astype(BF16)
    return pl.pallas_call(
        functools.partial(_moe_kernel, final_norm=final_norm),
        grid=(t // tm, n_exp // eps),
        in_specs=[
            pl.BlockSpec((tm, d), lambda i, s: (i, 0)),
            pl.BlockSpec((None, eps, d, hidden), lambda i, s: (layer, s, 0, 0)),
            pl.BlockSpec((None, eps, d, hidden), lambda i, s: (layer, s, 0, 0)),
            pl.BlockSpec((None, eps, hidden, d), lambda i, s: (layer, s, 0, 0)),
            pl.BlockSpec((tm, n_exp), lambda i, s: (i, 0)),
            pl.BlockSpec((8, tm), lambda i, s: (0, i)),
            pl.BlockSpec((tm, tm), lambda i, s: (0, 0)),
            pl.BlockSpec((tm, d), lambda i, s: (i, 0)),
            pl.BlockSpec((1, 1, d), lambda i, s: (i // per_b, 0, 0)),
            pl.BlockSpec((1, d), lambda i, s: (0, 0)),
        ],
        out_specs=pl.BlockSpec((tm, d), lambda i, s: (i, 0)),
        out_shape=jax.ShapeDtypeStruct((t, d), F32),
        scratch_shapes=[
            pltpu.VMEM((rows, tm), BF16),
            pltpu.VMEM((rows, d), BF16),
            pltpu.VMEM((rows, n_exp), F32),
            pltpu.VMEM((rows, d), F32),
            pltpu.SMEM((MOE_GROUPS,), jnp.int32),
            pltpu.SMEM((MOE_GROUPS,), jnp.int32),
        ],
        compiler_params=_cparams("arbitrary", "arbitrary"),
        name="moe",
    )(h, w_gate, w_up, w_down, comb, gid_t, tri, x2d, g2[:, None, :], final_g.reshape(1, d))


def _compress_kernel(ch_ref, w1a_ref, w1b_ref, pos_ref, w1f_ref, b1_ref, w2_ref, o_ref):
    ch = ch_ref[0, 0, 0]
    n = ch.shape[0]
    a = jnp.dot(ch, w1a_ref[0], preferred_element_type=F32)
    b = jnp.dot(ch, w1b_ref[0], preferred_element_type=F32)
    const = jnp.dot(pos_ref[0], w1f_ref[0], precision=HIGHEST, preferred_element_type=F32)[0:1]
    hid = a + pltpu.roll(b, n - 1, 0) + (const + b1_ref[0])
    act = jax.nn.gelu(hid, approximate=True)
    o_ref[0, 0, 0] = jnp.dot(act.astype(BF16), w2_ref[0], preferred_element_type=F32).astype(o_ref.dtype)


def _nsa_compress(ch, cmp_pos, cmp_w1, cmp_b1, cmp_w2):
    b, two, g, nc, f = ch.shape
    hid = cmp_w1.shape[-1]
    w1a = cmp_w1[:, :f, :].astype(BF16)
    w1b = cmp_w1[:, f:, :].astype(BF16)
    pos = jnp.zeros((two, 8, 2 * f), F32).at[:, 0, :].set(cmp_pos.reshape(two, 2 * f))
    w2 = jnp.concatenate([cmp_w2, cmp_w2], axis=-1).astype(BF16)
    return pl.pallas_call(
        _compress_kernel,
        grid=(b, two, g),
        in_specs=[
            pl.BlockSpec((1, 1, 1, nc, f), lambda i, j, k: (i, j, k, 0, 0)),
            pl.BlockSpec((1, f, hid), lambda i, j, k: (j, 0, 0)),
            pl.BlockSpec((1, f, hid), lambda i, j, k: (j, 0, 0)),
            pl.BlockSpec((1, 8, 2 * f), lambda i, j, k: (j, 0, 0)),
            pl.BlockSpec((1, 2 * f, hid), lambda i, j, k: (j, 0, 0)),
            pl.BlockSpec((1, 1, hid), lambda i, j, k: (j, 0, 0)),
            pl.BlockSpec((1, hid, LANES), lambda i, j, k: (j, 0, 0)),
        ],
        out_specs=pl.BlockSpec((1, 1, 1, nc, LANES), lambda i, j, k: (i, j, k, 0, 0)),
        out_shape=jax.ShapeDtypeStruct((b, two, g, nc, LANES), BF16),
        compiler_params=_cparams("arbitrary", "arbitrary", "arbitrary"),
        name="nsa_compress",
    )(ch, w1a, w1b, pos, cmp_w1.astype(F32), cmp_b1[:, None, :].astype(F32), w2)


def _stack_group_queries(q_ref, qm, tq):
    for j in range(q_ref.shape[1] // LANES):
        lo, hi = _split_halves(q_ref[:, j * LANES:(j + 1) * LANES])
        qm[(2 * j) * tq:(2 * j + 1) * tq, 0:LANES] = lo
        qm[(2 * j + 1) * tq:(2 * j + 2) * tq, 0:LANES] = hi


def _own_group_values_with_ones(v, g):
    lane = lax.broadcasted_iota(jnp.int32, v.shape, 1)
    own_half = (lane >= HEAD_DIM) == (g == 1)
    return jnp.where(own_half, v, jnp.ones_like(v))


def _gated_pairs_to_token_major(ot, gt_ref, branch, tq, hpg):
    g = pl.program_id(1)
    tiles = []
    for j in range(hpg // 2):
        parts = []
        for h in (2 * j, 2 * j + 1):
            row = gt_ref[pl.ds((g * hpg + h) * 3 + branch, 1), :]
            parts.append(ot[:, h * tq:(h + 1) * tq] * row)
        tiles.append(jnp.concatenate(parts, axis=0).T)
    return tiles


NSA_CMP_CLASSES = 4


def _nsa_window_tiles(q0, qm, kw_ref, vw_ref, gt_ref, acc_ref, *, tq, hpg):
    g = pl.program_id(1)
    span = NSA_WINDOW + tq
    start = pl.multiple_of(jnp.maximum(q0 - NSA_WINDOW, 0), tq)
    k = kw_ref[pl.ds(start, span), :]
    v = vw_ref[pl.ds(start, span), :]
    s = lax.dot_general(k, qm[...], NT_DIMS, preferred_element_type=F32)
    k_idx = start + lax.broadcasted_iota(jnp.int32, (span, tq), 0)
    q_idx = q0 + lax.broadcasted_iota(jnp.int32, (span, tq), 1)
    dist = q_idx - k_idx
    bias = jnp.where(dist >= 0, jnp.where(dist < NSA_WINDOW, 0.0, NEG_INF), NEG_INF)
    s = s + jnp.concatenate([bias] * hpg, axis=1)
    m = jnp.max(s, axis=0, keepdims=True)
    e = jnp.exp2(s - m)
    r = 1.0 / jnp.maximum(jnp.sum(e, axis=0, keepdims=True), 1e-30)
    acc_ref[...] = lax.dot_general(v, e.astype(BF16), TN_DIMS, preferred_element_type=F32) * r
    ot = acc_ref[pl.ds(pl.multiple_of(g * HEAD_DIM, HEAD_DIM), HEAD_DIM), :]
    return _gated_pairs_to_token_major(ot, gt_ref, 2, tq, hpg)


def _nsa_cmp_body(q0, qm, kc_ref, vc_ref, mt_ref, gt_ref, kw_ref, vw_ref, o_ref, mask_ref, acc_ref,
                  *, tq, hpg, n_top, n_cmp, n_blk):
    win_tiles = _nsa_window_tiles(q0, qm, kw_ref, vw_ref, gt_ref, acc_ref, tq=tq, hpg=hpg)
    kc = kc_ref[0, 0, 0, 0:n_cmp, :]
    s = lax.dot_general(kc, qm[...], NT_DIMS, preferred_element_type=F32)
    n_idx = lax.broadcasted_iota(jnp.int32, (n_cmp, tq), 0)
    q_idx = q0 + lax.broadcasted_iota(jnp.int32, (n_cmp, tq), 1)
    valid = n_idx * NSA_CMP_STRIDE + (NSA_CMP_BLOCK - 1) <= q_idx
    bias = jnp.where(valid, 0.0, NEG_INF)
    s = s + jnp.concatenate([bias] * hpg, axis=1)
    m = jnp.maximum(jnp.max(s, axis=0, keepdims=True), 0.1 * NEG_INF)
    e = jnp.exp2(s - m)
    r = 1.0 / jnp.maximum(jnp.sum(e, axis=0, keepdims=True), 1e-30)
    p = e * r
    ot = lax.dot_general(vc_ref[0, 0, 0, 0:n_cmp, :], p.astype(BF16), TN_DIMS,
                         preferred_element_type=F32)
    tiles = _gated_pairs_to_token_major(ot[0:HEAD_DIM], gt_ref, 0, tq, hpg)
    for j, (tile, win) in enumerate(zip(tiles, win_tiles)):
        o_ref[:, j * LANES:(j + 1) * LANES] = (tile + win).astype(o_ref.dtype)

    psum = p[:, 0:tq]
    for h in range(1, hpg):
        psum = psum + p[:, h * tq:(h + 1) * tq]
    p_hi = psum.astype(BF16)
    p_lo = (psum - p_hi.astype(F32)).astype(BF16)
    mt = mt_ref[0:n_blk, 0:n_cmp]
    imp = (jnp.dot(mt, p_hi, preferred_element_type=F32)
           + jnp.dot(mt, p_lo, preferred_element_type=F32))
    j_idx = lax.broadcasted_iota(jnp.int32, imp.shape, 0)
    j_f = j_idx.astype(F32)
    q_blk = (q0 + lax.broadcasted_iota(jnp.int32, imp.shape, 1)) // NSA_SEL_BLOCK
    forced = (j_idx == 0) | (j_idx == q_blk) | (j_idx == q_blk - 1)
    val = jnp.where(forced, -jnp.inf, jnp.where(j_idx <= q_blk, imp, -NSA_FORCE))
    sel_bias = jnp.where(forced, 0.0, NEG_INF)
    for _ in range(max(n_top - 3, 0)):
        mx = jnp.max(val, axis=0, keepdims=True)
        first = jnp.min(jnp.where(val == mx, j_f, float(n_blk)), axis=0, keepdims=True)
        hit = j_f == first
        sel_bias = jnp.where(hit, 0.0, sel_bias)
        val = jnp.where(hit, -jnp.inf, val)
    n_lane = mask_ref.shape[3]
    if n_blk < n_lane:
        sel_bias = jnp.concatenate([sel_bias, jnp.full((n_lane - n_blk, tq), NEG_INF, F32)], axis=0)
    mask_ref[0, 0] = sel_bias.T.astype(mask_ref.dtype)


def _nsa_cmp_win_kernel(q_ref, kc_ref, vc_ref, mt_ref, gt_ref, kw_ref, vw_ref, o_ref, mask_ref,
                        qm, acc_ref, *, tq, hpg, n_top, n_sel):
    qi = pl.program_id(2)
    q0 = qi * tq
    _stack_group_queries(q_ref, qm, tq)
    nc = kc_ref.shape[3]
    seq = n_sel * NSA_SEL_BLOCK
    cls = (q0 + tq - 1) // (seq // NSA_CMP_CLASSES)
    for c in range(NSA_CMP_CLASSES):
        @pl.when(cls == c)
        def _(c=c):
            _nsa_cmp_body(q0, qm, kc_ref, vc_ref, mt_ref, gt_ref, kw_ref, vw_ref, o_ref, mask_ref,
                          acc_ref, tq=tq, hpg=hpg,
                          n_top=min(n_top, (c + 1) * n_sel // NSA_CMP_CLASSES),
                          n_cmp=(c + 1) * nc // NSA_CMP_CLASSES,
                          n_blk=(c + 1) * n_sel // NSA_CMP_CLASSES)


def _nsa_sel_kernel(q_ref, k_ref, v_ref, mask_ref, gt_ref, ocw_ref, o_ref,
                    qm, *scratch, tq, tk, hpg):
    g = pl.program_id(1)
    qi = pl.program_id(2)
    q0 = qi * tq
    _stack_group_queries(q_ref, qm, tq)
    acc_ref = scratch[-1]
    blocks = tk // NSA_SEL_BLOCK

    def scores_full(t):
        first_blk = t * blocks
        window = pl.multiple_of((first_blk // LANES) * LANES, LANES)
        bias_win = mask_ref[0, 0, :, pl.ds(window, LANES)]
        for h in range(hpg):
            qm[h * tq:(h + 1) * tq, LANES:2 * LANES] = bias_win
        lane = lax.broadcasted_iota(jnp.int32, (tk, LANES), 1)
        blk = first_blk - window + lax.broadcasted_iota(jnp.int32, (tk, LANES), 0) // NSA_SEL_BLOCK
        onehot = jnp.where(lane == blk, 1.0, 0.0).astype(BF16)
        k = k_ref[pl.ds(pl.multiple_of(t * tk, tk), tk), :]
        kx = jnp.concatenate([k, onehot], axis=1)
        return lax.dot_general(kx, qm[...], NT_DIMS, preferred_element_type=F32)

    def scores_diag(t):
        s = scores_full(t)
        k_idx = t * tk + lax.broadcasted_iota(jnp.int32, s.shape, 0)
        q_idx = q0 + lax.broadcasted_iota(jnp.int32, s.shape, 1) % tq
        return jnp.where(k_idx <= q_idx, s, NEG_INF)

    def v_tile(t):
        return _own_group_values_with_ones(v_ref[pl.ds(pl.multiple_of(t * tk, tk), tk), :], g)

    _pipelined_attention(q0 // tk, tq // tk, scores_full, scores_diag, v_tile, scratch)

    own = pl.multiple_of(g * HEAD_DIM, HEAD_DIM)
    other = pl.multiple_of((1 - g) * HEAD_DIM, HEAD_DIM)
    inv_l = 1.0 / jnp.maximum(acc_ref[pl.ds(other, 1), :], 1e-30)
    ot = acc_ref[pl.ds(own, HEAD_DIM), :] * inv_l
    tiles = _gated_pairs_to_token_major(ot, gt_ref, 1, tq, hpg)
    for j, tile in enumerate(tiles):
        sl = slice(j * LANES, (j + 1) * LANES)
        total = tile + ocw_ref[:, sl].astype(F32)
        o_ref[:, sl] = total.astype(o_ref.dtype)


def _nsa_attention(proj, gates_t, cmp_kv, batch, seq, d_model, *, tq=256, tq_sel=512, tk=512):
    assert tq_sel % tk == 0 and seq % tq_sel == 0
    t = proj.shape[0]
    groups = NSA_KV_GROUPS
    hpg = d_model // HEAD_DIM // groups
    gw = hpg * HEAD_DIM
    nq = seq // tq
    nq_sel = seq // tq_sel
    nc = cmp_kv.shape[3]
    n_sel = seq // NSA_SEL_BLOCK
    n_top = min(NSA_TOP_N, n_sel)
    ng = gates_t.shape[0]
    qcb = d_model // LANES

    ci = jnp.arange(nc)[None, :] * NSA_CMP_STRIDE
    sj = jnp.arange(n_sel)[:, None] * NSA_SEL_BLOCK
    mt = ((ci < sj + NSA_SEL_BLOCK) & (ci + NSA_CMP_BLOCK > sj)
          & (jnp.arange(nc)[None, :] < nc - 1)).astype(BF16)

    q_spec = pl.BlockSpec((tq, gw), lambda b, g, i: (b * nq + i, g))
    gt_spec = pl.BlockSpec((ng, tq), lambda b, g, i: (0, b * nq + i))
    o_spec = pl.BlockSpec((tq, gw), lambda b, g, i: (b * nq + i, g))
    o_shape = jax.ShapeDtypeStruct((t, d_model), BF16)
    sem = ("arbitrary", "arbitrary", "arbitrary")
    qm_scr = pltpu.VMEM((hpg * tq, LANES), BF16)

    n_lane = max(n_sel, LANES)
    o_cmp_win, sel_mask = pl.pallas_call(
        functools.partial(_nsa_cmp_win_kernel, tq=tq, hpg=hpg, n_top=n_top, n_sel=n_sel),
        grid=(batch, groups, nq),
        in_specs=[
            q_spec,
            pl.BlockSpec((1, 1, 1, nc, LANES), lambda b, g, i: (b, 0, g, 0, 0)),
            pl.BlockSpec((1, 1, 1, nc, LANES), lambda b, g, i: (b, 1, g, 0, 0)),
            pl.BlockSpec((n_sel, nc), lambda b, g, i: (0, 0)),
            gt_spec,
            pl.BlockSpec((seq, LANES), lambda b, g, i: (b, qcb + 5 + g)),
            pl.BlockSpec((seq, LANES), lambda b, g, i: (b, qcb + 7)),
        ],
        out_specs=[o_spec, pl.BlockSpec((1, 1, tq, n_lane), lambda b, g, i: (b, g, i, 0))],
        out_shape=[o_shape, jax.ShapeDtypeStruct((batch, groups, seq, n_lane), BF16)],
        scratch_shapes=[qm_scr, pltpu.VMEM((LANES, hpg * tq), F32)],
        compiler_params=_cparams(*sem),
        name="nsa_cmp_win",
    )(proj, cmp_kv, cmp_kv, mt, gates_t, proj, proj)

    rows_sel = pl.BlockSpec((tq_sel, gw), lambda b, g, i: (b * nq_sel + i, g))
    return pl.pallas_call(
        functools.partial(_nsa_sel_kernel, tq=tq_sel, tk=tk, hpg=hpg),
        grid=(batch, groups, nq_sel),
        in_specs=[
            rows_sel,
            pl.BlockSpec((seq, LANES), lambda b, g, i: (b, qcb + 2 + g)),
            pl.BlockSpec((seq, LANES), lambda b, g, i: (b, qcb + 4)),
            pl.BlockSpec((1, 1, tq_sel, n_lane), lambda b, g, i: (b, g, i, 0)),
            pl.BlockSpec((ng, tq_sel), lambda b, g, i: (0, b * nq_sel + i)),
            rows_sel,
        ],
        out_specs=rows_sel,
        out_shape=o_shape,
        scratch_shapes=([pltpu.VMEM((hpg * tq_sel, 2 * LANES), BF16)]
                        + _attn_scratch(tk, hpg * tq_sel, LANES)),
        compiler_params=_cparams(*sem),
        name="nsa_sel",
    )(proj, proj, proj, sel_mask, gates_t, o_cmp_win)


def _diff_in_weights(w_in, d_model):
    scale = HEAD_DIM ** -0.5 * LOG2E
    wq = w_in[:, :d_model] * scale
    return jnp.concatenate([wq, w_in[:, d_model:]], axis=1).astype(BF16)


def _nsa_in_weights(w_in, d_model):
    g, hd = NSA_KV_GROUPS, HEAD_DIM
    kvw = g * hd
    scale = hd ** -0.5 * LOG2E
    q = w_in[:, :d_model] * scale
    off = d_model
    sec = [w_in[:, off + i * kvw: off + (i + 1) * kvw] for i in range(6)]
    dup = lambda w: jnp.concatenate([w[:, i * hd:(i + 1) * hd] for i in range(g) for _ in range(2)], axis=1)
    w_main = jnp.concatenate([q, sec[0], sec[1], dup(sec[2]), sec[3], dup(sec[4]), sec[5]], axis=1)
    w_gates = w_in[:, off + 6 * kvw:].T
    return w_main.astype(BF16), w_gates.astype(BF16)


def _router_weights(w_grp, b_grp, w_exp, b_exp):
    d = w_grp.shape[0]
    n_exp = w_exp.shape[0] * w_exp.shape[2]
    we = jnp.transpose(w_exp, (1, 0, 2)).reshape(d, n_exp)
    pad = LANES - n_exp - w_grp.shape[1]
    wr = jnp.concatenate([we, w_grp, jnp.zeros((d, pad), F32)], axis=1).astype(F32)
    wr_hi = wr.astype(BF16)
    wr_lo = (wr - wr_hi.astype(F32)).astype(BF16)
    br = jnp.concatenate([b_exp.reshape(-1), b_grp, jnp.zeros((pad,), F32)])[None, :].astype(F32)
    return jnp.concatenate([wr_hi, wr_lo], axis=1), br, n_exp


def kernel(x, c, positions, ada_w, ada_b, norm_g, final_g, diff_w_in, diff_w_out, diff_lambda,
           diff_subln_g, nsa_w_in, nsa_w_out, nsa_cmp_pos, nsa_cmp_w1, nsa_cmp_b1, nsa_cmp_w2,
           moe_w_group, moe_b_group, moe_w_expert, moe_b_expert, moe_w_gate, moe_w_up, moe_w_down):
    batch, seq, d = x.shape
    depth = ada_w.shape[0]
    x2d = x.reshape(batch * seq, d)
    tables = _rope_tables(positions)
    mod = _adaln_mod(c, ada_w, ada_b)
    w_gate, w_up, w_down = (w.astype(BF16) for w in (moe_w_gate, moe_w_up, moe_w_down))

    for i in range(depth):
        sh1, sc1, g1, sh2, sc2, g2 = jnp.split(mod[i], 6, axis=-1)
        j = i // N_MIXERS
        if i % N_MIXERS == 0:
            lambda_init = 0.8 - 0.6 * math.exp(-0.3 * i)
            qkv = _norm_proj(x2d, norm_g[i, 0], sc1, sh1, _diff_in_weights(diff_w_in[j], d), tables,
                             seq, tn=512, plain_tiles=range(2 * d // 512, 3 * d // 512))
            o = _diff_attention(qkv, diff_lambda[j], diff_subln_g[j], batch, seq, lambda_init)
            w_out = diff_w_out[j]
        else:
            w_main, w_gates = _nsa_in_weights(nsa_w_in[j], d)
            proj, gates_t = _norm_proj(x2d, norm_g[i, 0], sc1, sh1, w_main, tables, seq,
                                       tn=256, plain_tiles=(d // 256,), w_gates=w_gates)
            groups = NSA_KV_GROUPS
            kcvc = proj[:, d:d + 2 * groups * HEAD_DIM]
            ch = kcvc.reshape(batch, seq, 2, groups, HEAD_DIM).transpose(0, 2, 3, 1, 4)
            ch = ch.reshape(batch, 2, groups, seq // NSA_CMP_STRIDE, NSA_CMP_STRIDE * HEAD_DIM)
            cmp_kv = _nsa_compress(ch, nsa_cmp_pos[j], nsa_cmp_w1[j], nsa_cmp_b1[j], nsa_cmp_w2[j])
            o = _nsa_attention(proj, gates_t, cmp_kv, batch, seq, d)
            w_out = nsa_w_out[j]
        wr, br, n_exp = _router_weights(moe_w_group[i], moe_b_group[i], moe_w_expert[i], moe_b_expert[i])
        x2d, h2, comb, gid_t = _outproj_router(o, w_out.astype(BF16), x2d, g1, norm_g[i, 1], sc2, sh2,
                                               wr, br, seq, n_exp)
        x2d = _moe(h2, w_gate, w_up, w_down, i, comb, gid_t, x2d, g2, final_g, seq,
                   final_norm=(i == depth - 1))
    return x2d.reshape(batch, seq, d)
```

```python
import functools
import math

import jax
import jax.numpy as jnp
from jax import lax
from jax.experimental import pallas as pl
from jax.experimental.pallas import tpu as pltpu

F32 = jnp.float32
BF16 = jnp.bfloat16
HIGHEST = lax.Precision.HIGHEST

ROPE_THETA = 500000.0
ROT_DIM = 16
NORM_EPS = 1e-6
NEG_INF = -1e30
HEAD_DIM = 64
DA_V_DIM = 2 * HEAD_DIM
DA_ONES_COLS = 16
NSA_KV_GROUPS = 2
NSA_CMP_BLOCK = 32
NSA_CMP_STRIDE = 16
NSA_SEL_BLOCK = 64
NSA_TOP_N = 16
NSA_WINDOW = 512
NSA_FORCE = 1e4
MOE_GROUPS = 4
MOE_EXPERTS_PER_GROUP = 8
N_MIXERS = 2
LOG2E = 1.4426950408889634

LANES = 128
PROJ_ROW_CHUNK = 256
VMEM_LIMIT_BYTES = 56 * 1024 * 1024

NT_DIMS = (((1,), (1,)), ((), ()))
TN_DIMS = (((0,), (0,)), ((), ()))


def _cparams(*sem):
    return pltpu.CompilerParams(dimension_semantics=sem, vmem_limit_bytes=VMEM_LIMIT_BYTES)


def _mod_kernel(c_ref, w_ref, b_ref, o_ref):
    c = c_ref[...]
    ca = c * jax.nn.sigmoid(c)
    o_ref[0] = jnp.dot(ca, w_ref[0], precision=HIGHEST, preferred_element_type=F32) + b_ref[0]


def _adaln_mod(c, ada_w, ada_b):
    depth, d, n = ada_w.shape
    b = c.shape[0]
    rows = 8
    tn = 1536
    c_pad = jnp.zeros((rows, d), F32).at[:b].set(c)
    out = pl.pallas_call(
        _mod_kernel,
        grid=(depth, n // tn),
        in_specs=[
            pl.BlockSpec((rows, d), lambda i, j: (0, 0)),
            pl.BlockSpec((1, d, tn), lambda i, j: (i, 0, j)),
            pl.BlockSpec((1, 1, tn), lambda i, j: (i, 0, j)),
        ],
        out_specs=pl.BlockSpec((1, rows, tn), lambda i, j: (i, 0, j)),
        out_shape=jax.ShapeDtypeStruct((depth, rows, n), F32),
        compiler_params=_cparams("arbitrary", "arbitrary"),
        name="adaln_mod",
    )(c_pad, ada_w, ada_b.reshape(depth, 1, n))
    return out[:, :b, :]


def _rope_tables(positions):
    half = ROT_DIM // 2
    inv_freq = ROPE_THETA ** (-jnp.arange(0, ROT_DIM, 2, dtype=F32) / ROT_DIM)
    ang = positions.astype(F32).reshape(-1)[:, None] * inv_freq
    cos, sin = jnp.cos(ang), jnp.sin(ang)
    t = ang.shape[0]
    rest = HEAD_DIM - ROT_DIM
    z8 = jnp.zeros((t, half), F32)
    cos_h = jnp.concatenate([cos, cos, jnp.ones((t, rest), F32)], axis=-1)
    sina_h = jnp.concatenate([-sin, z8, jnp.zeros((t, rest), F32)], axis=-1)
    sinb_h = jnp.concatenate([z8, sin, jnp.zeros((t, rest), F32)], axis=-1)
    rep = LANES // HEAD_DIM
    return tuple(jnp.tile(a, (1, rep)) for a in (cos_h, sina_h, sinb_h))


def _rms_modulate(x, g, sc, sh):
    ms = jnp.mean(x * x, axis=-1, keepdims=True)
    y = x * lax.rsqrt(ms + NORM_EPS)
    return (y * g) * (1.0 + sc) + sh


def _proj_kernel(*refs, plain_tiles, with_gates):
    if with_gates:
        (x_ref, g_ref, sc_ref, sh_ref, w_ref, cos_ref, sa_ref, sb_ref, wg_ref,
         o_ref, gt_ref, h_scr) = refs
    else:
        x_ref, g_ref, sc_ref, sh_ref, w_ref, cos_ref, sa_ref, sb_ref, o_ref, h_scr = refs
    j = pl.program_id(1)
    tn = o_ref.shape[1]

    @pl.when(j == 0)
    def _():
        h = _rms_modulate(x_ref[...], g_ref[...], sc_ref[0], sh_ref[0])
        h_scr[...] = h.astype(BF16)
        if with_gates:
            logits = lax.dot_general(wg_ref[...], h_scr[...], NT_DIMS, preferred_element_type=F32)
            gt_ref[...] = jax.nn.sigmoid(logits)

    is_plain = functools.reduce(jnp.logical_or, [j == t for t in plain_tiles], j < 0)
    for r in range(h_scr.shape[0] // PROJ_ROW_CHUNK):
        rows = slice(r * PROJ_ROW_CHUNK, (r + 1) * PROJ_ROW_CHUNK)
        acc = jnp.dot(h_scr[rows, :], w_ref[...], preferred_element_type=F32)
        cos = jnp.where(is_plain, 1.0, cos_ref[rows, :])
        sa = jnp.where(is_plain, 0.0, sa_ref[rows, :])
        sb = jnp.where(is_plain, 0.0, sb_ref[rows, :])
        shift_up = pltpu.roll(acc, tn - ROT_DIM // 2, 1)
        shift_dn = pltpu.roll(acc, ROT_DIM // 2, 1)
        for c in range(tn // LANES):
            sl = slice(c * LANES, (c + 1) * LANES)
            y = acc[:, sl] * cos + shift_up[:, sl] * sa + shift_dn[:, sl] * sb
            o_ref[rows, sl] = y.astype(o_ref.dtype)


def _norm_proj(x2d, g, sc, sh, w, tables, seq, *, tn, plain_tiles, w_gates=None, tm=1024):
    t, d = x2d.shape
    n = w.shape[1]
    with_gates = w_gates is not None
    per_b = seq // tm
    in_specs = [
        pl.BlockSpec((tm, d), lambda i, j: (i, 0)),
        pl.BlockSpec((1, d), lambda i, j: (0, 0)),
        pl.BlockSpec((1, 1, d), lambda i, j: (i // per_b, 0, 0)),
        pl.BlockSpec((1, 1, d), lambda i, j: (i // per_b, 0, 0)),
        pl.BlockSpec((d, tn), lambda i, j: (0, j)),
        pl.BlockSpec((tm, LANES), lambda i, j: (i, 0)),
        pl.BlockSpec((tm, LANES), lambda i, j: (i, 0)),
        pl.BlockSpec((tm, LANES), lambda i, j: (i, 0)),
    ]
    args = [x2d, g.reshape(1, d), sc[:, None, :], sh[:, None, :], w, *tables]
    out_specs = [pl.BlockSpec((tm, tn), lambda i, j: (i, j))]
    out_shape = [jax.ShapeDtypeStruct((t, n), BF16)]
    if with_gates:
        ng = w_gates.shape[0]
        in_specs.append(pl.BlockSpec((ng, d), lambda i, j: (0, 0)))
        args.append(w_gates)
        out_specs.append(pl.BlockSpec((ng, tm), lambda i, j: (0, i)))
        out_shape.append(jax.ShapeDtypeStruct((ng, t), F32))
    res = pl.pallas_call(
        functools.partial(_proj_kernel, plain_tiles=tuple(plain_tiles), with_gates=with_gates),
        grid=(t // tm, n // tn),
        in_specs=in_specs,
        out_specs=out_specs,
        out_shape=out_shape,
        scratch_shapes=[pltpu.VMEM((tm, d), BF16)],
        compiler_params=_cparams("arbitrary", "arbitrary"),
        name="norm_proj",
    )(*args)
    return res if with_gates else res[0]


def _attn_scratch(tk, nq, dv):
    return [
        pltpu.VMEM((2, tk, nq), F32),
        pltpu.VMEM((2, 1, nq), F32),
        pltpu.VMEM((2, tk, nq), BF16),
        pltpu.VMEM((2, 1, nq), F32),
        pltpu.VMEM((1, nq), F32),
        pltpu.VMEM((dv, nq), F32),
    ]


def _pipelined_attention(n_full, n_diag, scores_full, scores_diag, v_tile, scratch):
    s_scr, mx_scr, p_scr, al_scr, m_ref, acc_ref = scratch

    def stage_scores(fn, tile, slot):
        s = fn(tile)
        s_scr[slot] = s
        mx_scr[slot] = jnp.max(s, axis=0, keepdims=True)

    def stage_softmax(slot):
        m_prev = m_ref[...]
        m_new = jnp.maximum(m_prev, mx_scr[slot])
        alpha = jnp.exp2(m_prev - m_new)
        m_ref[...] = m_new
        al_scr[slot] = alpha
        p_scr[slot] = jnp.exp2(s_scr[slot] - m_new).astype(BF16)

    def stage_pv(pos, slot):
        tile = jnp.where(pos < n_diag, n_full + jnp.maximum(pos, 0), pos - n_diag)
        pv = lax.dot_general(v_tile(tile), p_scr[slot], TN_DIMS, preferred_element_type=F32)
        acc_ref[...] = acc_ref[...] * al_scr[slot] + pv

    def step(fn, pos, slot):
        nxt = pos + 1
        stage_scores(fn, jnp.where(nxt < n_diag, n_full + nxt, nxt - n_diag), 1 - slot)
        stage_softmax(slot)
        stage_pv(pos - 1, 1 - slot)

    m_ref[...] = jnp.full(m_ref.shape, NEG_INF, F32)
    acc_ref[...] = jnp.zeros(acc_ref.shape, F32)
    p_scr[1] = jnp.zeros(p_scr.shape[1:], BF16)
    al_scr[1] = jnp.ones(al_scr.shape[1:], F32)
    stage_scores(scores_diag, n_full, 0)
    for p in range(n_diag - 1):
        step(scores_diag, p, p % 2)

    first = n_diag - 1
    s0 = first % 2

    def pair(jj, carry):
        step(scores_full, first + 2 * jj, s0)
        step(scores_full, first + 2 * jj + 1, 1 - s0)
        return carry

    lax.fori_loop(0, n_full // 2, pair, 0)
    odd = n_full % 2 == 1
    last = first + n_full

    @pl.when(odd)
    def _():
        step(scores_full, last - 1, s0)
        stage_softmax(1 - s0)
        stage_pv(last - 1, s0)
        stage_pv(last, 1 - s0)

    @pl.when(jnp.logical_not(odd))
    def _():
        stage_softmax(s0)
        stage_pv(last - 1, 1 - s0)
        stage_pv(last, s0)


def _split_halves(q):
    lane = lax.broadcasted_iota(jnp.int32, q.shape, 1)
    zero = jnp.zeros_like(q)
    return jnp.where(lane < HEAD_DIM, q, zero), jnp.where(lane >= HEAD_DIM, q, zero)


def _diff_attn_kernel(q_ref, k_ref, v_ref, lam_ref, g_ref, o_ref, qm, *scratch,
                      tq, tk, lambda_init):
    qi = pl.program_id(2)
    lo, hi = _split_halves(q_ref[...])
    qm[0:tq, :] = lo
    qm[tq:2 * tq, :] = hi
    q0 = qi * tq
    acc_ref = scratch[-1]

    def scores_full(t):
        k = k_ref[pl.ds(pl.multiple_of(t * tk, tk), tk), :]
        return lax.dot_general(k, qm[...], NT_DIMS, preferred_element_type=F32)

    def scores_diag(t):
        s = scores_full(t)
        k_idx = t * tk + lax.broadcasted_iota(jnp.int32, s.shape, 0)
        q_idx = q0 + lax.broadcasted_iota(jnp.int32, s.shape, 1) % tq
        return jnp.where(k_idx <= q_idx, s, NEG_INF)

    def v_tile(t):
        v = v_ref[pl.ds(pl.multiple_of(t * tk, tk), tk), :]
        return jnp.concatenate([v, jnp.ones((tk, DA_ONES_COLS), BF16)], axis=1)

    _pipelined_attention(q0 // tk, tq // tk, scores_full, scores_diag, v_tile, scratch)

    lam = lam_ref[...]
    lam_full = (jnp.exp(jnp.sum(lam[0:1] * lam[1:2], axis=-1, keepdims=True))
                - jnp.exp(jnp.sum(lam[2:3] * lam[3:4], axis=-1, keepdims=True)) + lambda_init)
    acc = acc_ref[0:DA_V_DIM, :]
    l = jnp.maximum(acc_ref[DA_V_DIM:DA_V_DIM + 1, :], 1e-30)
    ot = acc[:, 0:tq] / l[:, 0:tq] - lam_full * (acc[:, tq:2 * tq] / l[:, tq:2 * tq])
    ms = jnp.mean(ot * ot, axis=0, keepdims=True)
    ot = (ot * lax.rsqrt(ms + NORM_EPS)) * g_ref[...] * (1.0 - lambda_init)
    o_ref[...] = ot.T.astype(o_ref.dtype)


def _diff_attention(qkv, lam, subln_g, batch, seq, lambda_init, *, tq=1024, tk=1024):
    assert tq % tk == 0 and seq % tq == 0
    t, n3 = qkv.shape
    d = n3 // 3
    heads = d // DA_V_DIM
    nq = seq // tq
    cb = d // LANES
    g_b = jnp.broadcast_to(subln_g.astype(F32)[:, None], (DA_V_DIM, tq))
    return pl.pallas_call(
        functools.partial(_diff_attn_kernel, tq=tq, tk=tk, lambda_init=lambda_init),
        grid=(batch, heads, nq),
        in_specs=[
            pl.BlockSpec((tq, LANES), lambda b, h, i: (b * nq + i, h)),
            pl.BlockSpec((seq, LANES), lambda b, h, i: (b, cb + h)),
            pl.BlockSpec((seq, LANES), lambda b, h, i: (b, 2 * cb + h)),
            pl.BlockSpec(lam.shape, lambda b, h, i: (0, 0)),
            pl.BlockSpec((DA_V_DIM, tq), lambda b, h, i: (0, 0)),
        ],
        out_specs=pl.BlockSpec((tq, LANES), lambda b, h, i: (b * nq + i, h)),
        out_shape=jax.ShapeDtypeStruct((t, d), BF16),
        scratch_shapes=[pltpu.VMEM((2 * tq, LANES), BF16)] + _attn_scratch(tk, 2 * tq, DA_V_DIM + DA_ONES_COLS),
        compiler_params=_cparams("arbitrary", "arbitrary", "arbitrary"),
        name="diff_attn",
    )(qkv, qkv, qkv, lam.astype(F32), g_b)


def _route(logits, n_exp):
    lane = lax.broadcasted_iota(jnp.int32, logits.shape, 1)
    lane_f = lane.astype(F32)
    big = float(LANES)
    is_grp = (lane >= n_exp) & (lane < n_exp + MOE_GROUPS)
    gl = jnp.where(is_grp, logits, -jnp.inf)
    ge = jnp.where(is_grp, jnp.exp(gl - jnp.max(gl, axis=-1, keepdims=True)), 0.0)
    pg = ge / jnp.sum(ge, axis=-1, keepdims=True)
    p_top = jnp.max(pg, axis=-1, keepdims=True)
    grp = jnp.min(jnp.where(is_grp & (pg == p_top), lane_f, big), axis=-1, keepdims=True) - n_exp
    in_grp = (lane < n_exp) & ((lane // MOE_EXPERTS_PER_GROUP).astype(F32) == grp)
    sel = jnp.where(in_grp, logits, -jnp.inf)
    v1 = jnp.max(sel, axis=-1, keepdims=True)
    i1 = jnp.min(jnp.where(sel == v1, lane_f, big), axis=-1, keepdims=True)
    sel2 = jnp.where(lane_f == i1, -jnp.inf, sel)
    v2 = jnp.max(sel2, axis=-1, keepdims=True)
    i2 = jnp.min(jnp.where(sel2 == v2, lane_f, big), axis=-1, keepdims=True)
    e2 = jnp.exp(v2 - v1)
    den = 1.0 + e2
    w1 = (1.0 / den) * p_top
    w2 = (e2 / den) * p_top
    return jnp.where(lane_f == i1, w1, 0.0) + jnp.where(lane_f == i2, w2, 0.0), grp


def _outproj_router_kernel(o_ref, w_ref, x_ref, g1_ref, ng_ref, sc_ref, sh_ref, wr_ref, br_ref,
                           xo_ref, h_ref, comb_ref, gid_ref, *, n_exp):
    mix = jnp.dot(o_ref[...], w_ref[...], preferred_element_type=F32)
    x = x_ref[...] + g1_ref[0] * mix
    xo_ref[...] = x
    h = _rms_modulate(x, ng_ref[...], sc_ref[0], sh_ref[0])
    h_hi = h.astype(BF16)
    h_ref[...] = h_hi
    h_lo = (h - h_hi.astype(F32)).astype(BF16)
    wr = wr_ref[...]
    hw = jnp.dot(h_hi, wr, preferred_element_type=F32)
    logits = (hw[:, 0:LANES] + hw[:, LANES:2 * LANES]
              + jnp.dot(h_lo, wr[:, 0:LANES], preferred_element_type=F32) + br_ref[...])
    comb, grp = _route(logits, n_exp)
    comb_ref[...] = comb[:, 0:n_exp]
    gid_ref[...] = jnp.broadcast_to(grp, logits.shape).T[0:gid_ref.shape[0], :]


def _outproj_router(o, w_out, x2d, g1, ng, sc, sh, wr, br, seq, n_exp, *, tm=1024):
    t, d = x2d.shape
    per_b = seq // tm
    bvec = lambda i: (i // per_b, 0, 0)
    return pl.pallas_call(
        functools.partial(_outproj_router_kernel, n_exp=n_exp),
        grid=(t // tm,),
        in_specs=[
            pl.BlockSpec((tm, d), lambda i: (i, 0)),
            pl.BlockSpec((d, d), lambda i: (0, 0)),
            pl.BlockSpec((tm, d), lambda i: (i, 0)),
            pl.BlockSpec((1, 1, d), bvec),
            pl.BlockSpec((1, d), lambda i: (0, 0)),
            pl.BlockSpec((1, 1, d), bvec),
            pl.BlockSpec((1, 1, d), bvec),
            pl.BlockSpec((d, 2 * LANES), lambda i: (0, 0)),
            pl.BlockSpec((1, LANES), lambda i: (0, 0)),
        ],
        out_specs=[
            pl.BlockSpec((tm, d), lambda i: (i, 0)),
            pl.BlockSpec((tm, d), lambda i: (i, 0)),
            pl.BlockSpec((tm, n_exp), lambda i: (i, 0)),
            pl.BlockSpec((8, tm), lambda i: (0, i)),
        ],
        out_shape=[
            jax.ShapeDtypeStruct((t, d), F32),
            jax.ShapeDtypeStruct((t, d), BF16),
            jax.ShapeDtypeStruct((t, n_exp), F32),
            jax.ShapeDtypeStruct((8, t), F32),
        ],
        compiler_params=_cparams("arbitrary"),
        name="outproj_router",
    )(o, w_out, x2d, g1[:, None, :], ng.reshape(1, d), sc[:, None, :], sh[:, None, :], wr, br)


MOE_CHUNK = 128
MOE_EXPERTS_PER_STEP = 4


def _moe_kernel(h_ref, wg_ref, wu_ref, wd_ref, comb_ref, gid_ref, tri_ref, x_ref, g2_ref, fg_ref,
                o_ref, perm_scr, hs_scr, cs_scr, ys_scr, off_smem, nch_smem, *, final_norm):
    step = pl.program_id(1)
    tm = h_ref.shape[0]
    rows = perm_scr.shape[0]
    n_exp = comb_ref.shape[1]
    steps_per_group = MOE_EXPERTS_PER_GROUP // MOE_EXPERTS_PER_STEP

    @pl.when(step == 0)
    def _():
        gid = gid_ref[0:1, :]
        grow = lax.broadcasted_iota(jnp.int32, (8, tm), 0).astype(F32)
        onehot = jnp.where(grow == gid, 1.0, 0.0)
        rank = jnp.dot(onehot.astype(BF16), tri_ref[...], preferred_element_type=F32)
        off = jnp.int32(0)
        off_col = jnp.zeros((8, 1), F32)
        grow_col = lax.broadcasted_iota(jnp.int32, (8, 1), 0)
        for g in range(MOE_GROUPS):
            cnt = jnp.sum(onehot[g:g + 1, :]).astype(jnp.int32)
            nch = (cnt + (MOE_CHUNK - 1)) // MOE_CHUNK
            off_smem[g] = off
            nch_smem[g] = nch
            off_col = jnp.where(grow_col == g, off.astype(F32), off_col)
            off = off + nch * MOE_CHUNK
        dest = jnp.sum(onehot * (rank + off_col), axis=0, keepdims=True)
        row_id = lax.broadcasted_iota(jnp.int32, (rows, tm), 0)
        perm = jnp.where(row_id == dest.astype(jnp.int32), 1.0, 0.0).astype(BF16)
        perm_scr[...] = perm
        hs_scr[...] = jnp.dot(perm, h_ref[...], preferred_element_type=F32).astype(BF16)
        comb = comb_ref[...]
        comb_hi = comb.astype(BF16)
        comb_lo = (comb - comb_hi.astype(F32)).astype(BF16)
        cs_scr[...] = (jnp.dot(perm, comb_hi, preferred_element_type=F32)
                       + jnp.dot(perm, comb_lo, preferred_element_type=F32))
        ys_scr[...] = jnp.zeros(ys_scr.shape, F32)

    g = step // steps_per_group
    off = off_smem[g]
    nch = nch_smem[g]

    def run_experts(r0, rows_n):
        lane = lax.broadcasted_iota(jnp.int32, (rows_n, n_exp), 1)
        xs = hs_scr[pl.ds(r0, rows_n), :]
        cs = cs_scr[pl.ds(r0, rows_n), :]
        y = jnp.zeros((rows_n, xs.shape[1]), F32)
        for k in range(MOE_EXPERTS_PER_STEP):
            gate = jnp.dot(xs, wg_ref[k], preferred_element_type=F32)
            up = jnp.dot(xs, wu_ref[k], preferred_element_type=F32)
            a = (gate * jax.nn.sigmoid(gate)) * up
            e = step * MOE_EXPERTS_PER_STEP + k
            ce = jnp.sum(jnp.where(lane == e, cs, 0.0), axis=-1, keepdims=True)
            y = y + jnp.dot((a * ce).astype(BF16), wd_ref[k], preferred_element_type=F32)
        ys_scr[pl.ds(r0, rows_n), :] += y

    def chunk_pair(c, carry):
        run_experts(pl.multiple_of(off + c * (2 * MOE_CHUNK), MOE_CHUNK), 2 * MOE_CHUNK)
        return carry

    lax.fori_loop(0, nch // 2, chunk_pair, 0)

    @pl.when(nch % 2 == 1)
    def _():
        run_experts(pl.multiple_of(off + (nch - 1) * MOE_CHUNK, MOE_CHUNK), MOE_CHUNK)

    @pl.when(step == pl.num_programs(1) - 1)
    def _():
        y = lax.dot_general(perm_scr[...], ys_scr[...].astype(BF16), TN_DIMS,
                            preferred_element_type=F32)
        x = x_ref[...] + g2_ref[0] * y
        if final_norm:
            ms = jnp.mean(x * x, axis=-1, keepdims=True)
            x = (x * lax.rsqrt(ms + NORM_EPS)) * fg_ref[...]
        o_ref[...] = x


def _moe(h, w_gate, w_up, w_down, layer, comb, gid_t, x2d, g2, final_g, seq, *, final_norm, tm=1024):
    t, d = x2d.shape
    _, n_exp, _, hidden = w_gate.shape
    per_b = seq // tm
    eps = MOE_EXPERTS_PER_STEP
    rows = tm + MOE_GROUPS * MOE_CHUNK
    idx = jnp.arange(tm)
    tri = (idx[:, None] < idx[None, :]).astype(BF16)
    return pl.pallas_call(
        functools.partial(_moe_kernel, final_norm=final_norm),
        grid=(t // tm, n_exp // eps),
        in_specs=[
            pl.BlockSpec((tm, d), lambda i, s: (i, 0)),
            pl.BlockSpec((None, eps, d, hidden), lambda i, s: (layer, s, 0, 0)),
            pl.BlockSpec((None, eps, d, hidden), lambda i, s: (layer, s, 0, 0)),
            pl.BlockSpec((None, eps, hidden, d), lambda i, s: (layer, s, 0, 0)),
            pl.BlockSpec((tm, n_exp), lambda i, s: (i, 0)),
            pl.BlockSpec((8, tm), lambda i, s: (0, i)),
            pl.BlockSpec((tm, tm), lambda i, s: (0, 0)),
            pl.BlockSpec((tm, d), lambda i, s: (i, 0)),
            pl.BlockSpec((1, 1, d), lambda i, s: (i // per_b, 0, 0)),
            pl.BlockSpec((1, d), lambda i, s: (0, 0)),
        ],
        out_specs=pl.BlockSpec((tm, d), lambda i, s: (i, 0)),
        out_shape=jax.ShapeDtypeStruct((t, d), F32),
        scratch_shapes=[
            pltpu.VMEM((rows, tm), BF16),
            pltpu.VMEM((rows, d), BF16),
            pltpu.VMEM((rows, n_exp), F32),
            pltpu.VMEM((rows, d), F32),
            pltpu.SMEM((MOE_GROUPS,), jnp.int32),
            pltpu.SMEM((MOE_GROUPS,), jnp.int32),
        ],
        compiler_params=_cparams("arbitrary", "arbitrary"),
        name="moe",
    )(h, w_gate, w_up, w_down, comb, gid_t, tri, x2d, g2[:, None, :], final_g.reshape(1, d))


def _compress_kernel(ch_ref, w1a_ref, w1b_ref, pos_ref, w1f_ref, b1_ref, w2_ref, o_ref):
    ch = ch_ref[0, 0, 0]
    n = ch.shape[0]
    a = jnp.dot(ch, w1a_ref[0], preferred_element_type=F32)
    b = jnp.dot(ch, w1b_ref[0], preferred_element_type=F32)
    const = jnp.dot(pos_ref[0], w1f_ref[0], precision=HIGHEST, preferred_element_type=F32)[0:1]
    hid = a + pltpu.roll(b, n - 1, 0) + (const + b1_ref[0])
    act = jax.nn.gelu(hid, approximate=True)
    o_ref[0, 0, 0] = jnp.dot(act.astype(BF16), w2_ref[0], preferred_element_type=F32).astype(o_ref.dtype)


def _nsa_compress(ch, cmp_pos, cmp_w1, cmp_b1, cmp_w2):
    b, two, g, nc, f = ch.shape
    hid = cmp_w1.shape[-1]
    w1a = cmp_w1[:, :f, :].astype(BF16)
    w1b = cmp_w1[:, f:, :].astype(BF16)
    pos = jnp.zeros((two, 8, 2 * f), F32).at[:, 0, :].set(cmp_pos.reshape(two, 2 * f))
    w2 = jnp.concatenate([cmp_w2, cmp_w2], axis=-1).astype(BF16)
    return pl.pallas_call(
        _compress_kernel,
        grid=(b, two, g),
        in_specs=[
            pl.BlockSpec((1, 1, 1, nc, f), lambda i, j, k: (i, j, k, 0, 0)),
            pl.BlockSpec((1, f, hid), lambda i, j, k: (j, 0, 0)),
            pl.BlockSpec((1, f, hid), lambda i, j, k: (j, 0, 0)),
            pl.BlockSpec((1, 8, 2 * f), lambda i, j, k: (j, 0, 0)),
            pl.BlockSpec((1, 2 * f, hid), lambda i, j, k: (j, 0, 0)),
            pl.BlockSpec((1, 1, hid), lambda i, j, k: (j, 0, 0)),
            pl.BlockSpec((1, hid, LANES), lambda i, j, k: (j, 0, 0)),
        ],
        out_specs=pl.BlockSpec((1, 1, 1, nc, LANES), lambda i, j, k: (i, j, k, 0, 0)),
        out_shape=jax.ShapeDtypeStruct((b, two, g, nc, LANES), BF16),
        compiler_params=_cparams("arbitrary", "arbitrary", "arbitrary"),
        name="nsa_compress",
    )(ch, w1a, w1b, pos, cmp_w1.astype(F32), cmp_b1[:, None, :].astype(F32), w2)


def _stack_group_queries(q_ref, qm, tq):
    for j in range(q_ref.shape[1] // LANES):
        lo, hi = _split_halves(q_ref[:, j * LANES:(j + 1) * LANES])
        qm[(2 * j) * tq:(2 * j + 1) * tq, 0:LANES] = lo
        qm[(2 * j + 1) * tq:(2 * j + 2) * tq, 0:LANES] = hi


def _own_group_values_with_ones(v, g):
    lane = lax.broadcasted_iota(jnp.int32, v.shape, 1)
    own_half = (lane >= HEAD_DIM) == (g == 1)
    return jnp.where(own_half, v, jnp.ones_like(v))


def _gated_pairs_to_token_major(ot, gt_ref, branch, tq, hpg):
    g = pl.program_id(1)
    tiles = []
    for j in range(hpg // 2):
        parts = []
        for h in (2 * j, 2 * j + 1):
            row = gt_ref[pl.ds((g * hpg + h) * 3 + branch, 1), :]
            parts.append(ot[:, h * tq:(h + 1) * tq] * row)
        tiles.append(jnp.concatenate(parts, axis=0).T)
    return tiles


NSA_CMP_CLASSES = 4


def _nsa_window_tiles(q0, qm, kw_ref, vw_ref, gt_ref, acc_ref, *, tq, hpg):
    g = pl.program_id(1)
    span = NSA_WINDOW + tq
    start = pl.multiple_of(jnp.maximum(q0 - NSA_WINDOW, 0), tq)
    k = kw_ref[pl.ds(start, span), :]
    v = vw_ref[pl.ds(start, span), :]
    s = lax.dot_general(k, qm[...], NT_DIMS, preferred_element_type=F32)
    k_idx = start + lax.broadcasted_iota(jnp.int32, (span, tq), 0)
    q_idx = q0 + lax.broadcasted_iota(jnp.int32, (span, tq), 1)
    dist = q_idx - k_idx
    bias = jnp.where(dist >= 0, jnp.where(dist < NSA_WINDOW, 0.0, NEG_INF), NEG_INF)
    s = s + jnp.concatenate([bias] * hpg, axis=1)
    m = jnp.max(s, axis=0, keepdims=True)
    e = jnp.exp2(s - m)
    r = 1.0 / jnp.maximum(jnp.sum(e, axis=0, keepdims=True), 1e-30)
    acc_ref[...] = lax.dot_general(v, e.astype(BF16), TN_DIMS, preferred_element_type=F32) * r
    ot = acc_ref[pl.ds(pl.multiple_of(g * HEAD_DIM, HEAD_DIM), HEAD_DIM), :]
    return _gated_pairs_to_token_major(ot, gt_ref, 2, tq, hpg)


def _nsa_cmp_body(q0, qm, kc_ref, vc_ref, mt_ref, gt_ref, kw_ref, vw_ref, o_ref, mask_ref, acc_ref,
                  *, tq, hpg, n_top, n_cmp, n_blk):
    win_tiles = _nsa_window_tiles(q0, qm, kw_ref, vw_ref, gt_ref, acc_ref, tq=tq, hpg=hpg)
    kc = kc_ref[0, 0, 0, 0:n_cmp, :]
    s = lax.dot_general(kc, qm[...], NT_DIMS, preferred_element_type=F32)
    n_idx = lax.broadcasted_iota(jnp.int32, (n_cmp, tq), 0)
    q_idx = q0 + lax.broadcasted_iota(jnp.int32, (n_cmp, tq), 1)
    valid = n_idx * NSA_CMP_STRIDE + (NSA_CMP_BLOCK - 1) <= q_idx
    bias = jnp.where(valid, 0.0, NEG_INF)
    s = s + jnp.concatenate([bias] * hpg, axis=1)
    m = jnp.maximum(jnp.max(s, axis=0, keepdims=True), 0.1 * NEG_INF)
    e = jnp.exp2(s - m)
    r = 1.0 / jnp.maximum(jnp.sum(e, axis=0, keepdims=True), 1e-30)
    p = e * r
    ot = lax.dot_general(vc_ref[0, 0, 0, 0:n_cmp, :], p.astype(BF16), TN_DIMS,
                         preferred_element_type=F32)
    tiles = _gated_pairs_to_token_major(ot[0:HEAD_DIM], gt_ref, 0, tq, hpg)
    for j, (tile, win) in enumerate(zip(tiles, win_tiles)):
        o_ref[:, j * LANES:(j + 1) * LANES] = (tile + win).astype(o_ref.dtype)

    psum = p[:, 0:tq]
    for h in range(1, hpg):
        psum = psum + p[:, h * tq:(h + 1) * tq]
    p_hi = psum.astype(BF16)
    p_lo = (psum - p_hi.astype(F32)).astype(BF16)
    mt = mt_ref[0:n_blk, 0:n_cmp]
    imp = (jnp.dot(mt, p_hi, preferred_element_type=F32)
           + jnp.dot(mt, p_lo, preferred_element_type=F32))
    j_idx = lax.broadcasted_iota(jnp.int32, imp.shape, 0)
    j_f = j_idx.astype(F32)
    q_blk = (q0 + lax.broadcasted_iota(jnp.int32, imp.shape, 1)) // NSA_SEL_BLOCK
    forced = (j_idx == 0) | (j_idx == q_blk) | (j_idx == q_blk - 1)
    val = jnp.where(forced, -jnp.inf, jnp.where(j_idx <= q_blk, imp, -NSA_FORCE))
    sel_bias = jnp.where(forced, 0.0, NEG_INF)
    for _ in range(max(n_top - 3, 0)):
        mx = jnp.max(val, axis=0, keepdims=True)
        first = jnp.min(jnp.where(val == mx, j_f, float(n_blk)), axis=0, keepdims=True)
        hit = j_f == first
        sel_bias = jnp.where(hit, 0.0, sel_bias)
        val = jnp.where(hit, -jnp.inf, val)
    n_lane = mask_ref.shape[3]
    if n_blk < n_lane:
        sel_bias = jnp.concatenate([sel_bias, jnp.full((n_lane - n_blk, tq), NEG_INF, F32)], axis=0)
    mask_ref[0, 0] = sel_bias.T.astype(mask_ref.dtype)


def _nsa_cmp_win_kernel(q_ref, kc_ref, vc_ref, mt_ref, gt_ref, kw_ref, vw_ref, o_ref, mask_ref,
                        qm, acc_ref, *, tq, hpg, n_top, n_sel):
    qi = pl.program_id(2)
    q0 = qi * tq
    _stack_group_queries(q_ref, qm, tq)
    nc = kc_ref.shape[3]
    seq = n_sel * NSA_SEL_BLOCK
    cls = (q0 + tq - 1) // (seq // NSA_CMP_CLASSES)
    for c in range(NSA_CMP_CLASSES):
        @pl.when(cls == c)
        def _(c=c):
            _nsa_cmp_body(q0, qm, kc_ref, vc_ref, mt_ref, gt_ref, kw_ref, vw_ref, o_ref, mask_ref,
                          acc_ref, tq=tq, hpg=hpg,
                          n_top=min(n_top, (c + 1) * n_sel // NSA_CMP_CLASSES),
                          n_cmp=(c + 1) * nc // NSA_CMP_CLASSES,
                          n_blk=(c + 1) * n_sel // NSA_CMP_CLASSES)


def _nsa_sel_kernel(q_ref, k_ref, v_ref, mask_ref, gt_ref, ocw_ref, o_ref,
                    qm, *scratch, tq, tk, hpg):
    g = pl.program_id(1)
    qi = pl.program_id(2)
    q0 = qi * tq
    _stack_group_queries(q_ref, qm, tq)
    acc_ref = scratch[-1]
    blocks = tk // NSA_SEL_BLOCK

    def scores_full(t):
        first_blk = t * blocks
        window = pl.multiple_of((first_blk // LANES) * LANES, LANES)
        bias_win = mask_ref[0, 0, :, pl.ds(window, LANES)]
        for h in range(hpg):
            qm[h * tq:(h + 1) * tq, LANES:2 * LANES] = bias_win
        lane = lax.broadcasted_iota(jnp.int32, (tk, LANES), 1)
        blk = first_blk - window + lax.broadcasted_iota(jnp.int32, (tk, LANES), 0) // NSA_SEL_BLOCK
        onehot = jnp.where(lane == blk, 1.0, 0.0).astype(BF16)
        k = k_ref[pl.ds(pl.multiple_of(t * tk, tk), tk), :]
        kx = jnp.concatenate([k, onehot], axis=1)
        return lax.dot_general(kx, qm[...], NT_DIMS, preferred_element_type=F32)

    def scores_diag(t):
        s = scores_full(t)
        k_idx = t * tk + lax.broadcasted_iota(jnp.int32, s.shape, 0)
        q_idx = q0 + lax.broadcasted_iota(jnp.int32, s.shape, 1) % tq
        return jnp.where(k_idx <= q_idx, s, NEG_INF)

    def v_tile(t):
        return _own_group_values_with_ones(v_ref[pl.ds(pl.multiple_of(t * tk, tk), tk), :], g)

    _pipelined_attention(q0 // tk, tq // tk, scores_full, scores_diag, v_tile, scratch)

    own = pl.multiple_of(g * HEAD_DIM, HEAD_DIM)
    other = pl.multiple_of((1 - g) * HEAD_DIM, HEAD_DIM)
    inv_l = 1.0 / jnp.maximum(acc_ref[pl.ds(other, 1), :], 1e-30)
    ot = acc_ref[pl.ds(own, HEAD_DIM), :] * inv_l
    tiles = _gated_pairs_to_token_major(ot, gt_ref, 1, tq, hpg)
    for j, tile in enumerate(tiles):
        sl = slice(j * LANES, (j + 1) * LANES)
        total = tile + ocw_ref[:, sl].astype(F32)
        o_ref[:, sl] = total.astype(o_ref.dtype)


def _nsa_attention(proj, gates_t, cmp_kv, batch, seq, d_model, *, tq=512, tq_sel=512, tk=512):
    assert tq_sel % tk == 0 and seq % tq_sel == 0
    t = proj.shape[0]
    groups = NSA_KV_GROUPS
    hpg = d_model // HEAD_DIM // groups
    gw = hpg * HEAD_DIM
    nq = seq // tq
    nq_sel = seq // tq_sel
    nc = cmp_kv.shape[3]
    n_sel = seq // NSA_SEL_BLOCK
    n_top = min(NSA_TOP_N, n_sel)
    ng = gates_t.shape[0]
    qcb = d_model // LANES

    ci = jnp.arange(nc)[None, :] * NSA_CMP_STRIDE
    sj = jnp.arange(n_sel)[:, None] * NSA_SEL_BLOCK
    mt = ((ci < sj + NSA_SEL_BLOCK) & (ci + NSA_CMP_BLOCK > sj)
          & (jnp.arange(nc)[None, :] < nc - 1)).astype(BF16)

    q_spec = pl.BlockSpec((tq, gw), lambda b, g, i: (b * nq + i, g))
    gt_spec = pl.BlockSpec((ng, tq), lambda b, g, i: (0, b * nq + i))
    o_spec = pl.BlockSpec((tq, gw), lambda b, g, i: (b * nq + i, g))
    o_shape = jax.ShapeDtypeStruct((t, d_model), BF16)
    sem = ("arbitrary", "arbitrary", "arbitrary")
    qm_scr = pltpu.VMEM((hpg * tq, LANES), BF16)

    n_lane = max(n_sel, LANES)
    o_cmp_win, sel_mask = pl.pallas_call(
        functools.partial(_nsa_cmp_win_kernel, tq=tq, hpg=hpg, n_top=n_top, n_sel=n_sel),
        grid=(batch, groups, nq),
        in_specs=[
            q_spec,
            pl.BlockSpec((1, 1, 1, nc, LANES), lambda b, g, i: (b, 0, g, 0, 0)),
            pl.BlockSpec((1, 1, 1, nc, LANES), lambda b, g, i: (b, 1, g, 0, 0)),
            pl.BlockSpec((n_sel, nc), lambda b, g, i: (0, 0)),
            gt_spec,
            pl.BlockSpec((seq, LANES), lambda b, g, i: (b, qcb + 5 + g)),
            pl.BlockSpec((seq, LANES), lambda b, g, i: (b, qcb + 7)),
        ],
        out_specs=[o_spec, pl.BlockSpec((1, 1, tq, n_lane), lambda b, g, i: (b, g, i, 0))],
        out_shape=[o_shape, jax.ShapeDtypeStruct((batch, groups, seq, n_lane), BF16)],
        scratch_shapes=[qm_scr, pltpu.VMEM((LANES, hpg * tq), F32)],
        compiler_params=_cparams(*sem),
        name="nsa_cmp_win",
    )(proj, cmp_kv, cmp_kv, mt, gates_t, proj, proj)

    rows_sel = pl.BlockSpec((tq_sel, gw), lambda b, g, i: (b * nq_sel + i, g))
    return pl.pallas_call(
        functools.partial(_nsa_sel_kernel, tq=tq_sel, tk=tk, hpg=hpg),
        grid=(batch, groups, nq_sel),
        in_specs=[
            rows_sel,
            pl.BlockSpec((seq, LANES), lambda b, g, i: (b, qcb + 2 + g)),
            pl.BlockSpec((seq, LANES), lambda b, g, i: (b, qcb + 4)),
            pl.BlockSpec((1, 1, tq_sel, n_lane), lambda b, g, i: (b, g, i, 0)),
            pl.BlockSpec((ng, tq_sel), lambda b, g, i: (0, b * nq_sel + i)),
            rows_sel,
        ],
        out_specs=rows_sel,
        out_shape=o_shape,
        scratch_shapes=([pltpu.VMEM((hpg * tq_sel, 2 * LANES), BF16)]
                        + _attn_scratch(tk, hpg * tq_sel, LANES)),
        compiler_params=_cparams(*sem),
        name="nsa_sel",
    )(proj, proj, proj, sel_mask, gates_t, o_cmp_win)


def _diff_in_weights(w_in, d_model):
    scale = HEAD_DIM ** -0.5 * LOG2E
    wq = w_in[:, :d_model] * scale
    return jnp.concatenate([wq, w_in[:, d_model:]], axis=1).astype(BF16)


def _nsa_in_weights(w_in, d_model):
    g, hd = NSA_KV_GROUPS, HEAD_DIM
    kvw = g * hd
    scale = hd ** -0.5 * LOG2E
    q = w_in[:, :d_model] * scale
    off = d_model
    sec = [w_in[:, off + i * kvw: off + (i + 1) * kvw] for i in range(6)]
    dup = lambda w: jnp.concatenate([w[:, i * hd:(i + 1) * hd] for i in range(g) for _ in range(2)], axis=1)
    w_main = jnp.concatenate([q, sec[0], sec[1], dup(sec[2]), sec[3], dup(sec[4]), sec[5]], axis=1)
    w_gates = w_in[:, off + 6 * kvw:].T
    return w_main.astype(BF16), w_gates.astype(BF16)


def _router_weights(w_grp, b_grp, w_exp, b_exp):
    d = w_grp.shape[0]
    n_exp = w_exp.shape[0] * w_exp.shape[2]
    we = jnp.transpose(w_exp, (1, 0, 2)).reshape(d, n_exp)
    pad = LANES - n_exp - w_grp.shape[1]
    wr = jnp.concatenate([we, w_grp, jnp.zeros((d, pad), F32)], axis=1).astype(F32)
    wr_hi = wr.astype(BF16)
    wr_lo = (wr - wr_hi.astype(F32)).astype(BF16)
    br = jnp.concatenate([b_exp.reshape(-1), b_grp, jnp.zeros((pad,), F32)])[None, :].astype(F32)
    return jnp.concatenate([wr_hi, wr_lo], axis=1), br, n_exp


def kernel(x, c, positions, ada_w, ada_b, norm_g, final_g, diff_w_in, diff_w_out, diff_lambda,
           diff_subln_g, nsa_w_in, nsa_w_out, nsa_cmp_pos, nsa_cmp_w1, nsa_cmp_b1, nsa_cmp_w2,
           moe_w_group, moe_b_group, moe_w_expert, moe_b_expert, moe_w_gate, moe_w_up, moe_w_down):
    batch, seq, d = x.shape
    depth = ada_w.shape[0]
    x2d = x.reshape(batch * seq, d)
    tables = _rope_tables(positions)
    mod = _adaln_mod(c, ada_w, ada_b)
    w_gate, w_up, w_down = (w.astype(BF16) for w in (moe_w_gate, moe_w_up, moe_w_down))

    for i in range(depth):
        sh1, sc1, g1, sh2, sc2, g2 = jnp.split(mod[i], 6, axis=-1)
        j = i // N_MIXERS
        if i % N_MIXERS == 0:
            lambda_init = 0.8 - 0.6 * math.exp(-0.3 * i)
            qkv = _norm_proj(x2d, norm_g[i, 0], sc1, sh1, _diff_in_weights(diff_w_in[j], d), tables,
                             seq, tn=512, plain_tiles=range(2 * d // 512, 3 * d // 512))
            o = _diff_attention(qkv, diff_lambda[j], diff_subln_g[j], batch, seq, lambda_init)
            w_out = diff_w_out[j]
        else:
            w_main, w_gates = _nsa_in_weights(nsa_w_in[j], d)
            proj, gates_t = _norm_proj(x2d, norm_g[i, 0], sc1, sh1, w_main, tables, seq,
                                       tn=256, plain_tiles=(d // 256,), w_gates=w_gates)
            groups = NSA_KV_GROUPS
            kcvc = proj[:, d:d + 2 * groups * HEAD_DIM]
            ch = kcvc.reshape(batch, seq, 2, groups, HEAD_DIM).transpose(0, 2, 3, 1, 4)
            ch = ch.reshape(batch, 2, groups, seq // NSA_CMP_STRIDE, NSA_CMP_STRIDE * HEAD_DIM)
            cmp_kv = _nsa_compress(ch, nsa_cmp_pos[j], nsa_cmp_w1[j], nsa_cmp_b1[j], nsa_cmp_w2[j])
            o = _nsa_attention(proj, gates_t, cmp_kv, batch, seq, d)
            w_out = nsa_w_out[j]
        wr, br, n_exp = _router_weights(moe_w_group[i], moe_b_group[i], moe_w_expert[i], moe_b_expert[i])
        x2d, h2, comb, gid_t = _outproj_router(o, w_out.astype(BF16), x2d, g1, norm_g[i, 1], sc2, sh2,
                                               wr, br, seq, n_exp)
        x2d = _moe(h2, w_gate, w_up, w_down, i, comb, gid_t, x2d, g2, final_g, seq,
                   final_norm=(i == depth - 1))
    return x2d.reshape(batch, seq, d)
```
